```python
import math
import jax, jax.numpy as jnp
from jax import lax
import numpy as np

D_MODEL = 1024
BATCH = 2
SEQ = 8192
DEPTH = 4

MEM_LEN = 256
D_RNN = D_MODEL
LRU_BLOCKS = 8
LRU_BW = D_RNN // LRU_BLOCKS
CONV_W = 4
LRU_C = 8.0
N_HEADS = 16
HEAD_DIM = 64
N_KV = 4
GROUP = N_HEADS // N_KV
NSA_W = N_HEADS * HEAD_DIM
KV_W = N_KV * HEAD_DIM
CMP_STRIDE = 16
CMP_LEN = 2 * CMP_STRIDE
CMP_HIDDEN = 4 * HEAD_DIM
SEL_LEN = 64
N_SELECT = 16
WINDOW = 512
Q_BLOCK = 128
MEM_HEADS = 4
MEM_HEAD_DIM = D_MODEL // MEM_HEADS
MEM_W = MEM_HEADS * MEM_HEAD_DIM
D_FF = 4 * D_MODEL
ROPE_THETA = 10000.0
EPS = 1e-6
NEG = -1e30
FORCE = 1e4

IN_SPLITS = (D_RNN, D_RNN, NSA_W, 6 * KV_W, 3 * N_HEADS, MEM_W, 3 * D_MODEL)
D_IN = sum(IN_SPLITS)
IN_OFFSETS = tuple(int(v) for v in np.cumsum(IN_SPLITS)[:-1])

kernel_name = "hybrid_rglru_nsa_memory_block"


def rmsnorm(x, g):
    xf = x.astype(jnp.float32)
    y = xf * lax.rsqrt(jnp.mean(xf * xf, axis=-1, keepdims=True) + EPS)
    return (y * g.astype(jnp.float32)).astype(x.dtype)


def rope(x, pos):
    hd = x.shape[-1]
    half = hd // 2
    inv = ROPE_THETA ** (-jnp.arange(half, dtype=jnp.float32) * 2.0 / hd)
    ang = pos.astype(jnp.float32)[..., None] * inv
    cos = jnp.cos(ang)[:, :, None, :]
    sin = jnp.sin(ang)[:, :, None, :]
    xf = x.astype(jnp.float32)
    x1, x2 = xf[..., :half], xf[..., half:]
    return jnp.concatenate([x1 * cos - x2 * sin, x2 * cos + x1 * sin], axis=-1).astype(x.dtype)


def masked_softmax(sc, mask):
    sc = jnp.where(mask, sc.astype(jnp.float32), NEG)
    m = jnp.max(sc, axis=-1, keepdims=True)
    p = jnp.where(mask, jnp.exp(sc - m), 0.0)
    return p / jnp.maximum(jnp.sum(p, axis=-1, keepdims=True), 1e-30)


def block_diag_linear(x, w, b):
    B_, S_, C = x.shape
    xb = x.reshape(B_, S_, LRU_BLOCKS, LRU_BW)
    return (jnp.einsum('bsnc,ncd->bsnd', xb, w) + b).reshape(B_, S_, C)


def rglru_branch(xr, yr, conv_w, conv_b, wr, br, wi, bi, lam):
    xc = lax.conv_general_dilated(
        xr, conv_w[:, None, :], window_strides=(1,), padding=[(CONV_W - 1, 0)],
        dimension_numbers=('NWC', 'WIO', 'NWC'), feature_group_count=D_RNN) + conv_b
    r = jax.nn.sigmoid(block_diag_linear(xc, wr, br))
    i = jax.nn.sigmoid(block_diag_linear(xc, wi, bi))
    log_a = -LRU_C * jax.nn.softplus(-lam.astype(jnp.float32)) * r.astype(jnp.float32)
    a = jnp.exp(log_a)
    b = jnp.sqrt(-jnp.expm1(2.0 * log_a)) * (i * xc).astype(jnp.float32)

    def combine(left, right):
        a1, b1 = left
        a2, b2 = right
        return a1 * a2, a2 * b1 + b2

    _, h = lax.associative_scan(combine, (a, b), axis=1)
    return h.astype(xr.dtype) * jax.nn.gelu(yr)


def nsa_branch(q, kv, gates, positions, cmp_pe, cmp_w1, cmp_b1, cmp_w2):
    B_, S_, _ = q.shape
    n_cmp = S_ // CMP_STRIDE - 1
    n_sel = S_ // SEL_LEN
    top_n = min(N_SELECT, n_sel)
    n_qblk = S_ // Q_BLOCK
    scale = HEAD_DIM ** -0.5

    q = rope(q.reshape(B_, S_, N_HEADS, HEAD_DIM), positions)
    q = q.reshape(B_, S_, N_KV, GROUP, HEAD_DIM).transpose(0, 2, 3, 1, 4)
    kv = kv.reshape(B_, S_, 6, N_KV, HEAD_DIM)
    k_c, v_c, k_s, v_s, k_w, v_w = [kv[:, :, j] for j in range(6)]

    def compress(t, j):
        chunks = t.reshape(B_, S_ // CMP_STRIDE, CMP_STRIDE, N_KV, HEAD_DIM)
        blocks = jnp.concatenate([chunks[:, :-1], chunks[:, 1:]], axis=2)
        blocks = blocks + cmp_pe[j][None, None, :, None, :]
        flat = blocks.transpose(0, 1, 3, 2, 4).reshape(B_, n_cmp, N_KV, CMP_LEN * HEAD_DIM)
        hid = jax.nn.gelu(flat @ cmp_w1[j] + cmp_b1[j])
        return hid @ cmp_w2[j]

    cmp_end = jnp.arange(n_cmp) * CMP_STRIDE + CMP_LEN - 1
    k_cmp = rope(compress(k_c, 0), positions[:, cmp_end]).transpose(0, 2, 1, 3)
    v_cmp = compress(v_c, 1).transpose(0, 2, 1, 3)

    k_sel = rope(k_s, positions).transpose(0, 2, 1, 3).reshape(B_, N_KV, n_sel, SEL_LEN, HEAD_DIM)
    v_sel = v_s.transpose(0, 2, 1, 3).reshape(B_, N_KV, n_sel, SEL_LEN, HEAD_DIM)
    c0 = jnp.arange(n_cmp)[:, None] * CMP_STRIDE
    s0 = jnp.arange(n_sel)[None, :] * SEL_LEN
    overlap = jnp.clip(jnp.minimum(c0 + CMP_LEN, s0 + SEL_LEN) - jnp.maximum(c0, s0), 0, None).astype(jnp.float32) / CMP_LEN

    pad = ((0, 0), (0, 0), (WINDOW, 0), (0, 0))
    k_win = jnp.pad(rope(k_w, positions).transpose(0, 2, 1, 3), pad)
    v_win = jnp.pad(v_w.transpose(0, 2, 1, 3), pad)

    g = jax.nn.sigmoid(gates.astype(jnp.float32)).reshape(B_, S_, N_KV, GROUP, 3).transpose(0, 2, 3, 1, 4)
    bi = jnp.arange(B_)[:, None, None, None]
    gi = jnp.arange(N_KV)[None, :, None, None]
    blk = jnp.arange(n_sel)

    def one_block(i):
        s = i * Q_BLOCK
        t = s + jnp.arange(Q_BLOCK)
        qb = lax.dynamic_slice_in_dim(q, s, Q_BLOCK, axis=3)
        gb = lax.dynamic_slice_in_dim(g, s, Q_BLOCK, axis=3)
        sc = jnp.einsum('bgrqd,bgcd->bgrqc', qb, k_cmp) * scale
        p_c = masked_softmax(sc, cmp_end[None, :] <= t[:, None])
        o_c = jnp.einsum('bgrqc,bgcd->bgrqd', p_c.astype(v_cmp.dtype), v_cmp)
        imp = jnp.einsum('bgrqc,cj->bgqj', p_c, overlap)
        forced = (blk[None, :] == 0) | (blk[None, :] == (t // SEL_LEN)[:, None])
        causal_blk = blk[None, :] * SEL_LEN <= t[:, None]
        imp = jnp.where(forced, FORCE, jnp.where(causal_blk, imp, -FORCE))
        _, idx = lax.top_k(imp, top_n)
        k_g = k_sel[bi, gi, idx].reshape(B_, N_KV, Q_BLOCK, top_n * SEL_LEN, HEAD_DIM)
        v_g = v_sel[bi, gi, idx].reshape(B_, N_KV, Q_BLOCK, top_n * SEL_LEN, HEAD_DIM)
        kpos = (idx[..., None] * SEL_LEN + jnp.arange(SEL_LEN)).reshape(B_, N_KV, Q_BLOCK, top_n * SEL_LEN)
        sc = jnp.einsum('bgrqd,bgqkd->bgrqk', qb, k_g) * scale
        p_s = masked_softmax(sc, (kpos <= t[:, None])[:, :, None])
        o_s = jnp.einsum('bgrqk,bgqkd->bgrqd', p_s.astype(v_g.dtype), v_g)
        k_wb = lax.dynamic_slice_in_dim(k_win, s, Q_BLOCK + WINDOW, axis=2)
        v_wb = lax.dynamic_slice_in_dim(v_win, s, Q_BLOCK + WINDOW, axis=2)
        wpos = s - WINDOW + jnp.arange(Q_BLOCK + WINDOW)
        wmask = (wpos[None, :] <= t[:, None]) & (wpos[None, :] > t[:, None] - WINDOW) & (wpos[None, :] >= 0)
        sc = jnp.einsum('bgrqd,bgkd->bgrqk', qb, k_wb) * scale
        p_w = masked_softmax(sc, wmask)
        o_w = jnp.einsum('bgrqk,bgkd->bgrqd', p_w.astype(v_wb.dtype), v_wb)
        o = gb[..., 0:1] * o_c + gb[..., 1:2] * o_s + gb[..., 2:3] * o_w
        return o.astype(q.dtype)

    out = lax.map(one_block, jnp.arange(n_qblk))
    return out.transpose(1, 0, 4, 2, 3, 5).reshape(B_, S_, NSA_W)


def memory_branch(qm, mem, ln_g, w_kv):
    B_, S_, _ = qm.shape
    M_ = mem.shape[1]
    kv = (rmsnorm(mem, ln_g) @ w_kv).reshape(B_, M_, 2, MEM_HEADS, MEM_HEAD_DIM)
    k, v = kv[:, :, 0], kv[:, :, 1]
    q = qm.reshape(B_, S_, MEM_HEADS, MEM_HEAD_DIM)
    sc = jnp.einsum('bshd,bmhd->bhsm', q, k) * (MEM_HEAD_DIM ** -0.5)
    p = jax.nn.softmax(sc.astype(jnp.float32), axis=-1)
    o = jnp.einsum('bhsm,bmhd->bshd', p.astype(v.dtype), v)
    return o.reshape(B_, S_, MEM_W)


def setup_inputs(seed: int = 0) -> dict:
    key = jax.random.key(seed)
    ks = iter(jax.random.split(key, 40))
    f32 = jnp.float32

    def nrm(shape, fan_in):
        return jax.random.normal(next(ks), shape, f32) * (fan_in ** -0.5)

    def gain(shape):
        return 1.0 + 0.05 * jax.random.normal(next(ks), shape, f32)

    def small(shape, s=0.02):
        return s * jax.random.normal(next(ks), shape, f32)

    x = jax.random.normal(next(ks), (BATCH, SEQ, D_MODEL), f32)
    mem = jax.random.normal(next(ks), (BATCH, MEM_LEN, D_MODEL), f32)
    positions = jnp.broadcast_to(jnp.arange(SEQ, dtype=jnp.int32)[None, :], (BATCH, SEQ))
    u = jax.random.uniform(next(ks), (DEPTH, D_RNN), f32, minval=0.9, maxval=0.999)
    sa = u ** (1.0 / LRU_C)
    lru_lambda = jnp.log(sa) - jnp.log1p(-sa)
    return {
        "x": x,
        "mem": mem,
        "positions": positions,
        "ln_mix_pre": gain((DEPTH, D_MODEL)),
        "w_in": nrm((DEPTH, D_MODEL, D_IN), D_MODEL),
        "conv_w": nrm((DEPTH, CONV_W, D_RNN), CONV_W),
        "conv_b": small((DEPTH, D_RNN)),
        "lru_wr": nrm((DEPTH, LRU_BLOCKS, LRU_BW, LRU_BW), LRU_BW),
        "lru_br": small((DEPTH, LRU_BLOCKS, LRU_BW)),
        "lru_wi": nrm((DEPTH, LRU_BLOCKS, LRU_BW, LRU_BW), LRU_BW),
        "lru_bi": small((DEPTH, LRU_BLOCKS, LRU_BW)),
        "lru_lambda": lru_lambda,
        "cmp_pe": small((DEPTH, 2, CMP_LEN, HEAD_DIM), 0.1),
        "cmp_w1": nrm((DEPTH, 2, CMP_LEN * HEAD_DIM, CMP_HIDDEN), CMP_LEN * HEAD_DIM),
        "cmp_b1": small((DEPTH, 2, CMP_HIDDEN)),
        "cmp_w2": nrm((DEPTH, 2, CMP_HIDDEN, HEAD_DIM), CMP_HIDDEN),
        "ln_mem": gain((DEPTH, D_MODEL)),
        "w_mem_kv": nrm((DEPTH, D_MODEL, 2 * MEM_W), D_MODEL),
        "w_br_rnn": nrm((DEPTH, D_RNN, D_MODEL), D_RNN),
        "w_br_nsa": nrm((DEPTH, NSA_W, D_MODEL), NSA_W),
        "w_br_mem": nrm((DEPTH, MEM_W, D_MODEL), MEM_W),
        "w_out": nrm((DEPTH, D_MODEL, D_MODEL), D_MODEL),
        "ln_mix_post": gain((DEPTH, D_MODEL)),
        "ln_mlp_pre": gain((DEPTH, D_MODEL)),
        "mlp_w1": nrm((DEPTH, D_MODEL, D_FF), D_MODEL),
        "mlp_w2": nrm((DEPTH, D_FF, D_MODEL), D_FF),
        "ln_mlp_post": gain((DEPTH, D_MODEL)),
    }


def reference(x, mem, positions, ln_mix_pre, w_in, conv_w, conv_b, lru_wr, lru_br, lru_wi, lru_bi,
              lru_lambda, cmp_pe, cmp_w1, cmp_b1, cmp_w2, ln_mem, w_mem_kv, w_br_rnn, w_br_nsa,
              w_br_mem, w_out, ln_mix_post, ln_mlp_pre, mlp_w1, mlp_w2, ln_mlp_post):
    for l in range(DEPTH):
        h = rmsnorm(x, ln_mix_pre[l])
        proj = h @ w_in[l]
        xr, yr, q, kv, g_nsa, qm, g_merge = jnp.split(proj, IN_OFFSETS, axis=-1)
        o_a = rglru_branch(xr, yr, conv_w[l], conv_b[l], lru_wr[l], lru_br[l],
                           lru_wi[l], lru_bi[l], lru_lambda[l])
        o_b = nsa_branch(q, kv, g_nsa, positions, cmp_pe[l], cmp_w1[l], cmp_b1[l], cmp_w2[l])
        o_c = memory_branch(qm, mem, ln_mem[l], w_mem_kv[l])
        ga, gb, gc = jnp.split(jax.nn.sigmoid(g_merge), 3, axis=-1)
        merged = ga * (o_a @ w_br_rnn[l]) + gb * (o_b @ w_br_nsa[l]) + gc * (o_c @ w_br_mem[l])
        x = x + rmsnorm(merged @ w_out[l], ln_mix_post[l])
        h = rmsnorm(x, ln_mlp_pre[l])
        x = x + rmsnorm(jnp.square(jax.nn.relu(h @ mlp_w1[l])) @ mlp_w2[l], ln_mlp_post[l])
    return x
```

```python
import functools

import jax
import jax.numpy as jnp
import numpy as np
from jax import lax
from jax.experimental import pallas as pl
from jax.experimental.pallas import tpu as pltpu

F32 = jnp.float32
BF16 = jnp.bfloat16

D_MODEL = 1024
LRU_BLOCKS = 8
LRU_BW = D_MODEL // LRU_BLOCKS
CONV_W = 4
LRU_C = 8.0
N_HEADS = 16
HEAD_DIM = 64
N_KV = 4
GROUP = N_HEADS // N_KV
KV_W = N_KV * HEAD_DIM
CMP_STRIDE = 16
CMP_LEN = 32
SEL_LEN = 64
SEL_SHIFT = 6
N_SELECT = 16
WINDOW = 512
Q_BLOCK = 128
MEM_HEADS = 4
MEM_HEAD_DIM = D_MODEL // MEM_HEADS
D_FF = 4 * D_MODEL
ROPE_THETA = 10000.0
EPS = 1e-6
NEG = -1e30
FORCE = 1e4

LANES = 128
SEL_TILE = 512
WIN_KEYS = WINDOW + Q_BLOCK
VMEM_LIMIT = 48 * 1024 * 1024

COL_XR, COL_YR, COL_Q, COL_QM, COL_GM, COL_KV, COL_GN = 0, 1024, 2048, 3072, 4096, 7168, 8704
D_INP = 9216


def _cparams(sem):
    return pltpu.CompilerParams(dimension_semantics=sem, vmem_limit_bytes=VMEM_LIMIT)


def _sigmoid(x):
    return 1.0 / (1.0 + jnp.exp(-x))


def _gelu_tanh(x):
    return 0.5 * x * (1.0 + jnp.tanh(0.7978845608028654 * (x + 0.044715 * (x * x * x))))


def _rms(x, g):
    return x * lax.rsqrt(jnp.mean(x * x, axis=-1, keepdims=True) + EPS) * g


def _masked_softmax(sc, mask):
    sc = jnp.where(mask, sc, NEG)
    m = jnp.max(sc, axis=-1, keepdims=True)
    p = jnp.where(mask, jnp.exp(sc - m), 0.0)
    return p / jnp.maximum(jnp.sum(p, axis=-1, keepdims=True), 1e-30)


def _rms_kernel(x_ref, g_ref, o_ref):
    o_ref[...] = _rms(x_ref[...], g_ref[...]).astype(o_ref.dtype)


def rmsnorm_bf16(x, g, tm):
    m, d = x.shape
    return pl.pallas_call(
        _rms_kernel,
        grid=(m // tm,),
        in_specs=[pl.BlockSpec((tm, d), lambda i: (i, 0)), pl.BlockSpec((1, d), lambda i: (0, 0))],
        out_specs=pl.BlockSpec((tm, d), lambda i: (i, 0)),
        out_shape=jax.ShapeDtypeStruct((m, d), BF16),
        compiler_params=_cparams(("arbitrary",)),
        name="rmsnorm",
    )(x, g.reshape(1, d))


def _mm_kernel(a_ref, w_ref, o_ref):
    o_ref[...] = jnp.dot(a_ref[...], w_ref[...], preferred_element_type=F32).astype(o_ref.dtype)


def matmul(a, w, tm, tn, out_dtype, name):
    m, k = a.shape
    n = w.shape[1]
    return pl.pallas_call(
        _mm_kernel,
        grid=(n // tn, m // tm),
        in_specs=[pl.BlockSpec((tm, k), lambda j, i: (i, 0)), pl.BlockSpec((k, tn), lambda j, i: (0, j))],
        out_specs=pl.BlockSpec((tm, tn), lambda j, i: (i, j)),
        out_shape=jax.ShapeDtypeStruct((m, n), out_dtype),
        compiler_params=_cparams(("arbitrary", "arbitrary")),
        name=name,
    )(a, w)


def _rope_table_kernel(pos_ref, inv_ref, cos_ref, sin_ref):
    ang = pos_ref[...].astype(F32) * inv_ref[...]
    lane = lax.broadcasted_iota(jnp.int32, ang.shape, 1)
    cos_ref[...] = jnp.cos(ang)
    sin_ref[...] = jnp.where((lane & (HEAD_DIM - 1)) < HEAD_DIM // 2, -1.0, 1.0) * jnp.sin(ang)


def rope_tables(pos_flat, tm):
    n = pos_flat.shape[0]
    half = HEAD_DIM // 2
    inv = ROPE_THETA ** (-jnp.arange(half, dtype=F32) * 2.0 / HEAD_DIM)
    inv_full = jnp.tile(inv, LANES // half).reshape(1, LANES)
    return pl.pallas_call(
        _rope_table_kernel,
        grid=(n // tm,),
        in_specs=[pl.BlockSpec((tm, 1), lambda i: (i, 0)), pl.BlockSpec((1, LANES), lambda i: (0, 0))],
        out_specs=[pl.BlockSpec((tm, LANES), lambda i: (i, 0))] * 2,
        out_shape=[jax.ShapeDtypeStruct((n, LANES), F32)] * 2,
        compiler_params=_cparams(("arbitrary",)),
        name="rope_tables",
    )(pos_flat.reshape(n, 1), inv_full)


def _rope128(x, cos_t, sin_t):
    lane = lax.broadcasted_iota(jnp.int32, x.shape, 1)
    first = (lane & (HEAD_DIM - 1)) < HEAD_DIM // 2
    partner = jnp.where(first, pltpu.roll(x, LANES - HEAD_DIM // 2, 1), pltpu.roll(x, HEAD_DIM // 2, 1))
    return x * cos_t + partner * sin_t


def _rglru_kernel(xr_ref, yr_ref, cw_ref, cb_ref, wr_ref, br_ref, wi_ref, bi_ref, lam_ref, o_ref, h_sc, tail_sc):
    @pl.when(pl.program_id(1) == 0)
    def _():
        h_sc[...] = jnp.zeros_like(h_sc)
        tail_sc[...] = jnp.zeros_like(tail_sc)

    xr = xr_ref[...]
    t_len, d = xr.shape
    tail = tail_sc[...]
    row8 = lax.broadcasted_iota(jnp.int32, (8, d), 0)
    cw = cw_ref[...]
    xc = cb_ref[...] + xr * cw[CONV_W - 1 : CONV_W, :]
    for k in range(1, CONV_W):
        rolled = pltpu.roll(xr, k, 0)
        head = jnp.where(row8 < k, pltpu.roll(tail, k, 0), rolled[0:8])
        shifted = jnp.concatenate([head, rolled[8:]], axis=0)
        xc = xc + shifted * cw[CONV_W - 1 - k : CONV_W - k, :]
    tail_sc[...] = xr[t_len - 8 :]

    xcb = xc.astype(BF16)
    rl, il = [], []
    for n in range(LRU_BLOCKS):
        xb = xcb[:, n * LRU_BW : (n + 1) * LRU_BW]
        rl.append(jnp.dot(xb, wr_ref[n], preferred_element_type=F32))
        il.append(jnp.dot(xb, wi_ref[n], preferred_element_type=F32))
    r = _sigmoid(jnp.concatenate(rl, axis=1) + br_ref[...])
    ig = _sigmoid(jnp.concatenate(il, axis=1) + bi_ref[...])
    softplus_neg_lam = jnp.log1p(jnp.exp(-lam_ref[...]))
    log_a = (-LRU_C * softplus_neg_lam) * r
    a = jnp.exp(log_a)
    b = jnp.sqrt(1.0 - jnp.exp(2.0 * log_a)) * (ig * xc)

    row = lax.broadcasted_iota(jnp.int32, (t_len, d), 0)
    step = 1
    while step < t_len:
        keep = row >= step
        a_sh = jnp.where(keep, pltpu.roll(a, step, 0), 1.0)
        b_sh = jnp.where(keep, pltpu.roll(b, step, 0), 0.0)
        b = a * b_sh + b
        a = a * a_sh
        step *= 2
    h = b + a * h_sc[0:1, :]
    h_sc[...] = jnp.broadcast_to(h[t_len - 1 : t_len, :], h_sc.shape)
    o_ref[...] = (h * _gelu_tanh(yr_ref[...])).astype(o_ref.dtype)


def rglru_branch(proj, batch, seq, conv_w, conv_b, wr, br, wi, bi, lam, t_len):
    d = D_MODEL
    nt = seq // t_len
    vec = lambda v: v.reshape(1, d)
    full2 = lambda shape: pl.BlockSpec(shape, lambda b, t: (0,) * len(shape))
    return pl.pallas_call(
        _rglru_kernel,
        grid=(batch, nt),
        in_specs=[
            pl.BlockSpec((t_len, d), lambda b, t: (b * nt + t, COL_XR // d)),
            pl.BlockSpec((t_len, d), lambda b, t: (b * nt + t, COL_YR // d)),
            full2((CONV_W, d)), full2((1, d)),
            full2((LRU_BLOCKS, LRU_BW, LRU_BW)), full2((1, d)),
            full2((LRU_BLOCKS, LRU_BW, LRU_BW)), full2((1, d)),
            full2((1, d)),
        ],
        out_specs=pl.BlockSpec((t_len, d), lambda b, t: (b * nt + t, 0)),
        out_shape=jax.ShapeDtypeStruct((batch * seq, d), BF16),
        scratch_shapes=[pltpu.VMEM((8, d), F32), pltpu.VMEM((8, d), F32)],
        compiler_params=_cparams(("arbitrary", "arbitrary")),
        name="rglru",
    )(proj, proj, conv_w, vec(conv_b), wr.astype(BF16), vec(br), wi.astype(BF16), vec(bi), vec(lam))


def _krope_kernel(ks_ref, kw_ref, cos_ref, sin_ref, kso_ref, kwo_ref):
    cos_t, sin_t = cos_ref[...], sin_ref[...]
    for src, dst in ((ks_ref, kso_ref), (kw_ref, kwo_ref)):
        x = src[...]
        parts = [_rope128(x[:, c * LANES : (c + 1) * LANES], cos_t, sin_t) for c in range(KV_W // LANES)]
        dst[...] = jnp.concatenate(parts, axis=1).astype(dst.dtype)


def rope_keys(proj, cos_t, sin_t, tm):
    n = proj.shape[0]
    blk = lambda col: pl.BlockSpec((tm, KV_W), lambda i: (i, col // KV_W))
    return pl.pallas_call(
        _krope_kernel,
        grid=(n // tm,),
        in_specs=[blk(COL_KV + 2 * KV_W), blk(COL_KV + 4 * KV_W),
                  pl.BlockSpec((tm, LANES), lambda i: (i, 0)), pl.BlockSpec((tm, LANES), lambda i: (i, 0))],
        out_specs=[pl.BlockSpec((tm, KV_W), lambda i: (i, 0))] * 2,
        out_shape=[jax.ShapeDtypeStruct((n, KV_W), BF16)] * 2,
        compiler_params=_cparams(("arbitrary",)),
        name="rope_keys",
    )(proj, proj, cos_t, sin_t)


def _compress_kernel(x_ref, pe_ref, w1_ref, b1_ref, w2_ref, cos_ref, sin_ref, o_ref, *, rotary):
    x = x_ref[0]
    half = CMP_STRIDE * HEAD_DIM
    pe = pe_ref[0]
    w1 = w1_ref[0]
    u = jnp.dot((x + pe[0:1]).astype(BF16), w1[:half], preferred_element_type=F32)
    v = jnp.dot((x + pe[1:2]).astype(BF16), w1[half:], preferred_element_type=F32)
    hid = _gelu_tanh(u + pltpu.roll(v, v.shape[0] - 1, 0) + b1_ref[0])
    out = jnp.dot(hid.astype(BF16), w2_ref[0], preferred_element_type=F32)
    if rotary:
        hh = HEAD_DIM // 2
        partner = jnp.concatenate([out[:, hh:], out[:, :hh]], axis=1)
        out = out * cos_ref[0][:, :HEAD_DIM] + partner * sin_ref[0][:, :HEAD_DIM]
    o_ref[0] = out.astype(o_ref.dtype)


def compress(x_chunks, j, pe, w1, b1, w2, cos_c, sin_c, batch, rotary):
    bg, nc, width = x_chunks.shape
    return pl.pallas_call(
        functools.partial(_compress_kernel, rotary=rotary),
        grid=(bg,),
        in_specs=[
            pl.BlockSpec((1, nc, width), lambda i: (i, 0, 0)),
            pl.BlockSpec((1, 2, width), lambda i: (j, 0, 0)),
            pl.BlockSpec((1, 2 * width, w1.shape[2]), lambda i: (j, 0, 0)),
            pl.BlockSpec((1, 1, w1.shape[2]), lambda i: (j, 0, 0)),
            pl.BlockSpec((1, w2.shape[1], HEAD_DIM), lambda i: (j, 0, 0)),
            pl.BlockSpec((1, nc, LANES), lambda i: (i // N_KV, 0, 0)),
            pl.BlockSpec((1, nc, LANES), lambda i: (i // N_KV, 0, 0)),
        ],
        out_specs=pl.BlockSpec((1, nc, HEAD_DIM), lambda i: (i, 0, 0)),
        out_shape=jax.ShapeDtypeStruct((bg, nc, HEAD_DIM), BF16),
        compiler_params=_cparams(("arbitrary",)),
        name="compress_k" if rotary else "compress_v",
    )(x_chunks, pe, w1, b1, w2, cos_c, sin_c)


def _nsa_kernel(q_ref, gate_ref, cos_ref, sin_ref, kct_ref, vc_ref, kst_ref, vs_ref, kwt_ref, vw_ref, ov_ref, o_ref):
    qb = Q_BLOCK
    s0 = pl.program_id(2) * qb
    nc = kct_ref.shape[3]
    scale = HEAD_DIM ** -0.5

    q = q_ref[...]
    cos_t, sin_t = cos_ref[...], sin_ref[...]
    heads = []
    for c in range(GROUP * HEAD_DIM // LANES):
        roped = _rope128(q[:, c * LANES : (c + 1) * LANES], cos_t, sin_t)
        heads += [roped[:, :HEAD_DIM], roped[:, HEAD_DIM:]]
    qs = (jnp.concatenate(heads, axis=0) * scale).astype(BF16)

    rows = GROUP * qb
    t4 = s0 + (lax.broadcasted_iota(jnp.int32, (rows, 1), 0) & (qb - 1))

    sc = jnp.dot(qs, kct_ref[0, 0], preferred_element_type=F32)
    cmp_end = lax.broadcasted_iota(jnp.int32, (1, nc), 1) * CMP_STRIDE + (CMP_LEN - 1)
    p_c = _masked_softmax(sc, cmp_end <= t4)
    o_c = jnp.dot(p_c.astype(BF16), vc_ref[0, 0], preferred_element_type=F32)

    p_sum = p_c[0:qb]
    for r in range(1, GROUP):
        p_sum = p_sum + p_c[r * qb : (r + 1) * qb]
    imp = jnp.dot(p_sum, ov_ref[...], preferred_element_type=F32, precision=lax.Precision.HIGHEST)
    blk = lax.broadcasted_iota(jnp.int32, (qb, LANES), 1)
    tq = s0 + lax.broadcasted_iota(jnp.int32, (qb, LANES), 0)
    forced = (blk == 0) | (blk == (tq >> SEL_SHIFT))
    imp = jnp.where(forced, FORCE, jnp.where(blk * SEL_LEN <= tq, imp, -FORCE))

    blk_f = blk.astype(F32)
    sel = jnp.zeros((qb, LANES), F32)
    work = imp
    for _ in range(N_SELECT):
        m = jnp.max(work, axis=1, keepdims=True)
        idx = jnp.min(jnp.where(work == m, blk_f, float(LANES)), axis=1, keepdims=True)
        hit = blk_f == idx
        sel = jnp.where(hit, 1.0, sel)
        work = jnp.where(hit, -3e38, work)
    sel_b = sel.astype(BF16)

    blocks_per_tile = SEL_TILE // SEL_LEN
    e_row = lax.broadcasted_iota(jnp.int32, (LANES, SEL_TILE), 0)
    e_col = lax.broadcasted_iota(jnp.int32, (LANES, SEL_TILE), 1) >> SEL_SHIFT
    k_lane = lax.broadcasted_iota(jnp.int32, (1, SEL_TILE), 1)

    def sel_step(kb, carry):
        m, l, acc = carry
        k0 = pl.multiple_of(kb * SEL_TILE, SEL_TILE)
        s_ = jnp.dot(qs, kst_ref[0, 0, :, pl.ds(k0, SEL_TILE)], preferred_element_type=F32)
        expand = jnp.where(e_row == kb * blocks_per_tile + e_col, 1.0, 0.0).astype(BF16)
        chosen = jnp.dot(sel_b, expand, preferred_element_type=F32)
        chosen = jnp.concatenate([chosen] * GROUP, axis=0)
        mask = jnp.where(k0 + k_lane <= t4, chosen, 0.0) > 0.5
        s_ = jnp.where(mask, s_, NEG)
        m_new = jnp.maximum(m, jnp.max(s_, axis=1, keepdims=True))
        alpha = jnp.exp(m - m_new)
        p = jnp.where(mask, jnp.exp(s_ - m_new), 0.0)
        l = alpha * l + jnp.sum(p, axis=1, keepdims=True)
        pv = jnp.dot(p.astype(BF16), vs_ref[0, 0, pl.ds(k0, SEL_TILE), :], preferred_element_type=F32)
        return m_new, l, alpha * acc + pv

    n_tiles = (s0 + qb - 1) // SEL_TILE + 1
    init = (jnp.full((rows, 1), NEG, F32), jnp.zeros((rows, 1), F32), jnp.zeros((rows, HEAD_DIM), F32))
    _, l_s, acc_s = lax.fori_loop(0, n_tiles, sel_step, init)
    o_s = acc_s / jnp.maximum(l_s, 1e-30)

    w0 = pl.multiple_of(jnp.maximum(s0 - WINDOW, 0), LANES)
    s_w = jnp.dot(qs, kwt_ref[0, 0, :, pl.ds(w0, WIN_KEYS)], preferred_element_type=F32)
    wpos = w0 + lax.broadcasted_iota(jnp.int32, (1, WIN_KEYS), 1)
    p_w = _masked_softmax(s_w, (wpos <= t4) & (wpos > t4 - WINDOW))
    o_w = jnp.dot(p_w.astype(BF16), vw_ref[0, 0, pl.ds(w0, WIN_KEYS), :], preferred_element_type=F32)

    gates = _sigmoid(gate_ref[...])
    outs = []
    for r in range(GROUP):
        sl = slice(r * qb, (r + 1) * qb)
        outs.append(gates[:, 3 * r : 3 * r + 1] * o_c[sl] + gates[:, 3 * r + 1 : 3 * r + 2] * o_s[sl]
                    + gates[:, 3 * r + 2 : 3 * r + 3] * o_w[sl])
    o_ref[...] = jnp.concatenate(outs, axis=1).astype(o_ref.dtype)


def nsa_attention(proj, cos_t, sin_t, kct, vc, kst, vs, kwt, vw, overlap, batch, seq):
    nq = seq // Q_BLOCK
    nc = kct.shape[3]
    gw = GROUP * HEAD_DIM
    per_bg = lambda shape: pl.BlockSpec((1, 1) + shape, lambda b, g, i: (b, g, 0, 0))
    return pl.pallas_call(
        _nsa_kernel,
        grid=(batch, N_KV, nq),
        in_specs=[
            pl.BlockSpec((Q_BLOCK, gw), lambda b, g, i: (b * nq + i, COL_Q // gw + g)),
            pl.BlockSpec((Q_BLOCK, LANES), lambda b, g, i: (b * nq + i, COL_GN // LANES + g)),
            pl.BlockSpec((Q_BLOCK, LANES), lambda b, g, i: (b * nq + i, 0)),
            pl.BlockSpec((Q_BLOCK, LANES), lambda b, g, i: (b * nq + i, 0)),
            per_bg((HEAD_DIM, nc)), per_bg((nc, HEAD_DIM)),
            per_bg((HEAD_DIM, seq)), per_bg((seq, HEAD_DIM)),
            per_bg((HEAD_DIM, seq)), per_bg((seq, HEAD_DIM)),
            pl.BlockSpec((nc, LANES), lambda b, g, i: (0, 0)),
        ],
        out_specs=pl.BlockSpec((Q_BLOCK, gw), lambda b, g, i: (b * nq + i, g)),
        out_shape=jax.ShapeDtypeStruct((batch * seq, N_HEADS * HEAD_DIM), BF16),
        compiler_params=_cparams(("arbitrary", "arbitrary", "arbitrary")),
        name="nsa_attention",
    )(proj, proj, cos_t, sin_t, kct, vc, kst, vs, kwt, vw, overlap)


def _mem_kernel(q_ref, kv_ref, o_ref):
    q = q_ref[...]
    kv = kv_ref[0]
    mem_w = MEM_HEADS * MEM_HEAD_DIM
    outs = []
    for h in range(MEM_HEADS):
        sl = slice(h * MEM_HEAD_DIM, (h + 1) * MEM_HEAD_DIM)
        qh = (q[:, sl] * (MEM_HEAD_DIM ** -0.5)).astype(BF16)
        s = lax.dot_general(qh, kv[:, sl], (((1,), (1,)), ((), ())), preferred_element_type=F32)
        p = jnp.exp(s - jnp.max(s, axis=-1, keepdims=True))
        p = p / jnp.sum(p, axis=-1, keepdims=True)
        outs.append(jnp.dot(p.astype(BF16), kv[:, mem_w + h * MEM_HEAD_DIM : mem_w + (h + 1) * MEM_HEAD_DIM],
                            preferred_element_type=F32))
    o_ref[...] = jnp.concatenate(outs, axis=1).astype(o_ref.dtype)


def memory_attention(proj, kv_mem, batch, seq, tm):
    nt = seq // tm
    mem_len = kv_mem.shape[1]
    return pl.pallas_call(
        _mem_kernel,
        grid=(batch, nt),
        in_specs=[
            pl.BlockSpec((tm, D_MODEL), lambda b, t: (b * nt + t, COL_QM // D_MODEL)),
            pl.BlockSpec((1, mem_len, 2 * D_MODEL), lambda b, t: (b, 0, 0)),
        ],
        out_specs=pl.BlockSpec((tm, D_MODEL), lambda b, t: (b * nt + t, 0)),
        out_shape=jax.ShapeDtypeStruct((batch * seq, D_MODEL), BF16),
        compiler_params=_cparams(("arbitrary", "arbitrary")),
        name="memory_attention",
    )(proj, kv_mem)


def _merge_kernel(oa_ref, ob_ref, oc_ref, ga_ref, gb_ref, gc_ref, wa_ref, wb_ref, wc_ref, wo_ref,
                  gpost_ref, x_ref, gnext_ref, xo_ref, ho_ref):
    merged = _sigmoid(ga_ref[...]) * jnp.dot(oa_ref[...], wa_ref[...], preferred_element_type=F32)
    merged += _sigmoid(gb_ref[...]) * jnp.dot(ob_ref[...], wb_ref[...], preferred_element_type=F32)
    merged += _sigmoid(gc_ref[...]) * jnp.dot(oc_ref[...], wc_ref[...], preferred_element_type=F32)
    y = jnp.dot(merged.astype(BF16), wo_ref[...], preferred_element_type=F32)
    x_new = x_ref[...] + _rms(y, gpost_ref[...])
    xo_ref[...] = x_new
    ho_ref[...] = _rms(x_new, gnext_ref[...]).astype(ho_ref.dtype)


def merge_out(oa, ob, oc, proj, wa, wb, wc, wo, g_post, x, g_next, tm):
    n, d = x.shape
    row = lambda col: pl.BlockSpec((tm, d), lambda i: (i, col))
    full = lambda shape: pl.BlockSpec(shape, lambda i: (0, 0))
    gm = COL_GM // d
    return pl.pallas_call(
        _merge_kernel,
        grid=(n // tm,),
        in_specs=[row(0), row(0), row(0), row(gm), row(gm + 1), row(gm + 2),
                  full((d, d)), full((d, d)), full((d, d)), full((d, d)), full((1, d)), row(0), full((1, d))],
        out_specs=[row(0), row(0)],
        out_shape=[jax.ShapeDtypeStruct((n, d), F32), jax.ShapeDtypeStruct((n, d), BF16)],
        compiler_params=_cparams(("arbitrary",)),
        name="merge_out",
    )(oa, ob, oc, proj, proj, proj, wa, wb, wc, wo, g_post.reshape(1, d), x, g_next.reshape(1, d))


def _mlp_kernel(h_ref, w1_ref, w2_ref, gpost_ref, x_ref, gnext_ref, xo_ref, ho_ref, acc_ref):
    k = pl.program_id(1)

    @pl.when(k == 0)
    def _():
        acc_ref[...] = jnp.zeros_like(acc_ref)

    u = jnp.maximum(jnp.dot(h_ref[...], w1_ref[...], preferred_element_type=F32), 0.0)
    acc_ref[...] += jnp.dot((u * u).astype(BF16), w2_ref[...], preferred_element_type=F32)

    @pl.when(k == pl.num_programs(1) - 1)
    def _():
        x_new = x_ref[...] + _rms(acc_ref[...], gpost_ref[...])
        xo_ref[...] = x_new
        ho_ref[...] = _rms(x_new, gnext_ref[...]).astype(ho_ref.dtype)


def mlp(h, w1, w2, g_post, x, g_next, tm, tf):
    n, d = x.shape
    ff = w1.shape[1]
    return pl.pallas_call(
        _mlp_kernel,
        grid=(n // tm, ff // tf),
        in_specs=[
            pl.BlockSpec((tm, d), lambda i, k: (i, 0)),
            pl.BlockSpec((d, tf), lambda i, k: (0, k)),
            pl.BlockSpec((tf, d), lambda i, k: (k, 0)),
            pl.BlockSpec((1, d), lambda i, k: (0, 0)),
            pl.BlockSpec((tm, d), lambda i, k: (i, 0)),
            pl.BlockSpec((1, d), lambda i, k: (0, 0)),
        ],
        out_specs=[pl.BlockSpec((tm, d), lambda i, k: (i, 0))] * 2,
        out_shape=[jax.ShapeDtypeStruct((n, d), F32), jax.ShapeDtypeStruct((n, d), BF16)],
        scratch_shapes=[pltpu.VMEM((tm, d), F32)],
        compiler_params=_cparams(("arbitrary", "arbitrary")),
        name="mlp",
    )(h, w1, w2, g_post.reshape(1, d), x, g_next.reshape(1, d))


def _pack_w_in(w):
    o_kv, o_gn, o_qm, o_gm = 3072, 4608, 4656, 5680
    per_group = GROUP * 3
    gn = w[:, o_gn:o_qm].reshape(w.shape[0], N_KV, per_group)
    gn = jnp.pad(gn, ((0, 0), (0, 0), (0, LANES - per_group))).reshape(w.shape[0], N_KV * LANES)
    packed = jnp.concatenate([w[:, :o_kv], w[:, o_qm:o_gm], w[:, o_gm:], w[:, o_kv:o_gn], gn], axis=1)
    assert packed.shape[1] == D_INP
    return packed.astype(BF16)


def _overlap_matrix(nc):
    c0 = np.arange(nc)[:, None] * CMP_STRIDE
    s0 = np.arange(LANES)[None, :] * SEL_LEN
    ov = np.clip(np.minimum(c0 + CMP_LEN, s0 + SEL_LEN) - np.maximum(c0, s0), 0, None).astype(np.float32) / CMP_LEN
    return jnp.asarray(ov)


def kernel(x, mem, positions, ln_mix_pre, w_in, conv_w, conv_b, lru_wr, lru_br, lru_wi, lru_bi, lru_lambda, cmp_pe, cmp_w1, cmp_b1, cmp_w2, ln_mem, w_mem_kv, w_br_rnn, w_br_nsa, w_br_mem, w_out, ln_mix_post, ln_mlp_pre, mlp_w1, mlp_w2, ln_mlp_post):
    batch, seq, d = x.shape
    depth = w_in.shape[0]
    mem_len = mem.shape[1]
    n = batch * seq
    nc = seq // CMP_STRIDE
    assert d == D_MODEL and seq % SEL_TILE == 0 and seq // SEL_LEN <= LANES and seq >= WIN_KEYS

    tm = min(512, seq)
    xf = x.reshape(n, d)
    memf = mem.reshape(batch * mem_len, d)
    cos_t, sin_t = rope_tables(positions.reshape(n), tm)
    pos_c = jnp.pad(positions[:, CMP_LEN - 1 :: CMP_STRIDE], ((0, 0), (0, 1)))
    cos_c, sin_c = rope_tables(pos_c.reshape(batch * nc), nc)
    cos_c, sin_c = cos_c.reshape(batch, nc, LANES), sin_c.reshape(batch, nc, LANES)
    overlap = _overlap_matrix(nc)

    h = rmsnorm_bf16(xf, ln_mix_pre[0], tm)
    for l in range(depth):
        proj = matmul(h, _pack_w_in(w_in[l]), tm, 1024, F32, "in_proj")

        o_a = rglru_branch(proj, batch, seq, conv_w[l], conv_b[l], lru_wr[l], lru_br[l], lru_wi[l], lru_bi[l],
                           lru_lambda[l], min(256, seq))

        ks, kw = rope_keys(proj, cos_t, sin_t, tm)
        to_t = lambda a: a.reshape(batch, seq, N_KV, HEAD_DIM).transpose(0, 2, 3, 1)
        to_v = lambda a: a.reshape(batch, seq, N_KV, HEAD_DIM).transpose(0, 2, 1, 3).astype(BF16)
        kvp = lambda j: proj[:, COL_KV + j * KV_W : COL_KV + (j + 1) * KV_W]
        chunks = lambda a: (a.reshape(batch, nc, CMP_STRIDE, N_KV, HEAD_DIM).transpose(0, 3, 1, 2, 4)
                            .reshape(batch * N_KV, nc, CMP_STRIDE * HEAD_DIM))
        pe = cmp_pe[l].reshape(2, 2, CMP_STRIDE * HEAD_DIM)
        w1c, b1c, w2c = cmp_w1[l].astype(BF16), cmp_b1[l].reshape(2, 1, -1), cmp_w2[l].astype(BF16)
        k_cmp = compress(chunks(kvp(0)), 0, pe, w1c, b1c, w2c, cos_c, sin_c, batch, True)
        v_cmp = compress(chunks(kvp(1)), 1, pe, w1c, b1c, w2c, cos_c, sin_c, batch, False)
        kct = k_cmp.reshape(batch, N_KV, nc, HEAD_DIM).transpose(0, 1, 3, 2)
        vc = v_cmp.reshape(batch, N_KV, nc, HEAD_DIM)
        o_b = nsa_attention(proj, cos_t, sin_t, kct, vc, to_t(ks), to_v(kvp(3)), to_t(kw), to_v(kvp(5)),
                            overlap, batch, seq)

        mem_h = rmsnorm_bf16(memf, ln_mem[l], mem_len)
        kv_mem = matmul(mem_h, w_mem_kv[l].astype(BF16), mem_len, 1024, BF16, "mem_kv")
        o_c = memory_attention(proj, kv_mem.reshape(batch, mem_len, 2 * D_MODEL), batch, seq, tm)

        xf, h = merge_out(o_a, o_b, o_c, proj, w_br_rnn[l].astype(BF16), w_br_nsa[l].astype(BF16),
                          w_br_mem[l].astype(BF16), w_out[l].astype(BF16), ln_mix_post[l], xf, ln_mlp_pre[l], 256)
        xf, h = mlp(h, mlp_w1[l].astype(BF16), mlp_w2[l].astype(BF16), ln_mlp_post[l], xf,
                    ln_mix_pre[(l + 1) % depth], tm, 1024)
    return xf.reshape(batch, seq, d)
```

```python
import functools

import jax
import jax.numpy as jnp
import numpy as np
from jax import lax
from jax.experimental import pallas as pl
from jax.experimental.pallas import tpu as pltpu

F32 = jnp.float32
BF16 = jnp.bfloat16

D_MODEL = 1024
LRU_BLOCKS = 8
LRU_BW = D_MODEL // LRU_BLOCKS
CONV_W = 4
LRU_C = 8.0
N_HEADS = 16
HEAD_DIM = 64
N_KV = 4
GROUP = N_HEADS // N_KV
KV_W = N_KV * HEAD_DIM
CMP_STRIDE = 16
CMP_LEN = 32
SEL_LEN = 64
SEL_SHIFT = 6
N_SELECT = 16
WINDOW = 512
Q_BLOCK = 128
MEM_HEADS = 4
MEM_HEAD_DIM = D_MODEL // MEM_HEADS
D_FF = 4 * D_MODEL
ROPE_THETA = 10000.0
EPS = 1e-6
NEG = -1e30
FORCE = 1e4

LANES = 128
SEL_TILE = 512
WIN_KEYS = WINDOW + Q_BLOCK
V_ROWS = 80
VMEM_LIMIT = 48 * 1024 * 1024

COL_XR, COL_YR, COL_Q, COL_QM, COL_GM, COL_KV, COL_GN = 0, 1024, 2048, 3072, 4096, 7168, 8704
D_INP = 9216


def _cparams(sem):
    return pltpu.CompilerParams(dimension_semantics=sem, vmem_limit_bytes=VMEM_LIMIT)


def _sigmoid(x):
    return 1.0 / (1.0 + jnp.exp(-x))


def _gelu_tanh(x):
    return 0.5 * x * (1.0 + jnp.tanh(0.7978845608028654 * (x + 0.044715 * (x * x * x))))


def _rms(x, g):
    return x * lax.rsqrt(jnp.mean(x * x, axis=-1, keepdims=True) + EPS) * g


def _masked_softmax(sc, mask):
    sc = jnp.where(mask, sc, NEG)
    m = jnp.max(sc, axis=-1, keepdims=True)
    p = jnp.where(mask, jnp.exp(sc - m), 0.0)
    return p / jnp.maximum(jnp.sum(p, axis=-1, keepdims=True), 1e-30)


def _rms_kernel(x_ref, g_ref, o_ref):
    o_ref[...] = _rms(x_ref[...], g_ref[...]).astype(o_ref.dtype)


def rmsnorm_bf16(x, g, tm):
    m, d = x.shape
    return pl.pallas_call(
        _rms_kernel,
        grid=(m // tm,),
        in_specs=[pl.BlockSpec((tm, d), lambda i: (i, 0)), pl.BlockSpec((1, d), lambda i: (0, 0))],
        out_specs=pl.BlockSpec((tm, d), lambda i: (i, 0)),
        out_shape=jax.ShapeDtypeStruct((m, d), BF16),
        compiler_params=_cparams(("arbitrary",)),
        name="rmsnorm",
    )(x, g.reshape(1, d))


def _mm_kernel(a_ref, w_ref, o_ref):
    o_ref[...] = jnp.dot(a_ref[...], w_ref[...], preferred_element_type=F32).astype(o_ref.dtype)


def matmul(a, w, tm, tn, out_dtype, name):
    m, k = a.shape
    n = w.shape[1]
    return pl.pallas_call(
        _mm_kernel,
        grid=(n // tn, m // tm),
        in_specs=[pl.BlockSpec((tm, k), lambda j, i: (i, 0)), pl.BlockSpec((k, tn), lambda j, i: (0, j))],
        out_specs=pl.BlockSpec((tm, tn), lambda j, i: (i, j)),
        out_shape=jax.ShapeDtypeStruct((m, n), out_dtype),
        compiler_params=_cparams(("arbitrary", "arbitrary")),
        name=name,
    )(a, w)


def _rope_table_kernel(pos_ref, inv_ref, cos_ref, sin_ref):
    ang = pos_ref[...].astype(F32) * inv_ref[...]
    lane = lax.broadcasted_iota(jnp.int32, ang.shape, 1)
    cos_ref[...] = jnp.cos(ang)
    sin_ref[...] = jnp.where((lane & (HEAD_DIM - 1)) < HEAD_DIM // 2, -1.0, 1.0) * jnp.sin(ang)


def rope_tables(pos_flat, tm):
    n = pos_flat.shape[0]
    half = HEAD_DIM // 2
    inv = ROPE_THETA ** (-jnp.arange(half, dtype=F32) * 2.0 / HEAD_DIM)
    inv_full = jnp.tile(inv, LANES // half).reshape(1, LANES)
    return pl.pallas_call(
        _rope_table_kernel,
        grid=(n // tm,),
        in_specs=[pl.BlockSpec((tm, 1), lambda i: (i, 0)), pl.BlockSpec((1, LANES), lambda i: (0, 0))],
        out_specs=[pl.BlockSpec((tm, LANES), lambda i: (i, 0))] * 2,
        out_shape=[jax.ShapeDtypeStruct((n, LANES), F32)] * 2,
        compiler_params=_cparams(("arbitrary",)),
        name="rope_tables",
    )(pos_flat.reshape(n, 1), inv_full)


def _rope_table_t_kernel(pos_ref, inv_ref, cos_ref, sin_ref):
    ang = inv_ref[...] * pos_ref[0].astype(F32)
    cos_ref[0] = jnp.cos(ang)
    sin_ref[0] = jnp.sin(ang)


def rope_tables_t(positions, tm):
    batch, seq = positions.shape
    half = HEAD_DIM // 2
    inv = (ROPE_THETA ** (-jnp.arange(half, dtype=F32) * 2.0 / HEAD_DIM)).reshape(half, 1)
    return pl.pallas_call(
        _rope_table_t_kernel,
        grid=(batch, seq // tm),
        in_specs=[pl.BlockSpec((1, 1, tm), lambda b, i: (b, 0, i)), pl.BlockSpec((half, 1), lambda b, i: (0, 0))],
        out_specs=[pl.BlockSpec((1, half, tm), lambda b, i: (b, 0, i))] * 2,
        out_shape=[jax.ShapeDtypeStruct((batch, half, seq), F32)] * 2,
        compiler_params=_cparams(("arbitrary", "arbitrary")),
        name="rope_tables_t",
    )(positions.reshape(batch, 1, seq), inv)


def _rope128(x, cos_t, sin_t):
    lane = lax.broadcasted_iota(jnp.int32, x.shape, 1)
    first = (lane & (HEAD_DIM - 1)) < HEAD_DIM // 2
    partner = jnp.where(first, pltpu.roll(x, LANES - HEAD_DIM // 2, 1), pltpu.roll(x, HEAD_DIM // 2, 1))
    return x * cos_t + partner * sin_t


def _rglru_kernel(xr_ref, yr_ref, cw_ref, cb_ref, wr_ref, br_ref, wi_ref, bi_ref, lam_ref, o_ref, h_sc, tail_sc):
    @pl.when(pl.program_id(1) == 0)
    def _():
        h_sc[...] = jnp.zeros_like(h_sc)
        tail_sc[...] = jnp.zeros_like(tail_sc)

    xr = xr_ref[...]
    t_len, d = xr.shape
    tail = tail_sc[...]
    row8 = lax.broadcasted_iota(jnp.int32, (8, d), 0)
    cw = cw_ref[...]
    xc = cb_ref[...] + xr * cw[CONV_W - 1 : CONV_W, :]
    for k in range(1, CONV_W):
        rolled = pltpu.roll(xr, k, 0)
        head = jnp.where(row8 < k, pltpu.roll(tail, k, 0), rolled[0:8])
        shifted = jnp.concatenate([head, rolled[8:]], axis=0)
        xc = xc + shifted * cw[CONV_W - 1 - k : CONV_W - k, :]
    tail_sc[...] = xr[t_len - 8 :]

    xcb = xc.astype(BF16)
    rl, il = [], []
    for n in range(LRU_BLOCKS):
        xb = xcb[:, n * LRU_BW : (n + 1) * LRU_BW]
        rl.append(jnp.dot(xb, wr_ref[n], preferred_element_type=F32))
        il.append(jnp.dot(xb, wi_ref[n], preferred_element_type=F32))
    r = _sigmoid(jnp.concatenate(rl, axis=1) + br_ref[...])
    ig = _sigmoid(jnp.concatenate(il, axis=1) + bi_ref[...])
    softplus_neg_lam = jnp.log1p(jnp.exp(-lam_ref[...]))
    log_a = (-LRU_C * softplus_neg_lam) * r
    a = jnp.exp(log_a)
    b = jnp.sqrt(1.0 - jnp.exp(2.0 * log_a)) * (ig * xc)

    row = lax.broadcasted_iota(jnp.int32, (t_len, d), 0)
    step = 1
    while step < t_len:
        keep = row >= step
        a_sh = jnp.where(keep, pltpu.roll(a, step, 0), 1.0)
        b_sh = jnp.where(keep, pltpu.roll(b, step, 0), 0.0)
        b = a * b_sh + b
        a = a * a_sh
        step *= 2
    h = b + a * h_sc[0:1, :]
    h_sc[...] = jnp.broadcast_to(h[t_len - 1 : t_len, :], h_sc.shape)
    o_ref[...] = (h * _gelu_tanh(yr_ref[...])).astype(o_ref.dtype)


def rglru_branch(proj, batch, seq, conv_w, conv_b, wr, br, wi, bi, lam, t_len):
    d = D_MODEL
    nt = seq // t_len
    vec = lambda v: v.reshape(1, d)
    full2 = lambda shape: pl.BlockSpec(shape, lambda b, t: (0,) * len(shape))
    return pl.pallas_call(
        _rglru_kernel,
        grid=(batch, nt),
        in_specs=[
            pl.BlockSpec((t_len, d), lambda b, t: (b * nt + t, COL_XR // d)),
            pl.BlockSpec((t_len, d), lambda b, t: (b * nt + t, COL_YR // d)),
            full2((CONV_W, d)), full2((1, d)),
            full2((LRU_BLOCKS, LRU_BW, LRU_BW)), full2((1, d)),
            full2((LRU_BLOCKS, LRU_BW, LRU_BW)), full2((1, d)),
            full2((1, d)),
        ],
        out_specs=pl.BlockSpec((t_len, d), lambda b, t: (b * nt + t, 0)),
        out_shape=jax.ShapeDtypeStruct((batch * seq, d), BF16),
        scratch_shapes=[pltpu.VMEM((8, d), F32), pltpu.VMEM((8, d), F32)],
        compiler_params=_cparams(("arbitrary", "arbitrary")),
        name="rglru",
    )(proj, proj, conv_w, vec(conv_b), wr.astype(BF16), vec(br), wi.astype(BF16), vec(bi), vec(lam))


def _krope_kernel(ks_ref, kw_ref, cos_ref, sin_ref, kso_ref, kwo_ref):
    cos_t, sin_t = cos_ref[...], sin_ref[...]
    for src, dst in ((ks_ref, kso_ref), (kw_ref, kwo_ref)):
        x = src[...]
        parts = [_rope128(x[:, c * LANES : (c + 1) * LANES], cos_t, sin_t) for c in range(KV_W // LANES)]
        dst[...] = jnp.concatenate(parts, axis=1).astype(dst.dtype)


def rope_keys(proj, cos_t, sin_t, tm):
    n = proj.shape[0]
    blk = lambda col: pl.BlockSpec((tm, KV_W), lambda i: (i, col // KV_W))
    return pl.pallas_call(
        _krope_kernel,
        grid=(n // tm,),
        in_specs=[blk(COL_KV + 2 * KV_W), blk(COL_KV + 4 * KV_W),
                  pl.BlockSpec((tm, LANES), lambda i: (i, 0)), pl.BlockSpec((tm, LANES), lambda i: (i, 0))],
        out_specs=[pl.BlockSpec((tm, KV_W), lambda i: (i, 0))] * 2,
        out_shape=[jax.ShapeDtypeStruct((n, KV_W), BF16)] * 2,
        compiler_params=_cparams(("arbitrary",)),
        name="rope_keys",
    )(proj, proj, cos_t, sin_t)


def _compress_kernel(x_ref, pe_ref, w1_ref, b1_ref, w2_ref, cos_ref, sin_ref, o_ref, *, rotary):
    x = x_ref[0]
    half = CMP_STRIDE * HEAD_DIM
    pe = pe_ref[0]
    w1 = w1_ref[0]
    u = jnp.dot((x + pe[0:1]).astype(BF16), w1[:half], preferred_element_type=F32)
    v = jnp.dot((x + pe[1:2]).astype(BF16), w1[half:], preferred_element_type=F32)
    hid = _gelu_tanh(u + pltpu.roll(v, v.shape[0] - 1, 0) + b1_ref[0])
    out = jnp.dot(hid.astype(BF16), w2_ref[0], preferred_element_type=F32)
    if rotary:
        hh = HEAD_DIM // 2
        partner = jnp.concatenate([out[:, hh:], out[:, :hh]], axis=1)
        out = out * cos_ref[0][:, :HEAD_DIM] + partner * sin_ref[0][:, :HEAD_DIM]
    o_ref[0] = out.astype(o_ref.dtype)


def compress(x_chunks, j, pe, w1, b1, w2, cos_c, sin_c, batch, rotary):
    bg, nc, width = x_chunks.shape
    return pl.pallas_call(
        functools.partial(_compress_kernel, rotary=rotary),
        grid=(bg,),
        in_specs=[
            pl.BlockSpec((1, nc, width), lambda i: (i, 0, 0)),
            pl.BlockSpec((1, 2, width), lambda i: (j, 0, 0)),
            pl.BlockSpec((1, 2 * width, w1.shape[2]), lambda i: (j, 0, 0)),
            pl.BlockSpec((1, 1, w1.shape[2]), lambda i: (j, 0, 0)),
            pl.BlockSpec((1, w2.shape[1], HEAD_DIM), lambda i: (j, 0, 0)),
            pl.BlockSpec((1, nc, LANES), lambda i: (i // N_KV, 0, 0)),
            pl.BlockSpec((1, nc, LANES), lambda i: (i // N_KV, 0, 0)),
        ],
        out_specs=pl.BlockSpec((1, nc, HEAD_DIM), lambda i: (i, 0, 0)),
        out_shape=jax.ShapeDtypeStruct((bg, nc, HEAD_DIM), BF16),
        compiler_params=_cparams(("arbitrary",)),
        name="compress_k" if rotary else "compress_v",
    )(x_chunks, pe, w1, b1, w2, cos_c, sin_c)


def _nsa_kernel(q_ref, gate_ref, cos_ref, sin_ref, kc_ref, vct_ref, ks_ref, vst_ref, kw_ref, vwt_ref, ovt_ref,
                o_ref, sel_sc, sa_sc, sb_sc):
    qb = Q_BLOCK
    s0 = pl.program_id(2) * qb
    nc = kc_ref.shape[2]
    hh = HEAD_DIM // 2
    cols = GROUP * qb

    qt = q_ref[...].T
    cos_t, sin_t = cos_ref[0], sin_ref[0]
    heads = []
    for r in range(GROUP):
        x1 = qt[r * HEAD_DIM : r * HEAD_DIM + hh]
        x2 = qt[r * HEAD_DIM + hh : (r + 1) * HEAD_DIM]
        heads.append(jnp.concatenate([x1 * cos_t - x2 * sin_t, x2 * cos_t + x1 * sin_t], axis=0))
    q_t = (jnp.concatenate(heads, axis=1) * HEAD_DIM ** -0.5).astype(BF16)

    tq = s0 + lax.broadcasted_iota(jnp.int32, (1, qb), 1)

    cmp_end = lax.broadcasted_iota(jnp.int32, (nc, qb), 0) * CMP_STRIDE + (CMP_LEN - 1)
    bias_c = jnp.where(cmp_end <= tq, 0.0, NEG)
    has_key = jnp.where(tq >= CMP_LEN - 1, 1.0, 0.0)
    sc = jnp.dot(kc_ref[0, 0], q_t, preferred_element_type=F32) + jnp.concatenate([bias_c] * GROUP, axis=1)
    p_c = jnp.exp(sc - jnp.max(sc, axis=0, keepdims=True))
    norm = jnp.concatenate([has_key] * GROUP, axis=1) / jnp.maximum(jnp.sum(p_c, axis=0, keepdims=True), 1e-30)
    p_c = p_c * norm
    o_c = jnp.dot(vct_ref[0, 0], p_c.astype(BF16), preferred_element_type=F32)
    p_sum = p_c[:, :qb]
    for r in range(1, GROUP):
        p_sum = p_sum + p_c[:, r * qb : (r + 1) * qb]

    imp = jnp.dot(ovt_ref[...], p_sum, preferred_element_type=F32, precision=lax.Precision.HIGHEST)
    blk = lax.broadcasted_iota(jnp.int32, (LANES, qb), 0)
    forced = (blk == 0) | (blk == (tq >> SEL_SHIFT))
    work = jnp.where(forced, FORCE, jnp.where(blk * SEL_LEN <= tq, imp, -FORCE))

    blk_f = blk.astype(F32)
    sel = jnp.zeros((LANES, qb), F32)
    for _ in range(N_SELECT):
        m = jnp.max(work, axis=0, keepdims=True)
        idx = jnp.min(jnp.where(work == m, blk_f, float(LANES)), axis=0, keepdims=True)
        hit = blk_f == idx
        sel = jnp.where(hit, 1.0, sel)
        work = jnp.where(hit, -3e38, work)
    sel_sc[...] = sel

    blocks_per_tile = SEL_TILE // SEL_LEN
    row_minus_lane = (lax.broadcasted_iota(jnp.int32, (SEL_TILE, qb), 0)
                      - lax.broadcasted_iota(jnp.int32, (SEL_TILE, qb), 1))
    n_pairs = (s0 + qb - 1) // (2 * SEL_TILE) + 1

    def scores(kb):
        k0 = pl.multiple_of(kb * SEL_TILE, SEL_TILE)
        picked = sel_sc[pl.ds(pl.multiple_of(kb * blocks_per_tile, blocks_per_tile), blocks_per_tile), :]
        picked = jnp.where(picked > 0.5, 0.0, NEG)
        picked = jnp.concatenate(
            [jnp.broadcast_to(picked[j : j + 1, :], (SEL_LEN, qb)) for j in range(blocks_per_tile)], axis=0)
        bias = jnp.where(row_minus_lane <= s0 - k0, picked, NEG)
        s_ = jnp.dot(ks_ref[0, 0, pl.ds(k0, SEL_TILE), :], q_t, preferred_element_type=F32)
        return s_ + jnp.concatenate([bias] * GROUP, axis=1)

    def absorb(kb, s_, m_run, acc):
        k0 = pl.multiple_of(kb * SEL_TILE, SEL_TILE)
        m_new = jnp.maximum(m_run, jnp.max(s_, axis=0, keepdims=True))
        p = jnp.exp(s_ - m_new).astype(BF16)
        pv = jnp.dot(vst_ref[0, 0, :, pl.ds(k0, SEL_TILE)], p, preferred_element_type=F32)
        return m_new, jnp.exp(m_run - m_new) * acc + pv

    sa_sc[...] = scores(0)

    def sel_step(j, carry):
        m_run, acc = carry
        sb_sc[...] = scores(2 * j + 1)
        m_run, acc = absorb(2 * j, sa_sc[...], m_run, acc)
        sa_sc[...] = scores(jnp.minimum(2 * j + 2, 2 * n_pairs - 2))
        return absorb(2 * j + 1, sb_sc[...], m_run, acc)

    init = (jnp.full((1, cols), NEG, F32), jnp.zeros((V_ROWS, cols), F32))
    _, acc_s = lax.fori_loop(0, n_pairs, sel_step, init)

    w0 = pl.multiple_of(jnp.maximum(s0 - WINDOW, 0), LANES)
    wpos = w0 + lax.broadcasted_iota(jnp.int32, (WIN_KEYS, qb), 0)
    bias_w = jnp.where((wpos <= tq) & (wpos > tq - WINDOW), 0.0, NEG)
    kw_tile = kw_ref[0, 0, pl.ds(w0, WIN_KEYS), :]
    vw_tile = vwt_ref[0, 0, :, pl.ds(w0, WIN_KEYS)]

    s_w = jnp.dot(kw_tile, q_t, preferred_element_type=F32) + jnp.concatenate([bias_w] * GROUP, axis=1)
    p_w = jnp.exp(s_w - jnp.max(s_w, axis=0, keepdims=True)).astype(BF16)
    acc_w = jnp.dot(vw_tile, p_w, preferred_element_type=F32)
    o_w = acc_w[:HEAD_DIM] / jnp.maximum(acc_w[HEAD_DIM : HEAD_DIM + 1], 1e-30)
    o_s = acc_s[:HEAD_DIM] / jnp.maximum(acc_s[HEAD_DIM : HEAD_DIM + 1], 1e-30)

    gates_t = _sigmoid(gate_ref[...]).T
    outs = []
    for r in range(GROUP):
        sl = slice(r * qb, (r + 1) * qb)
        outs.append(gates_t[3 * r : 3 * r + 1] * o_c[:, sl] + gates_t[3 * r + 1 : 3 * r + 2] * o_s[:, sl]
                    + gates_t[3 * r + 2 : 3 * r + 3] * o_w[:, sl])
    o_ref[...] = jnp.concatenate(outs, axis=0).T.astype(o_ref.dtype)


def nsa_attention(proj, cos_q, sin_q, kc, vct, ks, vst, kw, vwt, overlap_t, batch, seq):
    nq = seq // Q_BLOCK
    nc = kc.shape[2]
    gw = GROUP * HEAD_DIM
    per_bg = lambda shape: pl.BlockSpec((1, 1) + shape, lambda b, g, i: (b, g, 0, 0))
    return pl.pallas_call(
        _nsa_kernel,
        grid=(batch, N_KV, nq),
        in_specs=[
            pl.BlockSpec((Q_BLOCK, gw), lambda b, g, i: (b * nq + i, COL_Q // gw + g)),
            pl.BlockSpec((Q_BLOCK, LANES), lambda b, g, i: (b * nq + i, COL_GN // LANES + g)),
            pl.BlockSpec((1, HEAD_DIM // 2, Q_BLOCK), lambda b, g, i: (b, 0, i)),
            pl.BlockSpec((1, HEAD_DIM // 2, Q_BLOCK), lambda b, g, i: (b, 0, i)),
            per_bg((nc, HEAD_DIM)), per_bg((HEAD_DIM, nc)),
            per_bg((seq, HEAD_DIM)), per_bg((V_ROWS, seq)),
            per_bg((seq, HEAD_DIM)), per_bg((V_ROWS, seq)),
            pl.BlockSpec((LANES, nc), lambda b, g, i: (0, 0)),
        ],
        out_specs=pl.BlockSpec((Q_BLOCK, gw), lambda b, g, i: (b * nq + i, g)),
        out_shape=jax.ShapeDtypeStruct((batch * seq, N_HEADS * HEAD_DIM), BF16),
        scratch_shapes=[pltpu.VMEM((LANES, Q_BLOCK), F32), pltpu.VMEM((SEL_TILE, GROUP * Q_BLOCK), F32),
                        pltpu.VMEM((SEL_TILE, GROUP * Q_BLOCK), F32)],
        compiler_params=_cparams(("arbitrary", "arbitrary", "arbitrary")),
        name="nsa_attention",
    )(proj, proj, cos_q, sin_q, kc, vct, ks, vst, kw, vwt, overlap_t)


def _mem_kernel(q_ref, kv_ref, o_ref):
    q = q_ref[...]
    kv = kv_ref[0]
    mem_w = MEM_HEADS * MEM_HEAD_DIM
    outs = []
    for h in range(MEM_HEADS):
        sl = slice(h * MEM_HEAD_DIM, (h + 1) * MEM_HEAD_DIM)
        qh = (q[:, sl] * (MEM_HEAD_DIM ** -0.5)).astype(BF16)
        s = lax.dot_general(qh, kv[:, sl], (((1,), (1,)), ((), ())), preferred_element_type=F32)
        p = jnp.exp(s - jnp.max(s, axis=-1, keepdims=True))
        p = p / jnp.sum(p, axis=-1, keepdims=True)
        outs.append(jnp.dot(p.astype(BF16), kv[:, mem_w + h * MEM_HEAD_DIM : mem_w + (h + 1) * MEM_HEAD_DIM],
                            preferred_element_type=F32))
    o_ref[...] = jnp.concatenate(outs, axis=1).astype(o_ref.dtype)


def memory_attention(proj, kv_mem, batch, seq, tm):
    nt = seq // tm
    mem_len = kv_mem.shape[1]
    return pl.pallas_call(
        _mem_kernel,
        grid=(batch, nt),
        in_specs=[
            pl.BlockSpec((tm, D_MODEL), lambda b, t: (b * nt + t, COL_QM // D_MODEL)),
            pl.BlockSpec((1, mem_len, 2 * D_MODEL), lambda b, t: (b, 0, 0)),
        ],
        out_specs=pl.BlockSpec((tm, D_MODEL), lambda b, t: (b * nt + t, 0)),
        out_shape=jax.ShapeDtypeStruct((batch * seq, D_MODEL), BF16),
        compiler_params=_cparams(("arbitrary", "arbitrary")),
        name="memory_attention",
    )(proj, kv_mem)


def _merge_kernel(oa_ref, ob_ref, oc_ref, ga_ref, gb_ref, gc_ref, wa_ref, wb_ref, wc_ref, wo_ref,
                  gpost_ref, x_ref, gnext_ref, xo_ref, ho_ref):
    merged = _sigmoid(ga_ref[...]) * jnp.dot(oa_ref[...], wa_ref[...], preferred_element_type=F32)
    merged += _sigmoid(gb_ref[...]) * jnp.dot(ob_ref[...], wb_ref[...], preferred_element_type=F32)
    merged += _sigmoid(gc_ref[...]) * jnp.dot(oc_ref[...], wc_ref[...], preferred_element_type=F32)
    y = jnp.dot(merged.astype(BF16), wo_ref[...], preferred_element_type=F32)
    x_new = x_ref[...] + _rms(y, gpost_ref[...])
    xo_ref[...] = x_new
    ho_ref[...] = _rms(x_new, gnext_ref[...]).astype(ho_ref.dtype)


def merge_out(oa, ob, oc, proj, wa, wb, wc, wo, g_post, x, g_next, tm):
    n, d = x.shape
    row = lambda col: pl.BlockSpec((tm, d), lambda i: (i, col))
    full = lambda shape: pl.BlockSpec(shape, lambda i: (0, 0))
    gm = COL_GM // d
    return pl.pallas_call(
        _merge_kernel,
        grid=(n // tm,),
        in_specs=[row(0), row(0), row(0), row(gm), row(gm + 1), row(gm + 2),
                  full((d, d)), full((d, d)), full((d, d)), full((d, d)), full((1, d)), row(0), full((1, d))],
        out_specs=[row(0), row(0)],
        out_shape=[jax.ShapeDtypeStruct((n, d), F32), jax.ShapeDtypeStruct((n, d), BF16)],
        compiler_params=_cparams(("arbitrary",)),
        name="merge_out",
    )(oa, ob, oc, proj, proj, proj, wa, wb, wc, wo, g_post.reshape(1, d), x, g_next.reshape(1, d))


def _mlp_kernel(h_ref, w1_ref, w2_ref, gpost_ref, x_ref, gnext_ref, xo_ref, ho_ref, acc_ref):
    k = pl.program_id(1)

    @pl.when(k == 0)
    def _():
        acc_ref[...] = jnp.zeros_like(acc_ref)

    u = jnp.maximum(jnp.dot(h_ref[...], w1_ref[...], preferred_element_type=F32), 0.0)
    acc_ref[...] += jnp.dot((u * u).astype(BF16), w2_ref[...], preferred_element_type=F32)

    @pl.when(k == pl.num_programs(1) - 1)
    def _():
        x_new = x_ref[...] + _rms(acc_ref[...], gpost_ref[...])
        xo_ref[...] = x_new
        ho_ref[...] = _rms(x_new, gnext_ref[...]).astype(ho_ref.dtype)


def mlp(h, w1, w2, g_post, x, g_next, tm, tf):
    n, d = x.shape
    ff = w1.shape[1]
    return pl.pallas_call(
        _mlp_kernel,
        grid=(n // tm, ff // tf),
        in_specs=[
            pl.BlockSpec((tm, d), lambda i, k: (i, 0)),
            pl.BlockSpec((d, tf), lambda i, k: (0, k)),
            pl.BlockSpec((tf, d), lambda i, k: (k, 0)),
            pl.BlockSpec((1, d), lambda i, k: (0, 0)),
            pl.BlockSpec((tm, d), lambda i, k: (i, 0)),
            pl.BlockSpec((1, d), lambda i, k: (0, 0)),
        ],
        out_specs=[pl.BlockSpec((tm, d), lambda i, k: (i, 0))] * 2,
        out_shape=[jax.ShapeDtypeStruct((n, d), F32), jax.ShapeDtypeStruct((n, d), BF16)],
        scratch_shapes=[pltpu.VMEM((tm, d), F32)],
        compiler_params=_cparams(("arbitrary", "arbitrary")),
        name="mlp",
    )(h, w1, w2, g_post.reshape(1, d), x, g_next.reshape(1, d))


def _pack_w_in(w):
    o_kv, o_gn, o_qm, o_gm = 3072, 4608, 4656, 5680
    per_group = GROUP * 3
    gn = w[:, o_gn:o_qm].reshape(w.shape[0], N_KV, per_group)
    gn = jnp.pad(gn, ((0, 0), (0, 0), (0, LANES - per_group))).reshape(w.shape[0], N_KV * LANES)
    packed = jnp.concatenate([w[:, :o_kv], w[:, o_qm:o_gm], w[:, o_gm:], w[:, o_kv:o_gn], gn], axis=1)
    assert packed.shape[1] == D_INP
    return packed.astype(BF16)


def _overlap_matrix_t(nc):
    c0 = np.arange(nc)[:, None] * CMP_STRIDE
    s0 = np.arange(LANES)[None, :] * SEL_LEN
    ov = np.clip(np.minimum(c0 + CMP_LEN, s0 + SEL_LEN) - np.maximum(c0, s0), 0, None).astype(np.float32) / CMP_LEN
    return jnp.asarray(ov.T)


def kernel(x, mem, positions, ln_mix_pre, w_in, conv_w, conv_b, lru_wr, lru_br, lru_wi, lru_bi, lru_lambda, cmp_pe, cmp_w1, cmp_b1, cmp_w2, ln_mem, w_mem_kv, w_br_rnn, w_br_nsa, w_br_mem, w_out, ln_mix_post, ln_mlp_pre, mlp_w1, mlp_w2, ln_mlp_post):
    batch, seq, d = x.shape
    depth = w_in.shape[0]
    mem_len = mem.shape[1]
    n = batch * seq
    nc = seq // CMP_STRIDE
    assert d == D_MODEL and seq % (2 * SEL_TILE) == 0 and seq // SEL_LEN <= LANES and seq >= WIN_KEYS

    tm = min(512, seq)
    xf = x.reshape(n, d)
    memf = mem.reshape(batch * mem_len, d)
    cos_t, sin_t = rope_tables(positions.reshape(n), tm)
    pos_c = jnp.pad(positions[:, CMP_LEN - 1 :: CMP_STRIDE], ((0, 0), (0, 1)))
    cos_c, sin_c = rope_tables(pos_c.reshape(batch * nc), nc)
    cos_c, sin_c = cos_c.reshape(batch, nc, LANES), sin_c.reshape(batch, nc, LANES)
    cos_q, sin_q = rope_tables_t(positions, tm)
    overlap_t = _overlap_matrix_t(nc)
    ones_rows = jnp.concatenate([jnp.ones((1, seq), BF16), jnp.zeros((V_ROWS - HEAD_DIM - 1, seq), BF16)], axis=0)
    ones_rows = jnp.broadcast_to(ones_rows, (batch, N_KV, V_ROWS - HEAD_DIM, seq))

    h = rmsnorm_bf16(xf, ln_mix_pre[0], tm)
    for l in range(depth):
        proj = matmul(h, _pack_w_in(w_in[l]), tm, 1024, F32, "in_proj")

        o_a = rglru_branch(proj, batch, seq, conv_w[l], conv_b[l], lru_wr[l], lru_br[l], lru_wi[l], lru_bi[l],
                           lru_lambda[l], min(256, seq))

        ks, kw = rope_keys(proj, cos_t, sin_t, tm)
        to_k = lambda a: a.reshape(batch, seq, N_KV, HEAD_DIM).transpose(0, 2, 1, 3)
        to_vt = lambda a: jnp.concatenate(
            [a.reshape(batch, seq, N_KV, HEAD_DIM).transpose(0, 2, 3, 1).astype(BF16), ones_rows], axis=2)
        kvp = lambda j: proj[:, COL_KV + j * KV_W : COL_KV + (j + 1) * KV_W]
        chunks = lambda a: (a.reshape(batch, nc, CMP_STRIDE, N_KV, HEAD_DIM).transpose(0, 3, 1, 2, 4)
                            .reshape(batch * N_KV, nc, CMP_STRIDE * HEAD_DIM))
        pe = cmp_pe[l].reshape(2, 2, CMP_STRIDE * HEAD_DIM)
        w1c, b1c, w2c = cmp_w1[l].astype(BF16), cmp_b1[l].reshape(2, 1, -1), cmp_w2[l].astype(BF16)
        k_cmp = compress(chunks(kvp(0)), 0, pe, w1c, b1c, w2c, cos_c, sin_c, batch, True)
        v_cmp = compress(chunks(kvp(1)), 1, pe, w1c, b1c, w2c, cos_c, sin_c, batch, False)
        kc = k_cmp.reshape(batch, N_KV, nc, HEAD_DIM)
        vct = v_cmp.reshape(batch, N_KV, nc, HEAD_DIM).transpose(0, 1, 3, 2)
        o_b = nsa_attention(proj, cos_q, sin_q, kc, vct, to_k(ks), to_vt(kvp(3)), to_k(kw), to_vt(kvp(5)),
                            overlap_t, batch, seq)

        mem_h = rmsnorm_bf16(memf, ln_mem[l], mem_len)
        kv_mem = matmul(mem_h, w_mem_kv[l].astype(BF16), mem_len, 1024, BF16, "mem_kv")
        o_c = memory_attention(proj, kv_mem.reshape(batch, mem_len, 2 * D_MODEL), batch, seq, tm)

        xf, h = merge_out(o_a, o_b, o_c, proj, w_br_rnn[l].astype(BF16), w_br_nsa[l].astype(BF16),
                          w_br_mem[l].astype(BF16), w_out[l].astype(BF16), ln_mix_post[l], xf, ln_mlp_pre[l], 256)
        xf, h = mlp(h, mlp_w1[l].astype(BF16), mlp_w2[l].astype(BF16), ln_mlp_post[l], xf,
                    ln_mix_pre[(l + 1) % depth], tm, 1024)
    return xf.reshape(batch, seq, d)
```

```python
import functools

import jax
import jax.numpy as jnp
import numpy as np
from jax import lax
from jax.experimental import pallas as pl
from jax.experimental.pallas import tpu as pltpu

F32 = jnp.float32
BF16 = jnp.bfloat16

D_MODEL = 1024
LRU_BLOCKS = 8
LRU_BW = D_MODEL // LRU_BLOCKS
CONV_W = 4
LRU_C = 8.0
N_HEADS = 16
HEAD_DIM = 64
N_KV = 4
GROUP = N_HEADS // N_KV
KV_W = N_KV * HEAD_DIM
CMP_STRIDE = 16
CMP_LEN = 32
SEL_LEN = 64
SEL_SHIFT = 6
N_SELECT = 16
WINDOW = 512
Q_BLOCK = 128
MEM_HEADS = 4
MEM_HEAD_DIM = D_MODEL // MEM_HEADS
D_FF = 4 * D_MODEL
ROPE_THETA = 10000.0
EPS = 1e-6
NEG = -1e30
FORCE = 1e4
LOG2_E = 1.4426950408889634

LANES = 128
SEL_TILE = 512
WIN_KEYS = WINDOW + Q_BLOCK
V_ROWS = 80
VMEM_LIMIT = 48 * 1024 * 1024

COL_XR, COL_YR, COL_Q, COL_QM, COL_GM, COL_KV, COL_GN = 0, 1024, 2048, 3072, 4096, 7168, 8704
D_INP = 9216


def _cparams(sem):
    return pltpu.CompilerParams(dimension_semantics=sem, vmem_limit_bytes=VMEM_LIMIT)


def _sigmoid(x):
    return 1.0 / (1.0 + jnp.exp(-x))


def _gelu_tanh(x):
    return 0.5 * x * (1.0 + jnp.tanh(0.7978845608028654 * (x + 0.044715 * (x * x * x))))


def _rms(x, g):
    return x * lax.rsqrt(jnp.mean(x * x, axis=-1, keepdims=True) + EPS) * g


def _masked_softmax(sc, mask):
    sc = jnp.where(mask, sc, NEG)
    m = jnp.max(sc, axis=-1, keepdims=True)
    p = jnp.where(mask, jnp.exp(sc - m), 0.0)
    return p / jnp.maximum(jnp.sum(p, axis=-1, keepdims=True), 1e-30)


def _rms_kernel(x_ref, g_ref, o_ref):
    o_ref[...] = _rms(x_ref[...], g_ref[...]).astype(o_ref.dtype)


def rmsnorm_bf16(x, g, tm):
    m, d = x.shape
    return pl.pallas_call(
        _rms_kernel,
        grid=(m // tm,),
        in_specs=[pl.BlockSpec((tm, d), lambda i: (i, 0)), pl.BlockSpec((1, d), lambda i: (0, 0))],
        out_specs=pl.BlockSpec((tm, d), lambda i: (i, 0)),
        out_shape=jax.ShapeDtypeStruct((m, d), BF16),
        compiler_params=_cparams(("arbitrary",)),
        name="rmsnorm",
    )(x, g.reshape(1, d))


def _mm_kernel(a_ref, w_ref, o_ref):
    o_ref[...] = jnp.dot(a_ref[...], w_ref[...], preferred_element_type=F32).astype(o_ref.dtype)


def matmul(a, w, tm, tn, out_dtype, name):
    m, k = a.shape
    n = w.shape[1]
    return pl.pallas_call(
        _mm_kernel,
        grid=(n // tn, m // tm),
        in_specs=[pl.BlockSpec((tm, k), lambda j, i: (i, 0)), pl.BlockSpec((k, tn), lambda j, i: (0, j))],
        out_specs=pl.BlockSpec((tm, tn), lambda j, i: (i, j)),
        out_shape=jax.ShapeDtypeStruct((m, n), out_dtype),
        compiler_params=_cparams(("arbitrary", "arbitrary")),
        name=name,
    )(a, w)


def _rope_table_kernel(pos_ref, inv_ref, cos_ref, sin_ref):
    ang = pos_ref[...].astype(F32) * inv_ref[...]
    lane = lax.broadcasted_iota(jnp.int32, ang.shape, 1)
    cos_ref[...] = jnp.cos(ang)
    sin_ref[...] = jnp.where((lane & (HEAD_DIM - 1)) < HEAD_DIM // 2, -1.0, 1.0) * jnp.sin(ang)


def rope_tables(pos_flat, tm):
    n = pos_flat.shape[0]
    half = HEAD_DIM // 2
    inv = ROPE_THETA ** (-jnp.arange(half, dtype=F32) * 2.0 / HEAD_DIM)
    inv_full = jnp.tile(inv, LANES // half).reshape(1, LANES)
    return pl.pallas_call(
        _rope_table_kernel,
        grid=(n // tm,),
        in_specs=[pl.BlockSpec((tm, 1), lambda i: (i, 0)), pl.BlockSpec((1, LANES), lambda i: (0, 0))],
        out_specs=[pl.BlockSpec((tm, LANES), lambda i: (i, 0))] * 2,
        out_shape=[jax.ShapeDtypeStruct((n, LANES), F32)] * 2,
        compiler_params=_cparams(("arbitrary",)),
        name="rope_tables",
    )(pos_flat.reshape(n, 1), inv_full)


def _rope_table_t_kernel(pos_ref, inv_ref, cos_ref, sin_ref):
    ang = inv_ref[...] * pos_ref[0].astype(F32)
    cos_ref[0] = jnp.cos(ang)
    sin_ref[0] = jnp.sin(ang)


def rope_tables_t(positions, tm):
    batch, seq = positions.shape
    half = HEAD_DIM // 2
    inv = (ROPE_THETA ** (-jnp.arange(half, dtype=F32) * 2.0 / HEAD_DIM)).reshape(half, 1)
    return pl.pallas_call(
        _rope_table_t_kernel,
        grid=(batch, seq // tm),
        in_specs=[pl.BlockSpec((1, 1, tm), lambda b, i: (b, 0, i)), pl.BlockSpec((half, 1), lambda b, i: (0, 0))],
        out_specs=[pl.BlockSpec((1, half, tm), lambda b, i: (b, 0, i))] * 2,
        out_shape=[jax.ShapeDtypeStruct((batch, half, seq), F32)] * 2,
        compiler_params=_cparams(("arbitrary", "arbitrary")),
        name="rope_tables_t",
    )(positions.reshape(batch, 1, seq), inv)


def _rope128(x, cos_t, sin_t):
    lane = lax.broadcasted_iota(jnp.int32, x.shape, 1)
    first = (lane & (HEAD_DIM - 1)) < HEAD_DIM // 2
    partner = jnp.where(first, pltpu.roll(x, LANES - HEAD_DIM // 2, 1), pltpu.roll(x, HEAD_DIM // 2, 1))
    return x * cos_t + partner * sin_t


def _rglru_kernel(xr_ref, yr_ref, cw_ref, cb_ref, wr_ref, br_ref, wi_ref, bi_ref, lam_ref, o_ref, h_sc, tail_sc):
    @pl.when(pl.program_id(1) == 0)
    def _():
        h_sc[...] = jnp.zeros_like(h_sc)
        tail_sc[...] = jnp.zeros_like(tail_sc)

    xr = xr_ref[...].astype(F32)
    t_len, d = xr.shape
    tail = tail_sc[...]
    row8 = lax.broadcasted_iota(jnp.int32, (8, d), 0)
    cw = cw_ref[...]
    xc = cb_ref[...] + xr * cw[CONV_W - 1 : CONV_W, :]
    for k in range(1, CONV_W):
        rolled = pltpu.roll(xr, k, 0)
        head = jnp.where(row8 < k, pltpu.roll(tail, k, 0), rolled[0:8])
        shifted = jnp.concatenate([head, rolled[8:]], axis=0)
        xc = xc + shifted * cw[CONV_W - 1 - k : CONV_W - k, :]
    tail_sc[...] = xr[t_len - 8 :]

    xcb = xc.astype(BF16)
    rl, il = [], []
    for n in range(LRU_BLOCKS):
        xb = xcb[:, n * LRU_BW : (n + 1) * LRU_BW]
        rl.append(jnp.dot(xb, wr_ref[n], preferred_element_type=F32))
        il.append(jnp.dot(xb, wi_ref[n], preferred_element_type=F32))
    r = _sigmoid(jnp.concatenate(rl, axis=1) + br_ref[...])
    ig = _sigmoid(jnp.concatenate(il, axis=1) + bi_ref[...])
    softplus_neg_lam = jnp.log1p(jnp.exp(-lam_ref[...]))
    log_a = (-LRU_C * softplus_neg_lam) * r
    a = jnp.exp(log_a)
    b = jnp.sqrt(1.0 - jnp.exp(2.0 * log_a)) * (ig * xc)

    row = lax.broadcasted_iota(jnp.int32, (t_len, d), 0)
    step = 1
    while step < t_len:
        keep = row >= step
        a_sh = jnp.where(keep, pltpu.roll(a, step, 0), 1.0)
        b_sh = jnp.where(keep, pltpu.roll(b, step, 0), 0.0)
        b = a * b_sh + b
        a = a * a_sh
        step *= 2
    h = b + a * h_sc[0:1, :]
    h_sc[...] = jnp.broadcast_to(h[t_len - 1 : t_len, :], h_sc.shape)
    o_ref[...] = (h * _gelu_tanh(yr_ref[...].astype(F32))).astype(o_ref.dtype)


def rglru_branch(proj, batch, seq, conv_w, conv_b, wr, br, wi, bi, lam, t_len):
    d = D_MODEL
    nt = seq // t_len
    vec = lambda v: v.reshape(1, d)
    full2 = lambda shape: pl.BlockSpec(shape, lambda b, t: (0,) * len(shape))
    return pl.pallas_call(
        _rglru_kernel,
        grid=(batch, nt),
        in_specs=[
            pl.BlockSpec((t_len, d), lambda b, t: (b * nt + t, COL_XR // d)),
            pl.BlockSpec((t_len, d), lambda b, t: (b * nt + t, COL_YR // d)),
            full2((CONV_W, d)), full2((1, d)),
            full2((LRU_BLOCKS, LRU_BW, LRU_BW)), full2((1, d)),
            full2((LRU_BLOCKS, LRU_BW, LRU_BW)), full2((1, d)),
            full2((1, d)),
        ],
        out_specs=pl.BlockSpec((t_len, d), lambda b, t: (b * nt + t, 0)),
        out_shape=jax.ShapeDtypeStruct((batch * seq, d), BF16),
        scratch_shapes=[pltpu.VMEM((8, d), F32), pltpu.VMEM((8, d), F32)],
        compiler_params=_cparams(("arbitrary", "arbitrary")),
        name="rglru",
    )(proj, proj, conv_w, vec(conv_b), wr.astype(BF16), vec(br), wi.astype(BF16), vec(bi), vec(lam))


def _kv_prep_kernel(kvs_ref, kvw_ref, cos_ref, sin_ref, kso_ref, vso_ref, kwo_ref, vwo_ref):
    cos_t, sin_t = cos_ref[...], sin_ref[...]
    t_len = cos_t.shape[0]
    pad_row = lax.broadcasted_iota(jnp.int32, (V_ROWS - HEAD_DIM, t_len), 0)
    pad_rows = jnp.where(pad_row == 0, 1.0, 0.0).astype(BF16)
    for src, k_dst, v_dst in ((kvs_ref, kso_ref, vso_ref), (kvw_ref, kwo_ref, vwo_ref)):
        x = src[...].astype(F32)
        roped = [_rope128(x[:, c * LANES : (c + 1) * LANES], cos_t, sin_t) for c in range(KV_W // LANES)]
        v_t = x[:, KV_W:].T
        for g in range(N_KV):
            pair = roped[g * HEAD_DIM // LANES]
            lo = g * HEAD_DIM % LANES
            k_dst[0, g] = pair[:, lo : lo + HEAD_DIM].astype(BF16)
            v_dst[0, g, :HEAD_DIM, :] = v_t[g * HEAD_DIM : (g + 1) * HEAD_DIM].astype(BF16)
            v_dst[0, g, HEAD_DIM:, :] = pad_rows


def kv_prep(proj, cos_t, sin_t, batch, seq, tm):
    nt = seq // tm
    blk = lambda col: pl.BlockSpec((tm, 2 * KV_W), lambda b, t: (b * nt + t, col // (2 * KV_W)))
    tab = pl.BlockSpec((tm, LANES), lambda b, t: (b * nt + t, 0))
    k_spec = pl.BlockSpec((1, N_KV, tm, HEAD_DIM), lambda b, t: (b, 0, t, 0))
    v_spec = pl.BlockSpec((1, N_KV, V_ROWS, tm), lambda b, t: (b, 0, 0, t))
    k_shape = jax.ShapeDtypeStruct((batch, N_KV, seq, HEAD_DIM), BF16)
    v_shape = jax.ShapeDtypeStruct((batch, N_KV, V_ROWS, seq), BF16)
    return pl.pallas_call(
        _kv_prep_kernel,
        grid=(batch, nt),
        in_specs=[blk(COL_KV + 2 * KV_W), blk(COL_KV + 4 * KV_W), tab, tab],
        out_specs=[k_spec, v_spec, k_spec, v_spec],
        out_shape=[k_shape, v_shape, k_shape, v_shape],
        compiler_params=_cparams(("arbitrary", "arbitrary")),
        name="kv_prep",
    )(proj, proj, cos_t, sin_t)


def _compress_kernel(x_ref, pe_ref, w1_ref, b1_ref, w2_ref, cos_ref, sin_ref, o_ref, *, rotary):
    x = x_ref[0]
    half = CMP_STRIDE * HEAD_DIM
    pe = pe_ref[0]
    w1 = w1_ref[0]
    u = jnp.dot((x + pe[0:1]).astype(BF16), w1[:half], preferred_element_type=F32)
    v = jnp.dot((x + pe[1:2]).astype(BF16), w1[half:], preferred_element_type=F32)
    hid = _gelu_tanh(u + pltpu.roll(v, v.shape[0] - 1, 0) + b1_ref[0])
    out = jnp.dot(hid.astype(BF16), w2_ref[0], preferred_element_type=F32)
    if rotary:
        hh = HEAD_DIM // 2
        partner = jnp.concatenate([out[:, hh:], out[:, :hh]], axis=1)
        out = out * cos_ref[0][:, :HEAD_DIM] + partner * sin_ref[0][:, :HEAD_DIM]
    o_ref[0] = out.astype(o_ref.dtype)


def compress(x_chunks, j, pe, w1, b1, w2, cos_c, sin_c, batch, rotary):
    bg, nc, width = x_chunks.shape
    return pl.pallas_call(
        functools.partial(_compress_kernel, rotary=rotary),
        grid=(bg,),
        in_specs=[
            pl.BlockSpec((1, nc, width), lambda i: (i, 0, 0)),
            pl.BlockSpec((1, 2, width), lambda i: (j, 0, 0)),
            pl.BlockSpec((1, 2 * width, w1.shape[2]), lambda i: (j, 0, 0)),
            pl.BlockSpec((1, 1, w1.shape[2]), lambda i: (j, 0, 0)),
            pl.BlockSpec((1, w2.shape[1], HEAD_DIM), lambda i: (j, 0, 0)),
            pl.BlockSpec((1, nc, LANES), lambda i: (i // N_KV, 0, 0)),
            pl.BlockSpec((1, nc, LANES), lambda i: (i // N_KV, 0, 0)),
        ],
        out_specs=pl.BlockSpec((1, nc, HEAD_DIM), lambda i: (i, 0, 0)),
        out_shape=jax.ShapeDtypeStruct((bg, nc, HEAD_DIM), BF16),
        compiler_params=_cparams(("arbitrary",)),
        name="compress_k" if rotary else "compress_v",
    )(x_chunks, pe, w1, b1, w2, cos_c, sin_c)


def _nsa_kernel(q_ref, gate_ref, cos_ref, sin_ref, kc_ref, vct_ref, ks_ref, vst_ref, kw_ref, vwt_ref, ovt_ref,
                o_ref, sel_sc, sa_sc, sb_sc, oc_sc, imp_sc):
    qb = Q_BLOCK
    s0 = pl.program_id(2) * qb
    nc = kc_ref.shape[2]
    hh = HEAD_DIM // 2
    cols = GROUP * qb

    qt = q_ref[...].astype(F32).T
    cos_t, sin_t = cos_ref[0], sin_ref[0]
    heads = []
    for r in range(GROUP):
        x1 = qt[r * HEAD_DIM : r * HEAD_DIM + hh]
        x2 = qt[r * HEAD_DIM + hh : (r + 1) * HEAD_DIM]
        heads.append(jnp.concatenate([x1 * cos_t - x2 * sin_t, x2 * cos_t + x1 * sin_t], axis=0))
    q_t = (jnp.concatenate(heads, axis=1) * (HEAD_DIM ** -0.5 * LOG2_E)).astype(BF16)

    tq = s0 + lax.broadcasted_iota(jnp.int32, (1, qb), 1)

    def compressed(n_c):
        cmp_end = lax.broadcasted_iota(jnp.int32, (n_c, qb), 0) * CMP_STRIDE + (CMP_LEN - 1)
        bias_c = jnp.where(cmp_end <= tq, 0.0, NEG)
        has_key = jnp.where(tq >= CMP_LEN - 1, 1.0, 0.0)
        sc = jnp.dot(kc_ref[0, 0, :n_c, :], q_t, preferred_element_type=F32) + jnp.concatenate([bias_c] * GROUP, axis=1)
        p_c = jnp.exp2(sc - jnp.max(sc, axis=0, keepdims=True))
        norm = jnp.concatenate([has_key] * GROUP, axis=1) / jnp.maximum(jnp.sum(p_c, axis=0, keepdims=True), 1e-30)
        p_c = p_c * norm
        oc_sc[...] = jnp.dot(vct_ref[0, 0, :, :n_c], p_c.astype(BF16), preferred_element_type=F32)
        p_sum = p_c[:, :qb]
        for r in range(1, GROUP):
            p_sum = p_sum + p_c[:, r * qb : (r + 1) * qb]
        p_hi = p_sum.astype(BF16)
        p_lo = (p_sum - p_hi.astype(F32)).astype(BF16)
        ovt = ovt_ref[:, :n_c]
        imp_sc[...] = (jnp.dot(ovt, p_hi, preferred_element_type=F32)
                       + jnp.dot(ovt, p_lo, preferred_element_type=F32))

    last_cmp = (s0 + qb - CMP_LEN) // CMP_STRIDE
    for v in range(nc // LANES):
        pl.when(last_cmp // LANES == v)(functools.partial(compressed, (v + 1) * LANES))
    o_c = oc_sc[...]
    imp = imp_sc[...]

    blk = lax.broadcasted_iota(jnp.int32, (LANES, qb), 0)
    forced = (blk == 0) | (blk == (tq >> SEL_SHIFT))
    work = jnp.where(forced, FORCE, jnp.where(blk * SEL_LEN <= tq, imp, -FORCE))

    blk_f = blk.astype(F32)
    sel = jnp.zeros((LANES, qb), F32)
    for _ in range(N_SELECT):
        m = jnp.max(work, axis=0, keepdims=True)
        idx = jnp.min(jnp.where(work == m, blk_f, float(LANES)), axis=0, keepdims=True)
        hit = blk_f == idx
        sel = jnp.where(hit, 1.0, sel)
        work = jnp.where(hit, -3e38, work)
    sel_sc[...] = sel

    blocks_per_tile = SEL_TILE // SEL_LEN
    row_minus_lane = (lax.broadcasted_iota(jnp.int32, (SEL_TILE, qb), 0)
                      - lax.broadcasted_iota(jnp.int32, (SEL_TILE, qb), 1))
    n_pairs = (s0 + qb - 1) // (2 * SEL_TILE) + 1

    def scores(kb):
        k0 = pl.multiple_of(kb * SEL_TILE, SEL_TILE)
        picked = sel_sc[pl.ds(pl.multiple_of(kb * blocks_per_tile, blocks_per_tile), blocks_per_tile), :]
        picked = jnp.where(picked > 0.5, 0.0, NEG)
        picked = jnp.concatenate(
            [jnp.broadcast_to(picked[j : j + 1, :], (SEL_LEN, qb)) for j in range(blocks_per_tile)], axis=0)
        bias = jnp.where(row_minus_lane <= s0 - k0, picked, NEG)
        s_ = jnp.dot(ks_ref[0, 0, pl.ds(k0, SEL_TILE), :], q_t, preferred_element_type=F32)
        return s_ + jnp.concatenate([bias] * GROUP, axis=1)

    def absorb(kb, s_, m_run, acc):
        k0 = pl.multiple_of(kb * SEL_TILE, SEL_TILE)
        m_new = jnp.maximum(m_run, jnp.max(s_, axis=0, keepdims=True))
        p = jnp.exp2(s_ - m_new).astype(BF16)
        pv = jnp.dot(vst_ref[0, 0, :, pl.ds(k0, SEL_TILE)], p, preferred_element_type=F32)
        return m_new, jnp.exp2(m_run - m_new) * acc + pv

    sa_sc[...] = scores(0)

    def sel_step(j, carry):
        m_run, acc = carry
        sb_sc[...] = scores(2 * j + 1)
        m_run, acc = absorb(2 * j, sa_sc[...], m_run, acc)
        sa_sc[...] = scores(jnp.minimum(2 * j + 2, 2 * n_pairs - 2))
        return absorb(2 * j + 1, sb_sc[...], m_run, acc)

    init = (jnp.full((1, cols), NEG, F32), jnp.zeros((V_ROWS, cols), F32))
    _, acc_s = lax.fori_loop(0, n_pairs, sel_step, init)

    w0 = pl.multiple_of(jnp.maximum(s0 - WINDOW, 0), LANES)
    wpos = w0 + lax.broadcasted_iota(jnp.int32, (WIN_KEYS, qb), 0)
    bias_w = jnp.where((wpos <= tq) & (wpos > tq - WINDOW), 0.0, NEG)
    kw_tile = kw_ref[0, 0, pl.ds(w0, WIN_KEYS), :]
    vw_tile = vwt_ref[0, 0, :, pl.ds(w0, WIN_KEYS)]

    s_w = jnp.dot(kw_tile, q_t, preferred_element_type=F32) + jnp.concatenate([bias_w] * GROUP, axis=1)
    p_w = jnp.exp2(s_w - jnp.max(s_w, axis=0, keepdims=True)).astype(BF16)
    acc_w = jnp.dot(vw_tile, p_w, preferred_element_type=F32)
    o_w = acc_w[:HEAD_DIM] / jnp.maximum(acc_w[HEAD_DIM : HEAD_DIM + 1], 1e-30)
    o_s = acc_s[:HEAD_DIM] / jnp.maximum(acc_s[HEAD_DIM : HEAD_DIM + 1], 1e-30)

    gates_t = _sigmoid(gate_ref[...].astype(F32)).T
    outs = []
    for r in range(GROUP):
        sl = slice(r * qb, (r + 1) * qb)
        outs.append(gates_t[3 * r : 3 * r + 1] * o_c[:, sl] + gates_t[3 * r + 1 : 3 * r + 2] * o_s[:, sl]
                    + gates_t[3 * r + 2 : 3 * r + 3] * o_w[:, sl])
    o_ref[...] = jnp.concatenate(outs, axis=0).T.astype(o_ref.dtype)


def nsa_attention(proj, cos_q, sin_q, kc, vct, ks, vst, kw, vwt, overlap_t, batch, seq):
    nq = seq // Q_BLOCK
    nc = kc.shape[2]
    gw = GROUP * HEAD_DIM
    per_bg = lambda shape: pl.BlockSpec((1, 1) + shape, lambda b, g, i: (b, g, 0, 0))
    return pl.pallas_call(
        _nsa_kernel,
        grid=(batch, N_KV, nq),
        in_specs=[
            pl.BlockSpec((Q_BLOCK, gw), lambda b, g, i: (b * nq + i, COL_Q // gw + g)),
            pl.BlockSpec((Q_BLOCK, LANES), lambda b, g, i: (b * nq + i, COL_GN // LANES + g)),
            pl.BlockSpec((1, HEAD_DIM // 2, Q_BLOCK), lambda b, g, i: (b, 0, i)),
            pl.BlockSpec((1, HEAD_DIM // 2, Q_BLOCK), lambda b, g, i: (b, 0, i)),
            per_bg((nc, HEAD_DIM)), per_bg((HEAD_DIM, nc)),
            per_bg((seq, HEAD_DIM)), per_bg((V_ROWS, seq)),
            per_bg((seq, HEAD_DIM)), per_bg((V_ROWS, seq)),
            pl.BlockSpec((LANES, nc), lambda b, g, i: (0, 0)),
        ],
        out_specs=pl.BlockSpec((Q_BLOCK, gw), lambda b, g, i: (b * nq + i, g)),
        out_shape=jax.ShapeDtypeStruct((batch * seq, N_HEADS * HEAD_DIM), BF16),
        scratch_shapes=[pltpu.VMEM((LANES, Q_BLOCK), F32), pltpu.VMEM((SEL_TILE, GROUP * Q_BLOCK), F32),
                        pltpu.VMEM((SEL_TILE, GROUP * Q_BLOCK), F32), pltpu.VMEM((HEAD_DIM, GROUP * Q_BLOCK), F32),
                        pltpu.VMEM((LANES, Q_BLOCK), F32)],
        compiler_params=_cparams(("arbitrary", "arbitrary", "arbitrary")),
        name="nsa_attention",
    )(proj, proj, cos_q, sin_q, kc, vct, ks, vst, kw, vwt, overlap_t)


def _mem_kernel(q_ref, kv_ref, o_ref):
    q = q_ref[...]
    kv = kv_ref[0]
    mem_w = MEM_HEADS * MEM_HEAD_DIM
    outs = []
    for h in range(MEM_HEADS):
        sl = slice(h * MEM_HEAD_DIM, (h + 1) * MEM_HEAD_DIM)
        qh = (q[:, sl] * (MEM_HEAD_DIM ** -0.5)).astype(BF16)
        s = lax.dot_general(qh, kv[:, sl], (((1,), (1,)), ((), ())), preferred_element_type=F32)
        p = jnp.exp(s - jnp.max(s, axis=-1, keepdims=True))
        p = p / jnp.sum(p, axis=-1, keepdims=True)
        outs.append(jnp.dot(p.astype(BF16), kv[:, mem_w + h * MEM_HEAD_DIM : mem_w + (h + 1) * MEM_HEAD_DIM],
                            preferred_element_type=F32))
    o_ref[...] = jnp.concatenate(outs, axis=1).astype(o_ref.dtype)


def memory_attention(proj, kv_mem, batch, seq, tm):
    nt = seq // tm
    mem_len = kv_mem.shape[1]
    return pl.pallas_call(
        _mem_kernel,
        grid=(batch, nt),
        in_specs=[
            pl.BlockSpec((tm, D_MODEL), lambda b, t: (b * nt + t, COL_QM // D_MODEL)),
            pl.BlockSpec((1, mem_len, 2 * D_MODEL), lambda b, t: (b, 0, 0)),
        ],
        out_specs=pl.BlockSpec((tm, D_MODEL), lambda b, t: (b * nt + t, 0)),
        out_shape=jax.ShapeDtypeStruct((batch * seq, D_MODEL), BF16),
        compiler_params=_cparams(("arbitrary", "arbitrary")),
        name="memory_attention",
    )(proj, kv_mem)


def _merge_kernel(oa_ref, ob_ref, oc_ref, ga_ref, gb_ref, gc_ref, wa_ref, wb_ref, wc_ref, wo_ref,
                  gpost_ref, x_ref, gnext_ref, xo_ref, ho_ref):
    gate = lambda ref: _sigmoid(ref[...].astype(F32))
    merged = gate(ga_ref) * jnp.dot(oa_ref[...], wa_ref[...], preferred_element_type=F32)
    merged += gate(gb_ref) * jnp.dot(ob_ref[...], wb_ref[...], preferred_element_type=F32)
    merged += gate(gc_ref) * jnp.dot(oc_ref[...], wc_ref[...], preferred_element_type=F32)
    y = jnp.dot(merged.astype(BF16), wo_ref[...], preferred_element_type=F32)
    x_new = x_ref[...] + _rms(y, gpost_ref[...])
    xo_ref[...] = x_new
    ho_ref[...] = _rms(x_new, gnext_ref[...]).astype(ho_ref.dtype)


def merge_out(oa, ob, oc, proj, wa, wb, wc, wo, g_post, x, g_next, tm):
    n, d = x.shape
    row = lambda col: pl.BlockSpec((tm, d), lambda i: (i, col))
    full = lambda shape: pl.BlockSpec(shape, lambda i: (0, 0))
    gm = COL_GM // d
    return pl.pallas_call(
        _merge_kernel,
        grid=(n // tm,),
        in_specs=[row(0), row(0), row(0), row(gm), row(gm + 1), row(gm + 2),
                  full((d, d)), full((d, d)), full((d, d)), full((d, d)), full((1, d)), row(0), full((1, d))],
        out_specs=[row(0), row(0)],
        out_shape=[jax.ShapeDtypeStruct((n, d), F32), jax.ShapeDtypeStruct((n, d), BF16)],
        compiler_params=_cparams(("arbitrary",)),
        name="merge_out",
    )(oa, ob, oc, proj, proj, proj, wa, wb, wc, wo, g_post.reshape(1, d), x, g_next.reshape(1, d))


def _mlp_kernel(h_ref, w1_ref, w2_ref, gpost_ref, x_ref, gnext_ref, xo_ref, ho_ref, acc_ref):
    k = pl.program_id(1)

    @pl.when(k == 0)
    def _():
        acc_ref[...] = jnp.zeros_like(acc_ref)

    u = jnp.maximum(jnp.dot(h_ref[...], w1_ref[...], preferred_element_type=F32), 0.0)
    acc_ref[...] += jnp.dot((u * u).astype(BF16), w2_ref[...], preferred_element_type=F32)

    @pl.when(k == pl.num_programs(1) - 1)
    def _():
        x_new = x_ref[...] + _rms(acc_ref[...], gpost_ref[...])
        xo_ref[...] = x_new
        ho_ref[...] = _rms(x_new, gnext_ref[...]).astype(ho_ref.dtype)


def mlp(h, w1, w2, g_post, x, g_next, tm, tf):
    n, d = x.shape
    ff = w1.shape[1]
    return pl.pallas_call(
        _mlp_kernel,
        grid=(n // tm, ff // tf),
        in_specs=[
            pl.BlockSpec((tm, d), lambda i, k: (i, 0)),
            pl.BlockSpec((d, tf), lambda i, k: (0, k)),
            pl.BlockSpec((tf, d), lambda i, k: (k, 0)),
            pl.BlockSpec((1, d), lambda i, k: (0, 0)),
            pl.BlockSpec((tm, d), lambda i, k: (i, 0)),
            pl.BlockSpec((1, d), lambda i, k: (0, 0)),
        ],
        out_specs=[pl.BlockSpec((tm, d), lambda i, k: (i, 0))] * 2,
        out_shape=[jax.ShapeDtypeStruct((n, d), F32), jax.ShapeDtypeStruct((n, d), BF16)],
        scratch_shapes=[pltpu.VMEM((tm, d), F32)],
        compiler_params=_cparams(("arbitrary", "arbitrary")),
        name="mlp",
    )(h, w1, w2, g_post.reshape(1, d), x, g_next.reshape(1, d))


def _pack_w_in(w):
    o_kv, o_gn, o_qm, o_gm = 3072, 4608, 4656, 5680
    per_group = GROUP * 3
    gn = w[:, o_gn:o_qm].reshape(w.shape[0], N_KV, per_group)
    gn = jnp.pad(gn, ((0, 0), (0, 0), (0, LANES - per_group))).reshape(w.shape[0], N_KV * LANES)
    packed = jnp.concatenate([w[:, :o_kv], w[:, o_qm:o_gm], w[:, o_gm:], w[:, o_kv:o_gn], gn], axis=1)
    assert packed.shape[1] == D_INP
    return packed.astype(BF16)


def _overlap_matrix_t(nc):
    c0 = np.arange(nc)[:, None] * CMP_STRIDE
    s0 = np.arange(LANES)[None, :] * SEL_LEN
    ov = np.clip(np.minimum(c0 + CMP_LEN, s0 + SEL_LEN) - np.maximum(c0, s0), 0, None).astype(np.float32) / CMP_LEN
    return jnp.asarray(ov.T, dtype=BF16)


def kernel(x, mem, positions, ln_mix_pre, w_in, conv_w, conv_b, lru_wr, lru_br, lru_wi, lru_bi, lru_lambda, cmp_pe, cmp_w1, cmp_b1, cmp_w2, ln_mem, w_mem_kv, w_br_rnn, w_br_nsa, w_br_mem, w_out, ln_mix_post, ln_mlp_pre, mlp_w1, mlp_w2, ln_mlp_post):
    batch, seq, d = x.shape
    depth = w_in.shape[0]
    mem_len = mem.shape[1]
    n = batch * seq
    nc = seq // CMP_STRIDE
    assert d == D_MODEL and seq % (2 * SEL_TILE) == 0 and seq // SEL_LEN <= LANES and seq >= WIN_KEYS

    tm = min(512, seq)
    tm_big = min(1024, seq)
    xf = x.reshape(n, d)
    memf = mem.reshape(batch * mem_len, d)
    cos_t, sin_t = rope_tables(positions.reshape(n), tm)
    pos_c = jnp.pad(positions[:, CMP_LEN - 1 :: CMP_STRIDE], ((0, 0), (0, 1)))
    cos_c, sin_c = rope_tables(pos_c.reshape(batch * nc), nc)
    cos_c, sin_c = cos_c.reshape(batch, nc, LANES), sin_c.reshape(batch, nc, LANES)
    cos_q, sin_q = rope_tables_t(positions, tm)
    overlap_t = _overlap_matrix_t(nc)

    w_in_b, w_kv_b = w_in.astype(BF16), w_mem_kv.astype(BF16)
    wr_b, wi_b = lru_wr.astype(BF16), lru_wi.astype(BF16)
    w1c_b, w2c_b = cmp_w1.astype(BF16), cmp_w2.astype(BF16)
    wa_b, wb_b, wc_b, wo_b = (w.astype(BF16) for w in (w_br_rnn, w_br_nsa, w_br_mem, w_out))
    w1_b, w2_b = mlp_w1.astype(BF16), mlp_w2.astype(BF16)

    h = rmsnorm_bf16(xf, ln_mix_pre[0], tm)
    for l in range(depth):
        proj = matmul(h, _pack_w_in(w_in_b[l]), tm_big, 1024, BF16, "in_proj")

        o_a = rglru_branch(proj, batch, seq, conv_w[l], conv_b[l], wr_b[l], lru_br[l], wi_b[l], lru_bi[l],
                           lru_lambda[l], min(256, seq))

        ks, vst, kw, vwt = kv_prep(proj, cos_t, sin_t, batch, seq, tm)
        kvp = lambda j: proj[:, COL_KV + j * KV_W : COL_KV + (j + 1) * KV_W]
        chunks = lambda a: (a.reshape(batch, nc, CMP_STRIDE, N_KV, HEAD_DIM).transpose(0, 3, 1, 2, 4)
                            .reshape(batch * N_KV, nc, CMP_STRIDE * HEAD_DIM))
        pe = cmp_pe[l].reshape(2, 2, CMP_STRIDE * HEAD_DIM)
        b1c = cmp_b1[l].reshape(2, 1, -1)
        k_cmp = compress(chunks(kvp(0)), 0, pe, w1c_b[l], b1c, w2c_b[l], cos_c, sin_c, batch, True)
        v_cmp = compress(chunks(kvp(1)), 1, pe, w1c_b[l], b1c, w2c_b[l], cos_c, sin_c, batch, False)
        kc = k_cmp.reshape(batch, N_KV, nc, HEAD_DIM)
        vct = v_cmp.reshape(batch, N_KV, nc, HEAD_DIM).transpose(0, 1, 3, 2)
        o_b = nsa_attention(proj, cos_q, sin_q, kc, vct, ks, vst, kw, vwt, overlap_t, batch, seq)

        mem_h = rmsnorm_bf16(memf, ln_mem[l], mem_len)
        kv_mem = matmul(mem_h, w_kv_b[l], mem_len, 1024, BF16, "mem_kv")
        o_c = memory_attention(proj, kv_mem.reshape(batch, mem_len, 2 * D_MODEL), batch, seq, tm)

        xf, h = merge_out(o_a, o_b, o_c, proj, wa_b[l], wb_b[l], wc_b[l], wo_b[l], ln_mix_post[l], xf,
                          ln_mlp_pre[l], 256)
        xf, h = mlp(h, w1_b[l], w2_b[l], ln_mlp_post[l], xf, ln_mix_pre[(l + 1) % depth], tm_big, 1024)
    return xf.reshape(batch, seq, d)
```

```python
import functools

import jax
import jax.numpy as jnp
import numpy as np
from jax import lax
from jax.experimental import pallas as pl
from jax.experimental.pallas import tpu as pltpu

F32 = jnp.float32
BF16 = jnp.bfloat16

D_MODEL = 1024
LRU_BLOCKS = 8
LRU_BW = D_MODEL // LRU_BLOCKS
CONV_W = 4
LRU_C = 8.0
N_HEADS = 16
HEAD_DIM = 64
N_KV = 4
GROUP = N_HEADS // N_KV
KV_W = N_KV * HEAD_DIM
CMP_STRIDE = 16
CMP_LEN = 32
SEL_LEN = 64
SEL_SHIFT = 6
N_SELECT = 16
WINDOW = 512
Q_BLOCK = 256
MEM_HEADS = 4
MEM_HEAD_DIM = D_MODEL // MEM_HEADS
D_FF = 4 * D_MODEL
ROPE_THETA = 10000.0
EPS = 1e-6
NEG = -1e30
FORCE = 1e4
LOG2_E = 1.4426950408889634

LANES = 128
SEL_TILE = 512
WIN_KEYS = WINDOW + Q_BLOCK
V_ROWS = 80
VMEM_LIMIT = 48 * 1024 * 1024

COL_XR, COL_YR, COL_Q, COL_QM, COL_GM, COL_KV, COL_GN = 0, 1024, 2048, 3072, 4096, 7168, 8704
D_INP = 9216


def _cparams(sem):
    return pltpu.CompilerParams(dimension_semantics=sem, vmem_limit_bytes=VMEM_LIMIT)


def _sigmoid(x):
    return 1.0 / (1.0 + jnp.exp(-x))


def _gelu_tanh(x):
    return 0.5 * x * (1.0 + jnp.tanh(0.7978845608028654 * (x + 0.044715 * (x * x * x))))


def _rms(x, g):
    return x * lax.rsqrt(jnp.mean(x * x, axis=-1, keepdims=True) + EPS) * g


def _masked_softmax(sc, mask):
    sc = jnp.where(mask, sc, NEG)
    m = jnp.max(sc, axis=-1, keepdims=True)
    p = jnp.where(mask, jnp.exp(sc - m), 0.0)
    return p / jnp.maximum(jnp.sum(p, axis=-1, keepdims=True), 1e-30)


def _rms_kernel(x_ref, g_ref, o_ref):
    o_ref[...] = _rms(x_ref[...], g_ref[...]).astype(o_ref.dtype)


def rmsnorm_bf16(x, g, tm):
    m, d = x.shape
    return pl.pallas_call(
        _rms_kernel,
        grid=(m // tm,),
        in_specs=[pl.BlockSpec((tm, d), lambda i: (i, 0)), pl.BlockSpec((1, d), lambda i: (0, 0))],
        out_specs=pl.BlockSpec((tm, d), lambda i: (i, 0)),
        out_shape=jax.ShapeDtypeStruct((m, d), BF16),
        compiler_params=_cparams(("arbitrary",)),
        name="rmsnorm",
    )(x, g.reshape(1, d))


def _mm_kernel(a_ref, w_ref, o_ref):
    o_ref[...] = jnp.dot(a_ref[...], w_ref[...], preferred_element_type=F32).astype(o_ref.dtype)


def matmul(a, w, tm, tn, out_dtype, name):
    m, k = a.shape
    n = w.shape[1]
    return pl.pallas_call(
        _mm_kernel,
        grid=(n // tn, m // tm),
        in_specs=[pl.BlockSpec((tm, k), lambda j, i: (i, 0)), pl.BlockSpec((k, tn), lambda j, i: (0, j))],
        out_specs=pl.BlockSpec((tm, tn), lambda j, i: (i, j)),
        out_shape=jax.ShapeDtypeStruct((m, n), out_dtype),
        compiler_params=_cparams(("arbitrary", "arbitrary")),
        name=name,
    )(a, w)


def _rope_table_kernel(pos_ref, inv_ref, cos_ref, sin_ref):
    ang = pos_ref[...].astype(F32) * inv_ref[...]
    lane = lax.broadcasted_iota(jnp.int32, ang.shape, 1)
    cos_ref[...] = jnp.cos(ang)
    sin_ref[...] = jnp.where((lane & (HEAD_DIM - 1)) < HEAD_DIM // 2, -1.0, 1.0) * jnp.sin(ang)


def rope_tables(pos_flat, tm):
    n = pos_flat.shape[0]
    half = HEAD_DIM // 2
    inv = ROPE_THETA ** (-jnp.arange(half, dtype=F32) * 2.0 / HEAD_DIM)
    inv_full = jnp.tile(inv, LANES // half).reshape(1, LANES)
    return pl.pallas_call(
        _rope_table_kernel,
        grid=(n // tm,),
        in_specs=[pl.BlockSpec((tm, 1), lambda i: (i, 0)), pl.BlockSpec((1, LANES), lambda i: (0, 0))],
        out_specs=[pl.BlockSpec((tm, LANES), lambda i: (i, 0))] * 2,
        out_shape=[jax.ShapeDtypeStruct((n, LANES), F32)] * 2,
        compiler_params=_cparams(("arbitrary",)),
        name="rope_tables",
    )(pos_flat.reshape(n, 1), inv_full)


def _rope_table_t_kernel(pos_ref, inv_ref, cos_ref, sin_ref):
    ang = inv_ref[...] * pos_ref[0].astype(F32)
    cos_ref[0] = jnp.cos(ang)
    sin_ref[0] = jnp.sin(ang)


def rope_tables_t(positions, tm):
    batch, seq = positions.shape
    half = HEAD_DIM // 2
    inv = (ROPE_THETA ** (-jnp.arange(half, dtype=F32) * 2.0 / HEAD_DIM)).reshape(half, 1)
    return pl.pallas_call(
        _rope_table_t_kernel,
        grid=(batch, seq // tm),
        in_specs=[pl.BlockSpec((1, 1, tm), lambda b, i: (b, 0, i)), pl.BlockSpec((half, 1), lambda b, i: (0, 0))],
        out_specs=[pl.BlockSpec((1, half, tm), lambda b, i: (b, 0, i))] * 2,
        out_shape=[jax.ShapeDtypeStruct((batch, half, seq), F32)] * 2,
        compiler_params=_cparams(("arbitrary", "arbitrary")),
        name="rope_tables_t",
    )(positions.reshape(batch, 1, seq), inv)


def _rope128(x, cos_t, sin_t):
    lane = lax.broadcasted_iota(jnp.int32, x.shape, 1)
    first = (lane & (HEAD_DIM - 1)) < HEAD_DIM // 2
    partner = jnp.where(first, pltpu.roll(x, LANES - HEAD_DIM // 2, 1), pltpu.roll(x, HEAD_DIM // 2, 1))
    return x * cos_t + partner * sin_t


def _rglru_kernel(xr_ref, yr_ref, cw_ref, cb_ref, wr_ref, br_ref, wi_ref, bi_ref, lam_ref, o_ref, h_sc, tail_sc):
    @pl.when(pl.program_id(1) == 0)
    def _():
        h_sc[...] = jnp.zeros_like(h_sc)
        tail_sc[...] = jnp.zeros_like(tail_sc)

    xr = xr_ref[...].astype(F32)
    t_len, d = xr.shape
    tail = tail_sc[...]
    row8 = lax.broadcasted_iota(jnp.int32, (8, d), 0)
    cw = cw_ref[...]
    xc = cb_ref[...] + xr * cw[CONV_W - 1 : CONV_W, :]
    for k in range(1, CONV_W):
        rolled = pltpu.roll(xr, k, 0)
        head = jnp.where(row8 < k, pltpu.roll(tail, k, 0), rolled[0:8])
        shifted = jnp.concatenate([head, rolled[8:]], axis=0)
        xc = xc + shifted * cw[CONV_W - 1 - k : CONV_W - k, :]
    tail_sc[...] = xr[t_len - 8 :]

    xcb = xc.astype(BF16)
    rl, il = [], []
    for n in range(LRU_BLOCKS):
        xb = xcb[:, n * LRU_BW : (n + 1) * LRU_BW]
        rl.append(jnp.dot(xb, wr_ref[n], preferred_element_type=F32))
        il.append(jnp.dot(xb, wi_ref[n], preferred_element_type=F32))
    r = _sigmoid(jnp.concatenate(rl, axis=1) + br_ref[...])
    ig = _sigmoid(jnp.concatenate(il, axis=1) + bi_ref[...])
    softplus_neg_lam = jnp.log1p(jnp.exp(-lam_ref[...]))
    log_a = (-LRU_C * softplus_neg_lam) * r
    a = jnp.exp(log_a)
    b = jnp.sqrt(1.0 - jnp.exp(2.0 * log_a)) * (ig * xc)

    row = lax.broadcasted_iota(jnp.int32, (t_len, d), 0)
    step = 1
    while step < t_len:
        keep = row >= step
        a_sh = jnp.where(keep, pltpu.roll(a, step, 0), 1.0)
        b_sh = jnp.where(keep, pltpu.roll(b, step, 0), 0.0)
        b = a * b_sh + b
        a = a * a_sh
        step *= 2
    h = b + a * h_sc[0:1, :]
    h_sc[...] = jnp.broadcast_to(h[t_len - 1 : t_len, :], h_sc.shape)
    o_ref[...] = (h * _gelu_tanh(yr_ref[...].astype(F32))).astype(o_ref.dtype)


def rglru_branch(proj, batch, seq, conv_w, conv_b, wr, br, wi, bi, lam, t_len):
    d = D_MODEL
    nt = seq // t_len
    vec = lambda v: v.reshape(1, d)
    full2 = lambda shape: pl.BlockSpec(shape, lambda b, t: (0,) * len(shape))
    return pl.pallas_call(
        _rglru_kernel,
        grid=(batch, nt),
        in_specs=[
            pl.BlockSpec((t_len, d), lambda b, t: (b * nt + t, COL_XR // d)),
            pl.BlockSpec((t_len, d), lambda b, t: (b * nt + t, COL_YR // d)),
            full2((CONV_W, d)), full2((1, d)),
            full2((LRU_BLOCKS, LRU_BW, LRU_BW)), full2((1, d)),
            full2((LRU_BLOCKS, LRU_BW, LRU_BW)), full2((1, d)),
            full2((1, d)),
        ],
        out_specs=pl.BlockSpec((t_len, d), lambda b, t: (b * nt + t, 0)),
        out_shape=jax.ShapeDtypeStruct((batch * seq, d), BF16),
        scratch_shapes=[pltpu.VMEM((8, d), F32), pltpu.VMEM((8, d), F32)],
        compiler_params=_cparams(("arbitrary", "arbitrary")),
        name="rglru",
    )(proj, proj, conv_w, vec(conv_b), wr.astype(BF16), vec(br), wi.astype(BF16), vec(bi), vec(lam))


def _kv_prep_kernel(kvs_ref, kvw_ref, cos_ref, sin_ref, kso_ref, vso_ref, kwo_ref, vwo_ref):
    cos_t, sin_t = cos_ref[...], sin_ref[...]
    t_len = cos_t.shape[0]
    pad_row = lax.broadcasted_iota(jnp.int32, (V_ROWS - HEAD_DIM, t_len), 0)
    pad_rows = jnp.where(pad_row == 0, 1.0, 0.0).astype(BF16)
    for src, k_dst, v_dst in ((kvs_ref, kso_ref, vso_ref), (kvw_ref, kwo_ref, vwo_ref)):
        x = src[...].astype(F32)
        roped = [_rope128(x[:, c * LANES : (c + 1) * LANES], cos_t, sin_t) for c in range(KV_W // LANES)]
        v_t = x[:, KV_W:].T
        for g in range(N_KV):
            pair = roped[g * HEAD_DIM // LANES]
            lo = g * HEAD_DIM % LANES
            k_dst[0, g] = pair[:, lo : lo + HEAD_DIM].astype(BF16)
            v_dst[0, g, :HEAD_DIM, :] = v_t[g * HEAD_DIM : (g + 1) * HEAD_DIM].astype(BF16)
            v_dst[0, g, HEAD_DIM:, :] = pad_rows


def kv_prep(proj, cos_t, sin_t, batch, seq, tm):
    nt = seq // tm
    blk = lambda col: pl.BlockSpec((tm, 2 * KV_W), lambda b, t: (b * nt + t, col // (2 * KV_W)))
    tab = pl.BlockSpec((tm, LANES), lambda b, t: (b * nt + t, 0))
    k_spec = pl.BlockSpec((1, N_KV, tm, HEAD_DIM), lambda b, t: (b, 0, t, 0))
    v_spec = pl.BlockSpec((1, N_KV, V_ROWS, tm), lambda b, t: (b, 0, 0, t))
    k_shape = jax.ShapeDtypeStruct((batch, N_KV, seq, HEAD_DIM), BF16)
    v_shape = jax.ShapeDtypeStruct((batch, N_KV, V_ROWS, seq), BF16)
    return pl.pallas_call(
        _kv_prep_kernel,
        grid=(batch, nt),
        in_specs=[blk(COL_KV + 2 * KV_W), blk(COL_KV + 4 * KV_W), tab, tab],
        out_specs=[k_spec, v_spec, k_spec, v_spec],
        out_shape=[k_shape, v_shape, k_shape, v_shape],
        compiler_params=_cparams(("arbitrary", "arbitrary")),
        name="kv_prep",
    )(proj, proj, cos_t, sin_t)


def _compress_kernel(x_ref, pe_ref, w1_ref, b1_ref, w2_ref, cos_ref, sin_ref, o_ref, *, rotary):
    x = x_ref[0]
    half = CMP_STRIDE * HEAD_DIM
    pe = pe_ref[0]
    w1 = w1_ref[0]
    u = jnp.dot((x + pe[0:1]).astype(BF16), w1[:half], preferred_element_type=F32)
    v = jnp.dot((x + pe[1:2]).astype(BF16), w1[half:], preferred_element_type=F32)
    hid = _gelu_tanh(u + pltpu.roll(v, v.shape[0] - 1, 0) + b1_ref[0])
    out = jnp.dot(hid.astype(BF16), w2_ref[0], preferred_element_type=F32)
    if rotary:
        hh = HEAD_DIM // 2
        partner = jnp.concatenate([out[:, hh:], out[:, :hh]], axis=1)
        out = out * cos_ref[0][:, :HEAD_DIM] + partner * sin_ref[0][:, :HEAD_DIM]
    o_ref[0] = out.astype(o_ref.dtype)


def compress(x_chunks, j, pe, w1, b1, w2, cos_c, sin_c, batch, rotary):
    bg, nc, width = x_chunks.shape
    return pl.pallas_call(
        functools.partial(_compress_kernel, rotary=rotary),
        grid=(bg,),
        in_specs=[
            pl.BlockSpec((1, nc, width), lambda i: (i, 0, 0)),
            pl.BlockSpec((1, 2, width), lambda i: (j, 0, 0)),
            pl.BlockSpec((1, 2 * width, w1.shape[2]), lambda i: (j, 0, 0)),
            pl.BlockSpec((1, 1, w1.shape[2]), lambda i: (j, 0, 0)),
            pl.BlockSpec((1, w2.shape[1], HEAD_DIM), lambda i: (j, 0, 0)),
            pl.BlockSpec((1, nc, LANES), lambda i: (i // N_KV, 0, 0)),
            pl.BlockSpec((1, nc, LANES), lambda i: (i // N_KV, 0, 0)),
        ],
        out_specs=pl.BlockSpec((1, nc, HEAD_DIM), lambda i: (i, 0, 0)),
        out_shape=jax.ShapeDtypeStruct((bg, nc, HEAD_DIM), BF16),
        compiler_params=_cparams(("arbitrary",)),
        name="compress_k" if rotary else "compress_v",
    )(x_chunks, pe, w1, b1, w2, cos_c, sin_c)


def _nsa_kernel(q_ref, gate_ref, cos_ref, sin_ref, kc_ref, vct_ref, ks_ref, vst_ref, kw_ref, vwt_ref, ovt_ref,
                o_ref, sel_sc, sa_sc, sb_sc, oc_sc, imp_sc):
    qb = Q_BLOCK
    s0 = pl.program_id(2) * qb
    nc = kc_ref.shape[2]
    hh = HEAD_DIM // 2
    cols = GROUP * qb

    qt = q_ref[...].astype(F32).T
    cos_t, sin_t = cos_ref[0], sin_ref[0]
    heads = []
    for r in range(GROUP):
        x1 = qt[r * HEAD_DIM : r * HEAD_DIM + hh]
        x2 = qt[r * HEAD_DIM + hh : (r + 1) * HEAD_DIM]
        heads.append(jnp.concatenate([x1 * cos_t - x2 * sin_t, x2 * cos_t + x1 * sin_t], axis=0))
    q_t = (jnp.concatenate(heads, axis=1) * (HEAD_DIM ** -0.5 * LOG2_E)).astype(BF16)

    tq = s0 + lax.broadcasted_iota(jnp.int32, (1, qb), 1)

    def compressed(n_c):
        cmp_end = lax.broadcasted_iota(jnp.int32, (n_c, qb), 0) * CMP_STRIDE + (CMP_LEN - 1)
        bias_c = jnp.where(cmp_end <= tq, 0.0, NEG)
        has_key = jnp.where(tq >= CMP_LEN - 1, 1.0, 0.0)
        sc = jnp.dot(kc_ref[0, 0, :n_c, :], q_t, preferred_element_type=F32) + jnp.concatenate([bias_c] * GROUP, axis=1)
        p_c = jnp.exp2(sc - jnp.max(sc, axis=0, keepdims=True))
        norm = jnp.concatenate([has_key] * GROUP, axis=1) / jnp.maximum(jnp.sum(p_c, axis=0, keepdims=True), 1e-30)
        p_c = p_c * norm
        oc_sc[...] = jnp.dot(vct_ref[0, 0, :, :n_c], p_c.astype(BF16), preferred_element_type=F32)
        p_sum = p_c[:, :qb]
        for r in range(1, GROUP):
            p_sum = p_sum + p_c[:, r * qb : (r + 1) * qb]
        p_hi = p_sum.astype(BF16)
        p_lo = (p_sum - p_hi.astype(F32)).astype(BF16)
        ovt = ovt_ref[:, :n_c]
        imp_sc[...] = (jnp.dot(ovt, p_hi, preferred_element_type=F32)
                       + jnp.dot(ovt, p_lo, preferred_element_type=F32))

    last_cmp = (s0 + qb - CMP_LEN) // CMP_STRIDE
    for v in range(nc // LANES):
        pl.when(last_cmp // LANES == v)(functools.partial(compressed, (v + 1) * LANES))
    o_c = oc_sc[...]
    imp = imp_sc[...]

    w0 = pl.multiple_of(jnp.maximum(s0 - WINDOW, 0), LANES)
    wpos = w0 + lax.broadcasted_iota(jnp.int32, (WIN_KEYS, qb), 0)
    bias_w = jnp.where((wpos <= tq) & (wpos > tq - WINDOW), 0.0, NEG)
    kw_tile = kw_ref[0, 0, pl.ds(w0, WIN_KEYS), :]
    vw_tile = vwt_ref[0, 0, :, pl.ds(w0, WIN_KEYS)]
    s_w = jnp.dot(kw_tile, q_t, preferred_element_type=F32) + jnp.concatenate([bias_w] * GROUP, axis=1)
    p_w = jnp.exp2(s_w - jnp.max(s_w, axis=0, keepdims=True)).astype(BF16)
    acc_w = jnp.dot(vw_tile, p_w, preferred_element_type=F32)
    o_w = acc_w[:HEAD_DIM] / jnp.maximum(acc_w[HEAD_DIM : HEAD_DIM + 1], 1e-30)

    blk = lax.broadcasted_iota(jnp.int32, (LANES, qb), 0)
    forced = (blk == 0) | (blk == (tq >> SEL_SHIFT))
    work = jnp.where(forced, FORCE, jnp.where(blk * SEL_LEN <= tq, imp, -FORCE))

    blk_f = blk.astype(F32)
    sel = jnp.zeros((LANES, qb), F32)
    for _ in range(N_SELECT):
        m = jnp.max(work, axis=0, keepdims=True)
        idx = jnp.min(jnp.where(work == m, blk_f, float(LANES)), axis=0, keepdims=True)
        hit = blk_f == idx
        sel = jnp.where(hit, 1.0, sel)
        work = jnp.where(hit, -3e38, work)
    sel_sc[...] = sel

    blocks_per_tile = SEL_TILE // SEL_LEN
    row_minus_lane = (lax.broadcasted_iota(jnp.int32, (SEL_TILE, qb), 0)
                      - lax.broadcasted_iota(jnp.int32, (SEL_TILE, qb), 1))
    n_pairs = (s0 + qb - 1) // (2 * SEL_TILE) + 1

    def scores(kb):
        k0 = pl.multiple_of(kb * SEL_TILE, SEL_TILE)
        picked = sel_sc[pl.ds(pl.multiple_of(kb * blocks_per_tile, blocks_per_tile), blocks_per_tile), :]
        picked = jnp.where(picked > 0.5, 0.0, NEG)
        picked = jnp.concatenate(
            [jnp.broadcast_to(picked[j : j + 1, :], (SEL_LEN, qb)) for j in range(blocks_per_tile)], axis=0)
        bias = jnp.where(row_minus_lane <= s0 - k0, picked, NEG)
        s_ = jnp.dot(ks_ref[0, 0, pl.ds(k0, SEL_TILE), :], q_t, preferred_element_type=F32)
        return s_ + jnp.concatenate([bias] * GROUP, axis=1)

    def absorb(kb, s_, m_run, acc):
        k0 = pl.multiple_of(kb * SEL_TILE, SEL_TILE)
        m_new = jnp.maximum(m_run, jnp.max(s_, axis=0, keepdims=True))
        p = jnp.exp2(s_ - m_new).astype(BF16)
        pv = jnp.dot(vst_ref[0, 0, :, pl.ds(k0, SEL_TILE)], p, preferred_element_type=F32)
        return m_new, jnp.exp2(m_run - m_new) * acc + pv

    sa_sc[...] = scores(0)

    def sel_step(j, carry):
        m_run, acc = carry
        sb_sc[...] = scores(2 * j + 1)
        m_run, acc = absorb(2 * j, sa_sc[...], m_run, acc)
        sa_sc[...] = scores(jnp.minimum(2 * j + 2, 2 * n_pairs - 2))
        return absorb(2 * j + 1, sb_sc[...], m_run, acc)

    init = (jnp.full((1, cols), NEG, F32), jnp.zeros((V_ROWS, cols), F32))
    _, acc_s = lax.fori_loop(0, n_pairs, sel_step, init)

    o_s = acc_s[:HEAD_DIM] / jnp.maximum(acc_s[HEAD_DIM : HEAD_DIM + 1], 1e-30)

    gates_t = _sigmoid(gate_ref[...].astype(F32)).T
    outs = []
    for r in range(GROUP):
        sl = slice(r * qb, (r + 1) * qb)
        outs.append(gates_t[3 * r : 3 * r + 1] * o_c[:, sl] + gates_t[3 * r + 1 : 3 * r + 2] * o_s[:, sl]
                    + gates_t[3 * r + 2 : 3 * r + 3] * o_w[:, sl])
    o_ref[...] = jnp.concatenate(outs, axis=0).T.astype(o_ref.dtype)


def nsa_attention(proj, cos_q, sin_q, kc, vct, ks, vst, kw, vwt, overlap_t, batch, seq):
    nq = seq // Q_BLOCK
    nc = kc.shape[2]
    gw = GROUP * HEAD_DIM
    per_bg = lambda shape: pl.BlockSpec((1, 1) + shape, lambda b, g, i: (b, g, 0, 0))
    return pl.pallas_call(
        _nsa_kernel,
        grid=(batch, N_KV, nq),
        in_specs=[
            pl.BlockSpec((Q_BLOCK, gw), lambda b, g, i: (b * nq + i, COL_Q // gw + g)),
            pl.BlockSpec((Q_BLOCK, LANES), lambda b, g, i: (b * nq + i, COL_GN // LANES + g)),
            pl.BlockSpec((1, HEAD_DIM // 2, Q_BLOCK), lambda b, g, i: (b, 0, i)),
            pl.BlockSpec((1, HEAD_DIM // 2, Q_BLOCK), lambda b, g, i: (b, 0, i)),
            per_bg((nc, HEAD_DIM)), per_bg((HEAD_DIM, nc)),
            per_bg((seq, HEAD_DIM)), per_bg((V_ROWS, seq)),
            per_bg((seq, HEAD_DIM)), per_bg((V_ROWS, seq)),
            pl.BlockSpec((LANES, nc), lambda b, g, i: (0, 0)),
        ],
        out_specs=pl.BlockSpec((Q_BLOCK, gw), lambda b, g, i: (b * nq + i, g)),
        out_shape=jax.ShapeDtypeStruct((batch * seq, N_HEADS * HEAD_DIM), BF16),
        scratch_shapes=[pltpu.VMEM((LANES, Q_BLOCK), F32), pltpu.VMEM((SEL_TILE, GROUP * Q_BLOCK), F32),
                        pltpu.VMEM((SEL_TILE, GROUP * Q_BLOCK), F32), pltpu.VMEM((HEAD_DIM, GROUP * Q_BLOCK), F32),
                        pltpu.VMEM((LANES, Q_BLOCK), F32)],
        compiler_params=_cparams(("arbitrary", "arbitrary", "arbitrary")),
        name="nsa_attention",
    )(proj, proj, cos_q, sin_q, kc, vct, ks, vst, kw, vwt, overlap_t)


def _mem_kernel(q_ref, kv_ref, o_ref):
    q = q_ref[...]
    kv = kv_ref[0]
    mem_w = MEM_HEADS * MEM_HEAD_DIM
    outs = []
    for h in range(MEM_HEADS):
        sl = slice(h * MEM_HEAD_DIM, (h + 1) * MEM_HEAD_DIM)
        qh = (q[:, sl] * (MEM_HEAD_DIM ** -0.5)).astype(BF16)
        s = lax.dot_general(qh, kv[:, sl], (((1,), (1,)), ((), ())), preferred_element_type=F32)
        p = jnp.exp(s - jnp.max(s, axis=-1, keepdims=True))
        p = p / jnp.sum(p, axis=-1, keepdims=True)
        outs.append(jnp.dot(p.astype(BF16), kv[:, mem_w + h * MEM_HEAD_DIM : mem_w + (h + 1) * MEM_HEAD_DIM],
                            preferred_element_type=F32))
    o_ref[...] = jnp.concatenate(outs, axis=1).astype(o_ref.dtype)


def memory_attention(proj, kv_mem, batch, seq, tm):
    nt = seq // tm
    mem_len = kv_mem.shape[1]
    return pl.pallas_call(
        _mem_kernel,
        grid=(batch, nt),
        in_specs=[
            pl.BlockSpec((tm, D_MODEL), lambda b, t: (b * nt + t, COL_QM // D_MODEL)),
            pl.BlockSpec((1, mem_len, 2 * D_MODEL), lambda b, t: (b, 0, 0)),
        ],
        out_specs=pl.BlockSpec((tm, D_MODEL), lambda b, t: (b * nt + t, 0)),
        out_shape=jax.ShapeDtypeStruct((batch * seq, D_MODEL), BF16),
        compiler_params=_cparams(("arbitrary", "arbitrary")),
        name="memory_attention",
    )(proj, kv_mem)


def _merge_kernel(oa_ref, ob_ref, oc_ref, ga_ref, gb_ref, gc_ref, wa_ref, wb_ref, wc_ref, wo_ref,
                  gpost_ref, x_ref, gnext_ref, xo_ref, ho_ref):
    gate = lambda ref: _sigmoid(ref[...].astype(F32))
    merged = gate(ga_ref) * jnp.dot(oa_ref[...], wa_ref[...], preferred_element_type=F32)
    merged += gate(gb_ref) * jnp.dot(ob_ref[...], wb_ref[...], preferred_element_type=F32)
    merged += gate(gc_ref) * jnp.dot(oc_ref[...], wc_ref[...], preferred_element_type=F32)
    y = jnp.dot(merged.astype(BF16), wo_ref[...], preferred_element_type=F32)
    x_new = x_ref[...] + _rms(y, gpost_ref[...])
    xo_ref[...] = x_new
    ho_ref[...] = _rms(x_new, gnext_ref[...]).astype(ho_ref.dtype)


def merge_out(oa, ob, oc, proj, wa, wb, wc, wo, g_post, x, g_next, tm):
    n, d = x.shape
    row = lambda col: pl.BlockSpec((tm, d), lambda i: (i, col))
    full = lambda shape: pl.BlockSpec(shape, lambda i: (0, 0))
    gm = COL_GM // d
    return pl.pallas_call(
        _merge_kernel,
        grid=(n // tm,),
        in_specs=[row(0), row(0), row(0), row(gm), row(gm + 1), row(gm + 2),
                  full((d, d)), full((d, d)), full((d, d)), full((d, d)), full((1, d)), row(0), full((1, d))],
        out_specs=[row(0), row(0)],
        out_shape=[jax.ShapeDtypeStruct((n, d), F32), jax.ShapeDtypeStruct((n, d), BF16)],
        compiler_params=_cparams(("arbitrary",)),
        name="merge_out",
    )(oa, ob, oc, proj, proj, proj, wa, wb, wc, wo, g_post.reshape(1, d), x, g_next.reshape(1, d))


def _mlp_kernel(h_ref, w1_ref, w2_ref, gpost_ref, x_ref, gnext_ref, xo_ref, ho_ref, acc_ref):
    k = pl.program_id(1)

    @pl.when(k == 0)
    def _():
        acc_ref[...] = jnp.zeros_like(acc_ref)

    u = jnp.maximum(jnp.dot(h_ref[...], w1_ref[...], preferred_element_type=F32), 0.0)
    acc_ref[...] += jnp.dot((u * u).astype(BF16), w2_ref[...], preferred_element_type=F32)

    @pl.when(k == pl.num_programs(1) - 1)
    def _():
        x_new = x_ref[...] + _rms(acc_ref[...], gpost_ref[...])
        xo_ref[...] = x_new
        ho_ref[...] = _rms(x_new, gnext_ref[...]).astype(ho_ref.dtype)


def mlp(h, w1, w2, g_post, x, g_next, tm, tf):
    n, d = x.shape
    ff = w1.shape[1]
    return pl.pallas_call(
        _mlp_kernel,
        grid=(n // tm, ff // tf),
        in_specs=[
            pl.BlockSpec((tm, d), lambda i, k: (i, 0)),
            pl.BlockSpec((d, tf), lambda i, k: (0, k)),
            pl.BlockSpec((tf, d), lambda i, k: (k, 0)),
            pl.BlockSpec((1, d), lambda i, k: (0, 0)),
            pl.BlockSpec((tm, d), lambda i, k: (i, 0)),
            pl.BlockSpec((1, d), lambda i, k: (0, 0)),
        ],
        out_specs=[pl.BlockSpec((tm, d), lambda i, k: (i, 0))] * 2,
        out_shape=[jax.ShapeDtypeStruct((n, d), F32), jax.ShapeDtypeStruct((n, d), BF16)],
        scratch_shapes=[pltpu.VMEM((tm, d), F32)],
        compiler_params=_cparams(("arbitrary", "arbitrary")),
        name="mlp",
    )(h, w1, w2, g_post.reshape(1, d), x, g_next.reshape(1, d))


def _pack_w_in(w):
    o_kv, o_gn, o_qm, o_gm = 3072, 4608, 4656, 5680
    per_group = GROUP * 3
    gn = w[:, o_gn:o_qm].reshape(w.shape[0], N_KV, per_group)
    gn = jnp.pad(gn, ((0, 0), (0, 0), (0, LANES - per_group))).reshape(w.shape[0], N_KV * LANES)
    packed = jnp.concatenate([w[:, :o_kv], w[:, o_qm:o_gm], w[:, o_gm:], w[:, o_kv:o_gn], gn], axis=1)
    assert packed.shape[1] == D_INP
    return packed.astype(BF16)


def _overlap_matrix_t(nc):
    c0 = np.arange(nc)[:, None] * CMP_STRIDE
    s0 = np.arange(LANES)[None, :] * SEL_LEN
    ov = np.clip(np.minimum(c0 + CMP_LEN, s0 + SEL_LEN) - np.maximum(c0, s0), 0, None).astype(np.float32) / CMP_LEN
    return jnp.asarray(ov.T, dtype=BF16)


def kernel(x, mem, positions, ln_mix_pre, w_in, conv_w, conv_b, lru_wr, lru_br, lru_wi, lru_bi, lru_lambda, cmp_pe, cmp_w1, cmp_b1, cmp_w2, ln_mem, w_mem_kv, w_br_rnn, w_br_nsa, w_br_mem, w_out, ln_mix_post, ln_mlp_pre, mlp_w1, mlp_w2, ln_mlp_post):
    batch, seq, d = x.shape
    depth = w_in.shape[0]
    mem_len = mem.shape[1]
    n = batch * seq
    nc = seq // CMP_STRIDE
    assert d == D_MODEL and seq % (2 * SEL_TILE) == 0 and seq // SEL_LEN <= LANES and seq >= WIN_KEYS

    tm = min(512, seq)
    tm_big = min(1024, seq)
    xf = x.reshape(n, d)
    memf = mem.reshape(batch * mem_len, d)
    cos_t, sin_t = rope_tables(positions.reshape(n), tm)
    pos_c = jnp.pad(positions[:, CMP_LEN - 1 :: CMP_STRIDE], ((0, 0), (0, 1)))
    cos_c, sin_c = rope_tables(pos_c.reshape(batch * nc), nc)
    cos_c, sin_c = cos_c.reshape(batch, nc, LANES), sin_c.reshape(batch, nc, LANES)
    cos_q, sin_q = rope_tables_t(positions, tm)
    overlap_t = _overlap_matrix_t(nc)

    w_in_b, w_kv_b = w_in.astype(BF16), w_mem_kv.astype(BF16)
    wr_b, wi_b = lru_wr.astype(BF16), lru_wi.astype(BF16)
    w1c_b, w2c_b = cmp_w1.astype(BF16), cmp_w2.astype(BF16)
    wa_b, wb_b, wc_b, wo_b = (w.astype(BF16) for w in (w_br_rnn, w_br_nsa, w_br_mem, w_out))
    w1_b, w2_b = mlp_w1.astype(BF16), mlp_w2.astype(BF16)

    h = rmsnorm_bf16(xf, ln_mix_pre[0], tm)
    for l in range(depth):
        proj = matmul(h, _pack_w_in(w_in_b[l]), tm_big, 1024, BF16, "in_proj")

        o_a = rglru_branch(proj, batch, seq, conv_w[l], conv_b[l], wr_b[l], lru_br[l], wi_b[l], lru_bi[l],
                           lru_lambda[l], min(256, seq))

        ks, vst, kw, vwt = kv_prep(proj, cos_t, sin_t, batch, seq, tm)
        kvp = lambda j: proj[:, COL_KV + j * KV_W : COL_KV + (j + 1) * KV_W]
        chunks = lambda a: (a.reshape(batch, nc, CMP_STRIDE, N_KV, HEAD_DIM).transpose(0, 3, 1, 2, 4)
                            .reshape(batch * N_KV, nc, CMP_STRIDE * HEAD_DIM))
        pe = cmp_pe[l].reshape(2, 2, CMP_STRIDE * HEAD_DIM)
        b1c = cmp_b1[l].reshape(2, 1, -1)
        k_cmp = compress(chunks(kvp(0)), 0, pe, w1c_b[l], b1c, w2c_b[l], cos_c, sin_c, batch, True)
        v_cmp = compress(chunks(kvp(1)), 1, pe, w1c_b[l], b1c, w2c_b[l], cos_c, sin_c, batch, False)
        kc = k_cmp.reshape(batch, N_KV, nc, HEAD_DIM)
        vct = v_cmp.reshape(batch, N_KV, nc, HEAD_DIM).transpose(0, 1, 3, 2)
        o_b = nsa_attention(proj, cos_q, sin_q, kc, vct, ks, vst, kw, vwt, overlap_t, batch, seq)

        mem_h = rmsnorm_bf16(memf, ln_mem[l], mem_len)
        kv_mem = matmul(mem_h, w_kv_b[l], mem_len, 1024, BF16, "mem_kv")
        o_c = memory_attention(proj, kv_mem.reshape(batch, mem_len, 2 * D_MODEL), batch, seq, tm)

        xf, h = merge_out(o_a, o_b, o_c, proj, wa_b[l], wb_b[l], wc_b[l], wo_b[l], ln_mix_post[l], xf,
                          ln_mlp_pre[l], tm)
        xf, h = mlp(h, w1_b[l], w2_b[l], ln_mlp_post[l], xf, ln_mix_pre[(l + 1) % depth], tm_big, 1024)
    return xf.reshape(batch, seq, d)
```

```python
import functools

import jax
import jax.numpy as jnp
import numpy as np
from jax import lax
from jax.experimental import pallas as pl
from jax.experimental.pallas import tpu as pltpu

F32 = jnp.float32
BF16 = jnp.bfloat16

D_MODEL = 1024
LRU_BLOCKS = 8
LRU_BW = D_MODEL // LRU_BLOCKS
CONV_W = 4
LRU_C = 8.0
N_HEADS = 16
HEAD_DIM = 64
N_KV = 4
GROUP = N_HEADS // N_KV
KV_W = N_KV * HEAD_DIM
CMP_STRIDE = 16
CMP_LEN = 32
SEL_LEN = 64
SEL_SHIFT = 6
N_SELECT = 16
WINDOW = 512
Q_BLOCK = 256
MEM_HEADS = 4
MEM_HEAD_DIM = D_MODEL // MEM_HEADS
D_FF = 4 * D_MODEL
ROPE_THETA = 10000.0
EPS = 1e-6
NEG = -1e30
FORCE = 1e4
LOG2_E = 1.4426950408889634

LANES = 128
SUBLANES = 8
SEL_TILE = 512
WIN_KEYS = WINDOW + Q_BLOCK
V_ROWS = 80
VMEM_LIMIT = 48 * 1024 * 1024

COL_XR, COL_YR, COL_Q, COL_QM, COL_GM, COL_KV, COL_GN = 0, 1024, 2048, 3072, 4096, 7168, 8704
D_INP = 9216


def _cparams(sem):
    return pltpu.CompilerParams(dimension_semantics=sem, vmem_limit_bytes=VMEM_LIMIT)


def _sigmoid(x):
    return 0.5 * jnp.tanh(0.5 * x) + 0.5


def _gelu_tanh(x):
    return 0.5 * x * (1.0 + jnp.tanh(0.7978845608028654 * (x + 0.044715 * (x * x * x))))


def _rms(x, g):
    return x * lax.rsqrt(jnp.mean(x * x, axis=-1, keepdims=True) + EPS) * g


def _masked_softmax(sc, mask):
    sc = jnp.where(mask, sc, NEG)
    m = jnp.max(sc, axis=-1, keepdims=True)
    p = jnp.where(mask, jnp.exp(sc - m), 0.0)
    return p / jnp.maximum(jnp.sum(p, axis=-1, keepdims=True), 1e-30)


def _rms_kernel(x_ref, g_ref, o_ref):
    o_ref[...] = _rms(x_ref[...], g_ref[...]).astype(o_ref.dtype)


def rmsnorm_bf16(x, g, tm):
    m, d = x.shape
    return pl.pallas_call(
        _rms_kernel,
        grid=(m // tm,),
        in_specs=[pl.BlockSpec((tm, d), lambda i: (i, 0)), pl.BlockSpec((1, d), lambda i: (0, 0))],
        out_specs=pl.BlockSpec((tm, d), lambda i: (i, 0)),
        out_shape=jax.ShapeDtypeStruct((m, d), BF16),
        compiler_params=_cparams(("arbitrary",)),
        name="rmsnorm",
    )(x, g.reshape(1, d))


def _mm_kernel(a_ref, w_ref, o_ref):
    o_ref[...] = jnp.dot(a_ref[...], w_ref[...], preferred_element_type=F32).astype(o_ref.dtype)


def matmul(a, w, tm, tn, out_dtype, name):
    m, k = a.shape
    n = w.shape[1]
    return pl.pallas_call(
        _mm_kernel,
        grid=(n // tn, m // tm),
        in_specs=[pl.BlockSpec((tm, k), lambda j, i: (i, 0)), pl.BlockSpec((k, tn), lambda j, i: (0, j))],
        out_specs=pl.BlockSpec((tm, tn), lambda j, i: (i, j)),
        out_shape=jax.ShapeDtypeStruct((m, n), out_dtype),
        compiler_params=_cparams(("arbitrary", "arbitrary")),
        name=name,
    )(a, w)


def _rope_table_kernel(pos_ref, inv_ref, cos_ref, sin_ref):
    ang = pos_ref[...].astype(F32) * inv_ref[...]
    lane = lax.broadcasted_iota(jnp.int32, ang.shape, 1)
    cos_ref[...] = jnp.cos(ang)
    sin_ref[...] = jnp.where((lane & (HEAD_DIM - 1)) < HEAD_DIM // 2, -1.0, 1.0) * jnp.sin(ang)


def rope_tables(pos_flat, tm):
    n = pos_flat.shape[0]
    half = HEAD_DIM // 2
    inv = ROPE_THETA ** (-jnp.arange(half, dtype=F32) * 2.0 / HEAD_DIM)
    inv_full = jnp.tile(inv, LANES // half).reshape(1, LANES)
    return pl.pallas_call(
        _rope_table_kernel,
        grid=(n // tm,),
        in_specs=[pl.BlockSpec((tm, 1), lambda i: (i, 0)), pl.BlockSpec((1, LANES), lambda i: (0, 0))],
        out_specs=[pl.BlockSpec((tm, LANES), lambda i: (i, 0))] * 2,
        out_shape=[jax.ShapeDtypeStruct((n, LANES), F32)] * 2,
        compiler_params=_cparams(("arbitrary",)),
        name="rope_tables",
    )(pos_flat.reshape(n, 1), inv_full)


def _rope_table_t_kernel(pos_ref, inv_ref, cos_ref, sin_ref):
    ang = inv_ref[...] * pos_ref[0].astype(F32)
    cos_ref[0] = jnp.cos(ang)
    sin_ref[0] = jnp.sin(ang)


def rope_tables_t(positions, tm):
    batch, seq = positions.shape
    half = HEAD_DIM // 2
    inv = (ROPE_THETA ** (-jnp.arange(half, dtype=F32) * 2.0 / HEAD_DIM)).reshape(half, 1)
    return pl.pallas_call(
        _rope_table_t_kernel,
        grid=(batch, seq // tm),
        in_specs=[pl.BlockSpec((1, 1, tm), lambda b, i: (b, 0, i)), pl.BlockSpec((half, 1), lambda b, i: (0, 0))],
        out_specs=[pl.BlockSpec((1, half, tm), lambda b, i: (b, 0, i))] * 2,
        out_shape=[jax.ShapeDtypeStruct((batch, half, seq), F32)] * 2,
        compiler_params=_cparams(("arbitrary", "arbitrary")),
        name="rope_tables_t",
    )(positions.reshape(batch, 1, seq), inv)


def _rope128(x, cos_t, sin_t):
    lane = lax.broadcasted_iota(jnp.int32, x.shape, 1)
    first = (lane & (HEAD_DIM - 1)) < HEAD_DIM // 2
    partner = jnp.where(first, pltpu.roll(x, LANES - HEAD_DIM // 2, 1), pltpu.roll(x, HEAD_DIM // 2, 1))
    return x * cos_t + partner * sin_t


def _rglru_kernel(xr_ref, yr_ref, cw_ref, cb_ref, wr_ref, br_ref, wi_ref, bi_ref, lam_ref, o_ref, h_sc, tail_sc):
    @pl.when(pl.program_id(1) == 0)
    def _():
        h_sc[...] = jnp.zeros_like(h_sc)
        tail_sc[...] = jnp.zeros_like(tail_sc)

    xr = xr_ref[...].astype(F32)
    t_len, d = xr.shape
    tail = tail_sc[...]
    row8 = lax.broadcasted_iota(jnp.int32, (8, d), 0)
    cw = cw_ref[...]
    xc = cb_ref[...] + xr * cw[CONV_W - 1 : CONV_W, :]
    for k in range(1, CONV_W):
        rolled = pltpu.roll(xr, k, 0)
        head = jnp.where(row8 < k, pltpu.roll(tail, k, 0), rolled[0:8])
        shifted = jnp.concatenate([head, rolled[8:]], axis=0)
        xc = xc + shifted * cw[CONV_W - 1 - k : CONV_W - k, :]
    tail_sc[...] = xr[t_len - 8 :]

    xcb = xc.astype(BF16)
    rl, il = [], []
    for n in range(LRU_BLOCKS):
        xb = xcb[:, n * LRU_BW : (n + 1) * LRU_BW]
        rl.append(jnp.dot(xb, wr_ref[n], preferred_element_type=F32))
        il.append(jnp.dot(xb, wi_ref[n], preferred_element_type=F32))
    r = _sigmoid(jnp.concatenate(rl, axis=1) + br_ref[...])
    ig = _sigmoid(jnp.concatenate(il, axis=1) + bi_ref[...])
    softplus_neg_lam = jnp.log1p(jnp.exp(-lam_ref[...]))
    log_a = (-LRU_C * softplus_neg_lam) * r
    a = jnp.exp(log_a)
    one_minus_a2 = 1.0 - a * a
    root = jnp.where(one_minus_a2 > 0.0, one_minus_a2 * lax.rsqrt(one_minus_a2), 0.0)
    b = root * (ig * xc)

    sub = lax.broadcasted_iota(jnp.int32, (t_len, d), 0) & (SUBLANES - 1)
    step = 1
    while step < SUBLANES:
        keep = sub >= step
        a_sh = jnp.where(keep, pltpu.roll(a, step, 0), 1.0)
        b_sh = jnp.where(keep, pltpu.roll(b, step, 0), 0.0)
        b = a * b_sh + b
        a = a * a_sh
        step *= 2
    h_prev = h_sc[...]
    groups = []
    for g in range(t_len // SUBLANES):
        sl = slice(g * SUBLANES, (g + 1) * SUBLANES)
        groups.append(b[sl] + a[sl] * h_prev)
        h_prev = jnp.broadcast_to(groups[-1][SUBLANES - 1 :], (SUBLANES, d))
    h_sc[...] = h_prev
    h = jnp.concatenate(groups, axis=0)
    o_ref[...] = (h * _gelu_tanh(yr_ref[...].astype(F32))).astype(o_ref.dtype)


def rglru_branch(proj, batch, seq, conv_w, conv_b, wr, br, wi, bi, lam, t_len):
    d = D_MODEL
    nt = seq // t_len
    vec = lambda v: v.reshape(1, d)
    full2 = lambda shape: pl.BlockSpec(shape, lambda b, t: (0,) * len(shape))
    return pl.pallas_call(
        _rglru_kernel,
        grid=(batch, nt),
        in_specs=[
            pl.BlockSpec((t_len, d), lambda b, t: (b * nt + t, COL_XR // d)),
            pl.BlockSpec((t_len, d), lambda b, t: (b * nt + t, COL_YR // d)),
            full2((CONV_W, d)), full2((1, d)),
            full2((LRU_BLOCKS, LRU_BW, LRU_BW)), full2((1, d)),
            full2((LRU_BLOCKS, LRU_BW, LRU_BW)), full2((1, d)),
            full2((1, d)),
        ],
        out_specs=pl.BlockSpec((t_len, d), lambda b, t: (b * nt + t, 0)),
        out_shape=jax.ShapeDtypeStruct((batch * seq, d), BF16),
        scratch_shapes=[pltpu.VMEM((8, d), F32), pltpu.VMEM((8, d), F32)],
        compiler_params=_cparams(("arbitrary", "arbitrary")),
        name="rglru",
    )(proj, proj, conv_w, vec(conv_b), wr.astype(BF16), vec(br), wi.astype(BF16), vec(bi), vec(lam))


def _kv_prep_kernel(kvs_ref, kvw_ref, cos_ref, sin_ref, kso_ref, vso_ref, kwo_ref, vwo_ref):
    cos_t, sin_t = cos_ref[...], sin_ref[...]
    t_len = cos_t.shape[0]
    pad_row = lax.broadcasted_iota(jnp.int32, (V_ROWS - HEAD_DIM, t_len), 0)
    pad_rows = jnp.where(pad_row == 0, 1.0, 0.0).astype(BF16)
    for src, k_dst, v_dst in ((kvs_ref, kso_ref, vso_ref), (kvw_ref, kwo_ref, vwo_ref)):
        x = src[...].astype(F32)
        roped = [_rope128(x[:, c * LANES : (c + 1) * LANES], cos_t, sin_t) for c in range(KV_W // LANES)]
        v_t = x[:, KV_W:].T
        for g in range(N_KV):
            pair = roped[g * HEAD_DIM // LANES]
            lo = g * HEAD_DIM % LANES
            k_dst[0, g] = pair[:, lo : lo + HEAD_DIM].astype(BF16)
            v_dst[0, g, :HEAD_DIM, :] = v_t[g * HEAD_DIM : (g + 1) * HEAD_DIM].astype(BF16)
            v_dst[0, g, HEAD_DIM:, :] = pad_rows


def kv_prep(proj, cos_t, sin_t, batch, seq, tm):
    nt = seq // tm
    blk = lambda col: pl.BlockSpec((tm, 2 * KV_W), lambda b, t: (b * nt + t, col // (2 * KV_W)))
    tab = pl.BlockSpec((tm, LANES), lambda b, t: (b * nt + t, 0))
    k_spec = pl.BlockSpec((1, N_KV, tm, HEAD_DIM), lambda b, t: (b, 0, t, 0))
    v_spec = pl.BlockSpec((1, N_KV, V_ROWS, tm), lambda b, t: (b, 0, 0, t))
    k_shape = jax.ShapeDtypeStruct((batch, N_KV, seq, HEAD_DIM), BF16)
    v_shape = jax.ShapeDtypeStruct((batch, N_KV, V_ROWS, seq), BF16)
    return pl.pallas_call(
        _kv_prep_kernel,
        grid=(batch, nt),
        in_specs=[blk(COL_KV + 2 * KV_W), blk(COL_KV + 4 * KV_W), tab, tab],
        out_specs=[k_spec, v_spec, k_spec, v_spec],
        out_shape=[k_shape, v_shape, k_shape, v_shape],
        compiler_params=_cparams(("arbitrary", "arbitrary")),
        name="kv_prep",
    )(proj, proj, cos_t, sin_t)


def _compress_kernel(x_ref, pe_ref, w1_ref, b1_ref, w2_ref, cos_ref, sin_ref, o_ref, *, rotary):
    x = x_ref[0]
    half = CMP_STRIDE * HEAD_DIM
    pe = pe_ref[0]
    w1 = w1_ref[0]
    u = jnp.dot((x + pe[0:1]).astype(BF16), w1[:half], preferred_element_type=F32)
    v = jnp.dot((x + pe[1:2]).astype(BF16), w1[half:], preferred_element_type=F32)
    hid = _gelu_tanh(u + pltpu.roll(v, v.shape[0] - 1, 0) + b1_ref[0])
    out = jnp.dot(hid.astype(BF16), w2_ref[0], preferred_element_type=F32)
    if rotary:
        hh = HEAD_DIM // 2
        partner = jnp.concatenate([out[:, hh:], out[:, :hh]], axis=1)
        out = out * cos_ref[0][:, :HEAD_DIM] + partner * sin_ref[0][:, :HEAD_DIM]
    o_ref[0] = out.astype(o_ref.dtype)


def compress(x_chunks, j, pe, w1, b1, w2, cos_c, sin_c, batch, rotary):
    bg, nc, width = x_chunks.shape
    return pl.pallas_call(
        functools.partial(_compress_kernel, rotary=rotary),
        grid=(bg,),
        in_specs=[
            pl.BlockSpec((1, nc, width), lambda i: (i, 0, 0)),
            pl.BlockSpec((1, 2, width), lambda i: (j, 0, 0)),
            pl.BlockSpec((1, 2 * width, w1.shape[2]), lambda i: (j, 0, 0)),
            pl.BlockSpec((1, 1, w1.shape[2]), lambda i: (j, 0, 0)),
            pl.BlockSpec((1, w2.shape[1], HEAD_DIM), lambda i: (j, 0, 0)),
            pl.BlockSpec((1, nc, LANES), lambda i: (i // N_KV, 0, 0)),
            pl.BlockSpec((1, nc, LANES), lambda i: (i // N_KV, 0, 0)),
        ],
        out_specs=pl.BlockSpec((1, nc, HEAD_DIM), lambda i: (i, 0, 0)),
        out_shape=jax.ShapeDtypeStruct((bg, nc, HEAD_DIM), BF16),
        compiler_params=_cparams(("arbitrary",)),
        name="compress_k" if rotary else "compress_v",
    )(x_chunks, pe, w1, b1, w2, cos_c, sin_c)


def _nsa_kernel(q_ref, gate_ref, cos_ref, sin_ref, kc_ref, vct_ref, ks_ref, vst_ref, kw_ref, vwt_ref, ovt_ref,
                o_ref, sel_sc, sa_sc, sb_sc, oc_sc, imp_sc):
    qb = Q_BLOCK
    s0 = pl.program_id(2) * qb
    nc = kc_ref.shape[2]
    hh = HEAD_DIM // 2
    cols = GROUP * qb

    qt = q_ref[...].astype(F32).T
    cos_t, sin_t = cos_ref[0], sin_ref[0]
    heads = []
    for r in range(GROUP):
        x1 = qt[r * HEAD_DIM : r * HEAD_DIM + hh]
        x2 = qt[r * HEAD_DIM + hh : (r + 1) * HEAD_DIM]
        heads.append(jnp.concatenate([x1 * cos_t - x2 * sin_t, x2 * cos_t + x1 * sin_t], axis=0))
    q_t = (jnp.concatenate(heads, axis=1) * (HEAD_DIM ** -0.5 * LOG2_E)).astype(BF16)

    tq = s0 + lax.broadcasted_iota(jnp.int32, (1, qb), 1)

    def compressed(n_c):
        cmp_end = lax.broadcasted_iota(jnp.int32, (n_c, qb), 0) * CMP_STRIDE + (CMP_LEN - 1)
        bias_c = jnp.where(cmp_end <= tq, 0.0, NEG)
        has_key = jnp.where(tq >= CMP_LEN - 1, 1.0, 0.0)
        sc = jnp.dot(kc_ref[0, 0, :n_c, :], q_t, preferred_element_type=F32) + jnp.concatenate([bias_c] * GROUP, axis=1)
        p_c = jnp.exp2(sc - jnp.max(sc, axis=0, keepdims=True))
        norm = jnp.concatenate([has_key] * GROUP, axis=1) / jnp.maximum(jnp.sum(p_c, axis=0, keepdims=True), 1e-30)
        p_c = p_c * norm
        oc_sc[...] = jnp.dot(vct_ref[0, 0, :, :n_c], p_c.astype(BF16), preferred_element_type=F32)
        p_sum = p_c[:, :qb]
        for r in range(1, GROUP):
            p_sum = p_sum + p_c[:, r * qb : (r + 1) * qb]
        p_hi = p_sum.astype(BF16)
        p_lo = (p_sum - p_hi.astype(F32)).astype(BF16)
        ovt = ovt_ref[:, :n_c]
        imp_sc[...] = (jnp.dot(ovt, p_hi, preferred_element_type=F32)
                       + jnp.dot(ovt, p_lo, preferred_element_type=F32))

    last_cmp = (s0 + qb - CMP_LEN) // CMP_STRIDE
    for v in range(nc // LANES):
        pl.when(last_cmp // LANES == v)(functools.partial(compressed, (v + 1) * LANES))
    o_c = oc_sc[...]
    imp = imp_sc[...]

    w0 = pl.multiple_of(jnp.maximum(s0 - WINDOW, 0), LANES)
    wpos = w0 + lax.broadcasted_iota(jnp.int32, (WIN_KEYS, qb), 0)
    bias_w = jnp.where((wpos <= tq) & (wpos > tq - WINDOW), 0.0, NEG)
    kw_tile = kw_ref[0, 0, pl.ds(w0, WIN_KEYS), :]
    vw_tile = vwt_ref[0, 0, :, pl.ds(w0, WIN_KEYS)]
    s_w = jnp.dot(kw_tile, q_t, preferred_element_type=F32) + jnp.concatenate([bias_w] * GROUP, axis=1)
    p_w = jnp.exp2(s_w - jnp.max(s_w, axis=0, keepdims=True)).astype(BF16)
    acc_w = jnp.dot(vw_tile, p_w, preferred_element_type=F32)
    o_w = acc_w[:HEAD_DIM] / jnp.maximum(acc_w[HEAD_DIM : HEAD_DIM + 1], 1e-30)

    blk = lax.broadcasted_iota(jnp.int32, (LANES, qb), 0)
    forced = (blk == 0) | (blk == (tq >> SEL_SHIFT))
    work = jnp.where(forced, FORCE, jnp.where(blk * SEL_LEN <= tq, imp, -FORCE))

    blk_f = blk.astype(F32)
    sel = jnp.zeros((LANES, qb), F32)
    for _ in range(N_SELECT):
        m = jnp.max(work, axis=0, keepdims=True)
        idx = jnp.min(jnp.where(work == m, blk_f, float(LANES)), axis=0, keepdims=True)
        hit = blk_f == idx
        sel = jnp.where(hit, 1.0, sel)
        work = jnp.where(hit, -3e38, work)
    sel_sc[...] = sel

    blocks_per_tile = SEL_TILE // SEL_LEN
    row_minus_lane = (lax.broadcasted_iota(jnp.int32, (SEL_TILE, qb), 0)
                      - lax.broadcasted_iota(jnp.int32, (SEL_TILE, qb), 1))
    n_tiles = (s0 + qb - 1) // SEL_TILE + 1

    def scores(kb):
        k0 = pl.multiple_of(kb * SEL_TILE, SEL_TILE)
        picked = sel_sc[pl.ds(pl.multiple_of(kb * blocks_per_tile, blocks_per_tile), blocks_per_tile), :]
        picked = jnp.where(picked > 0.5, 0.0, NEG)
        picked = jnp.concatenate(
            [jnp.broadcast_to(picked[j : j + 1, :], (SEL_LEN, qb)) for j in range(blocks_per_tile)], axis=0)
        bias = jnp.where(row_minus_lane <= s0 - k0, picked, NEG)
        s_ = jnp.dot(ks_ref[0, 0, pl.ds(k0, SEL_TILE), :], q_t, preferred_element_type=F32)
        return s_ + jnp.concatenate([bias] * GROUP, axis=1)

    def absorb(kb, s_, m_run, acc):
        k0 = pl.multiple_of(kb * SEL_TILE, SEL_TILE)
        m_new = jnp.maximum(m_run, jnp.max(s_, axis=0, keepdims=True))
        p = jnp.exp2(s_ - m_new).astype(BF16)
        pv = jnp.dot(vst_ref[0, 0, :, pl.ds(k0, SEL_TILE)], p, preferred_element_type=F32)
        return m_new, jnp.exp2(m_run - m_new) * acc + pv

    sa_sc[...] = scores(0)

    def sel_step(j, carry):
        m_run, acc = carry
        sb_sc[...] = scores(2 * j + 1)
        m_run, acc = absorb(2 * j, sa_sc[...], m_run, acc)
        sa_sc[...] = scores(2 * j + 2)
        return absorb(2 * j + 1, sb_sc[...], m_run, acc)

    init = (jnp.full((1, cols), NEG, F32), jnp.zeros((V_ROWS, cols), F32))
    full_trips = (n_tiles - 1) // 2
    m_run, acc_s = lax.fori_loop(0, full_trips, sel_step, init)
    m_run, acc_s = absorb(2 * full_trips, sa_sc[...], m_run, acc_s)

    def last_tile(carry):
        return absorb(n_tiles - 1, scores(n_tiles - 1), *carry)

    _, acc_s = lax.cond(n_tiles - 2 * full_trips == 2, last_tile, lambda carry: carry, (m_run, acc_s))

    o_s = acc_s[:HEAD_DIM] / jnp.maximum(acc_s[HEAD_DIM : HEAD_DIM + 1], 1e-30)

    gates_t = _sigmoid(gate_ref[...].astype(F32)).T
    outs = []
    for r in range(GROUP):
        sl = slice(r * qb, (r + 1) * qb)
        outs.append(gates_t[3 * r : 3 * r + 1] * o_c[:, sl] + gates_t[3 * r + 1 : 3 * r + 2] * o_s[:, sl]
                    + gates_t[3 * r + 2 : 3 * r + 3] * o_w[:, sl])
    o_ref[...] = jnp.concatenate(outs, axis=0).T.astype(o_ref.dtype)


def nsa_attention(proj, cos_q, sin_q, kc, vct, ks, vst, kw, vwt, overlap_t, batch, seq):
    nq = seq // Q_BLOCK
    nc = kc.shape[2]
    gw = GROUP * HEAD_DIM
    per_bg = lambda shape: pl.BlockSpec((1, 1) + shape, lambda b, g, i: (b, g, 0, 0))
    return pl.pallas_call(
        _nsa_kernel,
        grid=(batch, N_KV, nq),
        in_specs=[
            pl.BlockSpec((Q_BLOCK, gw), lambda b, g, i: (b * nq + i, COL_Q // gw + g)),
            pl.BlockSpec((Q_BLOCK, LANES), lambda b, g, i: (b * nq + i, COL_GN // LANES + g)),
            pl.BlockSpec((1, HEAD_DIM // 2, Q_BLOCK), lambda b, g, i: (b, 0, i)),
            pl.BlockSpec((1, HEAD_DIM // 2, Q_BLOCK), lambda b, g, i: (b, 0, i)),
            per_bg((nc, HEAD_DIM)), per_bg((HEAD_DIM, nc)),
            per_bg((seq, HEAD_DIM)), per_bg((V_ROWS, seq)),
            per_bg((seq, HEAD_DIM)), per_bg((V_ROWS, seq)),
            pl.BlockSpec((LANES, nc), lambda b, g, i: (0, 0)),
        ],
        out_specs=pl.BlockSpec((Q_BLOCK, gw), lambda b, g, i: (b * nq + i, g)),
        out_shape=jax.ShapeDtypeStruct((batch * seq, N_HEADS * HEAD_DIM), BF16),
        scratch_shapes=[pltpu.VMEM((LANES, Q_BLOCK), F32), pltpu.VMEM((SEL_TILE, GROUP * Q_BLOCK), F32),
                        pltpu.VMEM((SEL_TILE, GROUP * Q_BLOCK), F32), pltpu.VMEM((HEAD_DIM, GROUP * Q_BLOCK), F32),
                        pltpu.VMEM((LANES, Q_BLOCK), F32)],
        compiler_params=_cparams(("arbitrary", "arbitrary", "arbitrary")),
        name="nsa_attention",
    )(proj, proj, cos_q, sin_q, kc, vct, ks, vst, kw, vwt, overlap_t)


def _mem_kernel(q_ref, kv_ref, o_ref):
    q = q_ref[...]
    kv = kv_ref[0]
    mem_w = MEM_HEADS * MEM_HEAD_DIM
    outs = []
    for h in range(MEM_HEADS):
        sl = slice(h * MEM_HEAD_DIM, (h + 1) * MEM_HEAD_DIM)
        qh = (q[:, sl] * (MEM_HEAD_DIM ** -0.5)).astype(BF16)
        s = lax.dot_general(qh, kv[:, sl], (((1,), (1,)), ((), ())), preferred_element_type=F32)
        p = jnp.exp(s - jnp.max(s, axis=-1, keepdims=True))
        p = p / jnp.sum(p, axis=-1, keepdims=True)
        outs.append(jnp.dot(p.astype(BF16), kv[:, mem_w + h * MEM_HEAD_DIM : mem_w + (h + 1) * MEM_HEAD_DIM],
                            preferred_element_type=F32))
    o_ref[...] = jnp.concatenate(outs, axis=1).astype(o_ref.dtype)


def memory_attention(proj, kv_mem, batch, seq, tm):
    nt = seq // tm
    mem_len = kv_mem.shape[1]
    return pl.pallas_call(
        _mem_kernel,
        grid=(batch, nt),
        in_specs=[
            pl.BlockSpec((tm, D_MODEL), lambda b, t: (b * nt + t, COL_QM // D_MODEL)),
            pl.BlockSpec((1, mem_len, 2 * D_MODEL), lambda b, t: (b, 0, 0)),
        ],
        out_specs=pl.BlockSpec((tm, D_MODEL), lambda b, t: (b * nt + t, 0)),
        out_shape=jax.ShapeDtypeStruct((batch * seq, D_MODEL), BF16),
        compiler_params=_cparams(("arbitrary", "arbitrary")),
        name="memory_attention",
    )(proj, kv_mem)


def _merge_kernel(oa_ref, ob_ref, oc_ref, ga_ref, gb_ref, gc_ref, wa_ref, wb_ref, wc_ref, wo_ref,
                  gpost_ref, x_ref, gnext_ref, xo_ref, ho_ref):
    gate = lambda ref: _sigmoid(ref[...].astype(F32))
    merged = gate(ga_ref) * jnp.dot(oa_ref[...], wa_ref[...], preferred_element_type=F32)
    merged += gate(gb_ref) * jnp.dot(ob_ref[...], wb_ref[...], preferred_element_type=F32)
    merged += gate(gc_ref) * jnp.dot(oc_ref[...], wc_ref[...], preferred_element_type=F32)
    y = jnp.dot(merged.astype(BF16), wo_ref[...], preferred_element_type=F32)
    x_new = x_ref[...] + _rms(y, gpost_ref[...])
    xo_ref[...] = x_new
    ho_ref[...] = _rms(x_new, gnext_ref[...]).astype(ho_ref.dtype)


def merge_out(oa, ob, oc, proj, wa, wb, wc, wo, g_post, x, g_next, tm):
    n, d = x.shape
    row = lambda col: pl.BlockSpec((tm, d), lambda i: (i, col))
    full = lambda shape: pl.BlockSpec(shape, lambda i: (0, 0))
    gm = COL_GM // d
    return pl.pallas_call(
        _merge_kernel,
        grid=(n // tm,),
        in_specs=[row(0), row(0), row(0), row(gm), row(gm + 1), row(gm + 2),
                  full((d, d)), full((d, d)), full((d, d)), full((d, d)), full((1, d)), row(0), full((1, d))],
        out_specs=[row(0), row(0)],
        out_shape=[jax.ShapeDtypeStruct((n, d), F32), jax.ShapeDtypeStruct((n, d), BF16)],
        compiler_params=_cparams(("arbitrary",)),
        name="merge_out",
    )(oa, ob, oc, proj, proj, proj, wa, wb, wc, wo, g_post.reshape(1, d), x, g_next.reshape(1, d))


def _mlp_kernel(h_ref, w1_ref, w2_ref, gpost_ref, x_ref, gnext_ref, xo_ref, ho_ref, acc_ref):
    k = pl.program_id(1)

    @pl.when(k == 0)
    def _():
        acc_ref[...] = jnp.zeros_like(acc_ref)

    u = jnp.maximum(jnp.dot(h_ref[...], w1_ref[...], preferred_element_type=F32), 0.0)
    acc_ref[...] += jnp.dot((u * u).astype(BF16), w2_ref[...], preferred_element_type=F32)

    @pl.when(k == pl.num_programs(1) - 1)
    def _():
        x_new = x_ref[...] + _rms(acc_ref[...], gpost_ref[...])
        xo_ref[...] = x_new
        ho_ref[...] = _rms(x_new, gnext_ref[...]).astype(ho_ref.dtype)


def mlp(h, w1, w2, g_post, x, g_next, tm, tf):
    n, d = x.shape
    ff = w1.shape[1]
    return pl.pallas_call(
        _mlp_kernel,
        grid=(n // tm, ff // tf),
        in_specs=[
            pl.BlockSpec((tm, d), lambda i, k: (i, 0)),
            pl.BlockSpec((d, tf), lambda i, k: (0, k)),
            pl.BlockSpec((tf, d), lambda i, k: (k, 0)),
            pl.BlockSpec((1, d), lambda i, k: (0, 0)),
            pl.BlockSpec((tm, d), lambda i, k: (i, 0)),
            pl.BlockSpec((1, d), lambda i, k: (0, 0)),
        ],
        out_specs=[pl.BlockSpec((tm, d), lambda i, k: (i, 0))] * 2,
        out_shape=[jax.ShapeDtypeStruct((n, d), F32), jax.ShapeDtypeStruct((n, d), BF16)],
        scratch_shapes=[pltpu.VMEM((tm, d), F32)],
        compiler_params=_cparams(("arbitrary", "arbitrary")),
        name="mlp",
    )(h, w1, w2, g_post.reshape(1, d), x, g_next.reshape(1, d))


def _pack_w_in(w):
    o_kv, o_gn, o_qm, o_gm = 3072, 4608, 4656, 5680
    per_group = GROUP * 3
    gn = w[:, o_gn:o_qm].reshape(w.shape[0], N_KV, per_group)
    gn = jnp.pad(gn, ((0, 0), (0, 0), (0, LANES - per_group))).reshape(w.shape[0], N_KV * LANES)
    packed = jnp.concatenate([w[:, :o_kv], w[:, o_qm:o_gm], w[:, o_gm:], w[:, o_kv:o_gn], gn], axis=1)
    assert packed.shape[1] == D_INP
    return packed.astype(BF16)


def _overlap_matrix_t(nc):
    c0 = np.arange(nc)[:, None] * CMP_STRIDE
    s0 = np.arange(LANES)[None, :] * SEL_LEN
    ov = np.clip(np.minimum(c0 + CMP_LEN, s0 + SEL_LEN) - np.maximum(c0, s0), 0, None).astype(np.float32) / CMP_LEN
    return jnp.asarray(ov.T, dtype=BF16)


def kernel(x, mem, positions, ln_mix_pre, w_in, conv_w, conv_b, lru_wr, lru_br, lru_wi, lru_bi, lru_lambda, cmp_pe, cmp_w1, cmp_b1, cmp_w2, ln_mem, w_mem_kv, w_br_rnn, w_br_nsa, w_br_mem, w_out, ln_mix_post, ln_mlp_pre, mlp_w1, mlp_w2, ln_mlp_post):
    batch, seq, d = x.shape
    depth = w_in.shape[0]
    mem_len = mem.shape[1]
    n = batch * seq
    nc = seq // CMP_STRIDE
    assert d == D_MODEL and seq % (2 * SEL_TILE) == 0 and seq // SEL_LEN <= LANES and seq >= WIN_KEYS

    tm = min(512, seq)
    tm_big = min(1024, seq)
    xf = x.reshape(n, d)
    memf = mem.reshape(batch * mem_len, d)
    cos_t, sin_t = rope_tables(positions.reshape(n), tm)
    pos_c = jnp.pad(positions[:, CMP_LEN - 1 :: CMP_STRIDE], ((0, 0), (0, 1)))
    cos_c, sin_c = rope_tables(pos_c.reshape(batch * nc), nc)
    cos_c, sin_c = cos_c.reshape(batch, nc, LANES), sin_c.reshape(batch, nc, LANES)
    cos_q, sin_q = rope_tables_t(positions, tm)
    overlap_t = _overlap_matrix_t(nc)

    w_in_b, w_kv_b = w_in.astype(BF16), w_mem_kv.astype(BF16)
    wr_b, wi_b = lru_wr.astype(BF16), lru_wi.astype(BF16)
    w1c_b, w2c_b = cmp_w1.astype(BF16), cmp_w2.astype(BF16)
    wa_b, wb_b, wc_b, wo_b = (w.astype(BF16) for w in (w_br_rnn, w_br_nsa, w_br_mem, w_out))
    w1_b, w2_b = mlp_w1.astype(BF16), mlp_w2.astype(BF16)

    h = rmsnorm_bf16(xf, ln_mix_pre[0], tm)
    for l in range(depth):
        proj = matmul(h, _pack_w_in(w_in_b[l]), tm_big, 1024, BF16, "in_proj")

        o_a = rglru_branch(proj, batch, seq, conv_w[l], conv_b[l], wr_b[l], lru_br[l], wi_b[l], lru_bi[l],
                           lru_lambda[l], min(256, seq))

        ks, vst, kw, vwt = kv_prep(proj, cos_t, sin_t, batch, seq, tm)
        kvp = lambda j: proj[:, COL_KV + j * KV_W : COL_KV + (j + 1) * KV_W]
        chunks = lambda a: (a.reshape(batch, nc, CMP_STRIDE, N_KV, HEAD_DIM).transpose(0, 3, 1, 2, 4)
                            .reshape(batch * N_KV, nc, CMP_STRIDE * HEAD_DIM))
        pe = cmp_pe[l].reshape(2, 2, CMP_STRIDE * HEAD_DIM)
        b1c = cmp_b1[l].reshape(2, 1, -1)
        k_cmp = compress(chunks(kvp(0)), 0, pe, w1c_b[l], b1c, w2c_b[l], cos_c, sin_c, batch, True)
        v_cmp = compress(chunks(kvp(1)), 1, pe, w1c_b[l], b1c, w2c_b[l], cos_c, sin_c, batch, False)
        kc = k_cmp.reshape(batch, N_KV, nc, HEAD_DIM)
        vct = v_cmp.reshape(batch, N_KV, nc, HEAD_DIM).transpose(0, 1, 3, 2)
        o_b = nsa_attention(proj, cos_q, sin_q, kc, vct, ks, vst, kw, vwt, overlap_t, batch, seq)

        mem_h = rmsnorm_bf16(memf, ln_mem[l], mem_len)
        kv_mem = matmul(mem_h, w_kv_b[l], mem_len, 1024, BF16, "mem_kv")
        o_c = memory_attention(proj, kv_mem.reshape(batch, mem_len, 2 * D_MODEL), batch, seq, tm)

        xf, h = merge_out(o_a, o_b, o_c, proj, wa_b[l], wb_b[l], wc_b[l], wo_b[l], ln_mix_post[l], xf,
                          ln_mlp_pre[l], tm)
        xf, h = mlp(h, w1_b[l], w2_b[l], ln_mlp_post[l], xf, ln_mix_pre[(l + 1) % depth], tm_big, 1024)
    return xf.reshape(batch, seq, d)
```

```python
import functools

import jax
import jax.numpy as jnp
import numpy as np
from jax import lax
from jax.experimental import pallas as pl
from jax.experimental.pallas import tpu as pltpu

F32 = jnp.float32
BF16 = jnp.bfloat16

D_MODEL = 1024
LRU_BLOCKS = 8
LRU_BW = D_MODEL // LRU_BLOCKS
CONV_W = 4
LRU_C = 8.0
N_HEADS = 16
HEAD_DIM = 64
N_KV = 4
GROUP = N_HEADS // N_KV
KV_W = N_KV * HEAD_DIM
CMP_STRIDE = 16
CMP_LEN = 32
SEL_LEN = 64
SEL_SHIFT = 6
N_SELECT = 16
WINDOW = 512
Q_BLOCK = 256
MEM_HEADS = 4
MEM_HEAD_DIM = D_MODEL // MEM_HEADS
D_FF = 4 * D_MODEL
ROPE_THETA = 10000.0
EPS = 1e-6
NEG = -1e30
FORCE = 1e4
LOG2_E = 1.4426950408889634

LANES = 128
SUBLANES = 8
SEL_TILE = 512
WIN_KEYS = WINDOW + Q_BLOCK
V_ROWS = 80
VMEM_LIMIT = 48 * 1024 * 1024

COL_XR, COL_YR, COL_Q, COL_QM, COL_GM, COL_KV, COL_GN = 0, 1024, 2048, 3072, 4096, 7168, 8704
D_INP = 9216


def _cparams(sem):
    return pltpu.CompilerParams(dimension_semantics=sem, vmem_limit_bytes=VMEM_LIMIT)


def _sigmoid(x):
    return 0.5 * jnp.tanh(0.5 * x) + 0.5


def _gelu_tanh(x):
    return 0.5 * x * (1.0 + jnp.tanh(0.7978845608028654 * (x + 0.044715 * (x * x * x))))


def _rms(x, g):
    return x * lax.rsqrt(jnp.mean(x * x, axis=-1, keepdims=True) + EPS) * g


def _masked_softmax(sc, mask):
    sc = jnp.where(mask, sc, NEG)
    m = jnp.max(sc, axis=-1, keepdims=True)
    p = jnp.where(mask, jnp.exp(sc - m), 0.0)
    return p / jnp.maximum(jnp.sum(p, axis=-1, keepdims=True), 1e-30)


def _rms_kernel(x_ref, g_ref, o_ref):
    o_ref[...] = _rms(x_ref[...], g_ref[...]).astype(o_ref.dtype)


def rmsnorm_bf16(x, g, tm):
    m, d = x.shape
    return pl.pallas_call(
        _rms_kernel,
        grid=(m // tm,),
        in_specs=[pl.BlockSpec((tm, d), lambda i: (i, 0)), pl.BlockSpec((1, d), lambda i: (0, 0))],
        out_specs=pl.BlockSpec((tm, d), lambda i: (i, 0)),
        out_shape=jax.ShapeDtypeStruct((m, d), BF16),
        compiler_params=_cparams(("arbitrary",)),
        name="rmsnorm",
    )(x, g.reshape(1, d))


def _mm_kernel(a_ref, w_ref, o_ref):
    o_ref[...] = jnp.dot(a_ref[...], w_ref[...], preferred_element_type=F32).astype(o_ref.dtype)


def matmul(a, w, layer, tm, tn, out_dtype, name):
    m, k = a.shape
    n = w.shape[2]
    return pl.pallas_call(
        _mm_kernel,
        grid=(n // tn, m // tm),
        in_specs=[pl.BlockSpec((tm, k), lambda j, i: (i, 0)), pl.BlockSpec((None, k, tn), lambda j, i: (layer, 0, j))],
        out_specs=pl.BlockSpec((tm, tn), lambda j, i: (i, j)),
        out_shape=jax.ShapeDtypeStruct((m, n), out_dtype),
        compiler_params=_cparams(("arbitrary", "arbitrary")),
        name=name,
    )(a, w)


def _rope_table_kernel(pos_ref, inv_ref, cos_ref, sin_ref):
    ang = pos_ref[...].astype(F32) * inv_ref[...]
    lane = lax.broadcasted_iota(jnp.int32, ang.shape, 1)
    cos_ref[...] = jnp.cos(ang)
    sin_ref[...] = jnp.where((lane & (HEAD_DIM - 1)) < HEAD_DIM // 2, -1.0, 1.0) * jnp.sin(ang)


def rope_tables(pos_flat, tm):
    n = pos_flat.shape[0]
    half = HEAD_DIM // 2
    inv = ROPE_THETA ** (-jnp.arange(half, dtype=F32) * 2.0 / HEAD_DIM)
    inv_full = jnp.tile(inv, LANES // half).reshape(1, LANES)
    return pl.pallas_call(
        _rope_table_kernel,
        grid=(n // tm,),
        in_specs=[pl.BlockSpec((tm, 1), lambda i: (i, 0)), pl.BlockSpec((1, LANES), lambda i: (0, 0))],
        out_specs=[pl.BlockSpec((tm, LANES), lambda i: (i, 0))] * 2,
        out_shape=[jax.ShapeDtypeStruct((n, LANES), F32)] * 2,
        compiler_params=_cparams(("arbitrary",)),
        name="rope_tables",
    )(pos_flat.reshape(n, 1), inv_full)


def _rope_table_t_kernel(pos_ref, inv_ref, cos_ref, sin_ref):
    ang = inv_ref[...] * pos_ref[0].astype(F32)
    cos_ref[0] = jnp.cos(ang)
    sin_ref[0] = jnp.sin(ang)


def rope_tables_t(positions, tm):
    batch, seq = positions.shape
    half = HEAD_DIM // 2
    inv = (ROPE_THETA ** (-jnp.arange(half, dtype=F32) * 2.0 / HEAD_DIM)).reshape(half, 1)
    return pl.pallas_call(
        _rope_table_t_kernel,
        grid=(batch, seq // tm),
        in_specs=[pl.BlockSpec((1, 1, tm), lambda b, i: (b, 0, i)), pl.BlockSpec((half, 1), lambda b, i: (0, 0))],
        out_specs=[pl.BlockSpec((1, half, tm), lambda b, i: (b, 0, i))] * 2,
        out_shape=[jax.ShapeDtypeStruct((batch, half, seq), F32)] * 2,
        compiler_params=_cparams(("arbitrary", "arbitrary")),
        name="rope_tables_t",
    )(positions.reshape(batch, 1, seq), inv)


def _rope128(x, cos_t, sin_t):
    lane = lax.broadcasted_iota(jnp.int32, x.shape, 1)
    first = (lane & (HEAD_DIM - 1)) < HEAD_DIM // 2
    partner = jnp.where(first, pltpu.roll(x, LANES - HEAD_DIM // 2, 1), pltpu.roll(x, HEAD_DIM // 2, 1))
    return x * cos_t + partner * sin_t


def _rglru_kernel(xr_ref, yr_ref, cw_ref, cb_ref, wr_ref, br_ref, wi_ref, bi_ref, lam_ref, o_ref, h_sc, tail_sc):
    @pl.when(pl.program_id(1) == 0)
    def _():
        h_sc[...] = jnp.zeros_like(h_sc)
        tail_sc[...] = jnp.zeros_like(tail_sc)

    xr = xr_ref[...].astype(F32)
    t_len, d = xr.shape
    tail = tail_sc[...]
    row8 = lax.broadcasted_iota(jnp.int32, (8, d), 0)
    cw = cw_ref[...]
    xc = cb_ref[...] + xr * cw[CONV_W - 1 : CONV_W, :]
    for k in range(1, CONV_W):
        rolled = pltpu.roll(xr, k, 0)
        head = jnp.where(row8 < k, pltpu.roll(tail, k, 0), rolled[0:8])
        shifted = jnp.concatenate([head, rolled[8:]], axis=0)
        xc = xc + shifted * cw[CONV_W - 1 - k : CONV_W - k, :]
    tail_sc[...] = xr[t_len - 8 :]

    xcb = xc.astype(BF16)
    rl, il = [], []
    for n in range(LRU_BLOCKS):
        xb = xcb[:, n * LRU_BW : (n + 1) * LRU_BW]
        rl.append(jnp.dot(xb, wr_ref[n], preferred_element_type=F32))
        il.append(jnp.dot(xb, wi_ref[n], preferred_element_type=F32))
    r = _sigmoid(jnp.concatenate(rl, axis=1) + br_ref[...])
    ig = _sigmoid(jnp.concatenate(il, axis=1) + bi_ref[...])
    softplus_neg_lam = jnp.log1p(jnp.exp(-lam_ref[...]))
    log_a = (-LRU_C * softplus_neg_lam) * r
    a = jnp.exp(log_a)
    one_minus_a2 = 1.0 - a * a
    root = jnp.where(one_minus_a2 > 0.0, one_minus_a2 * lax.rsqrt(one_minus_a2), 0.0)
    b = root * (ig * xc)

    sub = lax.broadcasted_iota(jnp.int32, (t_len, d), 0) & (SUBLANES - 1)
    step = 1
    while step < SUBLANES:
        keep = sub >= step
        a_sh = jnp.where(keep, pltpu.roll(a, step, 0), 1.0)
        b_sh = jnp.where(keep, pltpu.roll(b, step, 0), 0.0)
        b = a * b_sh + b
        a = a * a_sh
        step *= 2
    h_prev = h_sc[...]
    groups = []
    for g in range(t_len // SUBLANES):
        sl = slice(g * SUBLANES, (g + 1) * SUBLANES)
        groups.append(b[sl] + a[sl] * h_prev)
        h_prev = jnp.broadcast_to(groups[-1][SUBLANES - 1 :], (SUBLANES, d))
    h_sc[...] = h_prev
    h = jnp.concatenate(groups, axis=0)
    o_ref[...] = (h * _gelu_tanh(yr_ref[...].astype(F32))).astype(o_ref.dtype)


def rglru_branch(proj, batch, seq, conv_w, conv_b, wr, br, wi, bi, lam, t_len):
    d = D_MODEL
    nt = seq // t_len
    vec = lambda v: v.reshape(1, d)
    full2 = lambda shape: pl.BlockSpec(shape, lambda b, t: (0,) * len(shape))
    return pl.pallas_call(
        _rglru_kernel,
        grid=(batch, nt),
        in_specs=[
            pl.BlockSpec((t_len, d), lambda b, t: (b * nt + t, COL_XR // d)),
            pl.BlockSpec((t_len, d), lambda b, t: (b * nt + t, COL_YR // d)),
            full2((CONV_W, d)), full2((1, d)),
            full2((LRU_BLOCKS, LRU_BW, LRU_BW)), full2((1, d)),
            full2((LRU_BLOCKS, LRU_BW, LRU_BW)), full2((1, d)),
            full2((1, d)),
        ],
        out_specs=pl.BlockSpec((t_len, d), lambda b, t: (b * nt + t, 0)),
        out_shape=jax.ShapeDtypeStruct((batch * seq, d), BF16),
        scratch_shapes=[pltpu.VMEM((8, d), F32), pltpu.VMEM((8, d), F32)],
        compiler_params=_cparams(("arbitrary", "arbitrary")),
        name="rglru",
    )(proj, proj, conv_w, vec(conv_b), wr.astype(BF16), vec(br), wi.astype(BF16), vec(bi), vec(lam))


def _kv_prep_kernel(kvc_ref, kvs_ref, kvw_ref, cos_ref, sin_ref, kco_ref, vco_ref, kso_ref, vso_ref, kwo_ref, vwo_ref):
    cos_t, sin_t = cos_ref[...], sin_ref[...]
    t_len = cos_t.shape[0]
    xc = kvc_ref[...]
    for g in range(N_KV):
        kco_ref[0, g] = xc[:, g * HEAD_DIM : (g + 1) * HEAD_DIM]
        vco_ref[0, g] = xc[:, KV_W + g * HEAD_DIM : KV_W + (g + 1) * HEAD_DIM]
    pad_row = lax.broadcasted_iota(jnp.int32, (V_ROWS - HEAD_DIM, t_len), 0)
    pad_rows = jnp.where(pad_row == 0, 1.0, 0.0).astype(BF16)
    for src, k_dst, v_dst in ((kvs_ref, kso_ref, vso_ref), (kvw_ref, kwo_ref, vwo_ref)):
        x = src[...].astype(F32)
        roped = [_rope128(x[:, c * LANES : (c + 1) * LANES], cos_t, sin_t) for c in range(KV_W // LANES)]
        v_t = x[:, KV_W:].T
        for g in range(N_KV):
            pair = roped[g * HEAD_DIM // LANES]
            lo = g * HEAD_DIM % LANES
            k_dst[0, g] = pair[:, lo : lo + HEAD_DIM].astype(BF16)
            v_dst[0, g, :HEAD_DIM, :] = v_t[g * HEAD_DIM : (g + 1) * HEAD_DIM].astype(BF16)
            v_dst[0, g, HEAD_DIM:, :] = pad_rows


def kv_prep(proj, cos_t, sin_t, batch, seq, tm):
    nt = seq // tm
    blk = lambda col: pl.BlockSpec((tm, 2 * KV_W), lambda b, t: (b * nt + t, col // (2 * KV_W)))
    tab = pl.BlockSpec((tm, LANES), lambda b, t: (b * nt + t, 0))
    k_spec = pl.BlockSpec((1, N_KV, tm, HEAD_DIM), lambda b, t: (b, 0, t, 0))
    v_spec = pl.BlockSpec((1, N_KV, V_ROWS, tm), lambda b, t: (b, 0, 0, t))
    k_shape = jax.ShapeDtypeStruct((batch, N_KV, seq, HEAD_DIM), BF16)
    v_shape = jax.ShapeDtypeStruct((batch, N_KV, V_ROWS, seq), BF16)
    return pl.pallas_call(
        _kv_prep_kernel,
        grid=(batch, nt),
        in_specs=[blk(COL_KV), blk(COL_KV + 2 * KV_W), blk(COL_KV + 4 * KV_W), tab, tab],
        out_specs=[k_spec, k_spec, k_spec, v_spec, k_spec, v_spec],
        out_shape=[k_shape, k_shape, k_shape, v_shape, k_shape, v_shape],
        compiler_params=_cparams(("arbitrary", "arbitrary")),
        name="kv_prep",
    )(proj, proj, proj, cos_t, sin_t)


def _compress_kernel(x_ref, pe_ref, w1_ref, b1_ref, w2_ref, cos_ref, sin_ref, o_ref, *, rotary):
    x = x_ref[0]
    half = CMP_STRIDE * HEAD_DIM
    pe = pe_ref[0]
    w1 = w1_ref[0]
    u = jnp.dot((x + pe[0:1]).astype(BF16), w1[:half], preferred_element_type=F32)
    v = jnp.dot((x + pe[1:2]).astype(BF16), w1[half:], preferred_element_type=F32)
    hid = _gelu_tanh(u + pltpu.roll(v, v.shape[0] - 1, 0) + b1_ref[0])
    out = jnp.dot(hid.astype(BF16), w2_ref[0], preferred_element_type=F32)
    if rotary:
        hh = HEAD_DIM // 2
        partner = jnp.concatenate([out[:, hh:], out[:, :hh]], axis=1)
        out = out * cos_ref[0][:, :HEAD_DIM] + partner * sin_ref[0][:, :HEAD_DIM]
    o_ref[0] = out.astype(o_ref.dtype)


def compress(x_chunks, j, pe, w1, b1, w2, cos_c, sin_c, batch, rotary):
    bg, nc, width = x_chunks.shape
    return pl.pallas_call(
        functools.partial(_compress_kernel, rotary=rotary),
        grid=(bg,),
        in_specs=[
            pl.BlockSpec((1, nc, width), lambda i: (i, 0, 0)),
            pl.BlockSpec((1, 2, width), lambda i: (j, 0, 0)),
            pl.BlockSpec((1, 2 * width, w1.shape[2]), lambda i: (j, 0, 0)),
            pl.BlockSpec((1, 1, w1.shape[2]), lambda i: (j, 0, 0)),
            pl.BlockSpec((1, w2.shape[1], HEAD_DIM), lambda i: (j, 0, 0)),
            pl.BlockSpec((1, nc, LANES), lambda i: (i // N_KV, 0, 0)),
            pl.BlockSpec((1, nc, LANES), lambda i: (i // N_KV, 0, 0)),
        ],
        out_specs=pl.BlockSpec((1, nc, HEAD_DIM), lambda i: (i, 0, 0)),
        out_shape=jax.ShapeDtypeStruct((bg, nc, HEAD_DIM), BF16),
        compiler_params=_cparams(("arbitrary",)),
        name="compress_k" if rotary else "compress_v",
    )(x_chunks, pe, w1, b1, w2, cos_c, sin_c)


def _nsa_kernel(q_ref, gate_ref, cos_ref, sin_ref, kc_ref, vct_ref, ks_ref, vst_ref, kw_ref, vwt_ref, ovt_ref,
                o_ref, sel_sc, sa_sc, sb_sc, oc_sc, imp_sc):
    qb = Q_BLOCK
    s0 = pl.program_id(2) * qb
    nc = kc_ref.shape[2]
    hh = HEAD_DIM // 2
    cols = GROUP * qb

    qt = q_ref[...].astype(F32).T
    cos_t, sin_t = cos_ref[0], sin_ref[0]
    heads = []
    for r in range(GROUP):
        x1 = qt[r * HEAD_DIM : r * HEAD_DIM + hh]
        x2 = qt[r * HEAD_DIM + hh : (r + 1) * HEAD_DIM]
        heads.append(jnp.concatenate([x1 * cos_t - x2 * sin_t, x2 * cos_t + x1 * sin_t], axis=0))
    q_t = (jnp.concatenate(heads, axis=1) * (HEAD_DIM ** -0.5 * LOG2_E)).astype(BF16)

    tq = s0 + lax.broadcasted_iota(jnp.int32, (1, qb), 1)

    def compressed(n_c):
        cmp_end = lax.broadcasted_iota(jnp.int32, (n_c, qb), 0) * CMP_STRIDE + (CMP_LEN - 1)
        bias_c = jnp.where(cmp_end <= tq, 0.0, NEG)
        has_key = jnp.where(tq >= CMP_LEN - 1, 1.0, 0.0)
        sc = jnp.dot(kc_ref[0, 0, :n_c, :], q_t, preferred_element_type=F32) + jnp.concatenate([bias_c] * GROUP, axis=1)
        p_c = jnp.exp2(sc - jnp.max(sc, axis=0, keepdims=True))
        norm = jnp.concatenate([has_key] * GROUP, axis=1) / jnp.maximum(jnp.sum(p_c, axis=0, keepdims=True), 1e-30)
        p_c = p_c * norm
        oc_sc[...] = jnp.dot(vct_ref[0, 0, :, :n_c], p_c.astype(BF16), preferred_element_type=F32)
        p_sum = p_c[:, :qb]
        for r in range(1, GROUP):
            p_sum = p_sum + p_c[:, r * qb : (r + 1) * qb]
        p_hi = p_sum.astype(BF16)
        p_lo = (p_sum - p_hi.astype(F32)).astype(BF16)
        ovt = ovt_ref[:, :n_c]
        imp_sc[...] = (jnp.dot(ovt, p_hi, preferred_element_type=F32)
                       + jnp.dot(ovt, p_lo, preferred_element_type=F32))

    last_cmp = (s0 + qb - CMP_LEN) // CMP_STRIDE
    for v in range(nc // LANES):
        pl.when(last_cmp // LANES == v)(functools.partial(compressed, (v + 1) * LANES))
    o_c = oc_sc[...]
    imp = imp_sc[...]

    w0 = pl.multiple_of(jnp.maximum(s0 - WINDOW, 0), LANES)
    wpos = w0 + lax.broadcasted_iota(jnp.int32, (WIN_KEYS, qb), 0)
    bias_w = jnp.where((wpos <= tq) & (wpos > tq - WINDOW), 0.0, NEG)
    kw_tile = kw_ref[0, 0, pl.ds(w0, WIN_KEYS), :]
    vw_tile = vwt_ref[0, 0, :, pl.ds(w0, WIN_KEYS)]
    s_w = jnp.dot(kw_tile, q_t, preferred_element_type=F32) + jnp.concatenate([bias_w] * GROUP, axis=1)
    p_w = jnp.exp2(s_w - jnp.max(s_w, axis=0, keepdims=True)).astype(BF16)
    acc_w = jnp.dot(vw_tile, p_w, preferred_element_type=F32)
    o_w = acc_w[:HEAD_DIM] / jnp.maximum(acc_w[HEAD_DIM : HEAD_DIM + 1], 1e-30)

    blk = lax.broadcasted_iota(jnp.int32, (LANES, qb), 0)
    forced = (blk == 0) | (blk == (tq >> SEL_SHIFT))
    work = jnp.where(forced, FORCE, jnp.where(blk * SEL_LEN <= tq, imp, -FORCE))

    blk_f = blk.astype(F32)
    sel = jnp.zeros((LANES, qb), F32)
    for _ in range(N_SELECT):
        m = jnp.max(work, axis=0, keepdims=True)
        idx = jnp.min(jnp.where(work == m, blk_f, float(LANES)), axis=0, keepdims=True)
        hit = blk_f == idx
        sel = jnp.where(hit, 1.0, sel)
        work = jnp.where(hit, -3e38, work)
    sel_sc[...] = sel

    blocks_per_tile = SEL_TILE // SEL_LEN
    row_minus_lane = (lax.broadcasted_iota(jnp.int32, (SEL_TILE, qb), 0)
                      - lax.broadcasted_iota(jnp.int32, (SEL_TILE, qb), 1))
    n_tiles = (s0 + qb - 1) // SEL_TILE + 1

    def scores(kb):
        k0 = pl.multiple_of(kb * SEL_TILE, SEL_TILE)
        picked = sel_sc[pl.ds(pl.multiple_of(kb * blocks_per_tile, blocks_per_tile), blocks_per_tile), :]
        picked = jnp.where(picked > 0.5, 0.0, NEG)
        picked = jnp.concatenate(
            [jnp.broadcast_to(picked[j : j + 1, :], (SEL_LEN, qb)) for j in range(blocks_per_tile)], axis=0)
        bias = jnp.where(row_minus_lane <= s0 - k0, picked, NEG)
        s_ = jnp.dot(ks_ref[0, 0, pl.ds(k0, SEL_TILE), :], q_t, preferred_element_type=F32)
        return s_ + jnp.concatenate([bias] * GROUP, axis=1)

    def absorb(kb, s_, m_run, acc):
        k0 = pl.multiple_of(kb * SEL_TILE, SEL_TILE)
        m_new = jnp.maximum(m_run, jnp.max(s_, axis=0, keepdims=True))
        p = jnp.exp2(s_ - m_new).astype(BF16)
        pv = jnp.dot(vst_ref[0, 0, :, pl.ds(k0, SEL_TILE)], p, preferred_element_type=F32)
        return m_new, jnp.exp2(m_run - m_new) * acc + pv

    sa_sc[...] = scores(0)

    def sel_step(j, carry):
        m_run, acc = carry
        sb_sc[...] = scores(2 * j + 1)
        m_run, acc = absorb(2 * j, sa_sc[...], m_run, acc)
        sa_sc[...] = scores(2 * j + 2)
        return absorb(2 * j + 1, sb_sc[...], m_run, acc)

    init = (jnp.full((1, cols), NEG, F32), jnp.zeros((V_ROWS, cols), F32))
    full_trips = (n_tiles - 1) // 2
    m_run, acc_s = lax.fori_loop(0, full_trips, sel_step, init)
    m_run, acc_s = absorb(2 * full_trips, sa_sc[...], m_run, acc_s)

    def last_tile(carry):
        return absorb(n_tiles - 1, scores(n_tiles - 1), *carry)

    _, acc_s = lax.cond(n_tiles - 2 * full_trips == 2, last_tile, lambda carry: carry, (m_run, acc_s))

    o_s = acc_s[:HEAD_DIM] / jnp.maximum(acc_s[HEAD_DIM : HEAD_DIM + 1], 1e-30)

    gates_t = _sigmoid(gate_ref[...].astype(F32)).T
    outs = []
    for r in range(GROUP):
        sl = slice(r * qb, (r + 1) * qb)
        outs.append(gates_t[3 * r : 3 * r + 1] * o_c[:, sl] + gates_t[3 * r + 1 : 3 * r + 2] * o_s[:, sl]
                    + gates_t[3 * r + 2 : 3 * r + 3] * o_w[:, sl])
    o_ref[...] = jnp.concatenate(outs, axis=0).T.astype(o_ref.dtype)


def nsa_attention(proj, cos_q, sin_q, kc, vct, ks, vst, kw, vwt, overlap_t, batch, seq):
    nq = seq // Q_BLOCK
    nc = kc.shape[2]
    gw = GROUP * HEAD_DIM
    per_bg = lambda shape: pl.BlockSpec((1, 1) + shape, lambda b, g, i: (b, g, 0, 0))
    return pl.pallas_call(
        _nsa_kernel,
        grid=(batch, N_KV, nq),
        in_specs=[
            pl.BlockSpec((Q_BLOCK, gw), lambda b, g, i: (b * nq + i, COL_Q // gw + g)),
            pl.BlockSpec((Q_BLOCK, LANES), lambda b, g, i: (b * nq + i, COL_GN // LANES + g)),
            pl.BlockSpec((1, HEAD_DIM // 2, Q_BLOCK), lambda b, g, i: (b, 0, i)),
            pl.BlockSpec((1, HEAD_DIM // 2, Q_BLOCK), lambda b, g, i: (b, 0, i)),
            per_bg((nc, HEAD_DIM)), per_bg((HEAD_DIM, nc)),
            per_bg((seq, HEAD_DIM)), per_bg((V_ROWS, seq)),
            per_bg((seq, HEAD_DIM)), per_bg((V_ROWS, seq)),
            pl.BlockSpec((LANES, nc), lambda b, g, i: (0, 0)),
        ],
        out_specs=pl.BlockSpec((Q_BLOCK, gw), lambda b, g, i: (b * nq + i, g)),
        out_shape=jax.ShapeDtypeStruct((batch * seq, N_HEADS * HEAD_DIM), BF16),
        scratch_shapes=[pltpu.VMEM((LANES, Q_BLOCK), F32), pltpu.VMEM((SEL_TILE, GROUP * Q_BLOCK), F32),
                        pltpu.VMEM((SEL_TILE, GROUP * Q_BLOCK), F32), pltpu.VMEM((HEAD_DIM, GROUP * Q_BLOCK), F32),
                        pltpu.VMEM((LANES, Q_BLOCK), F32)],
        compiler_params=_cparams(("arbitrary", "arbitrary", "arbitrary")),
        name="nsa_attention",
    )(proj, proj, cos_q, sin_q, kc, vct, ks, vst, kw, vwt, overlap_t)


def _mem_kernel(q_ref, kv_ref, o_ref):
    q = q_ref[...]
    kv = kv_ref[0]
    mem_w = MEM_HEADS * MEM_HEAD_DIM
    outs = []
    for h in range(MEM_HEADS):
        sl = slice(h * MEM_HEAD_DIM, (h + 1) * MEM_HEAD_DIM)
        qh = (q[:, sl] * (MEM_HEAD_DIM ** -0.5)).astype(BF16)
        s = lax.dot_general(qh, kv[:, sl], (((1,), (1,)), ((), ())), preferred_element_type=F32)
        p = jnp.exp(s - jnp.max(s, axis=-1, keepdims=True))
        p = p / jnp.sum(p, axis=-1, keepdims=True)
        outs.append(jnp.dot(p.astype(BF16), kv[:, mem_w + h * MEM_HEAD_DIM : mem_w + (h + 1) * MEM_HEAD_DIM],
                            preferred_element_type=F32))
    o_ref[...] = jnp.concatenate(outs, axis=1).astype(o_ref.dtype)


def memory_attention(proj, kv_mem, batch, seq, tm):
    nt = seq // tm
    mem_len = kv_mem.shape[1]
    return pl.pallas_call(
        _mem_kernel,
        grid=(batch, nt),
        in_specs=[
            pl.BlockSpec((tm, D_MODEL), lambda b, t: (b * nt + t, COL_QM // D_MODEL)),
            pl.BlockSpec((1, mem_len, 2 * D_MODEL), lambda b, t: (b, 0, 0)),
        ],
        out_specs=pl.BlockSpec((tm, D_MODEL), lambda b, t: (b * nt + t, 0)),
        out_shape=jax.ShapeDtypeStruct((batch * seq, D_MODEL), BF16),
        compiler_params=_cparams(("arbitrary", "arbitrary")),
        name="memory_attention",
    )(proj, kv_mem)


def _merge_kernel(oa_ref, ob_ref, oc_ref, ga_ref, gb_ref, gc_ref, wa_ref, wb_ref, wc_ref, wo_ref,
                  gpost_ref, x_ref, gnext_ref, xo_ref, ho_ref):
    gate = lambda ref: _sigmoid(ref[...].astype(F32))
    merged = gate(ga_ref) * jnp.dot(oa_ref[...], wa_ref[...], preferred_element_type=F32)
    merged += gate(gb_ref) * jnp.dot(ob_ref[...], wb_ref[...], preferred_element_type=F32)
    merged += gate(gc_ref) * jnp.dot(oc_ref[...], wc_ref[...], preferred_element_type=F32)
    y = jnp.dot(merged.astype(BF16), wo_ref[...], preferred_element_type=F32)
    x_new = x_ref[...] + _rms(y, gpost_ref[...])
    xo_ref[...] = x_new
    ho_ref[...] = _rms(x_new, gnext_ref[...]).astype(ho_ref.dtype)


def merge_out(oa, ob, oc, proj, wa, wb, wc, wo, layer, g_post, x, g_next, tm):
    n, d = x.shape
    row = lambda col: pl.BlockSpec((tm, d), lambda i: (i, col))
    full = lambda shape: pl.BlockSpec(shape, lambda i: (0, 0))
    w_spec = pl.BlockSpec((None, d, d), lambda i: (layer, 0, 0))
    gm = COL_GM // d
    return pl.pallas_call(
        _merge_kernel,
        grid=(n // tm,),
        in_specs=[row(0), row(0), row(0), row(gm), row(gm + 1), row(gm + 2),
                  w_spec, w_spec, w_spec, w_spec, full((1, d)), row(0), full((1, d))],
        out_specs=[row(0), row(0)],
        out_shape=[jax.ShapeDtypeStruct((n, d), F32), jax.ShapeDtypeStruct((n, d), BF16)],
        compiler_params=_cparams(("arbitrary",)),
        name="merge_out",
    )(oa, ob, oc, proj, proj, proj, wa, wb, wc, wo, g_post.reshape(1, d), x, g_next.reshape(1, d))


def _mlp_kernel(h_ref, w1_ref, w2_ref, gpost_ref, x_ref, gnext_ref, xo_ref, ho_ref, acc_ref):
    k = pl.program_id(1)

    @pl.when(k == 0)
    def _():
        acc_ref[...] = jnp.zeros_like(acc_ref)

    u = jnp.maximum(jnp.dot(h_ref[...], w1_ref[...], preferred_element_type=F32), 0.0)
    acc_ref[...] += jnp.dot((u * u).astype(BF16), w2_ref[...], preferred_element_type=F32)

    @pl.when(k == pl.num_programs(1) - 1)
    def _():
        x_new = x_ref[...] + _rms(acc_ref[...], gpost_ref[...])
        xo_ref[...] = x_new
        ho_ref[...] = _rms(x_new, gnext_ref[...]).astype(ho_ref.dtype)


def mlp(h, w1, w2, layer, g_post, x, g_next, tm, tf):
    n, d = x.shape
    ff = w1.shape[2]
    return pl.pallas_call(
        _mlp_kernel,
        grid=(n // tm, ff // tf),
        in_specs=[
            pl.BlockSpec((tm, d), lambda i, k: (i, 0)),
            pl.BlockSpec((None, d, tf), lambda i, k: (layer, 0, k)),
            pl.BlockSpec((None, tf, d), lambda i, k: (layer, k, 0)),
            pl.BlockSpec((1, d), lambda i, k: (0, 0)),
            pl.BlockSpec((tm, d), lambda i, k: (i, 0)),
            pl.BlockSpec((1, d), lambda i, k: (0, 0)),
        ],
        out_specs=[pl.BlockSpec((tm, d), lambda i, k: (i, 0))] * 2,
        out_shape=[jax.ShapeDtypeStruct((n, d), F32), jax.ShapeDtypeStruct((n, d), BF16)],
        scratch_shapes=[pltpu.VMEM((tm, d), F32)],
        compiler_params=_cparams(("arbitrary", "arbitrary")),
        name="mlp",
    )(h, w1, w2, g_post.reshape(1, d), x, g_next.reshape(1, d))


def _pack_w_in(w):
    o_kv, o_gn, o_qm, o_gm = 3072, 4608, 4656, 5680
    per_group = GROUP * 3
    lead = w.shape[:-1]
    gn = w[..., o_gn:o_qm].reshape(lead + (N_KV, per_group))
    gn = jnp.pad(gn, ((0, 0),) * (len(lead) + 1) + ((0, LANES - per_group),)).reshape(lead + (N_KV * LANES,))
    packed = jnp.concatenate([w[..., :o_kv], w[..., o_qm:o_gm], w[..., o_gm:], w[..., o_kv:o_gn], gn], axis=-1)
    assert packed.shape[-1] == D_INP
    return packed.astype(BF16)


def _overlap_matrix_t(nc):
    c0 = np.arange(nc)[:, None] * CMP_STRIDE
    s0 = np.arange(LANES)[None, :] * SEL_LEN
    ov = np.clip(np.minimum(c0 + CMP_LEN, s0 + SEL_LEN) - np.maximum(c0, s0), 0, None).astype(np.float32) / CMP_LEN
    return jnp.asarray(ov.T, dtype=BF16)


def kernel(x, mem, positions, ln_mix_pre, w_in, conv_w, conv_b, lru_wr, lru_br, lru_wi, lru_bi, lru_lambda, cmp_pe, cmp_w1, cmp_b1, cmp_w2, ln_mem, w_mem_kv, w_br_rnn, w_br_nsa, w_br_mem, w_out, ln_mix_post, ln_mlp_pre, mlp_w1, mlp_w2, ln_mlp_post):
    batch, seq, d = x.shape
    depth = w_in.shape[0]
    mem_len = mem.shape[1]
    n = batch * seq
    nc = seq // CMP_STRIDE
    assert d == D_MODEL and seq % (2 * SEL_TILE) == 0 and seq // SEL_LEN <= LANES and seq >= WIN_KEYS

    tm = min(512, seq)
    tm_big = min(1024, seq)
    xf = x.reshape(n, d)
    memf = mem.reshape(batch * mem_len, d)
    cos_t, sin_t = rope_tables(positions.reshape(n), tm)
    pos_c = jnp.pad(positions[:, CMP_LEN - 1 :: CMP_STRIDE], ((0, 0), (0, 1)))
    cos_c, sin_c = rope_tables(pos_c.reshape(batch * nc), nc)
    cos_c, sin_c = cos_c.reshape(batch, nc, LANES), sin_c.reshape(batch, nc, LANES)
    cos_q, sin_q = rope_tables_t(positions, tm)
    overlap_t = _overlap_matrix_t(nc)

    w_in_b, w_kv_b = _pack_w_in(w_in.astype(BF16)), w_mem_kv.astype(BF16)
    wr_b, wi_b = lru_wr.astype(BF16), lru_wi.astype(BF16)
    w1c_b, w2c_b = cmp_w1.astype(BF16), cmp_w2.astype(BF16)
    wa_b, wb_b, wc_b, wo_b = (w.astype(BF16) for w in (w_br_rnn, w_br_nsa, w_br_mem, w_out))
    w1_b, w2_b = mlp_w1.astype(BF16), mlp_w2.astype(BF16)

    h = rmsnorm_bf16(xf, ln_mix_pre[0], tm)
    for l in range(depth):
        proj = matmul(h, w_in_b, l, tm_big, 1024, BF16, "in_proj")

        o_a = rglru_branch(proj, batch, seq, conv_w[l], conv_b[l], wr_b[l], lru_br[l], wi_b[l], lru_bi[l],
                           lru_lambda[l], min(256, seq))

        kc_raw, vc_raw, ks, vst, kw, vwt = kv_prep(proj, cos_t, sin_t, batch, seq, tm)
        chunks = lambda a: a.reshape(batch * N_KV, nc, CMP_STRIDE * HEAD_DIM)
        pe = cmp_pe[l].reshape(2, 2, CMP_STRIDE * HEAD_DIM)
        b1c = cmp_b1[l].reshape(2, 1, -1)
        k_cmp = compress(chunks(kc_raw), 0, pe, w1c_b[l], b1c, w2c_b[l], cos_c, sin_c, batch, True)
        v_cmp = compress(chunks(vc_raw), 1, pe, w1c_b[l], b1c, w2c_b[l], cos_c, sin_c, batch, False)
        kc = k_cmp.reshape(batch, N_KV, nc, HEAD_DIM)
        vct = v_cmp.reshape(batch, N_KV, nc, HEAD_DIM).transpose(0, 1, 3, 2)
        o_b = nsa_attention(proj, cos_q, sin_q, kc, vct, ks, vst, kw, vwt, overlap_t, batch, seq)

        mem_h = rmsnorm_bf16(memf, ln_mem[l], mem_len)
        kv_mem = matmul(mem_h, w_kv_b, l, mem_len, 1024, BF16, "mem_kv")
        o_c = memory_attention(proj, kv_mem.reshape(batch, mem_len, 2 * D_MODEL), batch, seq, tm)

        xf, h = merge_out(o_a, o_b, o_c, proj, wa_b, wb_b, wc_b, wo_b, l, ln_mix_post[l], xf, ln_mlp_pre[l], tm)
        xf, h = mlp(h, w1_b, w2_b, l, ln_mlp_post[l], xf, ln_mix_pre[(l + 1) % depth], tm_big, 1024)
    return xf.reshape(batch, seq, d)
```

```python
import functools

import jax
import jax.numpy as jnp
import numpy as np
from jax import lax
from jax.experimental import pallas as pl
from jax.experimental.pallas import tpu as pltpu

F32 = jnp.float32
BF16 = jnp.bfloat16

D_MODEL = 1024
LRU_BLOCKS = 8
LRU_BW = D_MODEL // LRU_BLOCKS
CONV_W = 4
LRU_C = 8.0
N_HEADS = 16
HEAD_DIM = 64
N_KV = 4
GROUP = N_HEADS // N_KV
KV_W = N_KV * HEAD_DIM
CMP_STRIDE = 16
CMP_LEN = 32
SEL_LEN = 64
SEL_SHIFT = 6
N_SELECT = 16
WINDOW = 512
Q_BLOCK = 256
MEM_HEADS = 4
MEM_HEAD_DIM = D_MODEL // MEM_HEADS
D_FF = 4 * D_MODEL
ROPE_THETA = 10000.0
EPS = 1e-6
NEG = -1e30
FORCE = 1e4
LOG2_E = 1.4426950408889634
MAX_OFFSET = 50.0
BOUND_SLACK = 1.01

LANES = 128
SUBLANES = 8
SEL_TILE = 512
WIN_KEYS = WINDOW + Q_BLOCK
V_ROWS = 80
VMEM_LIMIT = 48 * 1024 * 1024

COL_XR, COL_YR, COL_Q, COL_QM, COL_GM, COL_KV, COL_GN = 0, 1024, 2048, 3072, 4096, 7168, 8704
D_INP = 9216


def _cparams(sem):
    return pltpu.CompilerParams(dimension_semantics=sem, vmem_limit_bytes=VMEM_LIMIT)


def _sigmoid(x):
    return 0.5 * jnp.tanh(0.5 * x) + 0.5


def _gelu_tanh(x):
    return 0.5 * x * (1.0 + jnp.tanh(0.7978845608028654 * (x + 0.044715 * (x * x * x))))


def _rms(x, g):
    return x * lax.rsqrt(jnp.mean(x * x, axis=-1, keepdims=True) + EPS) * g


def _masked_softmax(sc, mask):
    sc = jnp.where(mask, sc, NEG)
    m = jnp.max(sc, axis=-1, keepdims=True)
    p = jnp.where(mask, jnp.exp(sc - m), 0.0)
    return p / jnp.maximum(jnp.sum(p, axis=-1, keepdims=True), 1e-30)


def _rms_kernel(x_ref, g_ref, o_ref):
    o_ref[...] = _rms(x_ref[...], g_ref[...]).astype(o_ref.dtype)


def rmsnorm_bf16(x, g, tm):
    m, d = x.shape
    return pl.pallas_call(
        _rms_kernel,
        grid=(m // tm,),
        in_specs=[pl.BlockSpec((tm, d), lambda i: (i, 0)), pl.BlockSpec((1, d), lambda i: (0, 0))],
        out_specs=pl.BlockSpec((tm, d), lambda i: (i, 0)),
        out_shape=jax.ShapeDtypeStruct((m, d), BF16),
        compiler_params=_cparams(("arbitrary",)),
        name="rmsnorm",
    )(x, g.reshape(1, d))


def _mm_kernel(a_ref, w_ref, o_ref):
    o_ref[...] = jnp.dot(a_ref[...], w_ref[...], preferred_element_type=F32).astype(o_ref.dtype)


def matmul(a, w, layer, tm, tn, out_dtype, name):
    m, k = a.shape
    n = w.shape[2]
    return pl.pallas_call(
        _mm_kernel,
        grid=(n // tn, m // tm),
        in_specs=[pl.BlockSpec((tm, k), lambda j, i: (i, 0)), pl.BlockSpec((None, k, tn), lambda j, i: (layer, 0, j))],
        out_specs=pl.BlockSpec((tm, tn), lambda j, i: (i, j)),
        out_shape=jax.ShapeDtypeStruct((m, n), out_dtype),
        compiler_params=_cparams(("arbitrary", "arbitrary")),
        name=name,
    )(a, w)


def _rope_table_kernel(pos_ref, inv_ref, cos_ref, sin_ref):
    ang = pos_ref[...].astype(F32) * inv_ref[...]
    lane = lax.broadcasted_iota(jnp.int32, ang.shape, 1)
    cos_ref[...] = jnp.cos(ang)
    sin_ref[...] = jnp.where((lane & (HEAD_DIM - 1)) < HEAD_DIM // 2, -1.0, 1.0) * jnp.sin(ang)


def rope_tables(pos_flat, tm):
    n = pos_flat.shape[0]
    half = HEAD_DIM // 2
    inv = ROPE_THETA ** (-jnp.arange(half, dtype=F32) * 2.0 / HEAD_DIM)
    inv_full = jnp.tile(inv, LANES // half).reshape(1, LANES)
    return pl.pallas_call(
        _rope_table_kernel,
        grid=(n // tm,),
        in_specs=[pl.BlockSpec((tm, 1), lambda i: (i, 0)), pl.BlockSpec((1, LANES), lambda i: (0, 0))],
        out_specs=[pl.BlockSpec((tm, LANES), lambda i: (i, 0))] * 2,
        out_shape=[jax.ShapeDtypeStruct((n, LANES), F32)] * 2,
        compiler_params=_cparams(("arbitrary",)),
        name="rope_tables",
    )(pos_flat.reshape(n, 1), inv_full)


def _rope_table_t_kernel(pos_ref, inv_ref, cos_ref, sin_ref):
    ang = inv_ref[...] * pos_ref[0].astype(F32)
    cos_ref[0] = jnp.cos(ang)
    sin_ref[0] = jnp.sin(ang)


def rope_tables_t(positions, tm):
    batch, seq = positions.shape
    half = HEAD_DIM // 2
    inv = (ROPE_THETA ** (-jnp.arange(half, dtype=F32) * 2.0 / HEAD_DIM)).reshape(half, 1)
    return pl.pallas_call(
        _rope_table_t_kernel,
        grid=(batch, seq // tm),
        in_specs=[pl.BlockSpec((1, 1, tm), lambda b, i: (b, 0, i)), pl.BlockSpec((half, 1), lambda b, i: (0, 0))],
        out_specs=[pl.BlockSpec((1, half, tm), lambda b, i: (b, 0, i))] * 2,
        out_shape=[jax.ShapeDtypeStruct((batch, half, seq), F32)] * 2,
        compiler_params=_cparams(("arbitrary", "arbitrary")),
        name="rope_tables_t",
    )(positions.reshape(batch, 1, seq), inv)


def _rope128(x, cos_t, sin_t):
    lane = lax.broadcasted_iota(jnp.int32, x.shape, 1)
    first = (lane & (HEAD_DIM - 1)) < HEAD_DIM // 2
    partner = jnp.where(first, pltpu.roll(x, LANES - HEAD_DIM // 2, 1), pltpu.roll(x, HEAD_DIM // 2, 1))
    return x * cos_t + partner * sin_t


def _rglru_kernel(xr_ref, yr_ref, cw_ref, cb_ref, wr_ref, br_ref, wi_ref, bi_ref, lam_ref, o_ref, h_sc, tail_sc):
    @pl.when(pl.program_id(1) == 0)
    def _():
        h_sc[...] = jnp.zeros_like(h_sc)
        tail_sc[...] = jnp.zeros_like(tail_sc)

    xr = xr_ref[...].astype(F32)
    t_len, d = xr.shape
    tail = tail_sc[...]
    row8 = lax.broadcasted_iota(jnp.int32, (8, d), 0)
    cw = cw_ref[...]
    xc = cb_ref[...] + xr * cw[CONV_W - 1 : CONV_W, :]
    for k in range(1, CONV_W):
        rolled = pltpu.roll(xr, k, 0)
        head = jnp.where(row8 < k, pltpu.roll(tail, k, 0), rolled[0:8])
        shifted = jnp.concatenate([head, rolled[8:]], axis=0)
        xc = xc + shifted * cw[CONV_W - 1 - k : CONV_W - k, :]
    tail_sc[...] = xr[t_len - 8 :]

    xcb = xc.astype(BF16)
    rl, il = [], []
    for n in range(LRU_BLOCKS):
        xb = xcb[:, n * LRU_BW : (n + 1) * LRU_BW]
        rl.append(jnp.dot(xb, wr_ref[n], preferred_element_type=F32))
        il.append(jnp.dot(xb, wi_ref[n], preferred_element_type=F32))
    r = _sigmoid(jnp.concatenate(rl, axis=1) + br_ref[...])
    ig = _sigmoid(jnp.concatenate(il, axis=1) + bi_ref[...])
    softplus_neg_lam = jnp.log1p(jnp.exp(-lam_ref[...]))
    log_a = (-LRU_C * softplus_neg_lam) * r
    a = jnp.exp(log_a)
    one_minus_a2 = 1.0 - a * a
    root = jnp.where(one_minus_a2 > 0.0, one_minus_a2 * lax.rsqrt(one_minus_a2), 0.0)
    b = root * (ig * xc)

    sub = lax.broadcasted_iota(jnp.int32, (t_len, d), 0) & (SUBLANES - 1)
    step = 1
    while step < SUBLANES:
        keep = sub >= step
        a_sh = jnp.where(keep, pltpu.roll(a, step, 0), 1.0)
        b_sh = jnp.where(keep, pltpu.roll(b, step, 0), 0.0)
        b = a * b_sh + b
        a = a * a_sh
        step *= 2
    h_prev = h_sc[...]
    groups = []
    for g in range(t_len // SUBLANES):
        sl = slice(g * SUBLANES, (g + 1) * SUBLANES)
        groups.append(b[sl] + a[sl] * h_prev)
        h_prev = jnp.broadcast_to(groups[-1][SUBLANES - 1 :], (SUBLANES, d))
    h_sc[...] = h_prev
    h = jnp.concatenate(groups, axis=0)
    o_ref[...] = (h * _gelu_tanh(yr_ref[...].astype(F32))).astype(o_ref.dtype)


def rglru_branch(proj, batch, seq, conv_w, conv_b, wr, br, wi, bi, lam, t_len):
    d = D_MODEL
    nt = seq // t_len
    vec = lambda v: v.reshape(1, d)
    full2 = lambda shape: pl.BlockSpec(shape, lambda b, t: (0,) * len(shape))
    return pl.pallas_call(
        _rglru_kernel,
        grid=(batch, nt),
        in_specs=[
            pl.BlockSpec((t_len, d), lambda b, t: (b * nt + t, COL_XR // d)),
            pl.BlockSpec((t_len, d), lambda b, t: (b * nt + t, COL_YR // d)),
            full2((CONV_W, d)), full2((1, d)),
            full2((LRU_BLOCKS, LRU_BW, LRU_BW)), full2((1, d)),
            full2((LRU_BLOCKS, LRU_BW, LRU_BW)), full2((1, d)),
            full2((1, d)),
        ],
        out_specs=pl.BlockSpec((t_len, d), lambda b, t: (b * nt + t, 0)),
        out_shape=jax.ShapeDtypeStruct((batch * seq, d), BF16),
        scratch_shapes=[pltpu.VMEM((8, d), F32), pltpu.VMEM((8, d), F32)],
        compiler_params=_cparams(("arbitrary", "arbitrary")),
        name="rglru",
    )(proj, proj, conv_w, vec(conv_b), wr.astype(BF16), vec(br), wi.astype(BF16), vec(bi), vec(lam))


def _kv_prep_kernel(kvc_ref, kvs_ref, kvw_ref, cos_ref, sin_ref, kco_ref, vco_ref, kso_ref, vso_ref, kwo_ref, vwo_ref):
    cos_t, sin_t = cos_ref[...], sin_ref[...]
    t_len = cos_t.shape[0]
    xc = kvc_ref[...]
    for g in range(N_KV):
        kco_ref[0, g] = xc[:, g * HEAD_DIM : (g + 1) * HEAD_DIM]
        vco_ref[0, g] = xc[:, KV_W + g * HEAD_DIM : KV_W + (g + 1) * HEAD_DIM]
    pad_row = lax.broadcasted_iota(jnp.int32, (V_ROWS - HEAD_DIM, t_len), 0)
    pad_rows = jnp.where(pad_row == 0, 1.0, 0.0).astype(BF16)
    pad_lane = lax.broadcasted_iota(jnp.int32, (t_len, LANES - HEAD_DIM), 1)
    pad_lanes = jnp.where(pad_lane == 0, 1.0, 0.0)
    for src, k_dst, v_dst in ((kvs_ref, kso_ref, vso_ref), (kvw_ref, kwo_ref, vwo_ref)):
        x = src[...].astype(F32)
        roped = [_rope128(x[:, c * LANES : (c + 1) * LANES], cos_t, sin_t) for c in range(KV_W // LANES)]
        v_t = x[:, KV_W:].T
        for g in range(N_KV):
            pair = roped[g * HEAD_DIM // LANES]
            lo = g * HEAD_DIM % LANES
            if k_dst.shape[-1] == LANES:
                k_dst[0, g] = jnp.concatenate([pair[:, lo : lo + HEAD_DIM], pad_lanes], axis=1).astype(BF16)
            else:
                k_dst[0, g] = pair[:, lo : lo + HEAD_DIM].astype(BF16)
            v_dst[0, g, :HEAD_DIM, :] = v_t[g * HEAD_DIM : (g + 1) * HEAD_DIM].astype(BF16)
            v_dst[0, g, HEAD_DIM:, :] = pad_rows


def kv_prep(proj, cos_t, sin_t, batch, seq, tm):
    nt = seq // tm
    blk = lambda col: pl.BlockSpec((tm, 2 * KV_W), lambda b, t: (b * nt + t, col // (2 * KV_W)))
    tab = pl.BlockSpec((tm, LANES), lambda b, t: (b * nt + t, 0))
    k_spec = pl.BlockSpec((1, N_KV, tm, HEAD_DIM), lambda b, t: (b, 0, t, 0))
    v_spec = pl.BlockSpec((1, N_KV, V_ROWS, tm), lambda b, t: (b, 0, 0, t))
    k_shape = jax.ShapeDtypeStruct((batch, N_KV, seq, HEAD_DIM), BF16)
    v_shape = jax.ShapeDtypeStruct((batch, N_KV, V_ROWS, seq), BF16)
    kx_spec = pl.BlockSpec((1, N_KV, tm, LANES), lambda b, t: (b, 0, t, 0))
    kx_shape = jax.ShapeDtypeStruct((batch, N_KV, seq, LANES), BF16)
    return pl.pallas_call(
        _kv_prep_kernel,
        grid=(batch, nt),
        in_specs=[blk(COL_KV), blk(COL_KV + 2 * KV_W), blk(COL_KV + 4 * KV_W), tab, tab],
        out_specs=[k_spec, k_spec, kx_spec, v_spec, k_spec, v_spec],
        out_shape=[k_shape, k_shape, kx_shape, v_shape, k_shape, v_shape],
        compiler_params=_cparams(("arbitrary", "arbitrary")),
        name="kv_prep",
    )(proj, proj, proj, cos_t, sin_t)


def _compress_kernel(x_ref, pe_ref, w1_ref, b1_ref, w2_ref, cos_ref, sin_ref, o_ref, *, rotary):
    x = x_ref[0]
    half = CMP_STRIDE * HEAD_DIM
    pe = pe_ref[0]
    w1 = w1_ref[0]
    u = jnp.dot((x + pe[0:1]).astype(BF16), w1[:half], preferred_element_type=F32)
    v = jnp.dot((x + pe[1:2]).astype(BF16), w1[half:], preferred_element_type=F32)
    hid = _gelu_tanh(u + pltpu.roll(v, v.shape[0] - 1, 0) + b1_ref[0])
    out = jnp.dot(hid.astype(BF16), w2_ref[0], preferred_element_type=F32)
    if rotary:
        hh = HEAD_DIM // 2
        partner = jnp.concatenate([out[:, hh:], out[:, :hh]], axis=1)
        out = out * cos_ref[0][:, :HEAD_DIM] + partner * sin_ref[0][:, :HEAD_DIM]
    o_ref[0] = out.astype(o_ref.dtype)


def compress(x_chunks, j, pe, w1, b1, w2, cos_c, sin_c, batch, rotary):
    bg, nc, width = x_chunks.shape
    return pl.pallas_call(
        functools.partial(_compress_kernel, rotary=rotary),
        grid=(bg,),
        in_specs=[
            pl.BlockSpec((1, nc, width), lambda i: (i, 0, 0)),
            pl.BlockSpec((1, 2, width), lambda i: (j, 0, 0)),
            pl.BlockSpec((1, 2 * width, w1.shape[2]), lambda i: (j, 0, 0)),
            pl.BlockSpec((1, 1, w1.shape[2]), lambda i: (j, 0, 0)),
            pl.BlockSpec((1, w2.shape[1], HEAD_DIM), lambda i: (j, 0, 0)),
            pl.BlockSpec((1, nc, LANES), lambda i: (i // N_KV, 0, 0)),
            pl.BlockSpec((1, nc, LANES), lambda i: (i // N_KV, 0, 0)),
        ],
        out_specs=pl.BlockSpec((1, nc, HEAD_DIM), lambda i: (i, 0, 0)),
        out_shape=jax.ShapeDtypeStruct((bg, nc, HEAD_DIM), BF16),
        compiler_params=_cparams(("arbitrary",)),
        name="compress_k" if rotary else "compress_v",
    )(x_chunks, pe, w1, b1, w2, cos_c, sin_c)


def _nsa_kernel(q_ref, gate_ref, cos_ref, sin_ref, kc_ref, vct_ref, ks_ref, vst_ref, kw_ref, vwt_ref, ovt_ref,
                o_ref, sel_sc, sa_sc, sb_sc, oc_sc, imp_sc, kmax_sc):
    qb = Q_BLOCK
    s0 = pl.program_id(2) * qb
    nc = kc_ref.shape[2]
    hh = HEAD_DIM // 2
    cols = GROUP * qb

    qt = q_ref[...].astype(F32).T
    cos_t, sin_t = cos_ref[0], sin_ref[0]
    heads = []
    for r in range(GROUP):
        x1 = qt[r * HEAD_DIM : r * HEAD_DIM + hh]
        x2 = qt[r * HEAD_DIM + hh : (r + 1) * HEAD_DIM]
        heads.append(jnp.concatenate([x1 * cos_t - x2 * sin_t, x2 * cos_t + x1 * sin_t], axis=0))
    q_t = (jnp.concatenate(heads, axis=1) * (HEAD_DIM ** -0.5 * LOG2_E)).astype(BF16)

    tq = s0 + lax.broadcasted_iota(jnp.int32, (1, qb), 1)

    @pl.when(pl.program_id(2) == 0)
    def _():
        def norm_step(t, best):
            k_rows = ks_ref[0, 0, pl.ds(pl.multiple_of(t * SEL_TILE, SEL_TILE), SEL_TILE), :].astype(F32)
            sq = jnp.sum(k_rows * k_rows, axis=1, keepdims=True)
            return jnp.maximum(best, jnp.max(sq, axis=0, keepdims=True))

        best = lax.fori_loop(0, ks_ref.shape[2] // SEL_TILE, norm_step, jnp.zeros((1, 1), F32))
        kmax_sc[...] = jnp.broadcast_to(jnp.sqrt(best), kmax_sc.shape)

    q_f32 = q_t.astype(F32)
    q_norm = jnp.sqrt(jnp.sum(q_f32 * q_f32, axis=0, keepdims=True))
    offset = q_norm * (kmax_sc[0:1, 0:1] * BOUND_SLACK)
    bound_ok = jnp.max(offset) <= MAX_OFFSET
    ext_row = lax.broadcasted_iota(jnp.int32, (SUBLANES, cols), 0)
    zero_rows = jnp.zeros((LANES - HEAD_DIM - SUBLANES, cols), F32)
    q_ext = jnp.concatenate([q_f32, jnp.zeros((SUBLANES, cols), F32), zero_rows], axis=0).astype(BF16)
    q_ext_off = jnp.concatenate([q_f32, jnp.where(ext_row == 0, -offset, 0.0), zero_rows], axis=0).astype(BF16)

    def compressed(n_c):
        cmp_end = lax.broadcasted_iota(jnp.int32, (n_c, qb), 0) * CMP_STRIDE + (CMP_LEN - 1)
        bias_c = jnp.where(cmp_end <= tq, 0.0, NEG)
        has_key = jnp.where(tq >= CMP_LEN - 1, 1.0, 0.0)
        sc = jnp.dot(kc_ref[0, 0, :n_c, :], q_t, preferred_element_type=F32) + jnp.concatenate([bias_c] * GROUP, axis=1)
        p_c = jnp.exp2(sc - jnp.max(sc, axis=0, keepdims=True))
        norm = jnp.concatenate([has_key] * GROUP, axis=1) / jnp.maximum(jnp.sum(p_c, axis=0, keepdims=True), 1e-30)
        p_c = p_c * norm
        oc_sc[...] = jnp.dot(vct_ref[0, 0, :, :n_c], p_c.astype(BF16), preferred_element_type=F32)
        p_sum = p_c[:, :qb]
        for r in range(1, GROUP):
            p_sum = p_sum + p_c[:, r * qb : (r + 1) * qb]
        p_hi = p_sum.astype(BF16)
        p_lo = (p_sum - p_hi.astype(F32)).astype(BF16)
        ovt = ovt_ref[:, :n_c]
        imp_sc[...] = (jnp.dot(ovt, p_hi, preferred_element_type=F32)
                       + jnp.dot(ovt, p_lo, preferred_element_type=F32))

    last_cmp = (s0 + qb - CMP_LEN) // CMP_STRIDE
    for v in range(nc // LANES):
        pl.when(last_cmp // LANES == v)(functools.partial(compressed, (v + 1) * LANES))
    o_c = oc_sc[...]
    imp = imp_sc[...]

    w0 = pl.multiple_of(jnp.maximum(s0 - WINDOW, 0), LANES)
    wpos = w0 + lax.broadcasted_iota(jnp.int32, (WIN_KEYS, qb), 0)
    bias_w = jnp.where((wpos <= tq) & (wpos > tq - WINDOW), 0.0, NEG)
    kw_tile = kw_ref[0, 0, pl.ds(w0, WIN_KEYS), :]
    vw_tile = vwt_ref[0, 0, :, pl.ds(w0, WIN_KEYS)]
    s_w = jnp.dot(kw_tile, q_t, preferred_element_type=F32) + jnp.concatenate([bias_w] * GROUP, axis=1)
    p_w = jnp.exp2(s_w - jnp.max(s_w, axis=0, keepdims=True)).astype(BF16)
    acc_w = jnp.dot(vw_tile, p_w, preferred_element_type=F32)
    o_w = acc_w[:HEAD_DIM] / jnp.maximum(acc_w[HEAD_DIM : HEAD_DIM + 1], 1e-30)

    blk = lax.broadcasted_iota(jnp.int32, (LANES, qb), 0)
    forced = (blk == 0) | (blk == (tq >> SEL_SHIFT))
    work = jnp.where(forced, FORCE, jnp.where(blk * SEL_LEN <= tq, imp, -FORCE))

    blk_f = blk.astype(F32)
    sel = jnp.zeros((LANES, qb), F32)
    for _ in range(N_SELECT):
        m = jnp.max(work, axis=0, keepdims=True)
        idx = jnp.min(jnp.where(work == m, blk_f, float(LANES)), axis=0, keepdims=True)
        hit = blk_f == idx
        sel = jnp.where(hit, 1.0, sel)
        work = jnp.where(hit, -3e38, work)
    sel_sc[...] = sel

    blocks_per_tile = SEL_TILE // SEL_LEN
    row_minus_lane = (lax.broadcasted_iota(jnp.int32, (SEL_TILE, qb), 0)
                      - lax.broadcasted_iota(jnp.int32, (SEL_TILE, qb), 1))
    n_tiles = (s0 + qb - 1) // SEL_TILE + 1

    def scores(kb, q_op):
        k0 = pl.multiple_of(kb * SEL_TILE, SEL_TILE)
        picked = sel_sc[pl.ds(pl.multiple_of(kb * blocks_per_tile, blocks_per_tile), blocks_per_tile), :]
        picked = jnp.where(picked > 0.5, 0.0, NEG)
        picked = jnp.concatenate(
            [jnp.broadcast_to(picked[j : j + 1, :], (SEL_LEN, qb)) for j in range(blocks_per_tile)], axis=0)
        bias = jnp.where(row_minus_lane <= s0 - k0, picked, NEG)
        s_ = jnp.dot(ks_ref[0, 0, pl.ds(k0, SEL_TILE), :], q_op, preferred_element_type=F32)
        return s_ + jnp.concatenate([bias] * GROUP, axis=1)

    def values(kb, p):
        k0 = pl.multiple_of(kb * SEL_TILE, SEL_TILE)
        return jnp.dot(vst_ref[0, 0, :, pl.ds(k0, SEL_TILE)], p, preferred_element_type=F32)

    def bounded_path():
        def tile(kb, acc):
            return acc + values(kb, jnp.exp2(scores(kb, q_ext_off)).astype(BF16))

        acc = lax.fori_loop(0, n_tiles // 2, lambda j, acc: tile(2 * j + 1, tile(2 * j, acc)),
                            jnp.zeros((V_ROWS, cols), F32))
        return lax.cond(n_tiles % 2 == 1, lambda acc: tile(n_tiles - 1, acc), lambda acc: acc, acc)

    def exact_path():
        def absorb(kb, s_, m_run, acc):
            m_new = jnp.maximum(m_run, jnp.max(s_, axis=0, keepdims=True))
            p = jnp.exp2(s_ - m_new).astype(BF16)
            return m_new, jnp.exp2(m_run - m_new) * acc + values(kb, p)

        sa_sc[...] = scores(0, q_ext)

        def sel_step(j, carry):
            m_run, acc = carry
            sb_sc[...] = scores(2 * j + 1, q_ext)
            m_run, acc = absorb(2 * j, sa_sc[...], m_run, acc)
            sa_sc[...] = scores(2 * j + 2, q_ext)
            return absorb(2 * j + 1, sb_sc[...], m_run, acc)

        init = (jnp.full((1, cols), NEG, F32), jnp.zeros((V_ROWS, cols), F32))
        full_trips = (n_tiles - 1) // 2
        m_run, acc = lax.fori_loop(0, full_trips, sel_step, init)
        m_run, acc = absorb(2 * full_trips, sa_sc[...], m_run, acc)

        def last_tile(carry):
            return absorb(n_tiles - 1, scores(n_tiles - 1, q_ext), *carry)

        return lax.cond(n_tiles - 2 * full_trips == 2, last_tile, lambda carry: carry, (m_run, acc))[1]

    acc_s = lax.cond(bound_ok, bounded_path, exact_path)

    o_s = acc_s[:HEAD_DIM] / jnp.maximum(acc_s[HEAD_DIM : HEAD_DIM + 1], 1e-30)

    gates_t = _sigmoid(gate_ref[...].astype(F32)).T
    outs = []
    for r in range(GROUP):
        sl = slice(r * qb, (r + 1) * qb)
        outs.append(gates_t[3 * r : 3 * r + 1] * o_c[:, sl] + gates_t[3 * r + 1 : 3 * r + 2] * o_s[:, sl]
                    + gates_t[3 * r + 2 : 3 * r + 3] * o_w[:, sl])
    o_ref[...] = jnp.concatenate(outs, axis=0).T.astype(o_ref.dtype)


def nsa_attention(proj, cos_q, sin_q, kc, vct, ks, vst, kw, vwt, overlap_t, batch, seq):
    nq = seq // Q_BLOCK
    nc = kc.shape[2]
    gw = GROUP * HEAD_DIM
    per_bg = lambda shape: pl.BlockSpec((1, 1) + shape, lambda b, g, i: (b, g, 0, 0))
    return pl.pallas_call(
        _nsa_kernel,
        grid=(batch, N_KV, nq),
        in_specs=[
            pl.BlockSpec((Q_BLOCK, gw), lambda b, g, i: (b * nq + i, COL_Q // gw + g)),
            pl.BlockSpec((Q_BLOCK, LANES), lambda b, g, i: (b * nq + i, COL_GN // LANES + g)),
            pl.BlockSpec((1, HEAD_DIM // 2, Q_BLOCK), lambda b, g, i: (b, 0, i)),
            pl.BlockSpec((1, HEAD_DIM // 2, Q_BLOCK), lambda b, g, i: (b, 0, i)),
            per_bg((nc, HEAD_DIM)), per_bg((HEAD_DIM, nc)),
            per_bg((seq, LANES)), per_bg((V_ROWS, seq)),
            per_bg((seq, HEAD_DIM)), per_bg((V_ROWS, seq)),
            pl.BlockSpec((LANES, nc), lambda b, g, i: (0, 0)),
        ],
        out_specs=pl.BlockSpec((Q_BLOCK, gw), lambda b, g, i: (b * nq + i, g)),
        out_shape=jax.ShapeDtypeStruct((batch * seq, N_HEADS * HEAD_DIM), BF16),
        scratch_shapes=[pltpu.VMEM((LANES, Q_BLOCK), F32), pltpu.VMEM((SEL_TILE, GROUP * Q_BLOCK), F32),
                        pltpu.VMEM((SEL_TILE, GROUP * Q_BLOCK), F32), pltpu.VMEM((HEAD_DIM, GROUP * Q_BLOCK), F32),
                        pltpu.VMEM((LANES, Q_BLOCK), F32), pltpu.VMEM((SUBLANES, LANES), F32)],
        compiler_params=_cparams(("arbitrary", "arbitrary", "arbitrary")),
        name="nsa_attention",
    )(proj, proj, cos_q, sin_q, kc, vct, ks, vst, kw, vwt, overlap_t)


def _mem_kernel(q_ref, kv_ref, o_ref):
    q = q_ref[...]
    kv = kv_ref[0]
    mem_w = MEM_HEADS * MEM_HEAD_DIM
    outs = []
    for h in range(MEM_HEADS):
        sl = slice(h * MEM_HEAD_DIM, (h + 1) * MEM_HEAD_DIM)
        qh = (q[:, sl] * (MEM_HEAD_DIM ** -0.5)).astype(BF16)
        s = lax.dot_general(qh, kv[:, sl], (((1,), (1,)), ((), ())), preferred_element_type=F32)
        p = jnp.exp(s - jnp.max(s, axis=-1, keepdims=True))
        p = p / jnp.sum(p, axis=-1, keepdims=True)
        outs.append(jnp.dot(p.astype(BF16), kv[:, mem_w + h * MEM_HEAD_DIM : mem_w + (h + 1) * MEM_HEAD_DIM],
                            preferred_element_type=F32))
    o_ref[...] = jnp.concatenate(outs, axis=1).astype(o_ref.dtype)


def memory_attention(proj, kv_mem, batch, seq, tm):
    nt = seq // tm
    mem_len = kv_mem.shape[1]
    return pl.pallas_call(
        _mem_kernel,
        grid=(batch, nt),
        in_specs=[
            pl.BlockSpec((tm, D_MODEL), lambda b, t: (b * nt + t, COL_QM // D_MODEL)),
            pl.BlockSpec((1, mem_len, 2 * D_MODEL), lambda b, t: (b, 0, 0)),
        ],
        out_specs=pl.BlockSpec((tm, D_MODEL), lambda b, t: (b * nt + t, 0)),
        out_shape=jax.ShapeDtypeStruct((batch * seq, D_MODEL), BF16),
        compiler_params=_cparams(("arbitrary", "arbitrary")),
        name="memory_attention",
    )(proj, kv_mem)


def _merge_kernel(oa_ref, ob_ref, oc_ref, ga_ref, gb_ref, gc_ref, wa_ref, wb_ref, wc_ref, wo_ref,
                  gpost_ref, x_ref, gnext_ref, xo_ref, ho_ref):
    gate = lambda ref: _sigmoid(ref[...].astype(F32))
    merged = gate(ga_ref) * jnp.dot(oa_ref[...], wa_ref[...], preferred_element_type=F32)
    merged += gate(gb_ref) * jnp.dot(ob_ref[...], wb_ref[...], preferred_element_type=F32)
    merged += gate(gc_ref) * jnp.dot(oc_ref[...], wc_ref[...], preferred_element_type=F32)
    y = jnp.dot(merged.astype(BF16), wo_ref[...], preferred_element_type=F32)
    x_new = x_ref[...] + _rms(y, gpost_ref[...])
    xo_ref[...] = x_new
    ho_ref[...] = _rms(x_new, gnext_ref[...]).astype(ho_ref.dtype)


def merge_out(oa, ob, oc, proj, wa, wb, wc, wo, layer, g_post, x, g_next, tm):
    n, d = x.shape
    row = lambda col: pl.BlockSpec((tm, d), lambda i: (i, col))
    full = lambda shape: pl.BlockSpec(shape, lambda i: (0, 0))
    w_spec = pl.BlockSpec((None, d, d), lambda i: (layer, 0, 0))
    gm = COL_GM // d
    return pl.pallas_call(
        _merge_kernel,
        grid=(n // tm,),
        in_specs=[row(0), row(0), row(0), row(gm), row(gm + 1), row(gm + 2),
                  w_spec, w_spec, w_spec, w_spec, full((1, d)), row(0), full((1, d))],
        out_specs=[row(0), row(0)],
        out_shape=[jax.ShapeDtypeStruct((n, d), F32), jax.ShapeDtypeStruct((n, d), BF16)],
        compiler_params=_cparams(("arbitrary",)),
        name="merge_out",
    )(oa, ob, oc, proj, proj, proj, wa, wb, wc, wo, g_post.reshape(1, d), x, g_next.reshape(1, d))


def _mlp_kernel(h_ref, w1_ref, w2_ref, gpost_ref, x_ref, gnext_ref, xo_ref, ho_ref, acc_ref):
    k = pl.program_id(1)

    @pl.when(k == 0)
    def _():
        acc_ref[...] = jnp.zeros_like(acc_ref)

    u = jnp.maximum(jnp.dot(h_ref[...], w1_ref[...], preferred_element_type=F32), 0.0)
    acc_ref[...] += jnp.dot((u * u).astype(BF16), w2_ref[...], preferred_element_type=F32)

    @pl.when(k == pl.num_programs(1) - 1)
    def _():
        x_new = x_ref[...] + _rms(acc_ref[...], gpost_ref[...])
        xo_ref[...] = x_new
        ho_ref[...] = _rms(x_new, gnext_ref[...]).astype(ho_ref.dtype)


def mlp(h, w1, w2, layer, g_post, x, g_next, tm, tf):
    n, d = x.shape
    ff = w1.shape[2]
    return pl.pallas_call(
        _mlp_kernel,
        grid=(n // tm, ff // tf),
        in_specs=[
            pl.BlockSpec((tm, d), lambda i, k: (i, 0)),
            pl.BlockSpec((None, d, tf), lambda i, k: (layer, 0, k)),
            pl.BlockSpec((None, tf, d), lambda i, k: (layer, k, 0)),
            pl.BlockSpec((1, d), lambda i, k: (0, 0)),
            pl.BlockSpec((tm, d), lambda i, k: (i, 0)),
            pl.BlockSpec((1, d), lambda i, k: (0, 0)),
        ],
        out_specs=[pl.BlockSpec((tm, d), lambda i, k: (i, 0))] * 2,
        out_shape=[jax.ShapeDtypeStruct((n, d), F32), jax.ShapeDtypeStruct((n, d), BF16)],
        scratch_shapes=[pltpu.VMEM((tm, d), F32)],
        compiler_params=_cparams(("arbitrary", "arbitrary")),
        name="mlp",
    )(h, w1, w2, g_post.reshape(1, d), x, g_next.reshape(1, d))


def _pack_w_in(w):
    o_kv, o_gn, o_qm, o_gm = 3072, 4608, 4656, 5680
    per_group = GROUP * 3
    lead = w.shape[:-1]
    gn = w[..., o_gn:o_qm].reshape(lead + (N_KV, per_group))
    gn = jnp.pad(gn, ((0, 0),) * (len(lead) + 1) + ((0, LANES - per_group),)).reshape(lead + (N_KV * LANES,))
    packed = jnp.concatenate([w[..., :o_kv], w[..., o_qm:o_gm], w[..., o_gm:], w[..., o_kv:o_gn], gn], axis=-1)
    assert packed.shape[-1] == D_INP
    return packed.astype(BF16)


def _overlap_matrix_t(nc):
    c0 = np.arange(nc)[:, None] * CMP_STRIDE
    s0 = np.arange(LANES)[None, :] * SEL_LEN
    ov = np.clip(np.minimum(c0 + CMP_LEN, s0 + SEL_LEN) - np.maximum(c0, s0), 0, None).astype(np.float32) / CMP_LEN
    return jnp.asarray(ov.T, dtype=BF16)


def kernel(x, mem, positions, ln_mix_pre, w_in, conv_w, conv_b, lru_wr, lru_br, lru_wi, lru_bi, lru_lambda, cmp_pe, cmp_w1, cmp_b1, cmp_w2, ln_mem, w_mem_kv, w_br_rnn, w_br_nsa, w_br_mem, w_out, ln_mix_post, ln_mlp_pre, mlp_w1, mlp_w2, ln_mlp_post):
    batch, seq, d = x.shape
    depth = w_in.shape[0]
    mem_len = mem.shape[1]
    n = batch * seq
    nc = seq // CMP_STRIDE
    assert d == D_MODEL and seq % (2 * SEL_TILE) == 0 and seq // SEL_LEN <= LANES and seq >= WIN_KEYS

    tm = min(512, seq)
    tm_big = min(1024, seq)
    xf = x.reshape(n, d)
    memf = mem.reshape(batch * mem_len, d)
    cos_t, sin_t = rope_tables(positions.reshape(n), tm)
    pos_c = jnp.pad(positions[:, CMP_LEN - 1 :: CMP_STRIDE], ((0, 0), (0, 1)))
    cos_c, sin_c = rope_tables(pos_c.reshape(batch * nc), nc)
    cos_c, sin_c = cos_c.reshape(batch, nc, LANES), sin_c.reshape(batch, nc, LANES)
    cos_q, sin_q = rope_tables_t(positions, tm)
    overlap_t = _overlap_matrix_t(nc)

    w_in_b, w_kv_b = _pack_w_in(w_in.astype(BF16)), w_mem_kv.astype(BF16)
    wr_b, wi_b = lru_wr.astype(BF16), lru_wi.astype(BF16)
    w1c_b, w2c_b = cmp_w1.astype(BF16), cmp_w2.astype(BF16)
    wa_b, wb_b, wc_b, wo_b = (w.astype(BF16) for w in (w_br_rnn, w_br_nsa, w_br_mem, w_out))
    w1_b, w2_b = mlp_w1.astype(BF16), mlp_w2.astype(BF16)

    h = rmsnorm_bf16(xf, ln_mix_pre[0], tm)
    for l in range(depth):
        proj = matmul(h, w_in_b, l, tm_big, 1024, BF16, "in_proj")

        o_a = rglru_branch(proj, batch, seq, conv_w[l], conv_b[l], wr_b[l], lru_br[l], wi_b[l], lru_bi[l],
                           lru_lambda[l], min(256, seq))

        kc_raw, vc_raw, ks, vst, kw, vwt = kv_prep(proj, cos_t, sin_t, batch, seq, tm)
        chunks = lambda a: a.reshape(batch * N_KV, nc, CMP_STRIDE * HEAD_DIM)
        pe = cmp_pe[l].reshape(2, 2, CMP_STRIDE * HEAD_DIM)
        b1c = cmp_b1[l].reshape(2, 1, -1)
        k_cmp = compress(chunks(kc_raw), 0, pe, w1c_b[l], b1c, w2c_b[l], cos_c, sin_c, batch, True)
        v_cmp = compress(chunks(vc_raw), 1, pe, w1c_b[l], b1c, w2c_b[l], cos_c, sin_c, batch, False)
        kc = k_cmp.reshape(batch, N_KV, nc, HEAD_DIM)
        vct = v_cmp.reshape(batch, N_KV, nc, HEAD_DIM).transpose(0, 1, 3, 2)
        o_b = nsa_attention(proj, cos_q, sin_q, kc, vct, ks, vst, kw, vwt, overlap_t, batch, seq)

        mem_h = rmsnorm_bf16(memf, ln_mem[l], mem_len)
        kv_mem = matmul(mem_h, w_kv_b, l, mem_len, 1024, BF16, "mem_kv")
        o_c = memory_attention(proj, kv_mem.reshape(batch, mem_len, 2 * D_MODEL), batch, seq, tm)

        xf, h = merge_out(o_a, o_b, o_c, proj, wa_b, wb_b, wc_b, wo_b, l, ln_mix_post[l], xf, ln_mlp_pre[l], tm)
        xf, h = mlp(h, w1_b, w2_b, l, ln_mlp_post[l], xf, ln_mix_pre[(l + 1) % depth], tm_big, 1024)
    return xf.reshape(batch, seq, d)
```

```python
import functools

import jax
import jax.numpy as jnp
import numpy as np
from jax import lax
from jax.experimental import pallas as pl
from jax.experimental.pallas import tpu as pltpu

F32 = jnp.float32
BF16 = jnp.bfloat16

D_MODEL = 1024
LRU_BLOCKS = 8
LRU_BW = D_MODEL // LRU_BLOCKS
CONV_W = 4
LRU_C = 8.0
N_HEADS = 16
HEAD_DIM = 64
N_KV = 4
GROUP = N_HEADS // N_KV
KV_W = N_KV * HEAD_DIM
CMP_STRIDE = 16
CMP_LEN = 32
SEL_LEN = 64
SEL_SHIFT = 6
N_SELECT = 16
WINDOW = 512
Q_BLOCK = 256
MEM_HEADS = 4
MEM_HEAD_DIM = D_MODEL // MEM_HEADS
D_FF = 4 * D_MODEL
ROPE_THETA = 10000.0
EPS = 1e-6
NEG = -1e30
FORCE = 1e4
TAKEN = -3e38
LOG2_E = 1.4426950408889634
MAX_OFFSET = 50.0
BOUND_SLACK = 1.01

LANES = 128
SUBLANES = 8
SEL_TILE = 512
WIN_KEYS = WINDOW + Q_BLOCK
V_ROWS = 80
VMEM_LIMIT = 48 * 1024 * 1024

COL_XR, COL_YR, COL_Q, COL_QM, COL_GM, COL_KV, COL_GN = 0, 1024, 2048, 3072, 4096, 7168, 8704
D_INP = 9216


def _cparams(sem):
    return pltpu.CompilerParams(dimension_semantics=sem, vmem_limit_bytes=VMEM_LIMIT)


def _sigmoid(x):
    return 0.5 * jnp.tanh(0.5 * x) + 0.5


def _gelu_tanh(x):
    return 0.5 * x * (1.0 + jnp.tanh(0.7978845608028654 * (x + 0.044715 * (x * x * x))))


def _rms(x, g):
    return x * lax.rsqrt(jnp.mean(x * x, axis=-1, keepdims=True) + EPS) * g


def _masked_softmax(sc, mask):
    sc = jnp.where(mask, sc, NEG)
    m = jnp.max(sc, axis=-1, keepdims=True)
    p = jnp.where(mask, jnp.exp(sc - m), 0.0)
    return p / jnp.maximum(jnp.sum(p, axis=-1, keepdims=True), 1e-30)


def _rms_kernel(x_ref, g_ref, o_ref):
    o_ref[...] = _rms(x_ref[...], g_ref[...]).astype(o_ref.dtype)


def rmsnorm_bf16(x, g, tm):
    m, d = x.shape
    return pl.pallas_call(
        _rms_kernel,
        grid=(m // tm,),
        in_specs=[pl.BlockSpec((tm, d), lambda i: (i, 0)), pl.BlockSpec((1, d), lambda i: (0, 0))],
        out_specs=pl.BlockSpec((tm, d), lambda i: (i, 0)),
        out_shape=jax.ShapeDtypeStruct((m, d), BF16),
        compiler_params=_cparams(("arbitrary",)),
        name="rmsnorm",
    )(x, g.reshape(1, d))


def _mm_kernel(a_ref, w_ref, o_ref):
    o_ref[...] = jnp.dot(a_ref[...], w_ref[...], preferred_element_type=F32).astype(o_ref.dtype)


def matmul(a, w, layer, tm, tn, out_dtype, name):
    m, k = a.shape
    n = w.shape[2]
    return pl.pallas_call(
        _mm_kernel,
        grid=(n // tn, m // tm),
        in_specs=[pl.BlockSpec((tm, k), lambda j, i: (i, 0)), pl.BlockSpec((None, k, tn), lambda j, i: (layer, 0, j))],
        out_specs=pl.BlockSpec((tm, tn), lambda j, i: (i, j)),
        out_shape=jax.ShapeDtypeStruct((m, n), out_dtype),
        compiler_params=_cparams(("arbitrary", "arbitrary")),
        name=name,
    )(a, w)


def _rope_table_kernel(pos_ref, inv_ref, cos_ref, sin_ref):
    ang = pos_ref[...].astype(F32) * inv_ref[...]
    lane = lax.broadcasted_iota(jnp.int32, ang.shape, 1)
    cos_ref[...] = jnp.cos(ang)
    sin_ref[...] = jnp.where((lane & (HEAD_DIM - 1)) < HEAD_DIM // 2, -1.0, 1.0) * jnp.sin(ang)


def rope_tables(pos_flat, tm):
    n = pos_flat.shape[0]
    half = HEAD_DIM // 2
    inv = ROPE_THETA ** (-jnp.arange(half, dtype=F32) * 2.0 / HEAD_DIM)
    inv_full = jnp.tile(inv, LANES // half).reshape(1, LANES)
    return pl.pallas_call(
        _rope_table_kernel,
        grid=(n // tm,),
        in_specs=[pl.BlockSpec((tm, 1), lambda i: (i, 0)), pl.BlockSpec((1, LANES), lambda i: (0, 0))],
        out_specs=[pl.BlockSpec((tm, LANES), lambda i: (i, 0))] * 2,
        out_shape=[jax.ShapeDtypeStruct((n, LANES), F32)] * 2,
        compiler_params=_cparams(("arbitrary",)),
        name="rope_tables",
    )(pos_flat.reshape(n, 1), inv_full)


def _rope_table_t_kernel(pos_ref, inv_ref, cos_ref, sin_ref):
    ang = inv_ref[...] * pos_ref[0].astype(F32)
    cos_ref[0] = jnp.cos(ang)
    sin_ref[0] = jnp.sin(ang)


def rope_tables_t(positions, tm):
    batch, seq = positions.shape
    half = HEAD_DIM // 2
    inv = (ROPE_THETA ** (-jnp.arange(half, dtype=F32) * 2.0 / HEAD_DIM)).reshape(half, 1)
    return pl.pallas_call(
        _rope_table_t_kernel,
        grid=(batch, seq // tm),
        in_specs=[pl.BlockSpec((1, 1, tm), lambda b, i: (b, 0, i)), pl.BlockSpec((half, 1), lambda b, i: (0, 0))],
        out_specs=[pl.BlockSpec((1, half, tm), lambda b, i: (b, 0, i))] * 2,
        out_shape=[jax.ShapeDtypeStruct((batch, half, seq), F32)] * 2,
        compiler_params=_cparams(("arbitrary", "arbitrary")),
        name="rope_tables_t",
    )(positions.reshape(batch, 1, seq), inv)


def _rope128(x, cos_t, sin_t):
    lane = lax.broadcasted_iota(jnp.int32, x.shape, 1)
    first = (lane & (HEAD_DIM - 1)) < HEAD_DIM // 2
    partner = jnp.where(first, pltpu.roll(x, LANES - HEAD_DIM // 2, 1), pltpu.roll(x, HEAD_DIM // 2, 1))
    return x * cos_t + partner * sin_t


def _rglru_kernel(xr_ref, yr_ref, cw_ref, cb_ref, wr_ref, br_ref, wi_ref, bi_ref, lam_ref, o_ref, h_sc, tail_sc):
    @pl.when(pl.program_id(1) == 0)
    def _():
        h_sc[...] = jnp.zeros_like(h_sc)
        tail_sc[...] = jnp.zeros_like(tail_sc)

    xr = xr_ref[...].astype(F32)
    t_len, d = xr.shape
    tail = tail_sc[...]
    row8 = lax.broadcasted_iota(jnp.int32, (8, d), 0)
    cw = cw_ref[...]
    xc = cb_ref[...] + xr * cw[CONV_W - 1 : CONV_W, :]
    for k in range(1, CONV_W):
        rolled = pltpu.roll(xr, k, 0)
        head = jnp.where(row8 < k, pltpu.roll(tail, k, 0), rolled[0:8])
        shifted = jnp.concatenate([head, rolled[8:]], axis=0)
        xc = xc + shifted * cw[CONV_W - 1 - k : CONV_W - k, :]
    tail_sc[...] = xr[t_len - 8 :]

    xcb = xc.astype(BF16)
    rl, il = [], []
    for n in range(LRU_BLOCKS):
        xb = xcb[:, n * LRU_BW : (n + 1) * LRU_BW]
        rl.append(jnp.dot(xb, wr_ref[n], preferred_element_type=F32))
        il.append(jnp.dot(xb, wi_ref[n], preferred_element_type=F32))
    r = _sigmoid(jnp.concatenate(rl, axis=1) + br_ref[...])
    ig = _sigmoid(jnp.concatenate(il, axis=1) + bi_ref[...])
    softplus_neg_lam = jnp.log1p(jnp.exp(-lam_ref[...]))
    log_a = (-LRU_C * softplus_neg_lam) * r
    a = jnp.exp(log_a)
    one_minus_a2 = 1.0 - a * a
    root = jnp.where(one_minus_a2 > 0.0, one_minus_a2 * lax.rsqrt(one_minus_a2), 0.0)
    b = root * (ig * xc)

    sub = lax.broadcasted_iota(jnp.int32, (t_len, d), 0) & (SUBLANES - 1)
    step = 1
    while step < SUBLANES:
        keep = sub >= step
        a_sh = jnp.where(keep, pltpu.roll(a, step, 0), 1.0)
        b_sh = jnp.where(keep, pltpu.roll(b, step, 0), 0.0)
        b = a * b_sh + b
        a = a * a_sh
        step *= 2
    h_prev = h_sc[...]
    groups = []
    for g in range(t_len // SUBLANES):
        sl = slice(g * SUBLANES, (g + 1) * SUBLANES)
        groups.append(b[sl] + a[sl] * h_prev)
        h_prev = jnp.broadcast_to(groups[-1][SUBLANES - 1 :], (SUBLANES, d))
    h_sc[...] = h_prev
    h = jnp.concatenate(groups, axis=0)
    o_ref[...] = (h * _gelu_tanh(yr_ref[...].astype(F32))).astype(o_ref.dtype)


def rglru_branch(proj, batch, seq, conv_w, conv_b, wr, br, wi, bi, lam, t_len):
    d = D_MODEL
    nt = seq // t_len
    vec = lambda v: v.reshape(1, d)
    full2 = lambda shape: pl.BlockSpec(shape, lambda b, t: (0,) * len(shape))
    return pl.pallas_call(
        _rglru_kernel,
        grid=(batch, nt),
        in_specs=[
            pl.BlockSpec((t_len, d), lambda b, t: (b * nt + t, COL_XR // d)),
            pl.BlockSpec((t_len, d), lambda b, t: (b * nt + t, COL_YR // d)),
            full2((CONV_W, d)), full2((1, d)),
            full2((LRU_BLOCKS, LRU_BW, LRU_BW)), full2((1, d)),
            full2((LRU_BLOCKS, LRU_BW, LRU_BW)), full2((1, d)),
            full2((1, d)),
        ],
        out_specs=pl.BlockSpec((t_len, d), lambda b, t: (b * nt + t, 0)),
        out_shape=jax.ShapeDtypeStruct((batch * seq, d), BF16),
        scratch_shapes=[pltpu.VMEM((8, d), F32), pltpu.VMEM((8, d), F32)],
        compiler_params=_cparams(("arbitrary", "arbitrary")),
        name="rglru",
    )(proj, proj, conv_w, vec(conv_b), wr.astype(BF16), vec(br), wi.astype(BF16), vec(bi), vec(lam))


def _kv_prep_kernel(kvc_ref, kvs_ref, kvw_ref, cos_ref, sin_ref, kco_ref, vco_ref, kso_ref, vso_ref, kwo_ref, vwo_ref):
    cos_t, sin_t = cos_ref[...], sin_ref[...]
    t_len = cos_t.shape[0]
    xc = kvc_ref[...]
    for g in range(N_KV):
        kco_ref[0, g] = xc[:, g * HEAD_DIM : (g + 1) * HEAD_DIM]
        vco_ref[0, g] = xc[:, KV_W + g * HEAD_DIM : KV_W + (g + 1) * HEAD_DIM]
    pad_row = lax.broadcasted_iota(jnp.int32, (V_ROWS - HEAD_DIM, t_len), 0)
    pad_rows = jnp.where(pad_row == 0, 1.0, 0.0).astype(BF16)
    pad_lane = lax.broadcasted_iota(jnp.int32, (t_len, LANES - HEAD_DIM), 1)
    pad_lanes = jnp.where(pad_lane == 0, 1.0, 0.0)
    for src, k_dst, v_dst in ((kvs_ref, kso_ref, vso_ref), (kvw_ref, kwo_ref, vwo_ref)):
        x = src[...].astype(F32)
        roped = [_rope128(x[:, c * LANES : (c + 1) * LANES], cos_t, sin_t) for c in range(KV_W // LANES)]
        v_t = x[:, KV_W:].T
        for g in range(N_KV):
            pair = roped[g * HEAD_DIM // LANES]
            lo = g * HEAD_DIM % LANES
            k_dst[0, g] = jnp.concatenate([pair[:, lo : lo + HEAD_DIM], pad_lanes], axis=1).astype(BF16)
            v_dst[0, g, :HEAD_DIM, :] = v_t[g * HEAD_DIM : (g + 1) * HEAD_DIM].astype(BF16)
            v_dst[0, g, HEAD_DIM:, :] = pad_rows


def kv_prep(proj, cos_t, sin_t, batch, seq, tm):
    nt = seq // tm
    blk = lambda col: pl.BlockSpec((tm, 2 * KV_W), lambda b, t: (b * nt + t, col // (2 * KV_W)))
    tab = pl.BlockSpec((tm, LANES), lambda b, t: (b * nt + t, 0))
    k_spec = pl.BlockSpec((1, N_KV, tm, HEAD_DIM), lambda b, t: (b, 0, t, 0))
    v_spec = pl.BlockSpec((1, N_KV, V_ROWS, tm), lambda b, t: (b, 0, 0, t))
    k_shape = jax.ShapeDtypeStruct((batch, N_KV, seq, HEAD_DIM), BF16)
    v_shape = jax.ShapeDtypeStruct((batch, N_KV, V_ROWS, seq), BF16)
    kx_spec = pl.BlockSpec((1, N_KV, tm, LANES), lambda b, t: (b, 0, t, 0))
    kx_shape = jax.ShapeDtypeStruct((batch, N_KV, seq, LANES), BF16)
    return pl.pallas_call(
        _kv_prep_kernel,
        grid=(batch, nt),
        in_specs=[blk(COL_KV), blk(COL_KV + 2 * KV_W), blk(COL_KV + 4 * KV_W), tab, tab],
        out_specs=[k_spec, k_spec, kx_spec, v_spec, kx_spec, v_spec],
        out_shape=[k_shape, k_shape, kx_shape, v_shape, kx_shape, v_shape],
        compiler_params=_cparams(("arbitrary", "arbitrary")),
        name="kv_prep",
    )(proj, proj, proj, cos_t, sin_t)


def _compress_kernel(x_ref, pe_ref, w1_ref, b1_ref, w2_ref, cos_ref, sin_ref, o_ref, *, rotary):
    x = x_ref[0]
    half = CMP_STRIDE * HEAD_DIM
    pe = pe_ref[0]
    w1 = w1_ref[0]
    u = jnp.dot((x + pe[0:1]).astype(BF16), w1[:half], preferred_element_type=F32)
    v = jnp.dot((x + pe[1:2]).astype(BF16), w1[half:], preferred_element_type=F32)
    hid = _gelu_tanh(u + pltpu.roll(v, v.shape[0] - 1, 0) + b1_ref[0])
    out = jnp.dot(hid.astype(BF16), w2_ref[0], preferred_element_type=F32)
    if rotary:
        hh = HEAD_DIM // 2
        partner = jnp.concatenate([out[:, hh:], out[:, :hh]], axis=1)
        out = out * cos_ref[0][:, :HEAD_DIM] + partner * sin_ref[0][:, :HEAD_DIM]
    o_ref[0] = out.astype(o_ref.dtype)


def compress(x_chunks, j, pe, w1, b1, w2, cos_c, sin_c, batch, rotary):
    bg, nc, width = x_chunks.shape
    return pl.pallas_call(
        functools.partial(_compress_kernel, rotary=rotary),
        grid=(bg,),
        in_specs=[
            pl.BlockSpec((1, nc, width), lambda i: (i, 0, 0)),
            pl.BlockSpec((1, 2, width), lambda i: (j, 0, 0)),
            pl.BlockSpec((1, 2 * width, w1.shape[2]), lambda i: (j, 0, 0)),
            pl.BlockSpec((1, 1, w1.shape[2]), lambda i: (j, 0, 0)),
            pl.BlockSpec((1, w2.shape[1], HEAD_DIM), lambda i: (j, 0, 0)),
            pl.BlockSpec((1, nc, LANES), lambda i: (i // N_KV, 0, 0)),
            pl.BlockSpec((1, nc, LANES), lambda i: (i // N_KV, 0, 0)),
        ],
        out_specs=pl.BlockSpec((1, nc, HEAD_DIM), lambda i: (i, 0, 0)),
        out_shape=jax.ShapeDtypeStruct((bg, nc, HEAD_DIM), BF16),
        compiler_params=_cparams(("arbitrary",)),
        name="compress_k" if rotary else "compress_v",
    )(x_chunks, pe, w1, b1, w2, cos_c, sin_c)


def _nsa_kernel(q_ref, gate_ref, cos_ref, sin_ref, kc_ref, vct_ref, ks_ref, vst_ref, kw_ref, vwt_ref, ovt_ref,
                o_ref, sel_sc, sa_sc, sb_sc, oc_sc, imp_sc, kmax_sc):
    qb = Q_BLOCK
    s0 = pl.program_id(2) * qb
    nc = kc_ref.shape[2]
    hh = HEAD_DIM // 2
    cols = GROUP * qb

    qt = q_ref[...].astype(F32).T
    cos_t, sin_t = cos_ref[0], sin_ref[0]
    heads = []
    for r in range(GROUP):
        x1 = qt[r * HEAD_DIM : r * HEAD_DIM + hh]
        x2 = qt[r * HEAD_DIM + hh : (r + 1) * HEAD_DIM]
        heads.append(jnp.concatenate([x1 * cos_t - x2 * sin_t, x2 * cos_t + x1 * sin_t], axis=0))
    q_t = (jnp.concatenate(heads, axis=1) * (HEAD_DIM ** -0.5 * LOG2_E)).astype(BF16)

    tq = s0 + lax.broadcasted_iota(jnp.int32, (1, qb), 1)

    @pl.when(pl.program_id(2) == 0)
    def _():
        for row, k_ref in enumerate((ks_ref, kw_ref)):
            def norm_step(t, best, k_ref=k_ref):
                k_rows = k_ref[0, 0, pl.ds(pl.multiple_of(t * SEL_TILE, SEL_TILE), SEL_TILE), :].astype(F32)
                sq = jnp.sum(k_rows * k_rows, axis=1, keepdims=True)
                return jnp.maximum(best, jnp.max(sq, axis=0, keepdims=True))

            best = lax.fori_loop(0, k_ref.shape[2] // SEL_TILE, norm_step, jnp.zeros((1, 1), F32))
            kmax_sc[row : row + 1, :] = jnp.broadcast_to(jnp.sqrt(best), (1, LANES))

    q_f32 = q_t.astype(F32)
    q_norm = jnp.sqrt(jnp.sum(q_f32 * q_f32, axis=0, keepdims=True))
    offset_s = q_norm * (kmax_sc[0:1, 0:1] * BOUND_SLACK)
    offset_w = q_norm * (kmax_sc[1:2, 0:1] * BOUND_SLACK)
    bound_ok = jnp.max(jnp.maximum(offset_s, offset_w)) <= MAX_OFFSET
    ext_row = lax.broadcasted_iota(jnp.int32, (SUBLANES, cols), 0)
    zero_rows = jnp.zeros((LANES - HEAD_DIM - SUBLANES, cols), F32)

    def extended(offset):
        return jnp.concatenate([q_f32, jnp.where(ext_row == 0, -offset, 0.0), zero_rows], axis=0).astype(BF16)

    q_ext = extended(jnp.zeros_like(offset_s))

    def compressed(n_c):
        cmp_end = lax.broadcasted_iota(jnp.int32, (n_c, qb), 0) * CMP_STRIDE + (CMP_LEN - 1)
        bias_c = jnp.where(cmp_end <= tq, 0.0, NEG)
        has_key = jnp.where(tq >= CMP_LEN - 1, 1.0, 0.0)
        sc = jnp.dot(kc_ref[0, 0, :n_c, :], q_t, preferred_element_type=F32) + jnp.concatenate([bias_c] * GROUP, axis=1)
        p_c = jnp.exp2(sc - jnp.max(sc, axis=0, keepdims=True))
        norm = jnp.concatenate([has_key] * GROUP, axis=1) / jnp.maximum(jnp.sum(p_c, axis=0, keepdims=True), 1e-30)
        p_c = p_c * norm
        oc_sc[...] = jnp.dot(vct_ref[0, 0, :, :n_c], p_c.astype(BF16), preferred_element_type=F32)
        p_sum = p_c[:, :qb]
        for r in range(1, GROUP):
            p_sum = p_sum + p_c[:, r * qb : (r + 1) * qb]
        p_hi = p_sum.astype(BF16)
        p_lo = (p_sum - p_hi.astype(F32)).astype(BF16)
        ovt = ovt_ref[:, :n_c]
        imp_sc[...] = (jnp.dot(ovt, p_hi, preferred_element_type=F32)
                       + jnp.dot(ovt, p_lo, preferred_element_type=F32))

    last_cmp = (s0 + qb - CMP_LEN) // CMP_STRIDE
    for v in range(nc // LANES):
        pl.when(last_cmp // LANES == v)(functools.partial(compressed, (v + 1) * LANES))
    o_c = oc_sc[...]
    imp = imp_sc[...]

    def window(bounded):
        w0 = pl.multiple_of(jnp.maximum(s0 - WINDOW, 0), LANES)
        wpos = w0 + lax.broadcasted_iota(jnp.int32, (WIN_KEYS, qb), 0)
        bias_w = jnp.where((wpos <= tq) & (wpos > tq - WINDOW), 0.0, NEG)
        q_op = extended(offset_w) if bounded else q_ext
        s_w = jnp.dot(kw_ref[0, 0, pl.ds(w0, WIN_KEYS), :], q_op, preferred_element_type=F32)
        s_w = s_w + jnp.concatenate([bias_w] * GROUP, axis=1)
        if not bounded:
            s_w = s_w - jnp.max(s_w, axis=0, keepdims=True)
        acc_w = jnp.dot(vwt_ref[0, 0, :, pl.ds(w0, WIN_KEYS)], jnp.exp2(s_w).astype(BF16),
                        preferred_element_type=F32)
        return acc_w[:HEAD_DIM] / jnp.maximum(acc_w[HEAD_DIM : HEAD_DIM + 1], 1e-30)

    def select_blocks():
        blk = lax.broadcasted_iota(jnp.int32, (LANES, qb), 0)
        forced = (blk == 0) | (blk == (tq >> SEL_SHIFT))
        work = jnp.where(forced, FORCE, jnp.where(blk * SEL_LEN <= tq, imp, -FORCE))
        blk_f = blk.astype(F32)
        for _ in range(N_SELECT):
            m = jnp.max(work, axis=0, keepdims=True)
            idx = jnp.min(jnp.where(work == m, blk_f, float(LANES)), axis=0, keepdims=True)
            work = jnp.where(blk_f == idx, TAKEN, work)
        sel_sc[...] = jnp.where(work == TAKEN, 1.0, 0.0)

    blocks_per_tile = SEL_TILE // SEL_LEN
    row_minus_lane = (lax.broadcasted_iota(jnp.int32, (SEL_TILE, qb), 0)
                      - lax.broadcasted_iota(jnp.int32, (SEL_TILE, qb), 1))
    n_tiles = (s0 + qb - 1) // SEL_TILE + 1

    def scores(kb, q_op):
        k0 = pl.multiple_of(kb * SEL_TILE, SEL_TILE)
        picked = sel_sc[pl.ds(pl.multiple_of(kb * blocks_per_tile, blocks_per_tile), blocks_per_tile), :]
        picked = jnp.where(picked > 0.5, 0.0, NEG)
        picked = jnp.concatenate(
            [jnp.broadcast_to(picked[j : j + 1, :], (SEL_LEN, qb)) for j in range(blocks_per_tile)], axis=0)
        bias = jnp.where(row_minus_lane <= s0 - k0, picked, NEG)
        s_ = jnp.dot(ks_ref[0, 0, pl.ds(k0, SEL_TILE), :], q_op, preferred_element_type=F32)
        return s_ + jnp.concatenate([bias] * GROUP, axis=1)

    def values(kb, p):
        k0 = pl.multiple_of(kb * SEL_TILE, SEL_TILE)
        return jnp.dot(vst_ref[0, 0, :, pl.ds(k0, SEL_TILE)], p, preferred_element_type=F32)

    def bounded_path():
        o_win = window(True)
        select_blocks()
        q_off = extended(offset_s)

        def tile(kb, acc):
            return acc + values(kb, jnp.exp2(scores(kb, q_off)).astype(BF16))

        acc = lax.fori_loop(0, n_tiles // 2, lambda j, acc: tile(2 * j + 1, tile(2 * j, acc)),
                            jnp.zeros((V_ROWS, cols), F32))
        return o_win, lax.cond(n_tiles % 2 == 1, lambda acc: tile(n_tiles - 1, acc), lambda acc: acc, acc)

    def exact_path():
        o_win = window(False)
        select_blocks()

        def absorb(kb, s_, m_run, acc):
            m_new = jnp.maximum(m_run, jnp.max(s_, axis=0, keepdims=True))
            p = jnp.exp2(s_ - m_new).astype(BF16)
            return m_new, jnp.exp2(m_run - m_new) * acc + values(kb, p)

        sa_sc[...] = scores(0, q_ext)

        def sel_step(j, carry):
            m_run, acc = carry
            sb_sc[...] = scores(2 * j + 1, q_ext)
            m_run, acc = absorb(2 * j, sa_sc[...], m_run, acc)
            sa_sc[...] = scores(2 * j + 2, q_ext)
            return absorb(2 * j + 1, sb_sc[...], m_run, acc)

        init = (jnp.full((1, cols), NEG, F32), jnp.zeros((V_ROWS, cols), F32))
        full_trips = (n_tiles - 1) // 2
        m_run, acc = lax.fori_loop(0, full_trips, sel_step, init)
        m_run, acc = absorb(2 * full_trips, sa_sc[...], m_run, acc)

        def last_tile(carry):
            return absorb(n_tiles - 1, scores(n_tiles - 1, q_ext), *carry)

        return o_win, lax.cond(n_tiles - 2 * full_trips == 2, last_tile, lambda carry: carry, (m_run, acc))[1]

    o_w, acc_s = lax.cond(bound_ok, bounded_path, exact_path)

    o_s = acc_s[:HEAD_DIM] / jnp.maximum(acc_s[HEAD_DIM : HEAD_DIM + 1], 1e-30)

    gates_t = _sigmoid(gate_ref[...].astype(F32)).T
    outs = []
    for r in range(GROUP):
        sl = slice(r * qb, (r + 1) * qb)
        outs.append(gates_t[3 * r : 3 * r + 1] * o_c[:, sl] + gates_t[3 * r + 1 : 3 * r + 2] * o_s[:, sl]
                    + gates_t[3 * r + 2 : 3 * r + 3] * o_w[:, sl])
    o_ref[...] = jnp.concatenate(outs, axis=0).T.astype(o_ref.dtype)


def nsa_attention(proj, cos_q, sin_q, kc, vct, ks, vst, kw, vwt, overlap_t, batch, seq):
    nq = seq // Q_BLOCK
    nc = kc.shape[2]
    gw = GROUP * HEAD_DIM
    per_bg = lambda shape: pl.BlockSpec((1, 1) + shape, lambda b, g, i: (b, g, 0, 0))
    return pl.pallas_call(
        _nsa_kernel,
        grid=(batch, N_KV, nq),
        in_specs=[
            pl.BlockSpec((Q_BLOCK, gw), lambda b, g, i: (b * nq + i, COL_Q // gw + g)),
            pl.BlockSpec((Q_BLOCK, LANES), lambda b, g, i: (b * nq + i, COL_GN // LANES + g)),
            pl.BlockSpec((1, HEAD_DIM // 2, Q_BLOCK), lambda b, g, i: (b, 0, i)),
            pl.BlockSpec((1, HEAD_DIM // 2, Q_BLOCK), lambda b, g, i: (b, 0, i)),
            per_bg((nc, HEAD_DIM)), per_bg((HEAD_DIM, nc)),
            per_bg((seq, LANES)), per_bg((V_ROWS, seq)),
            per_bg((seq, LANES)), per_bg((V_ROWS, seq)),
            pl.BlockSpec((LANES, nc), lambda b, g, i: (0, 0)),
        ],
        out_specs=pl.BlockSpec((Q_BLOCK, gw), lambda b, g, i: (b * nq + i, g)),
        out_shape=jax.ShapeDtypeStruct((batch * seq, N_HEADS * HEAD_DIM), BF16),
        scratch_shapes=[pltpu.VMEM((LANES, Q_BLOCK), F32), pltpu.VMEM((SEL_TILE, GROUP * Q_BLOCK), F32),
                        pltpu.VMEM((SEL_TILE, GROUP * Q_BLOCK), F32), pltpu.VMEM((HEAD_DIM, GROUP * Q_BLOCK), F32),
                        pltpu.VMEM((LANES, Q_BLOCK), F32), pltpu.VMEM((SUBLANES, LANES), F32)],
        compiler_params=_cparams(("arbitrary", "arbitrary", "arbitrary")),
        name="nsa_attention",
    )(proj, proj, cos_q, sin_q, kc, vct, ks, vst, kw, vwt, overlap_t)


def _mem_kernel(q_ref, kv_ref, o_ref):
    q = q_ref[...]
    kv = kv_ref[0]
    mem_w = MEM_HEADS * MEM_HEAD_DIM
    outs = []
    for h in range(MEM_HEADS):
        sl = slice(h * MEM_HEAD_DIM, (h + 1) * MEM_HEAD_DIM)
        qh = (q[:, sl] * (MEM_HEAD_DIM ** -0.5)).astype(BF16)
        s = lax.dot_general(qh, kv[:, sl], (((1,), (1,)), ((), ())), preferred_element_type=F32)
        p = jnp.exp(s - jnp.max(s, axis=-1, keepdims=True))
        p = p / jnp.sum(p, axis=-1, keepdims=True)
        outs.append(jnp.dot(p.astype(BF16), kv[:, mem_w + h * MEM_HEAD_DIM : mem_w + (h + 1) * MEM_HEAD_DIM],
                            preferred_element_type=F32))
    o_ref[...] = jnp.concatenate(outs, axis=1).astype(o_ref.dtype)


def memory_attention(proj, kv_mem, batch, seq, tm):
    nt = seq // tm
    mem_len = kv_mem.shape[1]
    return pl.pallas_call(
        _mem_kernel,
        grid=(batch, nt),
        in_specs=[
            pl.BlockSpec((tm, D_MODEL), lambda b, t: (b * nt + t, COL_QM // D_MODEL)),
            pl.BlockSpec((1, mem_len, 2 * D_MODEL), lambda b, t: (b, 0, 0)),
        ],
        out_specs=pl.BlockSpec((tm, D_MODEL), lambda b, t: (b * nt + t, 0)),
        out_shape=jax.ShapeDtypeStruct((batch * seq, D_MODEL), BF16),
        compiler_params=_cparams(("arbitrary", "arbitrary")),
        name="memory_attention",
    )(proj, kv_mem)


def _merge_kernel(oa_ref, ob_ref, oc_ref, ga_ref, gb_ref, gc_ref, wa_ref, wb_ref, wc_ref, wo_ref,
                  gpost_ref, x_ref, gnext_ref, xo_ref, ho_ref):
    gate = lambda ref: _sigmoid(ref[...].astype(F32))
    merged = gate(ga_ref) * jnp.dot(oa_ref[...], wa_ref[...], preferred_element_type=F32)
    merged += gate(gb_ref) * jnp.dot(ob_ref[...], wb_ref[...], preferred_element_type=F32)
    merged += gate(gc_ref) * jnp.dot(oc_ref[...], wc_ref[...], preferred_element_type=F32)
    y = jnp.dot(merged.astype(BF16), wo_ref[...], preferred_element_type=F32)
    x_new = x_ref[...] + _rms(y, gpost_ref[...])
    xo_ref[...] = x_new
    ho_ref[...] = _rms(x_new, gnext_ref[...]).astype(ho_ref.dtype)


def merge_out(oa, ob, oc, proj, wa, wb, wc, wo, layer, g_post, x, g_next, tm):
    n, d = x.shape
    row = lambda col: pl.BlockSpec((tm, d), lambda i: (i, col))
    full = lambda shape: pl.BlockSpec(shape, lambda i: (0, 0))
    w_spec = pl.BlockSpec((None, d, d), lambda i: (layer, 0, 0))
    gm = COL_GM // d
    return pl.pallas_call(
        _merge_kernel,
        grid=(n // tm,),
        in_specs=[row(0), row(0), row(0), row(gm), row(gm + 1), row(gm + 2),
                  w_spec, w_spec, w_spec, w_spec, full((1, d)), row(0), full((1, d))],
        out_specs=[row(0), row(0)],
        out_shape=[jax.ShapeDtypeStruct((n, d), F32), jax.ShapeDtypeStruct((n, d), BF16)],
        compiler_params=_cparams(("arbitrary",)),
        name="merge_out",
    )(oa, ob, oc, proj, proj, proj, wa, wb, wc, wo, g_post.reshape(1, d), x, g_next.reshape(1, d))


def _mlp_kernel(h_ref, w1_ref, w2_ref, gpost_ref, x_ref, gnext_ref, xo_ref, ho_ref, acc_ref):
    k = pl.program_id(1)

    @pl.when(k == 0)
    def _():
        acc_ref[...] = jnp.zeros_like(acc_ref)

    u = jnp.maximum(jnp.dot(h_ref[...], w1_ref[...], preferred_element_type=F32), 0.0)
    acc_ref[...] += jnp.dot((u * u).astype(BF16), w2_ref[...], preferred_element_type=F32)

    @pl.when(k == pl.num_programs(1) - 1)
    def _():
        x_new = x_ref[...] + _rms(acc_ref[...], gpost_ref[...])
        xo_ref[...] = x_new
        ho_ref[...] = _rms(x_new, gnext_ref[...]).astype(ho_ref.dtype)


def mlp(h, w1, w2, layer, g_post, x, g_next, tm, tf):
    n, d = x.shape
    ff = w1.shape[2]
    return pl.pallas_call(
        _mlp_kernel,
        grid=(n // tm, ff // tf),
        in_specs=[
            pl.BlockSpec((tm, d), lambda i, k: (i, 0)),
            pl.BlockSpec((None, d, tf), lambda i, k: (layer, 0, k)),
            pl.BlockSpec((None, tf, d), lambda i, k: (layer, k, 0)),
            pl.BlockSpec((1, d), lambda i, k: (0, 0)),
            pl.BlockSpec((tm, d), lambda i, k: (i, 0)),
            pl.BlockSpec((1, d), lambda i, k: (0, 0)),
        ],
        out_specs=[pl.BlockSpec((tm, d), lambda i, k: (i, 0))] * 2,
        out_shape=[jax.ShapeDtypeStruct((n, d), F32), jax.ShapeDtypeStruct((n, d), BF16)],
        scratch_shapes=[pltpu.VMEM((tm, d), F32)],
        compiler_params=_cparams(("arbitrary", "arbitrary")),
        name="mlp",
    )(h, w1, w2, g_post.reshape(1, d), x, g_next.reshape(1, d))


def _pack_w_in(w):
    o_kv, o_gn, o_qm, o_gm = 3072, 4608, 4656, 5680
    per_group = GROUP * 3
    lead = w.shape[:-1]
    gn = w[..., o_gn:o_qm].reshape(lead + (N_KV, per_group))
    gn = jnp.pad(gn, ((0, 0),) * (len(lead) + 1) + ((0, LANES - per_group),)).reshape(lead + (N_KV * LANES,))
    packed = jnp.concatenate([w[..., :o_kv], w[..., o_qm:o_gm], w[..., o_gm:], w[..., o_kv:o_gn], gn], axis=-1)
    assert packed.shape[-1] == D_INP
    return packed.astype(BF16)


def _overlap_matrix_t(nc):
    c0 = np.arange(nc)[:, None] * CMP_STRIDE
    s0 = np.arange(LANES)[None, :] * SEL_LEN
    ov = np.clip(np.minimum(c0 + CMP_LEN, s0 + SEL_LEN) - np.maximum(c0, s0), 0, None).astype(np.float32) / CMP_LEN
    return jnp.asarray(ov.T, dtype=BF16)


def kernel(x, mem, positions, ln_mix_pre, w_in, conv_w, conv_b, lru_wr, lru_br, lru_wi, lru_bi, lru_lambda, cmp_pe, cmp_w1, cmp_b1, cmp_w2, ln_mem, w_mem_kv, w_br_rnn, w_br_nsa, w_br_mem, w_out, ln_mix_post, ln_mlp_pre, mlp_w1, mlp_w2, ln_mlp_post):
    batch, seq, d = x.shape
    depth = w_in.shape[0]
    mem_len = mem.shape[1]
    n = batch * seq
    nc = seq // CMP_STRIDE
    assert d == D_MODEL and seq % (2 * SEL_TILE) == 0 and seq // SEL_LEN <= LANES and seq >= WIN_KEYS

    tm = min(512, seq)
    tm_big = min(1024, seq)
    xf = x.reshape(n, d)
    memf = mem.reshape(batch * mem_len, d)
    cos_t, sin_t = rope_tables(positions.reshape(n), tm)
    pos_c = jnp.pad(positions[:, CMP_LEN - 1 :: CMP_STRIDE], ((0, 0), (0, 1)))
    cos_c, sin_c = rope_tables(pos_c.reshape(batch * nc), nc)
    cos_c, sin_c = cos_c.reshape(batch, nc, LANES), sin_c.reshape(batch, nc, LANES)
    cos_q, sin_q = rope_tables_t(positions, tm)
    overlap_t = _overlap_matrix_t(nc)

    w_in_b, w_kv_b = _pack_w_in(w_in.astype(BF16)), w_mem_kv.astype(BF16)
    wr_b, wi_b = lru_wr.astype(BF16), lru_wi.astype(BF16)
    w1c_b, w2c_b = cmp_w1.astype(BF16), cmp_w2.astype(BF16)
    wa_b, wb_b, wc_b, wo_b = (w.astype(BF16) for w in (w_br_rnn, w_br_nsa, w_br_mem, w_out))
    w1_b, w2_b = mlp_w1.astype(BF16), mlp_w2.astype(BF16)

    h = rmsnorm_bf16(xf, ln_mix_pre[0], tm)
    for l in range(depth):
        proj = matmul(h, w_in_b, l, tm_big, 1024, BF16, "in_proj")

        o_a = rglru_branch(proj, batch, seq, conv_w[l], conv_b[l], wr_b[l], lru_br[l], wi_b[l], lru_bi[l],
                           lru_lambda[l], min(256, seq))

        kc_raw, vc_raw, ks, vst, kw, vwt = kv_prep(proj, cos_t, sin_t, batch, seq, tm)
        chunks = lambda a: a.reshape(batch * N_KV, nc, CMP_STRIDE * HEAD_DIM)
        pe = cmp_pe[l].reshape(2, 2, CMP_STRIDE * HEAD_DIM)
        b1c = cmp_b1[l].reshape(2, 1, -1)
        k_cmp = compress(chunks(kc_raw), 0, pe, w1c_b[l], b1c, w2c_b[l], cos_c, sin_c, batch, True)
        v_cmp = compress(chunks(vc_raw), 1, pe, w1c_b[l], b1c, w2c_b[l], cos_c, sin_c, batch, False)
        kc = k_cmp.reshape(batch, N_KV, nc, HEAD_DIM)
        vct = v_cmp.reshape(batch, N_KV, nc, HEAD_DIM).transpose(0, 1, 3, 2)
        o_b = nsa_attention(proj, cos_q, sin_q, kc, vct, ks, vst, kw, vwt, overlap_t, batch, seq)

        mem_h = rmsnorm_bf16(memf, ln_mem[l], mem_len)
        kv_mem = matmul(mem_h, w_kv_b, l, mem_len, 1024, BF16, "mem_kv")
        o_c = memory_attention(proj, kv_mem.reshape(batch, mem_len, 2 * D_MODEL), batch, seq, tm)

        xf, h = merge_out(o_a, o_b, o_c, proj, wa_b, wb_b, wc_b, wo_b, l, ln_mix_post[l], xf, ln_mlp_pre[l], tm)
        xf, h = mlp(h, w1_b, w2_b, l, ln_mlp_post[l], xf, ln_mix_pre[(l + 1) % depth], tm_big, 1024)
    return xf.reshape(batch, seq, d)
```

```python
import functools

import jax
import jax.numpy as jnp
import numpy as np
from jax import lax
from jax.experimental import pallas as pl
from jax.experimental.pallas import tpu as pltpu

F32 = jnp.float32
BF16 = jnp.bfloat16

D_MODEL = 1024
LRU_BLOCKS = 8
LRU_BW = D_MODEL // LRU_BLOCKS
CONV_W = 4
LRU_C = 8.0
N_HEADS = 16
HEAD_DIM = 64
N_KV = 4
GROUP = N_HEADS // N_KV
KV_W = N_KV * HEAD_DIM
CMP_STRIDE = 16
CMP_LEN = 32
SEL_LEN = 64
SEL_SHIFT = 6
N_SELECT = 16
WINDOW = 512
Q_BLOCK = 256
MEM_HEADS = 4
MEM_HEAD_DIM = D_MODEL // MEM_HEADS
D_FF = 4 * D_MODEL
ROPE_THETA = 10000.0
EPS = 1e-6
NEG = -1e30
FORCE = 1e4
TAKEN = -3e38
LOG2_E = 1.4426950408889634
MAX_OFFSET = 50.0
BOUND_SLACK = 1.01

LANES = 128
SUBLANES = 8
BF16_ROWS = 2 * SUBLANES
SEL_TILE = 512
WIN_KEYS = WINDOW + Q_BLOCK
V_ROWS = 80
VMEM_LIMIT = 48 * 1024 * 1024

COL_XR, COL_YR, COL_Q, COL_QM, COL_GM, COL_KV, COL_GN = 0, 1024, 2048, 3072, 4096, 7168, 8704
D_INP = 9216


def _cparams(sem):
    return pltpu.CompilerParams(dimension_semantics=sem, vmem_limit_bytes=VMEM_LIMIT)


def _sigmoid(x):
    return 0.5 * jnp.tanh(0.5 * x) + 0.5


def _gelu_tanh(x):
    return 0.5 * x * (1.0 + jnp.tanh(0.7978845608028654 * (x + 0.044715 * (x * x * x))))


def _rms(x, g):
    return x * lax.rsqrt(jnp.mean(x * x, axis=-1, keepdims=True) + EPS) * g


def _rms_kernel(x_ref, g_ref, o_ref):
    o_ref[...] = _rms(x_ref[...], g_ref[...]).astype(o_ref.dtype)


def rmsnorm_bf16(x, g, tm):
    m, d = x.shape
    return pl.pallas_call(
        _rms_kernel,
        grid=(m // tm,),
        in_specs=[pl.BlockSpec((tm, d), lambda i: (i, 0)), pl.BlockSpec((1, d), lambda i: (0, 0))],
        out_specs=pl.BlockSpec((tm, d), lambda i: (i, 0)),
        out_shape=jax.ShapeDtypeStruct((m, d), BF16),
        compiler_params=_cparams(("arbitrary",)),
        name="rmsnorm",
    )(x, g.reshape(1, d))


def _mm_kernel(a_ref, w_ref, o_ref):
    o_ref[...] = jnp.dot(a_ref[...], w_ref[...], preferred_element_type=F32).astype(o_ref.dtype)


def matmul(a, w, layer, tm, tn, out_dtype, name):
    m, k = a.shape
    n = w.shape[2]
    return pl.pallas_call(
        _mm_kernel,
        grid=(n // tn, m // tm),
        in_specs=[pl.BlockSpec((tm, k), lambda j, i: (i, 0)), pl.BlockSpec((None, k, tn), lambda j, i: (layer, 0, j))],
        out_specs=pl.BlockSpec((tm, tn), lambda j, i: (i, j)),
        out_shape=jax.ShapeDtypeStruct((m, n), out_dtype),
        compiler_params=_cparams(("arbitrary", "arbitrary")),
        name=name,
    )(a, w)


def _rope_table_kernel(pos_ref, inv_ref, cos_ref, sin_ref):
    ang = pos_ref[...].astype(F32) * inv_ref[...]
    lane = lax.broadcasted_iota(jnp.int32, ang.shape, 1)
    cos_ref[...] = jnp.cos(ang)
    sin_ref[...] = jnp.where((lane & (HEAD_DIM - 1)) < HEAD_DIM // 2, -1.0, 1.0) * jnp.sin(ang)


def rope_tables(pos_flat, tm):
    n = pos_flat.shape[0]
    half = HEAD_DIM // 2
    inv = ROPE_THETA ** (-jnp.arange(half, dtype=F32) * 2.0 / HEAD_DIM)
    inv_full = jnp.tile(inv, LANES // half).reshape(1, LANES)
    return pl.pallas_call(
        _rope_table_kernel,
        grid=(n // tm,),
        in_specs=[pl.BlockSpec((tm, 1), lambda i: (i, 0)), pl.BlockSpec((1, LANES), lambda i: (0, 0))],
        out_specs=[pl.BlockSpec((tm, LANES), lambda i: (i, 0))] * 2,
        out_shape=[jax.ShapeDtypeStruct((n, LANES), F32)] * 2,
        compiler_params=_cparams(("arbitrary",)),
        name="rope_tables",
    )(pos_flat.reshape(n, 1), inv_full)


def _rope_table_t_kernel(pos_ref, inv_ref, cos_ref, sin_ref):
    ang = inv_ref[...] * pos_ref[0].astype(F32)
    cos_ref[0] = jnp.cos(ang)
    sin_ref[0] = jnp.sin(ang)


def rope_tables_t(positions, tm):
    batch, seq = positions.shape
    half = HEAD_DIM // 2
    inv = (ROPE_THETA ** (-jnp.arange(half, dtype=F32) * 2.0 / HEAD_DIM)).reshape(half, 1)
    return pl.pallas_call(
        _rope_table_t_kernel,
        grid=(batch, seq // tm),
        in_specs=[pl.BlockSpec((1, 1, tm), lambda b, i: (b, 0, i)), pl.BlockSpec((half, 1), lambda b, i: (0, 0))],
        out_specs=[pl.BlockSpec((1, half, tm), lambda b, i: (b, 0, i))] * 2,
        out_shape=[jax.ShapeDtypeStruct((batch, half, seq), F32)] * 2,
        compiler_params=_cparams(("arbitrary", "arbitrary")),
        name="rope_tables_t",
    )(positions.reshape(batch, 1, seq), inv)


def _rope128(x, cos_t, sin_t):
    lane = lax.broadcasted_iota(jnp.int32, x.shape, 1)
    first = (lane & (HEAD_DIM - 1)) < HEAD_DIM // 2
    partner = jnp.where(first, pltpu.roll(x, LANES - HEAD_DIM // 2, 1), pltpu.roll(x, HEAD_DIM // 2, 1))
    return x * cos_t + partner * sin_t


def _rglru_kernel(xr_ref, yr_ref, cw_ref, cb_ref, wr_ref, br_ref, wi_ref, bi_ref, lam_ref, o_ref, h_sc, tail_sc):
    @pl.when(pl.program_id(1) == 0)
    def _():
        h_sc[...] = jnp.zeros_like(h_sc)
        tail_sc[...] = jnp.zeros_like(tail_sc)

    xr = xr_ref[...].astype(F32)
    t_len, d = xr.shape
    tail = tail_sc[...]
    row8 = lax.broadcasted_iota(jnp.int32, (8, d), 0)
    cw = cw_ref[...]
    xc = cb_ref[...] + xr * cw[CONV_W - 1 : CONV_W, :]
    for k in range(1, CONV_W):
        rolled = pltpu.roll(xr, k, 0)
        head = jnp.where(row8 < k, pltpu.roll(tail, k, 0), rolled[0:8])
        shifted = jnp.concatenate([head, rolled[8:]], axis=0)
        xc = xc + shifted * cw[CONV_W - 1 - k : CONV_W - k, :]
    tail_sc[...] = xr[t_len - 8 :]

    xcb = xc.astype(BF16)
    rl, il = [], []
    for n in range(LRU_BLOCKS):
        xb = xcb[:, n * LRU_BW : (n + 1) * LRU_BW]
        rl.append(jnp.dot(xb, wr_ref[n], preferred_element_type=F32))
        il.append(jnp.dot(xb, wi_ref[n], preferred_element_type=F32))
    r = _sigmoid(jnp.concatenate(rl, axis=1) + br_ref[...])
    ig = _sigmoid(jnp.concatenate(il, axis=1) + bi_ref[...])
    softplus_neg_lam = jnp.log1p(jnp.exp(-lam_ref[...]))
    log_a = (-LRU_C * softplus_neg_lam) * r
    a = jnp.exp(log_a)
    one_minus_a2 = 1.0 - a * a
    root = jnp.where(one_minus_a2 > 0.0, one_minus_a2 * lax.rsqrt(one_minus_a2), 0.0)
    b = root * (ig * xc)

    sub = lax.broadcasted_iota(jnp.int32, (t_len, d), 0) & (SUBLANES - 1)
    step = 1
    while step < SUBLANES:
        keep = sub >= step
        a_sh = jnp.where(keep, pltpu.roll(a, step, 0), 1.0)
        b_sh = jnp.where(keep, pltpu.roll(b, step, 0), 0.0)
        b = a * b_sh + b
        a = a * a_sh
        step *= 2
    h_prev = h_sc[...]
    groups = []
    for g in range(t_len // SUBLANES):
        sl = slice(g * SUBLANES, (g + 1) * SUBLANES)
        groups.append(b[sl] + a[sl] * h_prev)
        h_prev = jnp.broadcast_to(groups[-1][SUBLANES - 1 :], (SUBLANES, d))
    h_sc[...] = h_prev
    h = jnp.concatenate(groups, axis=0)
    o_ref[...] = (h * _gelu_tanh(yr_ref[...].astype(F32))).astype(o_ref.dtype)


def rglru_branch(proj, batch, seq, conv_w, conv_b, wr, br, wi, bi, lam, t_len):
    d = D_MODEL
    nt = seq // t_len
    vec = lambda v: v.reshape(1, d)
    full2 = lambda shape: pl.BlockSpec(shape, lambda b, t: (0,) * len(shape))
    return pl.pallas_call(
        _rglru_kernel,
        grid=(batch, nt),
        in_specs=[
            pl.BlockSpec((t_len, d), lambda b, t: (b * nt + t, COL_XR // d)),
            pl.BlockSpec((t_len, d), lambda b, t: (b * nt + t, COL_YR // d)),
            full2((CONV_W, d)), full2((1, d)),
            full2((LRU_BLOCKS, LRU_BW, LRU_BW)), full2((1, d)),
            full2((LRU_BLOCKS, LRU_BW, LRU_BW)), full2((1, d)),
            full2((1, d)),
        ],
        out_specs=pl.BlockSpec((t_len, d), lambda b, t: (b * nt + t, 0)),
        out_shape=jax.ShapeDtypeStruct((batch * seq, d), BF16),
        scratch_shapes=[pltpu.VMEM((8, d), F32), pltpu.VMEM((8, d), F32)],
        compiler_params=_cparams(("arbitrary", "arbitrary")),
        name="rglru",
    )(proj, proj, conv_w, vec(conv_b), wr.astype(BF16), vec(br), wi.astype(BF16), vec(bi), vec(lam))


def _kv_prep_kernel(kvc_ref, kvs_ref, kvw_ref, cos_ref, sin_ref, kco_ref, vco_ref, kso_ref, vso_ref, kwo_ref, vwo_ref):
    cos_t, sin_t = cos_ref[...], sin_ref[...]
    t_len = cos_t.shape[0]
    xc = kvc_ref[...]
    for g in range(N_KV):
        kco_ref[0, g] = xc[:, g * HEAD_DIM : (g + 1) * HEAD_DIM]
        vco_ref[0, g] = xc[:, KV_W + g * HEAD_DIM : KV_W + (g + 1) * HEAD_DIM]
    pad_row = lax.broadcasted_iota(jnp.int32, (V_ROWS - HEAD_DIM, t_len), 0)
    pad_rows = jnp.where(pad_row == 0, 1.0, 0.0).astype(BF16)
    pad_lane = lax.broadcasted_iota(jnp.int32, (t_len, LANES - HEAD_DIM), 1)
    pad_lanes = jnp.where(pad_lane == 0, 1.0, 0.0)
    for src, k_dst, v_dst in ((kvs_ref, kso_ref, vso_ref), (kvw_ref, kwo_ref, vwo_ref)):
        x = src[...].astype(F32)
        roped = [_rope128(x[:, c * LANES : (c + 1) * LANES], cos_t, sin_t) for c in range(KV_W // LANES)]
        v_t = x[:, KV_W:].T
        for g in range(N_KV):
            pair = roped[g * HEAD_DIM // LANES]
            lo = g * HEAD_DIM % LANES
            k_dst[0, g] = jnp.concatenate([pair[:, lo : lo + HEAD_DIM], pad_lanes], axis=1).astype(BF16)
            v_dst[0, g, :HEAD_DIM, :] = v_t[g * HEAD_DIM : (g + 1) * HEAD_DIM].astype(BF16)
            v_dst[0, g, HEAD_DIM:, :] = pad_rows


def kv_prep(proj, cos_t, sin_t, batch, seq, tm):
    nt = seq // tm
    blk = lambda col: pl.BlockSpec((tm, 2 * KV_W), lambda b, t: (b * nt + t, col // (2 * KV_W)))
    tab = pl.BlockSpec((tm, LANES), lambda b, t: (b * nt + t, 0))
    k_spec = pl.BlockSpec((1, N_KV, tm, HEAD_DIM), lambda b, t: (b, 0, t, 0))
    v_spec = pl.BlockSpec((1, N_KV, V_ROWS, tm), lambda b, t: (b, 0, 0, t))
    k_shape = jax.ShapeDtypeStruct((batch, N_KV, seq, HEAD_DIM), BF16)
    v_shape = jax.ShapeDtypeStruct((batch, N_KV, V_ROWS, seq), BF16)
    kx_spec = pl.BlockSpec((1, N_KV, tm, LANES), lambda b, t: (b, 0, t, 0))
    kx_shape = jax.ShapeDtypeStruct((batch, N_KV, seq, LANES), BF16)
    return pl.pallas_call(
        _kv_prep_kernel,
        grid=(batch, nt),
        in_specs=[blk(COL_KV), blk(COL_KV + 2 * KV_W), blk(COL_KV + 4 * KV_W), tab, tab],
        out_specs=[k_spec, k_spec, kx_spec, v_spec, kx_spec, v_spec],
        out_shape=[k_shape, k_shape, kx_shape, v_shape, kx_shape, v_shape],
        compiler_params=_cparams(("arbitrary", "arbitrary")),
        name="kv_prep",
    )(proj, proj, proj, cos_t, sin_t)


def _q_prep_kernel(q_ref, cos_ref, sin_ref, qt_ref, qn_ref):
    hh = HEAD_DIM // 2
    cos_t, sin_t = cos_ref[0], sin_ref[0]
    q = q_ref[...].astype(F32)
    for g in range(N_KV):
        qt = q[:, g * GROUP * HEAD_DIM : (g + 1) * GROUP * HEAD_DIM].T
        norms = []
        for r in range(GROUP):
            x1 = qt[r * HEAD_DIM : r * HEAD_DIM + hh]
            x2 = qt[r * HEAD_DIM + hh : (r + 1) * HEAD_DIM]
            head = jnp.concatenate([x1 * cos_t - x2 * sin_t, x2 * cos_t + x1 * sin_t], axis=0)
            head = (head * (HEAD_DIM ** -0.5 * LOG2_E)).astype(BF16)
            qt_ref[0, g, r] = head
            h32 = head.astype(F32)
            norms.append(jnp.sqrt(jnp.sum(h32 * h32, axis=0, keepdims=True)))
        qn_ref[0, g] = jnp.concatenate(norms, axis=0)


def q_prep(proj, cos_q, sin_q, batch, seq, tm):
    nt = seq // tm
    width = N_HEADS * HEAD_DIM
    tab = pl.BlockSpec((1, HEAD_DIM // 2, tm), lambda b, t: (b, 0, t))
    return pl.pallas_call(
        _q_prep_kernel,
        grid=(batch, nt),
        in_specs=[pl.BlockSpec((tm, width), lambda b, t: (b * nt + t, COL_Q // width)), tab, tab],
        out_specs=[pl.BlockSpec((1, N_KV, GROUP, HEAD_DIM, tm), lambda b, t: (b, 0, 0, 0, t)),
                   pl.BlockSpec((1, N_KV, GROUP, tm), lambda b, t: (b, 0, 0, t))],
        out_shape=[jax.ShapeDtypeStruct((batch, N_KV, GROUP, HEAD_DIM, seq), BF16),
                   jax.ShapeDtypeStruct((batch, N_KV, GROUP, seq), F32)],
        compiler_params=_cparams(("arbitrary", "arbitrary")),
        name="q_prep",
    )(proj, cos_q, sin_q)


def _compress_kernel(x_ref, pe_ref, w1_ref, b1_ref, w2_ref, cos_ref, sin_ref, o_ref, *, rotary):
    x = x_ref[0]
    half = CMP_STRIDE * HEAD_DIM
    pe = pe_ref[0]
    w1 = w1_ref[0]
    u = jnp.dot((x + pe[0:1]).astype(BF16), w1[:half], preferred_element_type=F32)
    v = jnp.dot((x + pe[1:2]).astype(BF16), w1[half:], preferred_element_type=F32)
    hid = _gelu_tanh(u + pltpu.roll(v, v.shape[0] - 1, 0) + b1_ref[0])
    out = jnp.dot(hid.astype(BF16), w2_ref[0], preferred_element_type=F32)
    if rotary:
        hh = HEAD_DIM // 2
        partner = jnp.concatenate([out[:, hh:], out[:, :hh]], axis=1)
        out = out * cos_ref[0][:, :HEAD_DIM] + partner * sin_ref[0][:, :HEAD_DIM]
    o_ref[0] = out.astype(o_ref.dtype)


def compress(x_chunks, j, pe, w1, b1, w2, cos_c, sin_c, batch, rotary):
    bg, nc, width = x_chunks.shape
    return pl.pallas_call(
        functools.partial(_compress_kernel, rotary=rotary),
        grid=(bg,),
        in_specs=[
            pl.BlockSpec((1, nc, width), lambda i: (i, 0, 0)),
            pl.BlockSpec((1, 2, width), lambda i: (j, 0, 0)),
            pl.BlockSpec((1, 2 * width, w1.shape[2]), lambda i: (j, 0, 0)),
            pl.BlockSpec((1, 1, w1.shape[2]), lambda i: (j, 0, 0)),
            pl.BlockSpec((1, w2.shape[1], HEAD_DIM), lambda i: (j, 0, 0)),
            pl.BlockSpec((1, nc, LANES), lambda i: (i // N_KV, 0, 0)),
            pl.BlockSpec((1, nc, LANES), lambda i: (i // N_KV, 0, 0)),
        ],
        out_specs=pl.BlockSpec((1, nc, HEAD_DIM), lambda i: (i, 0, 0)),
        out_shape=jax.ShapeDtypeStruct((bg, nc, HEAD_DIM), BF16),
        compiler_params=_cparams(("arbitrary",)),
        name="compress_k" if rotary else "compress_v",
    )(x_chunks, pe, w1, b1, w2, cos_c, sin_c)


def _nsa_kernel(qt_ref, qn_ref, gate_ref, kc_ref, vct_ref, ks_ref, vst_ref, kw_ref, vwt_ref, ovt_ref,
                o_ref, sel_sc, sa_sc, sb_sc, oc_sc, imp_sc, kmax_sc):
    qb = Q_BLOCK
    s0 = pl.program_id(2) * qb
    nc = kc_ref.shape[2]
    cols = GROUP * qb

    q_t = jnp.concatenate([qt_ref[0, 0, r] for r in range(GROUP)], axis=1)
    q_norm = jnp.concatenate([qn_ref[0, 0, r : r + 1, :] for r in range(GROUP)], axis=1)

    tq = s0 + lax.broadcasted_iota(jnp.int32, (1, qb), 1)

    @pl.when(pl.program_id(2) == 0)
    def _():
        for row, k_ref in enumerate((ks_ref, kw_ref)):
            def norm_step(t, best, k_ref=k_ref):
                k_rows = k_ref[0, 0, pl.ds(pl.multiple_of(t * SEL_TILE, SEL_TILE), SEL_TILE), :].astype(F32)
                sq = jnp.sum(k_rows * k_rows, axis=1, keepdims=True)
                return jnp.maximum(best, jnp.max(sq, axis=0, keepdims=True))

            best = lax.fori_loop(0, k_ref.shape[2] // SEL_TILE, norm_step, jnp.zeros((1, 1), F32))
            kmax_sc[row : row + 1, :] = jnp.broadcast_to(jnp.sqrt(best), (1, LANES))

    offset_s = q_norm * (kmax_sc[0:1, 0:1] * BOUND_SLACK)
    offset_w = q_norm * (kmax_sc[1:2, 0:1] * BOUND_SLACK)
    bound_ok = jnp.max(jnp.maximum(offset_s, offset_w)) <= MAX_OFFSET
    ext_row = lax.broadcasted_iota(jnp.int32, (BF16_ROWS, cols), 0)
    zero_rows = jnp.zeros((LANES - HEAD_DIM - BF16_ROWS, cols), BF16)

    def extended(offset):
        return jnp.concatenate([q_t, jnp.where(ext_row == 0, -offset, 0.0).astype(BF16), zero_rows], axis=0)

    q_ext = extended(jnp.zeros_like(offset_s))

    def compressed(n_c):
        cmp_end = lax.broadcasted_iota(jnp.int32, (n_c, qb), 0) * CMP_STRIDE + (CMP_LEN - 1)
        bias_c = jnp.where(cmp_end <= tq, 0.0, NEG)
        has_key = jnp.where(tq >= CMP_LEN - 1, 1.0, 0.0)
        sc = jnp.dot(kc_ref[0, 0, :n_c, :], q_t, preferred_element_type=F32) + jnp.concatenate([bias_c] * GROUP, axis=1)
        p_c = jnp.exp2(sc - jnp.max(sc, axis=0, keepdims=True))
        norm = jnp.concatenate([has_key] * GROUP, axis=1) / jnp.maximum(jnp.sum(p_c, axis=0, keepdims=True), 1e-30)
        p_c = p_c * norm
        oc_sc[...] = jnp.dot(vct_ref[0, 0, :, :n_c], p_c.astype(BF16), preferred_element_type=F32)
        p_sum = p_c[:, :qb]
        for r in range(1, GROUP):
            p_sum = p_sum + p_c[:, r * qb : (r + 1) * qb]
        p_hi = p_sum.astype(BF16)
        p_lo = (p_sum - p_hi.astype(F32)).astype(BF16)
        ovt = ovt_ref[:, :n_c]
        imp_sc[...] = (jnp.dot(ovt, p_hi, preferred_element_type=F32)
                       + jnp.dot(ovt, p_lo, preferred_element_type=F32))

    last_cmp = (s0 + qb - CMP_LEN) // CMP_STRIDE
    for v in range(nc // LANES):
        pl.when(last_cmp // LANES == v)(functools.partial(compressed, (v + 1) * LANES))
    o_c = oc_sc[...]
    imp = imp_sc[...]

    def window(bounded):
        w0 = pl.multiple_of(jnp.maximum(s0 - WINDOW, 0), LANES)
        wpos = w0 + lax.broadcasted_iota(jnp.int32, (WIN_KEYS, qb), 0)
        bias_w = jnp.where((wpos <= tq) & (wpos > tq - WINDOW), 0.0, NEG)
        q_op = extended(offset_w) if bounded else q_ext
        s_w = jnp.dot(kw_ref[0, 0, pl.ds(w0, WIN_KEYS), :], q_op, preferred_element_type=F32)
        s_w = s_w + jnp.concatenate([bias_w] * GROUP, axis=1)
        if not bounded:
            s_w = s_w - jnp.max(s_w, axis=0, keepdims=True)
        acc_w = jnp.dot(vwt_ref[0, 0, :, pl.ds(w0, WIN_KEYS)], jnp.exp2(s_w).astype(BF16),
                        preferred_element_type=F32)
        return acc_w[:HEAD_DIM] / jnp.maximum(acc_w[HEAD_DIM : HEAD_DIM + 1], 1e-30)

    def select_blocks():
        blk = lax.broadcasted_iota(jnp.int32, (LANES, qb), 0)
        forced = (blk == 0) | (blk == (tq >> SEL_SHIFT))
        work = jnp.where(forced, FORCE, jnp.where(blk * SEL_LEN <= tq, imp, -FORCE))
        blk_f = blk.astype(F32)
        for _ in range(N_SELECT):
            m = jnp.max(work, axis=0, keepdims=True)
            idx = jnp.min(jnp.where(work == m, blk_f, float(LANES)), axis=0, keepdims=True)
            work = jnp.where(blk_f == idx, TAKEN, work)
        sel_sc[...] = jnp.where(work == TAKEN, 1.0, 0.0)

    blocks_per_tile = SEL_TILE // SEL_LEN
    row_minus_lane = (lax.broadcasted_iota(jnp.int32, (SEL_TILE, qb), 0)
                      - lax.broadcasted_iota(jnp.int32, (SEL_TILE, qb), 1))
    n_tiles = (s0 + qb - 1) // SEL_TILE + 1

    def scores(kb, q_op):
        k0 = pl.multiple_of(kb * SEL_TILE, SEL_TILE)
        picked = sel_sc[pl.ds(pl.multiple_of(kb * blocks_per_tile, blocks_per_tile), blocks_per_tile), :]
        picked = jnp.where(picked > 0.5, 0.0, NEG)
        picked = jnp.concatenate(
            [jnp.broadcast_to(picked[j : j + 1, :], (SEL_LEN, qb)) for j in range(blocks_per_tile)], axis=0)
        bias = jnp.where(row_minus_lane <= s0 - k0, picked, NEG)
        s_ = jnp.dot(ks_ref[0, 0, pl.ds(k0, SEL_TILE), :], q_op, preferred_element_type=F32)
        return s_ + jnp.concatenate([bias] * GROUP, axis=1)

    def values(kb, p):
        k0 = pl.multiple_of(kb * SEL_TILE, SEL_TILE)
        return jnp.dot(vst_ref[0, 0, :, pl.ds(k0, SEL_TILE)], p, preferred_element_type=F32)

    def bounded_path():
        o_win = window(True)
        select_blocks()
        q_off = extended(offset_s)

        def tile(kb, acc):
            return acc + values(kb, jnp.exp2(scores(kb, q_off)).astype(BF16))

        acc = lax.fori_loop(0, n_tiles // 2, lambda j, acc: tile(2 * j + 1, tile(2 * j, acc)),
                            jnp.zeros((V_ROWS, cols), F32))
        return o_win, lax.cond(n_tiles % 2 == 1, lambda acc: tile(n_tiles - 1, acc), lambda acc: acc, acc)

    def exact_path():
        o_win = window(False)
        select_blocks()

        def absorb(kb, s_, m_run, acc):
            m_new = jnp.maximum(m_run, jnp.max(s_, axis=0, keepdims=True))
            p = jnp.exp2(s_ - m_new).astype(BF16)
            return m_new, jnp.exp2(m_run - m_new) * acc + values(kb, p)

        sa_sc[...] = scores(0, q_ext)

        def sel_step(j, carry):
            m_run, acc = carry
            sb_sc[...] = scores(2 * j + 1, q_ext)
            m_run, acc = absorb(2 * j, sa_sc[...], m_run, acc)
            sa_sc[...] = scores(2 * j + 2, q_ext)
            return absorb(2 * j + 1, sb_sc[...], m_run, acc)

        init = (jnp.full((1, cols), NEG, F32), jnp.zeros((V_ROWS, cols), F32))
        full_trips = (n_tiles - 1) // 2
        m_run, acc = lax.fori_loop(0, full_trips, sel_step, init)
        m_run, acc = absorb(2 * full_trips, sa_sc[...], m_run, acc)

        def last_tile(carry):
            return absorb(n_tiles - 1, scores(n_tiles - 1, q_ext), *carry)

        return o_win, lax.cond(n_tiles - 2 * full_trips == 2, last_tile, lambda carry: carry, (m_run, acc))[1]

    o_w, acc_s = lax.cond(bound_ok, bounded_path, exact_path)

    o_s = acc_s[:HEAD_DIM] / jnp.maximum(acc_s[HEAD_DIM : HEAD_DIM + 1], 1e-30)

    gates_t = _sigmoid(gate_ref[...].astype(F32)).T
    outs = []
    for r in range(GROUP):
        sl = slice(r * qb, (r + 1) * qb)
        outs.append(gates_t[3 * r : 3 * r + 1] * o_c[:, sl] + gates_t[3 * r + 1 : 3 * r + 2] * o_s[:, sl]
                    + gates_t[3 * r + 2 : 3 * r + 3] * o_w[:, sl])
    o_ref[...] = jnp.concatenate(outs, axis=0).T.astype(o_ref.dtype)


def nsa_attention(proj, q_t, q_norm, kc, vct, ks, vst, kw, vwt, overlap_t, batch, seq):
    nq = seq // Q_BLOCK
    nc = kc.shape[2]
    gw = GROUP * HEAD_DIM
    per_bg = lambda shape: pl.BlockSpec((1, 1) + shape, lambda b, g, i: (b, g, 0, 0))
    return pl.pallas_call(
        _nsa_kernel,
        grid=(batch, N_KV, nq),
        in_specs=[
            pl.BlockSpec((1, 1, GROUP, HEAD_DIM, Q_BLOCK), lambda b, g, i: (b, g, 0, 0, i)),
            pl.BlockSpec((1, 1, GROUP, Q_BLOCK), lambda b, g, i: (b, g, 0, i)),
            pl.BlockSpec((Q_BLOCK, LANES), lambda b, g, i: (b * nq + i, COL_GN // LANES + g)),
            per_bg((nc, HEAD_DIM)), per_bg((HEAD_DIM, nc)),
            per_bg((seq, LANES)), per_bg((V_ROWS, seq)),
            per_bg((seq, LANES)), per_bg((V_ROWS, seq)),
            pl.BlockSpec((LANES, nc), lambda b, g, i: (0, 0)),
        ],
        out_specs=pl.BlockSpec((Q_BLOCK, gw), lambda b, g, i: (b * nq + i, g)),
        out_shape=jax.ShapeDtypeStruct((batch * seq, N_HEADS * HEAD_DIM), BF16),
        scratch_shapes=[pltpu.VMEM((LANES, Q_BLOCK), F32), pltpu.VMEM((SEL_TILE, GROUP * Q_BLOCK), F32),
                        pltpu.VMEM((SEL_TILE, GROUP * Q_BLOCK), F32), pltpu.VMEM((HEAD_DIM, GROUP * Q_BLOCK), F32),
                        pltpu.VMEM((LANES, Q_BLOCK), F32), pltpu.VMEM((SUBLANES, LANES), F32)],
        compiler_params=_cparams(("arbitrary", "arbitrary", "arbitrary")),
        name="nsa_attention",
    )(q_t, q_norm, proj, kc, vct, ks, vst, kw, vwt, overlap_t)


def _mem_kernel(q_ref, kv_ref, o_ref):
    q = q_ref[...]
    kv = kv_ref[0]
    mem_w = MEM_HEADS * MEM_HEAD_DIM
    outs = []
    for h in range(MEM_HEADS):
        sl = slice(h * MEM_HEAD_DIM, (h + 1) * MEM_HEAD_DIM)
        qh = (q[:, sl] * (MEM_HEAD_DIM ** -0.5)).astype(BF16)
        s = lax.dot_general(qh, kv[:, sl], (((1,), (1,)), ((), ())), preferred_element_type=F32)
        p = jnp.exp(s - jnp.max(s, axis=-1, keepdims=True))
        p = p / jnp.sum(p, axis=-1, keepdims=True)
        outs.append(jnp.dot(p.astype(BF16), kv[:, mem_w + h * MEM_HEAD_DIM : mem_w + (h + 1) * MEM_HEAD_DIM],
                            preferred_element_type=F32))
    o_ref[...] = jnp.concatenate(outs, axis=1).astype(o_ref.dtype)


def memory_attention(proj, kv_mem, batch, seq, tm):
    nt = seq // tm
    mem_len = kv_mem.shape[1]
    return pl.pallas_call(
        _mem_kernel,
        grid=(batch, nt),
        in_specs=[
            pl.BlockSpec((tm, D_MODEL), lambda b, t: (b * nt + t, COL_QM // D_MODEL)),
            pl.BlockSpec((1, mem_len, 2 * D_MODEL), lambda b, t: (b, 0, 0)),
        ],
        out_specs=pl.BlockSpec((tm, D_MODEL), lambda b, t: (b * nt + t, 0)),
        out_shape=jax.ShapeDtypeStruct((batch * seq, D_MODEL), BF16),
        compiler_params=_cparams(("arbitrary", "arbitrary")),
        name="memory_attention",
    )(proj, kv_mem)


def _merge_kernel(oa_ref, ob_ref, oc_ref, ga_ref, gb_ref, gc_ref, wa_ref, wb_ref, wc_ref, wo_ref,
                  gpost_ref, x_ref, gnext_ref, xo_ref, ho_ref):
    gate = lambda ref: _sigmoid(ref[...].astype(F32))
    merged = gate(ga_ref) * jnp.dot(oa_ref[...], wa_ref[...], preferred_element_type=F32)
    merged += gate(gb_ref) * jnp.dot(ob_ref[...], wb_ref[...], preferred_element_type=F32)
    merged += gate(gc_ref) * jnp.dot(oc_ref[...], wc_ref[...], preferred_element_type=F32)
    y = jnp.dot(merged.astype(BF16), wo_ref[...], preferred_element_type=F32)
    x_new = x_ref[...] + _rms(y, gpost_ref[...])
    xo_ref[...] = x_new
    ho_ref[...] = _rms(x_new, gnext_ref[...]).astype(ho_ref.dtype)


def merge_out(oa, ob, oc, proj, wa, wb, wc, wo, layer, g_post, x, g_next, tm):
    n, d = x.shape
    row = lambda col: pl.BlockSpec((tm, d), lambda i: (i, col))
    full = lambda shape: pl.BlockSpec(shape, lambda i: (0, 0))
    w_spec = pl.BlockSpec((None, d, d), lambda i: (layer, 0, 0))
    gm = COL_GM // d
    return pl.pallas_call(
        _merge_kernel,
        grid=(n // tm,),
        in_specs=[row(0), row(0), row(0), row(gm), row(gm + 1), row(gm + 2),
                  w_spec, w_spec, w_spec, w_spec, full((1, d)), row(0), full((1, d))],
        out_specs=[row(0), row(0)],
        out_shape=[jax.ShapeDtypeStruct((n, d), F32), jax.ShapeDtypeStruct((n, d), BF16)],
        compiler_params=_cparams(("arbitrary",)),
        name="merge_out",
    )(oa, ob, oc, proj, proj, proj, wa, wb, wc, wo, g_post.reshape(1, d), x, g_next.reshape(1, d))


def _mlp_kernel(h_ref, w1_ref, w2_ref, gpost_ref, x_ref, gnext_ref, xo_ref, ho_ref, acc_ref):
    k = pl.program_id(1)

    @pl.when(k == 0)
    def _():
        acc_ref[...] = jnp.zeros_like(acc_ref)

    u = jnp.maximum(jnp.dot(h_ref[...], w1_ref[...], preferred_element_type=F32), 0.0)
    acc_ref[...] += jnp.dot((u * u).astype(BF16), w2_ref[...], preferred_element_type=F32)

    @pl.when(k == pl.num_programs(1) - 1)
    def _():
        x_new = x_ref[...] + _rms(acc_ref[...], gpost_ref[...])
        xo_ref[...] = x_new
        ho_ref[...] = _rms(x_new, gnext_ref[...]).astype(ho_ref.dtype)


def mlp(h, w1, w2, layer, g_post, x, g_next, tm, tf):
    n, d = x.shape
    ff = w1.shape[2]
    return pl.pallas_call(
        _mlp_kernel,
        grid=(n // tm, ff // tf),
        in_specs=[
            pl.BlockSpec((tm, d), lambda i, k: (i, 0)),
            pl.BlockSpec((None, d, tf), lambda i, k: (layer, 0, k)),
            pl.BlockSpec((None, tf, d), lambda i, k: (layer, k, 0)),
            pl.BlockSpec((1, d), lambda i, k: (0, 0)),
            pl.BlockSpec((tm, d), lambda i, k: (i, 0)),
            pl.BlockSpec((1, d), lambda i, k: (0, 0)),
        ],
        out_specs=[pl.BlockSpec((tm, d), lambda i, k: (i, 0))] * 2,
        out_shape=[jax.ShapeDtypeStruct((n, d), F32), jax.ShapeDtypeStruct((n, d), BF16)],
        scratch_shapes=[pltpu.VMEM((tm, d), F32)],
        compiler_params=_cparams(("arbitrary", "arbitrary")),
        name="mlp",
    )(h, w1, w2, g_post.reshape(1, d), x, g_next.reshape(1, d))


def _pack_w_in(w):
    o_kv, o_gn, o_qm, o_gm = 3072, 4608, 4656, 5680
    per_group = GROUP * 3
    lead = w.shape[:-1]
    gn = w[..., o_gn:o_qm].reshape(lead + (N_KV, per_group))
    gn = jnp.pad(gn, ((0, 0),) * (len(lead) + 1) + ((0, LANES - per_group),)).reshape(lead + (N_KV * LANES,))
    packed = jnp.concatenate([w[..., :o_kv], w[..., o_qm:o_gm], w[..., o_gm:], w[..., o_kv:o_gn], gn], axis=-1)
    assert packed.shape[-1] == D_INP
    return packed.astype(BF16)


def _overlap_matrix_t(nc):
    c0 = np.arange(nc)[:, None] * CMP_STRIDE
    s0 = np.arange(LANES)[None, :] * SEL_LEN
    ov = np.clip(np.minimum(c0 + CMP_LEN, s0 + SEL_LEN) - np.maximum(c0, s0), 0, None).astype(np.float32) / CMP_LEN
    return jnp.asarray(ov.T, dtype=BF16)


def kernel(x, mem, positions, ln_mix_pre, w_in, conv_w, conv_b, lru_wr, lru_br, lru_wi, lru_bi, lru_lambda, cmp_pe, cmp_w1, cmp_b1, cmp_w2, ln_mem, w_mem_kv, w_br_rnn, w_br_nsa, w_br_mem, w_out, ln_mix_post, ln_mlp_pre, mlp_w1, mlp_w2, ln_mlp_post):
    batch, seq, d = x.shape
    depth = w_in.shape[0]
    mem_len = mem.shape[1]
    n = batch * seq
    nc = seq // CMP_STRIDE
    assert d == D_MODEL and seq % (2 * SEL_TILE) == 0 and seq // SEL_LEN <= LANES and seq >= WIN_KEYS

    tm = min(512, seq)
    tm_big = min(1024, seq)
    xf = x.reshape(n, d)
    memf = mem.reshape(batch * mem_len, d)
    cos_t, sin_t = rope_tables(positions.reshape(n), tm)
    pos_c = jnp.pad(positions[:, CMP_LEN - 1 :: CMP_STRIDE], ((0, 0), (0, 1)))
    cos_c, sin_c = rope_tables(pos_c.reshape(batch * nc), nc)
    cos_c, sin_c = cos_c.reshape(batch, nc, LANES), sin_c.reshape(batch, nc, LANES)
    cos_q, sin_q = rope_tables_t(positions, tm)
    overlap_t = _overlap_matrix_t(nc)

    w_in_b, w_kv_b = _pack_w_in(w_in.astype(BF16)), w_mem_kv.astype(BF16)
    wr_b, wi_b = lru_wr.astype(BF16), lru_wi.astype(BF16)
    w1c_b, w2c_b = cmp_w1.astype(BF16), cmp_w2.astype(BF16)
    wa_b, wb_b, wc_b, wo_b = (w.astype(BF16) for w in (w_br_rnn, w_br_nsa, w_br_mem, w_out))
    w1_b, w2_b = mlp_w1.astype(BF16), mlp_w2.astype(BF16)

    h = rmsnorm_bf16(xf, ln_mix_pre[0], tm)
    for l in range(depth):
        proj = matmul(h, w_in_b, l, tm_big, 1024, BF16, "in_proj")

        o_a = rglru_branch(proj, batch, seq, conv_w[l], conv_b[l], wr_b[l], lru_br[l], wi_b[l], lru_bi[l],
                           lru_lambda[l], min(256, seq))

        kc_raw, vc_raw, ks, vst, kw, vwt = kv_prep(proj, cos_t, sin_t, batch, seq, tm)
        chunks = lambda a: a.reshape(batch * N_KV, nc, CMP_STRIDE * HEAD_DIM)
        pe = cmp_pe[l].reshape(2, 2, CMP_STRIDE * HEAD_DIM)
        b1c = cmp_b1[l].reshape(2, 1, -1)
        k_cmp = compress(chunks(kc_raw), 0, pe, w1c_b[l], b1c, w2c_b[l], cos_c, sin_c, batch, True)
        v_cmp = compress(chunks(vc_raw), 1, pe, w1c_b[l], b1c, w2c_b[l], cos_c, sin_c, batch, False)
        kc = k_cmp.reshape(batch, N_KV, nc, HEAD_DIM)
        vct = v_cmp.reshape(batch, N_KV, nc, HEAD_DIM).transpose(0, 1, 3, 2)
        q_t, q_norm = q_prep(proj, cos_q, sin_q, batch, seq, tm)
        o_b = nsa_attention(proj, q_t, q_norm, kc, vct, ks, vst, kw, vwt, overlap_t, batch, seq)

        mem_h = rmsnorm_bf16(memf, ln_mem[l], mem_len)
        kv_mem = matmul(mem_h, w_kv_b, l, mem_len, 1024, BF16, "mem_kv")
        o_c = memory_attention(proj, kv_mem.reshape(batch, mem_len, 2 * D_MODEL), batch, seq, tm)

        xf, h = merge_out(o_a, o_b, o_c, proj, wa_b, wb_b, wc_b, wo_b, l, ln_mix_post[l], xf, ln_mlp_pre[l], tm)
        xf, h = mlp(h, w1_b, w2_b, l, ln_mlp_post[l], xf, ln_mix_pre[(l + 1) % depth], tm_big, 1024)
    return xf.reshape(batch, seq, d)
```

```python
import functools

import jax
import jax.numpy as jnp
import numpy as np
from jax import lax
from jax.experimental import pallas as pl
from jax.experimental.pallas import tpu as pltpu

F32 = jnp.float32
BF16 = jnp.bfloat16

D_MODEL = 1024
LRU_BLOCKS = 8
LRU_BW = D_MODEL // LRU_BLOCKS
CONV_W = 4
LRU_C = 8.0
N_HEADS = 16
HEAD_DIM = 64
N_KV = 4
GROUP = N_HEADS // N_KV
KV_W = N_KV * HEAD_DIM
CMP_STRIDE = 16
CMP_LEN = 32
SEL_LEN = 64
SEL_SHIFT = 6
N_SELECT = 16
WINDOW = 512
Q_BLOCK = 256
MEM_HEADS = 4
MEM_HEAD_DIM = D_MODEL // MEM_HEADS
D_FF = 4 * D_MODEL
ROPE_THETA = 10000.0
EPS = 1e-6
NEG = -1e30
FORCE = 1e4
TAKEN = -3e38
LOG2_E = 1.4426950408889634
MAX_OFFSET = 45.0
BOUND_SLACK = 1.01

LANES = 128
SUBLANES = 8
SEL_TILE = 512
WIN_KEYS = WINDOW + Q_BLOCK
V_ROWS = 80
VMEM_LIMIT = 48 * 1024 * 1024

COL_XR, COL_YR, COL_Q, COL_QM, COL_GM, COL_KV, COL_GN = 0, 1024, 2048, 3072, 4096, 7168, 8704
D_INP = 9216


def _cparams(sem):
    return pltpu.CompilerParams(dimension_semantics=sem, vmem_limit_bytes=VMEM_LIMIT)


def _sigmoid(x):
    return 0.5 * jnp.tanh(0.5 * x) + 0.5


def _gelu_tanh(x):
    return 0.5 * x * (1.0 + jnp.tanh(0.7978845608028654 * (x + 0.044715 * (x * x * x))))


def _rms(x, g):
    return x * lax.rsqrt(jnp.mean(x * x, axis=-1, keepdims=True) + EPS) * g


def _rms_kernel(x_ref, g_ref, o_ref):
    o_ref[...] = _rms(x_ref[...], g_ref[...]).astype(o_ref.dtype)


def rmsnorm_bf16(x, g, tm):
    m, d = x.shape
    return pl.pallas_call(
        _rms_kernel,
        grid=(m // tm,),
        in_specs=[pl.BlockSpec((tm, d), lambda i: (i, 0)), pl.BlockSpec((1, d), lambda i: (0, 0))],
        out_specs=pl.BlockSpec((tm, d), lambda i: (i, 0)),
        out_shape=jax.ShapeDtypeStruct((m, d), BF16),
        compiler_params=_cparams(("arbitrary",)),
        name="rmsnorm",
    )(x, g.reshape(1, d))


def _mm_kernel(a_ref, w_ref, o_ref):
    o_ref[...] = jnp.dot(a_ref[...], w_ref[...], preferred_element_type=F32).astype(o_ref.dtype)


def matmul(a, w, layer, tm, tn, out_dtype, name):
    m, k = a.shape
    n = w.shape[2]
    return pl.pallas_call(
        _mm_kernel,
        grid=(n // tn, m // tm),
        in_specs=[pl.BlockSpec((tm, k), lambda j, i: (i, 0)), pl.BlockSpec((None, k, tn), lambda j, i: (layer, 0, j))],
        out_specs=pl.BlockSpec((tm, tn), lambda j, i: (i, j)),
        out_shape=jax.ShapeDtypeStruct((m, n), out_dtype),
        compiler_params=_cparams(("arbitrary", "arbitrary")),
        name=name,
    )(a, w)


def _rope_table_kernel(pos_ref, inv_ref, cos_ref, sin_ref):
    ang = pos_ref[...].astype(F32) * inv_ref[...]
    lane = lax.broadcasted_iota(jnp.int32, ang.shape, 1)
    cos_ref[...] = jnp.cos(ang)
    sin_ref[...] = jnp.where((lane & (HEAD_DIM - 1)) < HEAD_DIM // 2, -1.0, 1.0) * jnp.sin(ang)


def rope_tables(pos_flat, tm):
    n = pos_flat.shape[0]
    half = HEAD_DIM // 2
    inv = ROPE_THETA ** (-jnp.arange(half, dtype=F32) * 2.0 / HEAD_DIM)
    inv_full = jnp.tile(inv, LANES // half).reshape(1, LANES)
    return pl.pallas_call(
        _rope_table_kernel,
        grid=(n // tm,),
        in_specs=[pl.BlockSpec((tm, 1), lambda i: (i, 0)), pl.BlockSpec((1, LANES), lambda i: (0, 0))],
        out_specs=[pl.BlockSpec((tm, LANES), lambda i: (i, 0))] * 2,
        out_shape=[jax.ShapeDtypeStruct((n, LANES), F32)] * 2,
        compiler_params=_cparams(("arbitrary",)),
        name="rope_tables",
    )(pos_flat.reshape(n, 1), inv_full)


def _rope_table_t_kernel(pos_ref, inv_ref, cos_ref, sin_ref):
    ang = inv_ref[...] * pos_ref[0].astype(F32)
    cos_ref[0] = jnp.cos(ang)
    sin_ref[0] = jnp.sin(ang)


def rope_tables_t(positions, tm):
    batch, seq = positions.shape
    half = HEAD_DIM // 2
    inv = (ROPE_THETA ** (-jnp.arange(half, dtype=F32) * 2.0 / HEAD_DIM)).reshape(half, 1)
    return pl.pallas_call(
        _rope_table_t_kernel,
        grid=(batch, seq // tm),
        in_specs=[pl.BlockSpec((1, 1, tm), lambda b, i: (b, 0, i)), pl.BlockSpec((half, 1), lambda b, i: (0, 0))],
        out_specs=[pl.BlockSpec((1, half, tm), lambda b, i: (b, 0, i))] * 2,
        out_shape=[jax.ShapeDtypeStruct((batch, half, seq), F32)] * 2,
        compiler_params=_cparams(("arbitrary", "arbitrary")),
        name="rope_tables_t",
    )(positions.reshape(batch, 1, seq), inv)


def _rope128(x, cos_t, sin_t):
    lane = lax.broadcasted_iota(jnp.int32, x.shape, 1)
    first = (lane & (HEAD_DIM - 1)) < HEAD_DIM // 2
    partner = jnp.where(first, pltpu.roll(x, LANES - HEAD_DIM // 2, 1), pltpu.roll(x, HEAD_DIM // 2, 1))
    return x * cos_t + partner * sin_t


def _rglru_kernel(xr_ref, yr_ref, cw_ref, cb_ref, wr_ref, br_ref, wi_ref, bi_ref, lam_ref, o_ref, h_sc, tail_sc):
    @pl.when(pl.program_id(1) == 0)
    def _():
        h_sc[...] = jnp.zeros_like(h_sc)
        tail_sc[...] = jnp.zeros_like(tail_sc)

    xr = xr_ref[...].astype(F32)
    t_len, d = xr.shape
    tail = tail_sc[...]
    row8 = lax.broadcasted_iota(jnp.int32, (SUBLANES, d), 0)
    cw = cw_ref[...]
    xc = cb_ref[...] + xr * cw[CONV_W - 1 : CONV_W, :]
    for k in range(1, CONV_W):
        rolled = pltpu.roll(xr, k, 0)
        head = jnp.where(row8 < k, pltpu.roll(tail, k, 0), rolled[0:SUBLANES])
        shifted = jnp.concatenate([head, rolled[SUBLANES:]], axis=0)
        xc = xc + shifted * cw[CONV_W - 1 - k : CONV_W - k, :]
    tail_sc[...] = xr[t_len - SUBLANES :]

    xcb = xc.astype(BF16)
    rl, il = [], []
    for n in range(LRU_BLOCKS):
        xb = xcb[:, n * LRU_BW : (n + 1) * LRU_BW]
        rl.append(jnp.dot(xb, wr_ref[n], preferred_element_type=F32))
        il.append(jnp.dot(xb, wi_ref[n], preferred_element_type=F32))
    r = _sigmoid(jnp.concatenate(rl, axis=1) + br_ref[...])
    ig = _sigmoid(jnp.concatenate(il, axis=1) + bi_ref[...])
    softplus_neg_lam = jnp.log1p(jnp.exp(-lam_ref[...]))
    log_a = (-LRU_C * softplus_neg_lam) * r
    a = jnp.exp(log_a)
    one_minus_a2 = 1.0 - a * a
    root = jnp.where(one_minus_a2 > 0.0, one_minus_a2 * lax.rsqrt(one_minus_a2), 0.0)
    b = root * (ig * xc)

    sub = lax.broadcasted_iota(jnp.int32, (t_len, d), 0) & (SUBLANES - 1)
    step = 1
    while step < SUBLANES:
        keep = sub >= step
        a_sh = jnp.where(keep, pltpu.roll(a, step, 0), 1.0)
        b_sh = jnp.where(keep, pltpu.roll(b, step, 0), 0.0)
        b = a * b_sh + b
        a = a * a_sh
        step *= 2
    h_prev = h_sc[...]
    groups = []
    for g in range(t_len // SUBLANES):
        sl = slice(g * SUBLANES, (g + 1) * SUBLANES)
        groups.append(b[sl] + a[sl] * h_prev)
        h_prev = jnp.broadcast_to(groups[-1][SUBLANES - 1 :], (SUBLANES, d))
    h_sc[...] = h_prev
    h = jnp.concatenate(groups, axis=0)
    o_ref[...] = (h * _gelu_tanh(yr_ref[...].astype(F32))).astype(o_ref.dtype)


def rglru_branch(proj, batch, seq, conv_w, conv_b, wr, br, wi, bi, lam, t_len):
    d = D_MODEL
    nt = seq // t_len
    vec = lambda v: v.reshape(1, d)
    full2 = lambda shape: pl.BlockSpec(shape, lambda b, t: (0,) * len(shape))
    return pl.pallas_call(
        _rglru_kernel,
        grid=(batch, nt),
        in_specs=[
            pl.BlockSpec((t_len, d), lambda b, t: (b * nt + t, COL_XR // d)),
            pl.BlockSpec((t_len, d), lambda b, t: (b * nt + t, COL_YR // d)),
            full2((CONV_W, d)), full2((1, d)),
            full2((LRU_BLOCKS, LRU_BW, LRU_BW)), full2((1, d)),
            full2((LRU_BLOCKS, LRU_BW, LRU_BW)), full2((1, d)),
            full2((1, d)),
        ],
        out_specs=pl.BlockSpec((t_len, d), lambda b, t: (b * nt + t, 0)),
        out_shape=jax.ShapeDtypeStruct((batch * seq, d), BF16),
        scratch_shapes=[pltpu.VMEM((SUBLANES, d), F32), pltpu.VMEM((SUBLANES, d), F32)],
        compiler_params=_cparams(("arbitrary", "arbitrary")),
        name="rglru",
    )(proj, proj, conv_w, vec(conv_b), wr.astype(BF16), vec(br), wi.astype(BF16), vec(bi), vec(lam))


def _kv_prep_kernel(kvc_ref, kvs_ref, kvw_ref, cos_ref, sin_ref, kco_ref, vco_ref, kso_ref, vso_ref, kwo_ref, vwo_ref):
    cos_t, sin_t = cos_ref[...], sin_ref[...]
    t_len = cos_t.shape[0]
    xc = kvc_ref[...]
    for g in range(N_KV):
        kco_ref[0, g] = xc[:, g * HEAD_DIM : (g + 1) * HEAD_DIM]
        vco_ref[0, g] = xc[:, KV_W + g * HEAD_DIM : KV_W + (g + 1) * HEAD_DIM]
    pad_row = lax.broadcasted_iota(jnp.int32, (V_ROWS - HEAD_DIM, t_len), 0)
    pad_rows = jnp.where(pad_row == 0, 1.0, 0.0).astype(BF16)
    pad_lane = lax.broadcasted_iota(jnp.int32, (t_len, LANES - HEAD_DIM), 1)
    pad_lanes = jnp.where(pad_lane == 0, 1.0, 0.0)
    for src, k_dst, v_dst in ((kvs_ref, kso_ref, vso_ref), (kvw_ref, kwo_ref, vwo_ref)):
        x = src[...].astype(F32)
        roped = [_rope128(x[:, c * LANES : (c + 1) * LANES], cos_t, sin_t) for c in range(KV_W // LANES)]
        v_t = x[:, KV_W:].T
        for g in range(N_KV):
            pair = roped[g * HEAD_DIM // LANES]
            lo = g * HEAD_DIM % LANES
            k_dst[0, g] = jnp.concatenate([pair[:, lo : lo + HEAD_DIM], pad_lanes], axis=1).astype(BF16)
            v_dst[0, g, :HEAD_DIM, :] = v_t[g * HEAD_DIM : (g + 1) * HEAD_DIM].astype(BF16)
            v_dst[0, g, HEAD_DIM:, :] = pad_rows


def kv_prep(proj, cos_t, sin_t, batch, seq, tm):
    nt = seq // tm
    blk = lambda col: pl.BlockSpec((tm, 2 * KV_W), lambda b, t: (b * nt + t, col // (2 * KV_W)))
    tab = pl.BlockSpec((tm, LANES), lambda b, t: (b * nt + t, 0))
    k_spec = pl.BlockSpec((1, N_KV, tm, HEAD_DIM), lambda b, t: (b, 0, t, 0))
    v_spec = pl.BlockSpec((1, N_KV, V_ROWS, tm), lambda b, t: (b, 0, 0, t))
    k_shape = jax.ShapeDtypeStruct((batch, N_KV, seq, HEAD_DIM), BF16)
    v_shape = jax.ShapeDtypeStruct((batch, N_KV, V_ROWS, seq), BF16)
    kx_spec = pl.BlockSpec((1, N_KV, tm, LANES), lambda b, t: (b, 0, t, 0))
    kx_shape = jax.ShapeDtypeStruct((batch, N_KV, seq, LANES), BF16)
    return pl.pallas_call(
        _kv_prep_kernel,
        grid=(batch, nt),
        in_specs=[blk(COL_KV), blk(COL_KV + 2 * KV_W), blk(COL_KV + 4 * KV_W), tab, tab],
        out_specs=[k_spec, k_spec, kx_spec, v_spec, kx_spec, v_spec],
        out_shape=[k_shape, k_shape, kx_shape, v_shape, kx_shape, v_shape],
        compiler_params=_cparams(("arbitrary", "arbitrary")),
        name="kv_prep",
    )(proj, proj, proj, cos_t, sin_t)


def _compress_kernel(x_ref, pe_ref, w1_ref, b1_ref, w2_ref, cos_ref, sin_ref, o_ref, *, rotary):
    x = x_ref[0]
    half = CMP_STRIDE * HEAD_DIM
    pe = pe_ref[0]
    w1 = w1_ref[0]
    u = jnp.dot((x + pe[0:1]).astype(BF16), w1[:half], preferred_element_type=F32)
    v = jnp.dot((x + pe[1:2]).astype(BF16), w1[half:], preferred_element_type=F32)
    hid = _gelu_tanh(u + pltpu.roll(v, v.shape[0] - 1, 0) + b1_ref[0])
    out = jnp.dot(hid.astype(BF16), w2_ref[0], preferred_element_type=F32)
    if rotary:
        hh = HEAD_DIM // 2
        partner = jnp.concatenate([out[:, hh:], out[:, :hh]], axis=1)
        out = out * cos_ref[0][:, :HEAD_DIM] + partner * sin_ref[0][:, :HEAD_DIM]
    o_ref[0] = out.astype(o_ref.dtype)


def compress(x_chunks, j, pe, w1, b1, w2, cos_c, sin_c, batch, rotary):
    bg, nc, width = x_chunks.shape
    return pl.pallas_call(
        functools.partial(_compress_kernel, rotary=rotary),
        grid=(bg,),
        in_specs=[
            pl.BlockSpec((1, nc, width), lambda i: (i, 0, 0)),
            pl.BlockSpec((1, 2, width), lambda i: (j, 0, 0)),
            pl.BlockSpec((1, 2 * width, w1.shape[2]), lambda i: (j, 0, 0)),
            pl.BlockSpec((1, 1, w1.shape[2]), lambda i: (j, 0, 0)),
            pl.BlockSpec((1, w2.shape[1], HEAD_DIM), lambda i: (j, 0, 0)),
            pl.BlockSpec((1, nc, LANES), lambda i: (i // N_KV, 0, 0)),
            pl.BlockSpec((1, nc, LANES), lambda i: (i // N_KV, 0, 0)),
        ],
        out_specs=pl.BlockSpec((1, nc, HEAD_DIM), lambda i: (i, 0, 0)),
        out_shape=jax.ShapeDtypeStruct((bg, nc, HEAD_DIM), BF16),
        compiler_params=_cparams(("arbitrary",)),
        name="compress_k" if rotary else "compress_v",
    )(x_chunks, pe, w1, b1, w2, cos_c, sin_c)


def _nsa_kernel(q_ref, gate_ref, cos_ref, sin_ref, kc_ref, vct_ref, ks_ref, vst_ref, kw_ref, vwt_ref, ovt_ref,
                o_ref, sel_sc, sa_sc, sb_sc, oc_sc, imp_sc, kmax_sc):
    qb = Q_BLOCK
    s0 = pl.program_id(2) * qb
    nc = kc_ref.shape[2]
    hh = HEAD_DIM // 2
    cols = GROUP * qb

    qt = q_ref[...].astype(F32).T
    cos_t, sin_t = cos_ref[0], sin_ref[0]
    heads = []
    for r in range(GROUP):
        x1 = qt[r * HEAD_DIM : r * HEAD_DIM + hh]
        x2 = qt[r * HEAD_DIM + hh : (r + 1) * HEAD_DIM]
        heads.append(jnp.concatenate([x1 * cos_t - x2 * sin_t, x2 * cos_t + x1 * sin_t], axis=0))
    q_t = (jnp.concatenate(heads, axis=1) * (HEAD_DIM ** -0.5 * LOG2_E)).astype(BF16)

    tq = s0 + lax.broadcasted_iota(jnp.int32, (1, qb), 1)

    @pl.when(pl.program_id(2) == 0)
    def _():
        for row, k_ref in enumerate((ks_ref, kw_ref)):
            def norm_step(t, best, k_ref=k_ref):
                k_rows = k_ref[0, 0, pl.ds(pl.multiple_of(t * SEL_TILE, SEL_TILE), SEL_TILE), :].astype(F32)
                sq = jnp.sum(k_rows * k_rows, axis=1, keepdims=True)
                return jnp.maximum(best, jnp.max(sq, axis=0, keepdims=True))

            best = lax.fori_loop(0, k_ref.shape[2] // SEL_TILE, norm_step, jnp.zeros((1, 1), F32))
            kmax_sc[row : row + 1, :] = jnp.broadcast_to(jnp.sqrt(best), (1, LANES))

    q_f32 = q_t.astype(F32)
    q_norm = jnp.sqrt(jnp.sum(q_f32 * q_f32, axis=0, keepdims=True))
    offset_s = q_norm * (kmax_sc[0:1, 0:1] * BOUND_SLACK)
    offset_w = q_norm * (kmax_sc[1:2, 0:1] * BOUND_SLACK)
    bound_ok = jnp.max(jnp.maximum(offset_s, offset_w)) <= MAX_OFFSET
    ext_row = lax.broadcasted_iota(jnp.int32, (SUBLANES, cols), 0)
    zero_rows = jnp.zeros((LANES - HEAD_DIM - SUBLANES, cols), F32)

    def extended(offset):
        return jnp.concatenate([q_f32, jnp.where(ext_row == 0, -offset, 0.0), zero_rows], axis=0).astype(BF16)

    q_ext = extended(jnp.zeros_like(offset_s))

    def compressed(n_c):
        cmp_end = lax.broadcasted_iota(jnp.int32, (n_c, qb), 0) * CMP_STRIDE + (CMP_LEN - 1)
        bias_c = jnp.where(cmp_end <= tq, 0.0, NEG)
        has_key = jnp.where(tq >= CMP_LEN - 1, 1.0, 0.0)
        sc = jnp.dot(kc_ref[0, 0, :n_c, :], q_t, preferred_element_type=F32) + jnp.concatenate([bias_c] * GROUP, axis=1)
        p_c = jnp.exp2(sc - jnp.max(sc, axis=0, keepdims=True))
        norm = jnp.concatenate([has_key] * GROUP, axis=1) / jnp.maximum(jnp.sum(p_c, axis=0, keepdims=True), 1e-30)
        p_c = p_c * norm
        oc_sc[...] = jnp.dot(vct_ref[0, 0, :, :n_c], p_c.astype(BF16), preferred_element_type=F32)
        p_sum = p_c[:, :qb]
        for r in range(1, GROUP):
            p_sum = p_sum + p_c[:, r * qb : (r + 1) * qb]
        p_hi = p_sum.astype(BF16)
        p_lo = (p_sum - p_hi.astype(F32)).astype(BF16)
        ovt = ovt_ref[:, :n_c]
        imp_sc[...] = (jnp.dot(ovt, p_hi, preferred_element_type=F32)
                       + jnp.dot(ovt, p_lo, preferred_element_type=F32))

    last_cmp = (s0 + qb - CMP_LEN) // CMP_STRIDE
    for v in range(nc // LANES):
        pl.when(last_cmp // LANES == v)(functools.partial(compressed, (v + 1) * LANES))
    o_c = oc_sc[...]
    imp = imp_sc[...]

    def window(bounded):
        w0 = pl.multiple_of(jnp.maximum(s0 - WINDOW, 0), LANES)
        wpos = w0 + lax.broadcasted_iota(jnp.int32, (WIN_KEYS, qb), 0)
        bias_w = jnp.where((wpos <= tq) & (wpos > tq - WINDOW), 0.0, NEG)
        q_op = extended(offset_w) if bounded else q_ext
        s_w = jnp.dot(kw_ref[0, 0, pl.ds(w0, WIN_KEYS), :], q_op, preferred_element_type=F32)
        s_w = s_w + jnp.concatenate([bias_w] * GROUP, axis=1)
        if not bounded:
            s_w = s_w - jnp.max(s_w, axis=0, keepdims=True)
        acc_w = jnp.dot(vwt_ref[0, 0, :, pl.ds(w0, WIN_KEYS)], jnp.exp2(s_w).astype(BF16),
                        preferred_element_type=F32)
        return acc_w[:HEAD_DIM] / jnp.maximum(acc_w[HEAD_DIM : HEAD_DIM + 1], 1e-30)

    def select_blocks():
        blk = lax.broadcasted_iota(jnp.int32, (LANES, qb), 0)
        forced = (blk == 0) | (blk == (tq >> SEL_SHIFT))
        work = jnp.where(forced, FORCE, jnp.where(blk * SEL_LEN <= tq, imp, -FORCE))
        blk_f = blk.astype(F32)
        for _ in range(N_SELECT):
            m = jnp.max(work, axis=0, keepdims=True)
            idx = jnp.min(jnp.where(work == m, blk_f, float(LANES)), axis=0, keepdims=True)
            work = jnp.where(blk_f == idx, TAKEN, work)
        sel_sc[...] = jnp.where(work == TAKEN, 1.0, 0.0)

    blocks_per_tile = SEL_TILE // SEL_LEN
    row_minus_lane = (lax.broadcasted_iota(jnp.int32, (SEL_TILE, qb), 0)
                      - lax.broadcasted_iota(jnp.int32, (SEL_TILE, qb), 1))
    n_tiles = (s0 + qb - 1) // SEL_TILE + 1

    def scores(kb, q_op):
        k0 = pl.multiple_of(kb * SEL_TILE, SEL_TILE)
        picked = sel_sc[pl.ds(pl.multiple_of(kb * blocks_per_tile, blocks_per_tile), blocks_per_tile), :]
        picked = jnp.where(picked > 0.5, 0.0, NEG)
        picked = jnp.concatenate(
            [jnp.broadcast_to(picked[j : j + 1, :], (SEL_LEN, qb)) for j in range(blocks_per_tile)], axis=0)
        bias = jnp.where(row_minus_lane <= s0 - k0, picked, NEG)
        s_ = jnp.dot(ks_ref[0, 0, pl.ds(k0, SEL_TILE), :], q_op, preferred_element_type=F32)
        return s_ + jnp.concatenate([bias] * GROUP, axis=1)

    def values(kb, p):
        k0 = pl.multiple_of(kb * SEL_TILE, SEL_TILE)
        return jnp.dot(vst_ref[0, 0, :, pl.ds(k0, SEL_TILE)], p, preferred_element_type=F32)

    def bounded_path():
        o_win = window(True)
        select_blocks()
        q_off = extended(offset_s)

        def tile(kb, acc):
            return acc + values(kb, jnp.exp2(scores(kb, q_off)).astype(BF16))

        acc = lax.fori_loop(0, n_tiles // 2, lambda j, acc: tile(2 * j + 1, tile(2 * j, acc)),
                            jnp.zeros((V_ROWS, cols), F32))
        return o_win, lax.cond(n_tiles % 2 == 1, lambda acc: tile(n_tiles - 1, acc), lambda acc: acc, acc)

    def exact_path():
        o_win = window(False)
        select_blocks()

        def absorb(kb, s_, m_run, acc):
            m_new = jnp.maximum(m_run, jnp.max(s_, axis=0, keepdims=True))
            p = jnp.exp2(s_ - m_new).astype(BF16)
            return m_new, jnp.exp2(m_run - m_new) * acc + values(kb, p)

        sa_sc[...] = scores(0, q_ext)

        def sel_step(j, carry):
            m_run, acc = carry
            sb_sc[...] = scores(2 * j + 1, q_ext)
            m_run, acc = absorb(2 * j, sa_sc[...], m_run, acc)
            sa_sc[...] = scores(2 * j + 2, q_ext)
            return absorb(2 * j + 1, sb_sc[...], m_run, acc)

        init = (jnp.full((1, cols), NEG, F32), jnp.zeros((V_ROWS, cols), F32))
        full_trips = (n_tiles - 1) // 2
        m_run, acc = lax.fori_loop(0, full_trips, sel_step, init)
        m_run, acc = absorb(2 * full_trips, sa_sc[...], m_run, acc)

        def last_tile(carry):
            return absorb(n_tiles - 1, scores(n_tiles - 1, q_ext), *carry)

        return o_win, lax.cond(n_tiles - 2 * full_trips == 2, last_tile, lambda carry: carry, (m_run, acc))[1]

    o_w, acc_s = lax.cond(bound_ok, bounded_path, exact_path)

    o_s = acc_s[:HEAD_DIM] / jnp.maximum(acc_s[HEAD_DIM : HEAD_DIM + 1], 1e-30)

    gates_t = _sigmoid(gate_ref[...].astype(F32)).T
    outs = []
    for r in range(GROUP):
        sl = slice(r * qb, (r + 1) * qb)
        outs.append(gates_t[3 * r : 3 * r + 1] * o_c[:, sl] + gates_t[3 * r + 1 : 3 * r + 2] * o_s[:, sl]
                    + gates_t[3 * r + 2 : 3 * r + 3] * o_w[:, sl])
    o_ref[...] = jnp.concatenate(outs, axis=0).T.astype(o_ref.dtype)


def nsa_attention(proj, cos_q, sin_q, kc, vct, ks, vst, kw, vwt, overlap_t, batch, seq):
    nq = seq // Q_BLOCK
    nc = kc.shape[2]
    gw = GROUP * HEAD_DIM
    per_bg = lambda shape: pl.BlockSpec((1, 1) + shape, lambda b, g, i: (b, g, 0, 0))
    return pl.pallas_call(
        _nsa_kernel,
        grid=(batch, N_KV, nq),
        in_specs=[
            pl.BlockSpec((Q_BLOCK, gw), lambda b, g, i: (b * nq + i, COL_Q // gw + g)),
            pl.BlockSpec((Q_BLOCK, LANES), lambda b, g, i: (b * nq + i, COL_GN // LANES + g)),
            pl.BlockSpec((1, HEAD_DIM // 2, Q_BLOCK), lambda b, g, i: (b, 0, i)),
            pl.BlockSpec((1, HEAD_DIM // 2, Q_BLOCK), lambda b, g, i: (b, 0, i)),
            per_bg((nc, HEAD_DIM)), per_bg((HEAD_DIM, nc)),
            per_bg((seq, LANES)), per_bg((V_ROWS, seq)),
            per_bg((seq, LANES)), per_bg((V_ROWS, seq)),
            pl.BlockSpec((LANES, nc), lambda b, g, i: (0, 0)),
        ],
        out_specs=pl.BlockSpec((Q_BLOCK, gw), lambda b, g, i: (b * nq + i, g)),
        out_shape=jax.ShapeDtypeStruct((batch * seq, N_HEADS * HEAD_DIM), BF16),
        scratch_shapes=[pltpu.VMEM((LANES, Q_BLOCK), F32), pltpu.VMEM((SEL_TILE, GROUP * Q_BLOCK), F32),
                        pltpu.VMEM((SEL_TILE, GROUP * Q_BLOCK), F32), pltpu.VMEM((HEAD_DIM, GROUP * Q_BLOCK), F32),
                        pltpu.VMEM((LANES, Q_BLOCK), F32), pltpu.VMEM((SUBLANES, LANES), F32)],
        compiler_params=_cparams(("arbitrary", "arbitrary", "arbitrary")),
        name="nsa_attention",
    )(proj, proj, cos_q, sin_q, kc, vct, ks, vst, kw, vwt, overlap_t)


def _mem_kernel(q_ref, kv_ref, o_ref):
    q = q_ref[...]
    kv = kv_ref[0]
    mem_w = MEM_HEADS * MEM_HEAD_DIM
    outs = []
    for h in range(MEM_HEADS):
        sl = slice(h * MEM_HEAD_DIM, (h + 1) * MEM_HEAD_DIM)
        qh = (q[:, sl] * (MEM_HEAD_DIM ** -0.5)).astype(BF16)
        s = lax.dot_general(qh, kv[:, sl], (((1,), (1,)), ((), ())), preferred_element_type=F32)
        p = jnp.exp(s - jnp.max(s, axis=-1, keepdims=True))
        p = p / jnp.sum(p, axis=-1, keepdims=True)
        outs.append(jnp.dot(p.astype(BF16), kv[:, mem_w + h * MEM_HEAD_DIM : mem_w + (h + 1) * MEM_HEAD_DIM],
                            preferred_element_type=F32))
    o_ref[...] = jnp.concatenate(outs, axis=1).astype(o_ref.dtype)


def memory_attention(proj, kv_mem, batch, seq, tm):
    nt = seq // tm
    mem_len = kv_mem.shape[1]
    return pl.pallas_call(
        _mem_kernel,
        grid=(batch, nt),
        in_specs=[
            pl.BlockSpec((tm, D_MODEL), lambda b, t: (b * nt + t, COL_QM // D_MODEL)),
            pl.BlockSpec((1, mem_len, 2 * D_MODEL), lambda b, t: (b, 0, 0)),
        ],
        out_specs=pl.BlockSpec((tm, D_MODEL), lambda b, t: (b * nt + t, 0)),
        out_shape=jax.ShapeDtypeStruct((batch * seq, D_MODEL), BF16),
        compiler_params=_cparams(("arbitrary", "arbitrary")),
        name="memory_attention",
    )(proj, kv_mem)


def _merge_kernel(oa_ref, ob_ref, oc_ref, ga_ref, gb_ref, gc_ref, wa_ref, wb_ref, wc_ref, wo_ref,
                  gpost_ref, x_ref, gnext_ref, xo_ref, ho_ref):
    gate = lambda ref: _sigmoid(ref[...].astype(F32))
    merged = gate(ga_ref) * jnp.dot(oa_ref[...], wa_ref[...], preferred_element_type=F32)
    merged += gate(gb_ref) * jnp.dot(ob_ref[...], wb_ref[...], preferred_element_type=F32)
    merged += gate(gc_ref) * jnp.dot(oc_ref[...], wc_ref[...], preferred_element_type=F32)
    y = jnp.dot(merged.astype(BF16), wo_ref[...], preferred_element_type=F32)
    x_new = x_ref[...] + _rms(y, gpost_ref[...])
    xo_ref[...] = x_new
    ho_ref[...] = _rms(x_new, gnext_ref[...]).astype(ho_ref.dtype)


def merge_out(oa, ob, oc, proj, wa, wb, wc, wo, layer, g_post, x, g_next, tm):
    n, d = x.shape
    row = lambda col: pl.BlockSpec((tm, d), lambda i: (i, col))
    full = lambda shape: pl.BlockSpec(shape, lambda i: (0, 0))
    w_spec = pl.BlockSpec((None, d, d), lambda i: (layer, 0, 0))
    gm = COL_GM // d
    return pl.pallas_call(
        _merge_kernel,
        grid=(n // tm,),
        in_specs=[row(0), row(0), row(0), row(gm), row(gm + 1), row(gm + 2),
                  w_spec, w_spec, w_spec, w_spec, full((1, d)), row(0), full((1, d))],
        out_specs=[row(0), row(0)],
        out_shape=[jax.ShapeDtypeStruct((n, d), F32), jax.ShapeDtypeStruct((n, d), BF16)],
        compiler_params=_cparams(("arbitrary",)),
        name="merge_out",
    )(oa, ob, oc, proj, proj, proj, wa, wb, wc, wo, g_post.reshape(1, d), x, g_next.reshape(1, d))


def _mlp_kernel(h_ref, w1_ref, w2_ref, gpost_ref, x_ref, gnext_ref, xo_ref, ho_ref, acc_ref):
    k = pl.program_id(1)

    @pl.when(k == 0)
    def _():
        acc_ref[...] = jnp.zeros_like(acc_ref)

    u = jnp.maximum(jnp.dot(h_ref[...], w1_ref[...], preferred_element_type=F32), 0.0)
    acc_ref[...] += jnp.dot((u * u).astype(BF16), w2_ref[...], preferred_element_type=F32)

    @pl.when(k == pl.num_programs(1) - 1)
    def _():
        x_new = x_ref[...] + _rms(acc_ref[...], gpost_ref[...])
        xo_ref[...] = x_new
        ho_ref[...] = _rms(x_new, gnext_ref[...]).astype(ho_ref.dtype)


def mlp(h, w1, w2, layer, g_post, x, g_next, tm, tf):
    n, d = x.shape
    ff = w1.shape[2]
    return pl.pallas_call(
        _mlp_kernel,
        grid=(n // tm, ff // tf),
        in_specs=[
            pl.BlockSpec((tm, d), lambda i, k: (i, 0)),
            pl.BlockSpec((None, d, tf), lambda i, k: (layer, 0, k)),
            pl.BlockSpec((None, tf, d), lambda i, k: (layer, k, 0)),
            pl.BlockSpec((1, d), lambda i, k: (0, 0)),
            pl.BlockSpec((tm, d), lambda i, k: (i, 0)),
            pl.BlockSpec((1, d), lambda i, k: (0, 0)),
        ],
        out_specs=[pl.BlockSpec((tm, d), lambda i, k: (i, 0))] * 2,
        out_shape=[jax.ShapeDtypeStruct((n, d), F32), jax.ShapeDtypeStruct((n, d), BF16)],
        scratch_shapes=[pltpu.VMEM((tm, d), F32)],
        compiler_params=_cparams(("arbitrary", "arbitrary")),
        name="mlp",
    )(h, w1, w2, g_post.reshape(1, d), x, g_next.reshape(1, d))


def _pack_w_in(w):
    o_kv, o_gn, o_qm, o_gm = 3072, 4608, 4656, 5680
    per_group = GROUP * 3
    lead = w.shape[:-1]
    gn = w[..., o_gn:o_qm].reshape(lead + (N_KV, per_group))
    gn = jnp.pad(gn, ((0, 0),) * (len(lead) + 1) + ((0, LANES - per_group),)).reshape(lead + (N_KV * LANES,))
    packed = jnp.concatenate([w[..., :o_kv], w[..., o_qm:o_gm], w[..., o_gm:], w[..., o_kv:o_gn], gn], axis=-1)
    assert packed.shape[-1] == D_INP
    return packed.astype(BF16)


def _overlap_matrix_t(nc):
    c0 = np.arange(nc)[:, None] * CMP_STRIDE
    s0 = np.arange(LANES)[None, :] * SEL_LEN
    ov = np.clip(np.minimum(c0 + CMP_LEN, s0 + SEL_LEN) - np.maximum(c0, s0), 0, None).astype(np.float32) / CMP_LEN
    return jnp.asarray(ov.T, dtype=BF16)


def kernel(x, mem, positions, ln_mix_pre, w_in, conv_w, conv_b, lru_wr, lru_br, lru_wi, lru_bi, lru_lambda, cmp_pe, cmp_w1, cmp_b1, cmp_w2, ln_mem, w_mem_kv, w_br_rnn, w_br_nsa, w_br_mem, w_out, ln_mix_post, ln_mlp_pre, mlp_w1, mlp_w2, ln_mlp_post):
    batch, seq, d = x.shape
    depth = w_in.shape[0]
    mem_len = mem.shape[1]
    n = batch * seq
    nc = seq // CMP_STRIDE
    assert d == D_MODEL and seq % (2 * SEL_TILE) == 0 and seq // SEL_LEN <= LANES and seq >= WIN_KEYS

    tm = min(512, seq)
    tm_big = min(1024, seq)
    xf = x.reshape(n, d)
    memf = mem.reshape(batch * mem_len, d)
    cos_t, sin_t = rope_tables(positions.reshape(n), tm)
    pos_c = jnp.pad(positions[:, CMP_LEN - 1 :: CMP_STRIDE], ((0, 0), (0, 1)))
    cos_c, sin_c = rope_tables(pos_c.reshape(batch * nc), nc)
    cos_c, sin_c = cos_c.reshape(batch, nc, LANES), sin_c.reshape(batch, nc, LANES)
    cos_q, sin_q = rope_tables_t(positions, tm)
    overlap_t = _overlap_matrix_t(nc)

    w_in_b, w_kv_b = _pack_w_in(w_in.astype(BF16)), w_mem_kv.astype(BF16)
    wr_b, wi_b = lru_wr.astype(BF16), lru_wi.astype(BF16)
    w1c_b, w2c_b = cmp_w1.astype(BF16), cmp_w2.astype(BF16)
    wa_b, wb_b, wc_b, wo_b = (w.astype(BF16) for w in (w_br_rnn, w_br_nsa, w_br_mem, w_out))
    w1_b, w2_b = mlp_w1.astype(BF16), mlp_w2.astype(BF16)

    h = rmsnorm_bf16(xf, ln_mix_pre[0], tm)
    for l in range(depth):
        proj = matmul(h, w_in_b, l, tm_big, 1024, BF16, "in_proj")

        o_a = rglru_branch(proj, batch, seq, conv_w[l], conv_b[l], wr_b[l], lru_br[l], wi_b[l], lru_bi[l],
                           lru_lambda[l], min(256, seq))

        kc_raw, vc_raw, ks, vst, kw, vwt = kv_prep(proj, cos_t, sin_t, batch, seq, tm)
        chunks = lambda a: a.reshape(batch * N_KV, nc, CMP_STRIDE * HEAD_DIM)
        pe = cmp_pe[l].reshape(2, 2, CMP_STRIDE * HEAD_DIM)
        b1c = cmp_b1[l].reshape(2, 1, -1)
        k_cmp = compress(chunks(kc_raw), 0, pe, w1c_b[l], b1c, w2c_b[l], cos_c, sin_c, batch, True)
        v_cmp = compress(chunks(vc_raw), 1, pe, w1c_b[l], b1c, w2c_b[l], cos_c, sin_c, batch, False)
        kc = k_cmp.reshape(batch, N_KV, nc, HEAD_DIM)
        vct = v_cmp.reshape(batch, N_KV, nc, HEAD_DIM).transpose(0, 1, 3, 2)
        o_b = nsa_attention(proj, cos_q, sin_q, kc, vct, ks, vst, kw, vwt, overlap_t, batch, seq)

        mem_h = rmsnorm_bf16(memf, ln_mem[l], mem_len)
        kv_mem = matmul(mem_h, w_kv_b, l, mem_len, 1024, BF16, "mem_kv")
        o_c = memory_attention(proj, kv_mem.reshape(batch, mem_len, 2 * D_MODEL), batch, seq, tm)

        xf, h = merge_out(o_a, o_b, o_c, proj, wa_b, wb_b, wc_b, wo_b, l, ln_mix_post[l], xf, ln_mlp_pre[l], tm)
        xf, h = mlp(h, w1_b, w2_b, l, ln_mlp_post[l], xf, ln_mix_pre[(l + 1) % depth], tm_big, 1024)
    return xf.reshape(batch, seq, d)
```

```python
import functools

import jax
import jax.numpy as jnp
import numpy as np
from jax import lax
from jax.experimental import pallas as pl
from jax.experimental.pallas import tpu as pltpu

F32 = jnp.float32
BF16 = jnp.bfloat16

D_MODEL = 1024
LRU_BLOCKS = 8
LRU_BW = D_MODEL // LRU_BLOCKS
CONV_W = 4
LRU_C = 8.0
N_HEADS = 16
HEAD_DIM = 64
N_KV = 4
GROUP = N_HEADS // N_KV
KV_W = N_KV * HEAD_DIM
CMP_STRIDE = 16
CMP_LEN = 32
SEL_LEN = 64
SEL_SHIFT = 6
N_SELECT = 16
WINDOW = 512
Q_BLOCK = 256
MEM_HEADS = 4
MEM_HEAD_DIM = D_MODEL // MEM_HEADS
D_FF = 4 * D_MODEL
ROPE_THETA = 10000.0
EPS = 1e-6
NEG = -1e30
FORCE = 1e4
TAKEN = -3e38
LOG2_E = 1.4426950408889634
MAX_OFFSET = 45.0
BOUND_SLACK = 1.01

LANES = 128
SUBLANES = 8
SEL_TILE = 512
WIN_KEYS = WINDOW + Q_BLOCK
V_ROWS = 80
VMEM_LIMIT = 48 * 1024 * 1024

COL_XR, COL_YR, COL_Q, COL_QM, COL_GM, COL_KV, COL_GN = 0, 1024, 2048, 3072, 4096, 7168, 8704
D_INP = 9216


def _cparams(sem):
    return pltpu.CompilerParams(dimension_semantics=sem, vmem_limit_bytes=VMEM_LIMIT)


def _sigmoid(x):
    return 0.5 * jnp.tanh(0.5 * x) + 0.5


def _gelu_tanh(x):
    return 0.5 * x * (1.0 + jnp.tanh(0.7978845608028654 * (x + 0.044715 * (x * x * x))))


def _rms(x, g):
    return x * lax.rsqrt(jnp.mean(x * x, axis=-1, keepdims=True) + EPS) * g


def _rms_kernel(x_ref, g_ref, o_ref):
    o_ref[...] = _rms(x_ref[...], g_ref[...]).astype(o_ref.dtype)


def rmsnorm_bf16(x, g, tm):
    m, d = x.shape
    return pl.pallas_call(
        _rms_kernel,
        grid=(m // tm,),
        in_specs=[pl.BlockSpec((tm, d), lambda i: (i, 0)), pl.BlockSpec((1, d), lambda i: (0, 0))],
        out_specs=pl.BlockSpec((tm, d), lambda i: (i, 0)),
        out_shape=jax.ShapeDtypeStruct((m, d), BF16),
        compiler_params=_cparams(("arbitrary",)),
        name="rmsnorm",
    )(x, g.reshape(1, d))


def _mm_kernel(a_ref, w_ref, o_ref):
    o_ref[...] = jnp.dot(a_ref[...], w_ref[...], preferred_element_type=F32).astype(o_ref.dtype)


def matmul(a, w, layer, tm, tn, out_dtype, name):
    m, k = a.shape
    n = w.shape[2]
    return pl.pallas_call(
        _mm_kernel,
        grid=(n // tn, m // tm),
        in_specs=[pl.BlockSpec((tm, k), lambda j, i: (i, 0)), pl.BlockSpec((None, k, tn), lambda j, i: (layer, 0, j))],
        out_specs=pl.BlockSpec((tm, tn), lambda j, i: (i, j)),
        out_shape=jax.ShapeDtypeStruct((m, n), out_dtype),
        compiler_params=_cparams(("arbitrary", "arbitrary")),
        name=name,
    )(a, w)


def _rope_table_kernel(pos_ref, inv_ref, cos_ref, sin_ref):
    ang = pos_ref[...].astype(F32) * inv_ref[...]
    lane = lax.broadcasted_iota(jnp.int32, ang.shape, 1)
    cos_ref[...] = jnp.cos(ang)
    sin_ref[...] = jnp.where((lane & (HEAD_DIM - 1)) < HEAD_DIM // 2, -1.0, 1.0) * jnp.sin(ang)


def rope_tables(pos_flat, tm):
    n = pos_flat.shape[0]
    half = HEAD_DIM // 2
    inv = ROPE_THETA ** (-jnp.arange(half, dtype=F32) * 2.0 / HEAD_DIM)
    inv_full = jnp.tile(inv, LANES // half).reshape(1, LANES)
    return pl.pallas_call(
        _rope_table_kernel,
        grid=(n // tm,),
        in_specs=[pl.BlockSpec((tm, 1), lambda i: (i, 0)), pl.BlockSpec((1, LANES), lambda i: (0, 0))],
        out_specs=[pl.BlockSpec((tm, LANES), lambda i: (i, 0))] * 2,
        out_shape=[jax.ShapeDtypeStruct((n, LANES), F32)] * 2,
        compiler_params=_cparams(("arbitrary",)),
        name="rope_tables",
    )(pos_flat.reshape(n, 1), inv_full)


def _rope_table_t_kernel(pos_ref, inv_ref, cos_ref, sin_ref):
    ang = inv_ref[...] * pos_ref[0].astype(F32)
    cos_ref[0] = jnp.cos(ang)
    sin_ref[0] = jnp.sin(ang)


def rope_tables_t(positions, tm):
    batch, seq = positions.shape
    half = HEAD_DIM // 2
    inv = (ROPE_THETA ** (-jnp.arange(half, dtype=F32) * 2.0 / HEAD_DIM)).reshape(half, 1)
    return pl.pallas_call(
        _rope_table_t_kernel,
        grid=(batch, seq // tm),
        in_specs=[pl.BlockSpec((1, 1, tm), lambda b, i: (b, 0, i)), pl.BlockSpec((half, 1), lambda b, i: (0, 0))],
        out_specs=[pl.BlockSpec((1, half, tm), lambda b, i: (b, 0, i))] * 2,
        out_shape=[jax.ShapeDtypeStruct((batch, half, seq), F32)] * 2,
        compiler_params=_cparams(("arbitrary", "arbitrary")),
        name="rope_tables_t",
    )(positions.reshape(batch, 1, seq), inv)


def _rope128(x, cos_t, sin_t):
    lane = lax.broadcasted_iota(jnp.int32, x.shape, 1)
    first = (lane & (HEAD_DIM - 1)) < HEAD_DIM // 2
    partner = jnp.where(first, pltpu.roll(x, LANES - HEAD_DIM // 2, 1), pltpu.roll(x, HEAD_DIM // 2, 1))
    return x * cos_t + partner * sin_t


def _rglru_kernel(xr_ref, yr_ref, cw_ref, cb_ref, wr_ref, br_ref, wi_ref, bi_ref, lam_ref, o_ref, h_sc, tail_sc):
    @pl.when(pl.program_id(1) == 0)
    def _():
        h_sc[...] = jnp.zeros_like(h_sc)
        tail_sc[...] = jnp.zeros_like(tail_sc)

    xr = xr_ref[...].astype(F32)
    t_len, d = xr.shape
    tail = tail_sc[...]
    row8 = lax.broadcasted_iota(jnp.int32, (SUBLANES, d), 0)
    cw = cw_ref[...]
    xc = cb_ref[...] + xr * cw[CONV_W - 1 : CONV_W, :]
    for k in range(1, CONV_W):
        rolled = pltpu.roll(xr, k, 0)
        head = jnp.where(row8 < k, pltpu.roll(tail, k, 0), rolled[0:SUBLANES])
        shifted = jnp.concatenate([head, rolled[SUBLANES:]], axis=0)
        xc = xc + shifted * cw[CONV_W - 1 - k : CONV_W - k, :]
    tail_sc[...] = xr[t_len - SUBLANES :]

    xcb = xc.astype(BF16)
    rl, il = [], []
    for n in range(LRU_BLOCKS):
        xb = xcb[:, n * LRU_BW : (n + 1) * LRU_BW]
        rl.append(jnp.dot(xb, wr_ref[n], preferred_element_type=F32))
        il.append(jnp.dot(xb, wi_ref[n], preferred_element_type=F32))
    r = _sigmoid(jnp.concatenate(rl, axis=1) + br_ref[...])
    ig = _sigmoid(jnp.concatenate(il, axis=1) + bi_ref[...])
    softplus_neg_lam = jnp.log1p(jnp.exp(-lam_ref[...]))
    log_a = (-LRU_C * softplus_neg_lam) * r
    a = jnp.exp(log_a)
    one_minus_a2 = 1.0 - a * a
    root = jnp.where(one_minus_a2 > 0.0, one_minus_a2 * lax.rsqrt(one_minus_a2), 0.0)
    b = root * (ig * xc)

    sub = lax.broadcasted_iota(jnp.int32, (t_len, d), 0) & (SUBLANES - 1)
    step = 1
    while step < SUBLANES:
        keep = sub >= step
        a_sh = jnp.where(keep, pltpu.roll(a, step, 0), 1.0)
        b_sh = jnp.where(keep, pltpu.roll(b, step, 0), 0.0)
        b = a * b_sh + b
        a = a * a_sh
        step *= 2
    h_prev = h_sc[...]
    groups = []
    for g in range(t_len // SUBLANES):
        sl = slice(g * SUBLANES, (g + 1) * SUBLANES)
        groups.append(b[sl] + a[sl] * h_prev)
        h_prev = jnp.broadcast_to(groups[-1][SUBLANES - 1 :], (SUBLANES, d))
    h_sc[...] = h_prev
    h = jnp.concatenate(groups, axis=0)
    o_ref[...] = (h * _gelu_tanh(yr_ref[...].astype(F32))).astype(o_ref.dtype)


def rglru_branch(proj, batch, seq, conv_w, conv_b, wr, br, wi, bi, lam, t_len):
    d = D_MODEL
    nt = seq // t_len
    vec = lambda v: v.reshape(1, d)
    full2 = lambda shape: pl.BlockSpec(shape, lambda b, t: (0,) * len(shape))
    return pl.pallas_call(
        _rglru_kernel,
        grid=(batch, nt),
        in_specs=[
            pl.BlockSpec((t_len, d), lambda b, t: (b * nt + t, COL_XR // d)),
            pl.BlockSpec((t_len, d), lambda b, t: (b * nt + t, COL_YR // d)),
            full2((CONV_W, d)), full2((1, d)),
            full2((LRU_BLOCKS, LRU_BW, LRU_BW)), full2((1, d)),
            full2((LRU_BLOCKS, LRU_BW, LRU_BW)), full2((1, d)),
            full2((1, d)),
        ],
        out_specs=pl.BlockSpec((t_len, d), lambda b, t: (b * nt + t, 0)),
        out_shape=jax.ShapeDtypeStruct((batch * seq, d), BF16),
        scratch_shapes=[pltpu.VMEM((SUBLANES, d), F32), pltpu.VMEM((SUBLANES, d), F32)],
        compiler_params=_cparams(("arbitrary", "arbitrary")),
        name="rglru",
    )(proj, proj, conv_w, vec(conv_b), wr.astype(BF16), vec(br), wi.astype(BF16), vec(bi), vec(lam))


def _kv_prep_kernel(kvc_ref, kvs_ref, kvw_ref, cos_ref, sin_ref, kco_ref, vco_ref, kso_ref, vso_ref, kwo_ref, vwo_ref):
    cos_t, sin_t = cos_ref[...], sin_ref[...]
    t_len = cos_t.shape[0]
    xc = kvc_ref[...]
    for g in range(N_KV):
        kco_ref[0, g] = xc[:, g * HEAD_DIM : (g + 1) * HEAD_DIM]
        vco_ref[0, g] = xc[:, KV_W + g * HEAD_DIM : KV_W + (g + 1) * HEAD_DIM]
    pad_row = lax.broadcasted_iota(jnp.int32, (V_ROWS - HEAD_DIM, t_len), 0)
    pad_rows = jnp.where(pad_row == 0, 1.0, 0.0).astype(BF16)
    pad_lane = lax.broadcasted_iota(jnp.int32, (t_len, LANES - HEAD_DIM), 1)
    pad_lanes = jnp.where(pad_lane == 0, 1.0, 0.0)
    for src, k_dst, v_dst in ((kvs_ref, kso_ref, vso_ref), (kvw_ref, kwo_ref, vwo_ref)):
        x = src[...].astype(F32)
        roped = [_rope128(x[:, c * LANES : (c + 1) * LANES], cos_t, sin_t) for c in range(KV_W // LANES)]
        v_t = x[:, KV_W:].T
        for g in range(N_KV):
            pair = roped[g * HEAD_DIM // LANES]
            lo = g * HEAD_DIM % LANES
            k_dst[0, g] = jnp.concatenate([pair[:, lo : lo + HEAD_DIM], pad_lanes], axis=1).astype(BF16)
            v_dst[0, g, :HEAD_DIM, :] = v_t[g * HEAD_DIM : (g + 1) * HEAD_DIM].astype(BF16)
            v_dst[0, g, HEAD_DIM:, :] = pad_rows


def kv_prep(proj, cos_t, sin_t, batch, seq, tm):
    nt = seq // tm
    blk = lambda col: pl.BlockSpec((tm, 2 * KV_W), lambda b, t: (b * nt + t, col // (2 * KV_W)))
    tab = pl.BlockSpec((tm, LANES), lambda b, t: (b * nt + t, 0))
    k_spec = pl.BlockSpec((1, N_KV, tm, HEAD_DIM), lambda b, t: (b, 0, t, 0))
    v_spec = pl.BlockSpec((1, N_KV, V_ROWS, tm), lambda b, t: (b, 0, 0, t))
    k_shape = jax.ShapeDtypeStruct((batch, N_KV, seq, HEAD_DIM), BF16)
    v_shape = jax.ShapeDtypeStruct((batch, N_KV, V_ROWS, seq), BF16)
    kx_spec = pl.BlockSpec((1, N_KV, tm, LANES), lambda b, t: (b, 0, t, 0))
    kx_shape = jax.ShapeDtypeStruct((batch, N_KV, seq, LANES), BF16)
    return pl.pallas_call(
        _kv_prep_kernel,
        grid=(batch, nt),
        in_specs=[blk(COL_KV), blk(COL_KV + 2 * KV_W), blk(COL_KV + 4 * KV_W), tab, tab],
        out_specs=[k_spec, k_spec, kx_spec, v_spec, kx_spec, v_spec],
        out_shape=[k_shape, k_shape, kx_shape, v_shape, kx_shape, v_shape],
        compiler_params=_cparams(("arbitrary", "arbitrary")),
        name="kv_prep",
    )(proj, proj, proj, cos_t, sin_t)


def _compress_kernel(x_ref, pe_ref, w1_ref, b1_ref, w2_ref, cos_ref, sin_ref, o_ref, *, rotary):
    x = x_ref[0]
    half = CMP_STRIDE * HEAD_DIM
    pe = pe_ref[0]
    w1 = w1_ref[0]
    u = jnp.dot((x + pe[0:1]).astype(BF16), w1[:half], preferred_element_type=F32)
    v = jnp.dot((x + pe[1:2]).astype(BF16), w1[half:], preferred_element_type=F32)
    hid = _gelu_tanh(u + pltpu.roll(v, v.shape[0] - 1, 0) + b1_ref[0])
    out = jnp.dot(hid.astype(BF16), w2_ref[0], preferred_element_type=F32)
    if rotary:
        hh = HEAD_DIM // 2
        partner = jnp.concatenate([out[:, hh:], out[:, :hh]], axis=1)
        out = out * cos_ref[0][:, :HEAD_DIM] + partner * sin_ref[0][:, :HEAD_DIM]
    o_ref[0] = out.astype(o_ref.dtype)


def compress(x_chunks, j, pe, w1, b1, w2, cos_c, sin_c, batch, rotary):
    bg, nc, width = x_chunks.shape
    return pl.pallas_call(
        functools.partial(_compress_kernel, rotary=rotary),
        grid=(bg,),
        in_specs=[
            pl.BlockSpec((1, nc, width), lambda i: (i, 0, 0)),
            pl.BlockSpec((1, 2, width), lambda i: (j, 0, 0)),
            pl.BlockSpec((1, 2 * width, w1.shape[2]), lambda i: (j, 0, 0)),
            pl.BlockSpec((1, 1, w1.shape[2]), lambda i: (j, 0, 0)),
            pl.BlockSpec((1, w2.shape[1], HEAD_DIM), lambda i: (j, 0, 0)),
            pl.BlockSpec((1, nc, LANES), lambda i: (i // N_KV, 0, 0)),
            pl.BlockSpec((1, nc, LANES), lambda i: (i // N_KV, 0, 0)),
        ],
        out_specs=pl.BlockSpec((1, nc, HEAD_DIM), lambda i: (i, 0, 0)),
        out_shape=jax.ShapeDtypeStruct((bg, nc, HEAD_DIM), BF16),
        compiler_params=_cparams(("arbitrary",)),
        name="compress_k" if rotary else "compress_v",
    )(x_chunks, pe, w1, b1, w2, cos_c, sin_c)


def _nsa_kernel(q_ref, gate_ref, cos_ref, sin_ref, kc_ref, vct_ref, ks_ref, vst_ref, kw_ref, vwt_ref, ovt_ref,
                o_ref, sel_sc, sa_sc, sb_sc, oc_sc, imp_sc, kmax_sc):
    qb = Q_BLOCK
    s0 = pl.program_id(2) * qb
    nc = kc_ref.shape[2]
    hh = HEAD_DIM // 2
    cols = GROUP * qb

    qt = q_ref[...].astype(F32).T
    cos_t, sin_t = cos_ref[0], sin_ref[0]
    heads = []
    for r in range(GROUP):
        x1 = qt[r * HEAD_DIM : r * HEAD_DIM + hh]
        x2 = qt[r * HEAD_DIM + hh : (r + 1) * HEAD_DIM]
        heads.append(jnp.concatenate([x1 * cos_t - x2 * sin_t, x2 * cos_t + x1 * sin_t], axis=0))
    q_t = (jnp.concatenate(heads, axis=1) * (HEAD_DIM ** -0.5 * LOG2_E)).astype(BF16)

    tq = s0 + lax.broadcasted_iota(jnp.int32, (1, qb), 1)

    @pl.when(pl.program_id(2) == 0)
    def _():
        for row, k_ref in enumerate((ks_ref, kw_ref)):
            def norm_step(t, best, k_ref=k_ref):
                k_rows = k_ref[0, 0, pl.ds(pl.multiple_of(t * SEL_TILE, SEL_TILE), SEL_TILE), :].astype(F32)
                sq = jnp.sum(k_rows * k_rows, axis=1, keepdims=True)
                return jnp.maximum(best, jnp.max(sq, axis=0, keepdims=True))

            best = lax.fori_loop(0, k_ref.shape[2] // SEL_TILE, norm_step, jnp.zeros((1, 1), F32))
            kmax_sc[row : row + 1, :] = jnp.broadcast_to(jnp.sqrt(best), (1, LANES))

    q_f32 = q_t.astype(F32)
    q_norm = jnp.sqrt(jnp.sum(q_f32 * q_f32, axis=0, keepdims=True))
    offset_s = q_norm * (kmax_sc[0:1, 0:1] * BOUND_SLACK)
    offset_w = q_norm * (kmax_sc[1:2, 0:1] * BOUND_SLACK)
    bound_ok = jnp.max(jnp.maximum(offset_s, offset_w)) <= MAX_OFFSET
    ext_row = lax.broadcasted_iota(jnp.int32, (SUBLANES, cols), 0)
    zero_rows = jnp.zeros((LANES - HEAD_DIM - SUBLANES, cols), F32)

    def extended(offset):
        return jnp.concatenate([q_f32, jnp.where(ext_row == 0, -offset, 0.0), zero_rows], axis=0).astype(BF16)

    q_ext = extended(jnp.zeros_like(offset_s))

    def compressed(n_c):
        cmp_end = lax.broadcasted_iota(jnp.int32, (n_c, qb), 0) * CMP_STRIDE + (CMP_LEN - 1)
        bias_c = jnp.where(cmp_end <= tq, 0.0, NEG)
        has_key = jnp.where(tq >= CMP_LEN - 1, 1.0, 0.0)
        sc = jnp.dot(kc_ref[0, 0, :n_c, :], q_t, preferred_element_type=F32) + jnp.concatenate([bias_c] * GROUP, axis=1)
        p_c = jnp.exp2(sc - jnp.max(sc, axis=0, keepdims=True))
        norm = jnp.concatenate([has_key] * GROUP, axis=1) / jnp.maximum(jnp.sum(p_c, axis=0, keepdims=True), 1e-30)
        p_c = p_c * norm
        oc_sc[...] = jnp.dot(vct_ref[0, 0, :, :n_c], p_c.astype(BF16), preferred_element_type=F32)
        p_sum = p_c[:, :qb]
        for r in range(1, GROUP):
            p_sum = p_sum + p_c[:, r * qb : (r + 1) * qb]
        p_hi = p_sum.astype(BF16)
        p_lo = (p_sum - p_hi.astype(F32)).astype(BF16)
        ovt = ovt_ref[:, :n_c]
        imp_sc[...] = (jnp.dot(ovt, p_hi, preferred_element_type=F32)
                       + jnp.dot(ovt, p_lo, preferred_element_type=F32))

    last_cmp = (s0 + qb - CMP_LEN) // CMP_STRIDE
    for v in range(nc // LANES):
        pl.when(last_cmp // LANES == v)(functools.partial(compressed, (v + 1) * LANES))
    o_c = oc_sc[...]
    imp = imp_sc[...]

    def window(bounded):
        w0 = pl.multiple_of(jnp.maximum(s0 - WINDOW, 0), LANES)
        wpos = w0 + lax.broadcasted_iota(jnp.int32, (WIN_KEYS, qb), 0)
        bias_w = jnp.where((wpos <= tq) & (wpos > tq - WINDOW), 0.0, NEG)
        q_op = extended(offset_w) if bounded else q_ext
        s_w = jnp.dot(kw_ref[0, 0, pl.ds(w0, WIN_KEYS), :], q_op, preferred_element_type=F32)
        s_w = s_w + jnp.concatenate([bias_w] * GROUP, axis=1)
        if not bounded:
            s_w = s_w - jnp.max(s_w, axis=0, keepdims=True)
        acc_w = jnp.dot(vwt_ref[0, 0, :, pl.ds(w0, WIN_KEYS)], jnp.exp2(s_w).astype(BF16),
                        preferred_element_type=F32)
        return acc_w[:HEAD_DIM] / jnp.maximum(acc_w[HEAD_DIM : HEAD_DIM + 1], 1e-30)

    def select_blocks():
        blk = lax.broadcasted_iota(jnp.int32, (LANES, qb), 0)
        cur = tq >> SEL_SHIFT
        forced = (blk == 0) | (blk == cur)
        causal = blk * SEL_LEN <= tq
        work = jnp.where(forced, TAKEN, jnp.where(causal, imp, -FORCE))
        for r in range(N_SELECT - 1):
            hit = work == jnp.max(work, axis=0, keepdims=True)
            if r == N_SELECT - 2:
                hit = hit & (cur == 0)
            work = jnp.where(hit, TAKEN, work)
        taken = work == TAKEN
        sel_sc[...] = jnp.where(taken, 1.0, 0.0)
        n_taken = jnp.sum(jnp.where(taken & causal, 1.0, 0.0), axis=0, keepdims=True)

        @pl.when(jnp.max(n_taken) > N_SELECT)
        def _():
            work = jnp.where(forced, FORCE, jnp.where(causal, imp, -FORCE))
            blk_f = blk.astype(F32)
            for _ in range(N_SELECT):
                m = jnp.max(work, axis=0, keepdims=True)
                idx = jnp.min(jnp.where(work == m, blk_f, float(LANES)), axis=0, keepdims=True)
                work = jnp.where(blk_f == idx, TAKEN, work)
            sel_sc[...] = jnp.where(work == TAKEN, 1.0, 0.0)

    blocks_per_tile = SEL_TILE // SEL_LEN
    row_minus_lane = (lax.broadcasted_iota(jnp.int32, (SEL_TILE, qb), 0)
                      - lax.broadcasted_iota(jnp.int32, (SEL_TILE, qb), 1))
    n_tiles = (s0 + qb - 1) // SEL_TILE + 1

    def scores(kb, q_op):
        k0 = pl.multiple_of(kb * SEL_TILE, SEL_TILE)
        picked = sel_sc[pl.ds(pl.multiple_of(kb * blocks_per_tile, blocks_per_tile), blocks_per_tile), :]
        picked = jnp.where(picked > 0.5, 0.0, NEG)
        picked = jnp.concatenate(
            [jnp.broadcast_to(picked[j : j + 1, :], (SEL_LEN, qb)) for j in range(blocks_per_tile)], axis=0)
        bias = jnp.where(row_minus_lane <= s0 - k0, picked, NEG)
        s_ = jnp.dot(ks_ref[0, 0, pl.ds(k0, SEL_TILE), :], q_op, preferred_element_type=F32)
        return s_ + jnp.concatenate([bias] * GROUP, axis=1)

    def values(kb, p):
        k0 = pl.multiple_of(kb * SEL_TILE, SEL_TILE)
        return jnp.dot(vst_ref[0, 0, :, pl.ds(k0, SEL_TILE)], p, preferred_element_type=F32)

    def bounded_path():
        o_win = window(True)
        select_blocks()
        q_off = extended(offset_s)

        def tile(kb, acc):
            return acc + values(kb, jnp.exp2(scores(kb, q_off)).astype(BF16))

        acc = lax.fori_loop(0, n_tiles // 2, lambda j, acc: tile(2 * j + 1, tile(2 * j, acc)),
                            jnp.zeros((V_ROWS, cols), F32))
        return o_win, lax.cond(n_tiles % 2 == 1, lambda acc: tile(n_tiles - 1, acc), lambda acc: acc, acc)

    def exact_path():
        o_win = window(False)
        select_blocks()

        def absorb(kb, s_, m_run, acc):
            m_new = jnp.maximum(m_run, jnp.max(s_, axis=0, keepdims=True))
            p = jnp.exp2(s_ - m_new).astype(BF16)
            return m_new, jnp.exp2(m_run - m_new) * acc + values(kb, p)

        sa_sc[...] = scores(0, q_ext)

        def sel_step(j, carry):
            m_run, acc = carry
            sb_sc[...] = scores(2 * j + 1, q_ext)
            m_run, acc = absorb(2 * j, sa_sc[...], m_run, acc)
            sa_sc[...] = scores(2 * j + 2, q_ext)
            return absorb(2 * j + 1, sb_sc[...], m_run, acc)

        init = (jnp.full((1, cols), NEG, F32), jnp.zeros((V_ROWS, cols), F32))
        full_trips = (n_tiles - 1) // 2
        m_run, acc = lax.fori_loop(0, full_trips, sel_step, init)
        m_run, acc = absorb(2 * full_trips, sa_sc[...], m_run, acc)

        def last_tile(carry):
            return absorb(n_tiles - 1, scores(n_tiles - 1, q_ext), *carry)

        return o_win, lax.cond(n_tiles - 2 * full_trips == 2, last_tile, lambda carry: carry, (m_run, acc))[1]

    o_w, acc_s = lax.cond(bound_ok, bounded_path, exact_path)

    o_s = acc_s[:HEAD_DIM] / jnp.maximum(acc_s[HEAD_DIM : HEAD_DIM + 1], 1e-30)

    gates_t = _sigmoid(gate_ref[...].astype(F32)).T
    outs = []
    for r in range(GROUP):
        sl = slice(r * qb, (r + 1) * qb)
        outs.append(gates_t[3 * r : 3 * r + 1] * o_c[:, sl] + gates_t[3 * r + 1 : 3 * r + 2] * o_s[:, sl]
                    + gates_t[3 * r + 2 : 3 * r + 3] * o_w[:, sl])
    o_ref[...] = jnp.concatenate(outs, axis=0).T.astype(o_ref.dtype)


def nsa_attention(proj, cos_q, sin_q, kc, vct, ks, vst, kw, vwt, overlap_t, batch, seq):
    nq = seq // Q_BLOCK
    nc = kc.shape[2]
    gw = GROUP * HEAD_DIM
    per_bg = lambda shape: pl.BlockSpec((1, 1) + shape, lambda b, g, i: (b, g, 0, 0))
    return pl.pallas_call(
        _nsa_kernel,
        grid=(batch, N_KV, nq),
        in_specs=[
            pl.BlockSpec((Q_BLOCK, gw), lambda b, g, i: (b * nq + i, COL_Q // gw + g)),
            pl.BlockSpec((Q_BLOCK, LANES), lambda b, g, i: (b * nq + i, COL_GN // LANES + g)),
            pl.BlockSpec((1, HEAD_DIM // 2, Q_BLOCK), lambda b, g, i: (b, 0, i)),
            pl.BlockSpec((1, HEAD_DIM // 2, Q_BLOCK), lambda b, g, i: (b, 0, i)),
            per_bg((nc, HEAD_DIM)), per_bg((HEAD_DIM, nc)),
            per_bg((seq, LANES)), per_bg((V_ROWS, seq)),
            per_bg((seq, LANES)), per_bg((V_ROWS, seq)),
            pl.BlockSpec((LANES, nc), lambda b, g, i: (0, 0)),
        ],
        out_specs=pl.BlockSpec((Q_BLOCK, gw), lambda b, g, i: (b * nq + i, g)),
        out_shape=jax.ShapeDtypeStruct((batch * seq, N_HEADS * HEAD_DIM), BF16),
        scratch_shapes=[pltpu.VMEM((LANES, Q_BLOCK), F32), pltpu.VMEM((SEL_TILE, GROUP * Q_BLOCK), F32),
                        pltpu.VMEM((SEL_TILE, GROUP * Q_BLOCK), F32), pltpu.VMEM((HEAD_DIM, GROUP * Q_BLOCK), F32),
                        pltpu.VMEM((LANES, Q_BLOCK), F32), pltpu.VMEM((SUBLANES, LANES), F32)],
        compiler_params=_cparams(("arbitrary", "arbitrary", "arbitrary")),
        name="nsa_attention",
    )(proj, proj, cos_q, sin_q, kc, vct, ks, vst, kw, vwt, overlap_t)


def _mem_kernel(q_ref, kv_ref, o_ref):
    q = q_ref[...]
    kv = kv_ref[0]
    mem_w = MEM_HEADS * MEM_HEAD_DIM
    outs = []
    for h in range(MEM_HEADS):
        sl = slice(h * MEM_HEAD_DIM, (h + 1) * MEM_HEAD_DIM)
        qh = (q[:, sl] * (MEM_HEAD_DIM ** -0.5)).astype(BF16)
        s = lax.dot_general(qh, kv[:, sl], (((1,), (1,)), ((), ())), preferred_element_type=F32)
        p = jnp.exp(s - jnp.max(s, axis=-1, keepdims=True))
        p = p / jnp.sum(p, axis=-1, keepdims=True)
        outs.append(jnp.dot(p.astype(BF16), kv[:, mem_w + h * MEM_HEAD_DIM : mem_w + (h + 1) * MEM_HEAD_DIM],
                            preferred_element_type=F32))
    o_ref[...] = jnp.concatenate(outs, axis=1).astype(o_ref.dtype)


def memory_attention(proj, kv_mem, batch, seq, tm):
    nt = seq // tm
    mem_len = kv_mem.shape[1]
    return pl.pallas_call(
        _mem_kernel,
        grid=(batch, nt),
        in_specs=[
            pl.BlockSpec((tm, D_MODEL), lambda b, t: (b * nt + t, COL_QM // D_MODEL)),
            pl.BlockSpec((1, mem_len, 2 * D_MODEL), lambda b, t: (b, 0, 0)),
        ],
        out_specs=pl.BlockSpec((tm, D_MODEL), lambda b, t: (b * nt + t, 0)),
        out_shape=jax.ShapeDtypeStruct((batch * seq, D_MODEL), BF16),
        compiler_params=_cparams(("arbitrary", "arbitrary")),
        name="memory_attention",
    )(proj, kv_mem)


def _merge_kernel(oa_ref, ob_ref, oc_ref, ga_ref, gb_ref, gc_ref, wa_ref, wb_ref, wc_ref, wo_ref,
                  gpost_ref, x_ref, gnext_ref, xo_ref, ho_ref):
    gate = lambda ref: _sigmoid(ref[...].astype(F32))
    merged = gate(ga_ref) * jnp.dot(oa_ref[...], wa_ref[...], preferred_element_type=F32)
    merged += gate(gb_ref) * jnp.dot(ob_ref[...], wb_ref[...], preferred_element_type=F32)
    merged += gate(gc_ref) * jnp.dot(oc_ref[...], wc_ref[...], preferred_element_type=F32)
    y = jnp.dot(merged.astype(BF16), wo_ref[...], preferred_element_type=F32)
    x_new = x_ref[...] + _rms(y, gpost_ref[...])
    xo_ref[...] = x_new
    ho_ref[...] = _rms(x_new, gnext_ref[...]).astype(ho_ref.dtype)


def merge_out(oa, ob, oc, proj, wa, wb, wc, wo, layer, g_post, x, g_next, tm):
    n, d = x.shape
    row = lambda col: pl.BlockSpec((tm, d), lambda i: (i, col))
    full = lambda shape: pl.BlockSpec(shape, lambda i: (0, 0))
    w_spec = pl.BlockSpec((None, d, d), lambda i: (layer, 0, 0))
    gm = COL_GM // d
    return pl.pallas_call(
        _merge_kernel,
        grid=(n // tm,),
        in_specs=[row(0), row(0), row(0), row(gm), row(gm + 1), row(gm + 2),
                  w_spec, w_spec, w_spec, w_spec, full((1, d)), row(0), full((1, d))],
        out_specs=[row(0), row(0)],
        out_shape=[jax.ShapeDtypeStruct((n, d), F32), jax.ShapeDtypeStruct((n, d), BF16)],
        compiler_params=_cparams(("arbitrary",)),
        name="merge_out",
    )(oa, ob, oc, proj, proj, proj, wa, wb, wc, wo, g_post.reshape(1, d), x, g_next.reshape(1, d))


def _mlp_kernel(h_ref, w1_ref, w2_ref, gpost_ref, x_ref, gnext_ref, xo_ref, ho_ref, acc_ref):
    k = pl.program_id(1)

    @pl.when(k == 0)
    def _():
        acc_ref[...] = jnp.zeros_like(acc_ref)

    u = jnp.maximum(jnp.dot(h_ref[...], w1_ref[...], preferred_element_type=F32), 0.0)
    acc_ref[...] += jnp.dot((u * u).astype(BF16), w2_ref[...], preferred_element_type=F32)

    @pl.when(k == pl.num_programs(1) - 1)
    def _():
        x_new = x_ref[...] + _rms(acc_ref[...], gpost_ref[...])
        xo_ref[...] = x_new
        ho_ref[...] = _rms(x_new, gnext_ref[...]).astype(ho_ref.dtype)


def mlp(h, w1, w2, layer, g_post, x, g_next, tm, tf):
    n, d = x.shape
    ff = w1.shape[2]
    return pl.pallas_call(
        _mlp_kernel,
        grid=(n // tm, ff // tf),
        in_specs=[
            pl.BlockSpec((tm, d), lambda i, k: (i, 0)),
            pl.BlockSpec((None, d, tf), lambda i, k: (layer, 0, k)),
            pl.BlockSpec((None, tf, d), lambda i, k: (layer, k, 0)),
            pl.BlockSpec((1, d), lambda i, k: (0, 0)),
            pl.BlockSpec((tm, d), lambda i, k: (i, 0)),
            pl.BlockSpec((1, d), lambda i, k: (0, 0)),
        ],
        out_specs=[pl.BlockSpec((tm, d), lambda i, k: (i, 0))] * 2,
        out_shape=[jax.ShapeDtypeStruct((n, d), F32), jax.ShapeDtypeStruct((n, d), BF16)],
        scratch_shapes=[pltpu.VMEM((tm, d), F32)],
        compiler_params=_cparams(("arbitrary", "arbitrary")),
        name="mlp",
    )(h, w1, w2, g_post.reshape(1, d), x, g_next.reshape(1, d))


def _pack_w_in(w):
    o_kv, o_gn, o_qm, o_gm = 3072, 4608, 4656, 5680
    per_group = GROUP * 3
    lead = w.shape[:-1]
    gn = w[..., o_gn:o_qm].reshape(lead + (N_KV, per_group))
    gn = jnp.pad(gn, ((0, 0),) * (len(lead) + 1) + ((0, LANES - per_group),)).reshape(lead + (N_KV * LANES,))
    packed = jnp.concatenate([w[..., :o_kv], w[..., o_qm:o_gm], w[..., o_gm:], w[..., o_kv:o_gn], gn], axis=-1)
    assert packed.shape[-1] == D_INP
    return packed.astype(BF16)


def _overlap_matrix_t(nc):
    c0 = np.arange(nc)[:, None] * CMP_STRIDE
    s0 = np.arange(LANES)[None, :] * SEL_LEN
    ov = np.clip(np.minimum(c0 + CMP_LEN, s0 + SEL_LEN) - np.maximum(c0, s0), 0, None).astype(np.float32) / CMP_LEN
    return jnp.asarray(ov.T, dtype=BF16)


def kernel(x, mem, positions, ln_mix_pre, w_in, conv_w, conv_b, lru_wr, lru_br, lru_wi, lru_bi, lru_lambda, cmp_pe, cmp_w1, cmp_b1, cmp_w2, ln_mem, w_mem_kv, w_br_rnn, w_br_nsa, w_br_mem, w_out, ln_mix_post, ln_mlp_pre, mlp_w1, mlp_w2, ln_mlp_post):
    batch, seq, d = x.shape
    depth = w_in.shape[0]
    mem_len = mem.shape[1]
    n = batch * seq
    nc = seq // CMP_STRIDE
    assert d == D_MODEL and seq % (2 * SEL_TILE) == 0 and seq // SEL_LEN <= LANES and seq >= WIN_KEYS

    tm = min(512, seq)
    tm_big = min(1024, seq)
    xf = x.reshape(n, d)
    memf = mem.reshape(batch * mem_len, d)
    cos_t, sin_t = rope_tables(positions.reshape(n), tm)
    pos_c = jnp.pad(positions[:, CMP_LEN - 1 :: CMP_STRIDE], ((0, 0), (0, 1)))
    cos_c, sin_c = rope_tables(pos_c.reshape(batch * nc), nc)
    cos_c, sin_c = cos_c.reshape(batch, nc, LANES), sin_c.reshape(batch, nc, LANES)
    cos_q, sin_q = rope_tables_t(positions, tm)
    overlap_t = _overlap_matrix_t(nc)

    w_in_b, w_kv_b = _pack_w_in(w_in.astype(BF16)), w_mem_kv.astype(BF16)
    wr_b, wi_b = lru_wr.astype(BF16), lru_wi.astype(BF16)
    w1c_b, w2c_b = cmp_w1.astype(BF16), cmp_w2.astype(BF16)
    wa_b, wb_b, wc_b, wo_b = (w.astype(BF16) for w in (w_br_rnn, w_br_nsa, w_br_mem, w_out))
    w1_b, w2_b = mlp_w1.astype(BF16), mlp_w2.astype(BF16)

    h = rmsnorm_bf16(xf, ln_mix_pre[0], tm)
    for l in range(depth):
        proj = matmul(h, w_in_b, l, tm_big, 1024, BF16, "in_proj")

        o_a = rglru_branch(proj, batch, seq, conv_w[l], conv_b[l], wr_b[l], lru_br[l], wi_b[l], lru_bi[l],
                           lru_lambda[l], min(256, seq))

        kc_raw, vc_raw, ks, vst, kw, vwt = kv_prep(proj, cos_t, sin_t, batch, seq, tm)
        chunks = lambda a: a.reshape(batch * N_KV, nc, CMP_STRIDE * HEAD_DIM)
        pe = cmp_pe[l].reshape(2, 2, CMP_STRIDE * HEAD_DIM)
        b1c = cmp_b1[l].reshape(2, 1, -1)
        k_cmp = compress(chunks(kc_raw), 0, pe, w1c_b[l], b1c, w2c_b[l], cos_c, sin_c, batch, True)
        v_cmp = compress(chunks(vc_raw), 1, pe, w1c_b[l], b1c, w2c_b[l], cos_c, sin_c, batch, False)
        kc = k_cmp.reshape(batch, N_KV, nc, HEAD_DIM)
        vct = v_cmp.reshape(batch, N_KV, nc, HEAD_DIM).transpose(0, 1, 3, 2)
        o_b = nsa_attention(proj, cos_q, sin_q, kc, vct, ks, vst, kw, vwt, overlap_t, batch, seq)

        mem_h = rmsnorm_bf16(memf, ln_mem[l], mem_len)
        kv_mem = matmul(mem_h, w_kv_b, l, mem_len, 1024, BF16, "mem_kv")
        o_c = memory_attention(proj, kv_mem.reshape(batch, mem_len, 2 * D_MODEL), batch, seq, tm)

        xf, h = merge_out(o_a, o_b, o_c, proj, wa_b, wb_b, wc_b, wo_b, l, ln_mix_post[l], xf, ln_mlp_pre[l], tm)
        xf, h = mlp(h, w1_b, w2_b, l, ln_mlp_post[l], xf, ln_mix_pre[(l + 1) % depth], tm_big, 1024)
    return xf.reshape(batch, seq, d)
```

```python
import functools

import jax
import jax.numpy as jnp
import numpy as np
from jax import lax
from jax.experimental import pallas as pl
from jax.experimental.pallas import tpu as pltpu

F32 = jnp.float32
BF16 = jnp.bfloat16

D_MODEL = 1024
LRU_BLOCKS = 8
LRU_BW = D_MODEL // LRU_BLOCKS
CONV_W = 4
LRU_C = 8.0
N_HEADS = 16
HEAD_DIM = 64
N_KV = 4
GROUP = N_HEADS // N_KV
KV_W = N_KV * HEAD_DIM
CMP_STRIDE = 16
CMP_LEN = 32
SEL_LEN = 64
SEL_SHIFT = 6
N_SELECT = 16
WINDOW = 512
Q_BLOCK = 256
MEM_HEADS = 4
MEM_HEAD_DIM = D_MODEL // MEM_HEADS
D_FF = 4 * D_MODEL
ROPE_THETA = 10000.0
EPS = 1e-6
NEG = -1e30
FORCE = 1e4
TAKEN = -3e38
LOG2_E = 1.4426950408889634
MAX_OFFSET = 45.0
BOUND_SLACK = 1.01

LANES = 128
SUBLANES = 8
SEL_TILE = 512
WIN_KEYS = WINDOW + Q_BLOCK
V_ROWS = 80
VMEM_LIMIT = 48 * 1024 * 1024

COL_XR, COL_YR, COL_Q, COL_QM, COL_GM, COL_KV, COL_GN = 0, 1024, 2048, 3072, 4096, 7168, 8704
D_INP = 9216


def _cparams(sem):
    return pltpu.CompilerParams(dimension_semantics=sem, vmem_limit_bytes=VMEM_LIMIT)


def _sigmoid(x):
    return 0.5 * jnp.tanh(0.5 * x) + 0.5


def _gelu_tanh(x):
    return 0.5 * x * (1.0 + jnp.tanh(0.7978845608028654 * (x + 0.044715 * (x * x * x))))


def _rms(x, g):
    return x * lax.rsqrt(jnp.mean(x * x, axis=-1, keepdims=True) + EPS) * g


def _rms_kernel(x_ref, g_ref, o_ref):
    o_ref[...] = _rms(x_ref[...], g_ref[...]).astype(o_ref.dtype)


def rmsnorm_bf16(x, g, tm):
    m, d = x.shape
    return pl.pallas_call(
        _rms_kernel,
        grid=(m // tm,),
        in_specs=[pl.BlockSpec((tm, d), lambda i: (i, 0)), pl.BlockSpec((1, d), lambda i: (0, 0))],
        out_specs=pl.BlockSpec((tm, d), lambda i: (i, 0)),
        out_shape=jax.ShapeDtypeStruct((m, d), BF16),
        compiler_params=_cparams(("arbitrary",)),
        name="rmsnorm",
    )(x, g.reshape(1, d))


def _mm_kernel(a_ref, w_ref, o_ref):
    o_ref[...] = jnp.dot(a_ref[...], w_ref[...], preferred_element_type=F32).astype(o_ref.dtype)


def matmul(a, w, layer, tm, tn, out_dtype, name):
    m, k = a.shape
    n = w.shape[2]
    return pl.pallas_call(
        _mm_kernel,
        grid=(n // tn, m // tm),
        in_specs=[pl.BlockSpec((tm, k), lambda j, i: (i, 0)), pl.BlockSpec((None, k, tn), lambda j, i: (layer, 0, j))],
        out_specs=pl.BlockSpec((tm, tn), lambda j, i: (i, j)),
        out_shape=jax.ShapeDtypeStruct((m, n), out_dtype),
        compiler_params=_cparams(("arbitrary", "arbitrary")),
        name=name,
    )(a, w)


def _rope_table_kernel(pos_ref, inv_ref, cos_ref, sin_ref):
    ang = pos_ref[...].astype(F32) * inv_ref[...]
    lane = lax.broadcasted_iota(jnp.int32, ang.shape, 1)
    cos_ref[...] = jnp.cos(ang)
    sin_ref[...] = jnp.where((lane & (HEAD_DIM - 1)) < HEAD_DIM // 2, -1.0, 1.0) * jnp.sin(ang)


def rope_tables(pos_flat, tm):
    n = pos_flat.shape[0]
    half = HEAD_DIM // 2
    inv = ROPE_THETA ** (-jnp.arange(half, dtype=F32) * 2.0 / HEAD_DIM)
    inv_full = jnp.tile(inv, LANES // half).reshape(1, LANES)
    return pl.pallas_call(
        _rope_table_kernel,
        grid=(n // tm,),
        in_specs=[pl.BlockSpec((tm, 1), lambda i: (i, 0)), pl.BlockSpec((1, LANES), lambda i: (0, 0))],
        out_specs=[pl.BlockSpec((tm, LANES), lambda i: (i, 0))] * 2,
        out_shape=[jax.ShapeDtypeStruct((n, LANES), F32)] * 2,
        compiler_params=_cparams(("arbitrary",)),
        name="rope_tables",
    )(pos_flat.reshape(n, 1), inv_full)


def _rope_table_t_kernel(pos_ref, inv_ref, cos_ref, sin_ref):
    ang = inv_ref[...] * pos_ref[0].astype(F32)
    cos_ref[0] = jnp.cos(ang)
    sin_ref[0] = jnp.sin(ang)


def rope_tables_t(positions, tm):
    batch, seq = positions.shape
    half = HEAD_DIM // 2
    inv = (ROPE_THETA ** (-jnp.arange(half, dtype=F32) * 2.0 / HEAD_DIM)).reshape(half, 1)
    return pl.pallas_call(
        _rope_table_t_kernel,
        grid=(batch, seq // tm),
        in_specs=[pl.BlockSpec((1, 1, tm), lambda b, i: (b, 0, i)), pl.BlockSpec((half, 1), lambda b, i: (0, 0))],
        out_specs=[pl.BlockSpec((1, half, tm), lambda b, i: (b, 0, i))] * 2,
        out_shape=[jax.ShapeDtypeStruct((batch, half, seq), F32)] * 2,
        compiler_params=_cparams(("arbitrary", "arbitrary")),
        name="rope_tables_t",
    )(positions.reshape(batch, 1, seq), inv)


def _rope128(x, cos_t, sin_t):
    lane = lax.broadcasted_iota(jnp.int32, x.shape, 1)
    first = (lane & (HEAD_DIM - 1)) < HEAD_DIM // 2
    partner = jnp.where(first, pltpu.roll(x, LANES - HEAD_DIM // 2, 1), pltpu.roll(x, HEAD_DIM // 2, 1))
    return x * cos_t + partner * sin_t


def _rglru_kernel(xr_ref, yr_ref, cw_ref, cb_ref, wr_ref, br_ref, wi_ref, bi_ref, lam_ref, o_ref, h_sc, tail_sc):
    @pl.when(pl.program_id(1) == 0)
    def _():
        h_sc[...] = jnp.zeros_like(h_sc)
        tail_sc[...] = jnp.zeros_like(tail_sc)

    xr = xr_ref[...].astype(F32)
    t_len, d = xr.shape
    tail = tail_sc[...]
    row8 = lax.broadcasted_iota(jnp.int32, (SUBLANES, d), 0)
    cw = cw_ref[...]
    xc = cb_ref[...] + xr * cw[CONV_W - 1 : CONV_W, :]
    for k in range(1, CONV_W):
        rolled = pltpu.roll(xr, k, 0)
        head = jnp.where(row8 < k, pltpu.roll(tail, k, 0), rolled[0:SUBLANES])
        shifted = jnp.concatenate([head, rolled[SUBLANES:]], axis=0)
        xc = xc + shifted * cw[CONV_W - 1 - k : CONV_W - k, :]
    tail_sc[...] = xr[t_len - SUBLANES :]

    xcb = xc.astype(BF16)
    rl, il = [], []
    for n in range(LRU_BLOCKS):
        xb = xcb[:, n * LRU_BW : (n + 1) * LRU_BW]
        rl.append(jnp.dot(xb, wr_ref[n], preferred_element_type=F32))
        il.append(jnp.dot(xb, wi_ref[n], preferred_element_type=F32))
    r = _sigmoid(jnp.concatenate(rl, axis=1) + br_ref[...])
    ig = _sigmoid(jnp.concatenate(il, axis=1) + bi_ref[...])
    softplus_neg_lam = jnp.log1p(jnp.exp(-lam_ref[...]))
    log_a = (-LRU_C * softplus_neg_lam) * r
    a = jnp.exp(log_a)
    one_minus_a2 = 1.0 - a * a
    root = jnp.where(one_minus_a2 > 0.0, one_minus_a2 * lax.rsqrt(one_minus_a2), 0.0)
    b = root * (ig * xc)

    sub = lax.broadcasted_iota(jnp.int32, (t_len, d), 0) & (SUBLANES - 1)
    step = 1
    while step < SUBLANES:
        keep = sub >= step
        a_sh = jnp.where(keep, pltpu.roll(a, step, 0), 1.0)
        b_sh = jnp.where(keep, pltpu.roll(b, step, 0), 0.0)
        b = a * b_sh + b
        a = a * a_sh
        step *= 2
    h_prev = h_sc[...]
    groups = []
    for g in range(t_len // SUBLANES):
        sl = slice(g * SUBLANES, (g + 1) * SUBLANES)
        groups.append(b[sl] + a[sl] * h_prev)
        h_prev = jnp.broadcast_to(groups[-1][SUBLANES - 1 :], (SUBLANES, d))
    h_sc[...] = h_prev
    h = jnp.concatenate(groups, axis=0)
    o_ref[...] = (h * _gelu_tanh(yr_ref[...].astype(F32))).astype(o_ref.dtype)


def rglru_branch(proj, batch, seq, conv_w, conv_b, wr, br, wi, bi, lam, t_len):
    d = D_MODEL
    nt = seq // t_len
    vec = lambda v: v.reshape(1, d)
    full2 = lambda shape: pl.BlockSpec(shape, lambda b, t: (0,) * len(shape))
    return pl.pallas_call(
        _rglru_kernel,
        grid=(batch, nt),
        in_specs=[
            pl.BlockSpec((t_len, d), lambda b, t: (b * nt + t, COL_XR // d)),
            pl.BlockSpec((t_len, d), lambda b, t: (b * nt + t, COL_YR // d)),
            full2((CONV_W, d)), full2((1, d)),
            full2((LRU_BLOCKS, LRU_BW, LRU_BW)), full2((1, d)),
            full2((LRU_BLOCKS, LRU_BW, LRU_BW)), full2((1, d)),
            full2((1, d)),
        ],
        out_specs=pl.BlockSpec((t_len, d), lambda b, t: (b * nt + t, 0)),
        out_shape=jax.ShapeDtypeStruct((batch * seq, d), BF16),
        scratch_shapes=[pltpu.VMEM((SUBLANES, d), F32), pltpu.VMEM((SUBLANES, d), F32)],
        compiler_params=_cparams(("arbitrary", "arbitrary")),
        name="rglru",
    )(proj, proj, conv_w, vec(conv_b), wr.astype(BF16), vec(br), wi.astype(BF16), vec(bi), vec(lam))


def _kv_prep_kernel(kvc_ref, kvs_ref, kvw_ref, cos_ref, sin_ref, kco_ref, vco_ref, kso_ref, vso_ref, kwo_ref, vwo_ref):
    cos_t, sin_t = cos_ref[...], sin_ref[...]
    t_len = cos_t.shape[0]
    xc = kvc_ref[...]
    for g in range(N_KV):
        kco_ref[0, g] = xc[:, g * HEAD_DIM : (g + 1) * HEAD_DIM]
        vco_ref[0, g] = xc[:, KV_W + g * HEAD_DIM : KV_W + (g + 1) * HEAD_DIM]
    pad_row = lax.broadcasted_iota(jnp.int32, (V_ROWS - HEAD_DIM, t_len), 0)
    pad_rows = jnp.where(pad_row == 0, 1.0, 0.0).astype(BF16)
    pad_lane = lax.broadcasted_iota(jnp.int32, (t_len, LANES - HEAD_DIM), 1)
    pad_lanes = jnp.where(pad_lane == 0, 1.0, 0.0)
    for src, k_dst, v_dst in ((kvs_ref, kso_ref, vso_ref), (kvw_ref, kwo_ref, vwo_ref)):
        x = src[...].astype(F32)
        roped = [_rope128(x[:, c * LANES : (c + 1) * LANES], cos_t, sin_t) for c in range(KV_W // LANES)]
        v_t = x[:, KV_W:].T
        for g in range(N_KV):
            pair = roped[g * HEAD_DIM // LANES]
            lo = g * HEAD_DIM % LANES
            k_dst[0, g] = jnp.concatenate([pair[:, lo : lo + HEAD_DIM], pad_lanes], axis=1).astype(BF16)
            v_dst[0, g, :HEAD_DIM, :] = v_t[g * HEAD_DIM : (g + 1) * HEAD_DIM].astype(BF16)
            v_dst[0, g, HEAD_DIM:, :] = pad_rows


def kv_prep(proj, cos_t, sin_t, batch, seq, tm):
    nt = seq // tm
    blk = lambda col: pl.BlockSpec((tm, 2 * KV_W), lambda b, t: (b * nt + t, col // (2 * KV_W)))
    tab = pl.BlockSpec((tm, LANES), lambda b, t: (b * nt + t, 0))
    k_spec = pl.BlockSpec((1, N_KV, tm, HEAD_DIM), lambda b, t: (b, 0, t, 0))
    v_spec = pl.BlockSpec((1, N_KV, V_ROWS, tm), lambda b, t: (b, 0, 0, t))
    k_shape = jax.ShapeDtypeStruct((batch, N_KV, seq, HEAD_DIM), BF16)
    v_shape = jax.ShapeDtypeStruct((batch, N_KV, V_ROWS, seq), BF16)
    kx_spec = pl.BlockSpec((1, N_KV, tm, LANES), lambda b, t: (b, 0, t, 0))
    kx_shape = jax.ShapeDtypeStruct((batch, N_KV, seq, LANES), BF16)
    return pl.pallas_call(
        _kv_prep_kernel,
        grid=(batch, nt),
        in_specs=[blk(COL_KV), blk(COL_KV + 2 * KV_W), blk(COL_KV + 4 * KV_W), tab, tab],
        out_specs=[k_spec, k_spec, kx_spec, v_spec, kx_spec, v_spec],
        out_shape=[k_shape, k_shape, kx_shape, v_shape, kx_shape, v_shape],
        compiler_params=_cparams(("arbitrary", "arbitrary")),
        name="kv_prep",
    )(proj, proj, proj, cos_t, sin_t)


def _compress_kernel(x_ref, pe_ref, w1_ref, b1_ref, w2_ref, cos_ref, sin_ref, o_ref, *, rotary):
    x = x_ref[0]
    half = CMP_STRIDE * HEAD_DIM
    pe = pe_ref[0]
    w1 = w1_ref[0]
    u = jnp.dot((x + pe[0:1]).astype(BF16), w1[:half], preferred_element_type=F32)
    v = jnp.dot((x + pe[1:2]).astype(BF16), w1[half:], preferred_element_type=F32)
    hid = _gelu_tanh(u + pltpu.roll(v, v.shape[0] - 1, 0) + b1_ref[0])
    out = jnp.dot(hid.astype(BF16), w2_ref[0], preferred_element_type=F32)
    if rotary:
        hh = HEAD_DIM // 2
        partner = jnp.concatenate([out[:, hh:], out[:, :hh]], axis=1)
        out = out * cos_ref[0][:, :HEAD_DIM] + partner * sin_ref[0][:, :HEAD_DIM]
    o_ref[0] = out.astype(o_ref.dtype)


def compress(x_chunks, j, pe, w1, b1, w2, cos_c, sin_c, batch, rotary):
    bg, nc, width = x_chunks.shape
    return pl.pallas_call(
        functools.partial(_compress_kernel, rotary=rotary),
        grid=(bg,),
        in_specs=[
            pl.BlockSpec((1, nc, width), lambda i: (i, 0, 0)),
            pl.BlockSpec((1, 2, width), lambda i: (j, 0, 0)),
            pl.BlockSpec((1, 2 * width, w1.shape[2]), lambda i: (j, 0, 0)),
            pl.BlockSpec((1, 1, w1.shape[2]), lambda i: (j, 0, 0)),
            pl.BlockSpec((1, w2.shape[1], HEAD_DIM), lambda i: (j, 0, 0)),
            pl.BlockSpec((1, nc, LANES), lambda i: (i // N_KV, 0, 0)),
            pl.BlockSpec((1, nc, LANES), lambda i: (i // N_KV, 0, 0)),
        ],
        out_specs=pl.BlockSpec((1, nc, HEAD_DIM), lambda i: (i, 0, 0)),
        out_shape=jax.ShapeDtypeStruct((bg, nc, HEAD_DIM), BF16),
        compiler_params=_cparams(("arbitrary",)),
        name="compress_k" if rotary else "compress_v",
    )(x_chunks, pe, w1, b1, w2, cos_c, sin_c)


def _nsa_kernel(q_ref, gate_ref, cos_ref, sin_ref, kc_ref, vct_ref, ks_ref, vst_ref, kw_ref, vwt_ref, ovt_ref,
                o_ref, sel_sc, sa_sc, sb_sc, oc_sc, imp_sc, kmax_sc):
    qb = Q_BLOCK
    s0 = pl.program_id(2) * qb
    nc = kc_ref.shape[2]
    hh = HEAD_DIM // 2
    cols = GROUP * qb

    qt = q_ref[...].astype(F32).T
    cos_t, sin_t = cos_ref[0], sin_ref[0]
    heads = []
    for r in range(GROUP):
        x1 = qt[r * HEAD_DIM : r * HEAD_DIM + hh]
        x2 = qt[r * HEAD_DIM + hh : (r + 1) * HEAD_DIM]
        heads.append(jnp.concatenate([x1 * cos_t - x2 * sin_t, x2 * cos_t + x1 * sin_t], axis=0))
    q_t = (jnp.concatenate(heads, axis=1) * (HEAD_DIM ** -0.5 * LOG2_E)).astype(BF16)

    tq = s0 + lax.broadcasted_iota(jnp.int32, (1, qb), 1)

    @pl.when(pl.program_id(2) == 0)
    def _():
        for row, k_ref in enumerate((ks_ref, kw_ref)):
            def norm_step(t, best, k_ref=k_ref):
                k_rows = k_ref[0, 0, pl.ds(pl.multiple_of(t * SEL_TILE, SEL_TILE), SEL_TILE), :].astype(F32)
                sq = jnp.sum(k_rows * k_rows, axis=1, keepdims=True)
                return jnp.maximum(best, jnp.max(sq, axis=0, keepdims=True))

            best = lax.fori_loop(0, k_ref.shape[2] // SEL_TILE, norm_step, jnp.zeros((1, 1), F32))
            kmax_sc[row : row + 1, :] = jnp.broadcast_to(jnp.sqrt(best), (1, LANES))

    q_f32 = q_t.astype(F32)
    q_norm = jnp.sqrt(jnp.sum(q_f32 * q_f32, axis=0, keepdims=True))
    offset_s = q_norm * (kmax_sc[0:1, 0:1] * BOUND_SLACK)
    offset_w = q_norm * (kmax_sc[1:2, 0:1] * BOUND_SLACK)
    bound_ok = jnp.max(jnp.maximum(offset_s, offset_w)) <= MAX_OFFSET
    ext_row = lax.broadcasted_iota(jnp.int32, (SUBLANES, cols), 0)
    zero_rows = jnp.zeros((LANES - HEAD_DIM - SUBLANES, cols), F32)

    def extended(offset):
        return jnp.concatenate([q_f32, jnp.where(ext_row == 0, -offset, 0.0), zero_rows], axis=0).astype(BF16)

    q_ext = extended(jnp.zeros_like(offset_s))

    def compressed(n_c):
        cmp_end = lax.broadcasted_iota(jnp.int32, (n_c, qb), 0) * CMP_STRIDE + (CMP_LEN - 1)
        bias_c = jnp.where(cmp_end <= tq, 0.0, NEG)
        has_key = jnp.where(tq >= CMP_LEN - 1, 1.0, 0.0)
        sc = jnp.dot(kc_ref[0, 0, :n_c, :], q_t, preferred_element_type=F32) + jnp.concatenate([bias_c] * GROUP, axis=1)
        p_c = jnp.exp2(sc - jnp.max(sc, axis=0, keepdims=True))
        norm = jnp.concatenate([has_key] * GROUP, axis=1) / jnp.maximum(jnp.sum(p_c, axis=0, keepdims=True), 1e-30)
        p_c = p_c * norm
        oc_sc[...] = jnp.dot(vct_ref[0, 0, :, :n_c], p_c.astype(BF16), preferred_element_type=F32)
        p_sum = p_c[:, :qb]
        for r in range(1, GROUP):
            p_sum = p_sum + p_c[:, r * qb : (r + 1) * qb]
        p_hi = p_sum.astype(BF16)
        p_lo = (p_sum - p_hi.astype(F32)).astype(BF16)
        ovt = ovt_ref[:, :n_c]
        imp_sc[...] = (jnp.dot(ovt, p_hi, preferred_element_type=F32)
                       + jnp.dot(ovt, p_lo, preferred_element_type=F32))

    last_cmp = (s0 + qb - CMP_LEN) // CMP_STRIDE
    for v in range(nc // LANES):
        pl.when(last_cmp // LANES == v)(functools.partial(compressed, (v + 1) * LANES))
    o_c = oc_sc[...]
    imp = imp_sc[...]

    def window(bounded):
        w0 = pl.multiple_of(jnp.maximum(s0 - WINDOW, 0), LANES)
        wpos = w0 + lax.broadcasted_iota(jnp.int32, (WIN_KEYS, qb), 0)
        bias_w = jnp.where((wpos <= tq) & (wpos > tq - WINDOW), 0.0, NEG)
        q_op = extended(offset_w) if bounded else q_ext
        s_w = jnp.dot(kw_ref[0, 0, pl.ds(w0, WIN_KEYS), :], q_op, preferred_element_type=F32)
        s_w = s_w + jnp.concatenate([bias_w] * GROUP, axis=1)
        if not bounded:
            s_w = s_w - jnp.max(s_w, axis=0, keepdims=True)
        acc_w = jnp.dot(vwt_ref[0, 0, :, pl.ds(w0, WIN_KEYS)], jnp.exp2(s_w).astype(BF16),
                        preferred_element_type=F32)
        return acc_w[:HEAD_DIM] / jnp.maximum(acc_w[HEAD_DIM : HEAD_DIM + 1], 1e-30)

    def select_blocks():
        blk = lax.broadcasted_iota(jnp.int32, (LANES, qb), 0)
        cur = tq >> SEL_SHIFT
        forced = (blk == 0) | (blk == cur)
        causal = blk * SEL_LEN <= tq
        work = jnp.where(forced, TAKEN, jnp.where(causal, imp, -FORCE))
        for r in range(N_SELECT - 1):
            hit = work == jnp.max(work, axis=0, keepdims=True)
            if r == N_SELECT - 2:
                hit = hit & (cur == 0)
            work = jnp.where(hit, TAKEN, work)
        taken = work == TAKEN
        sel_sc[...] = jnp.where(taken, 1.0, 0.0)
        n_taken = jnp.sum(jnp.where(taken & causal, 1.0, 0.0), axis=0, keepdims=True)

        @pl.when(jnp.max(n_taken) > N_SELECT)
        def _():
            work = jnp.where(forced, FORCE, jnp.where(causal, imp, -FORCE))
            blk_f = blk.astype(F32)
            for _ in range(N_SELECT):
                m = jnp.max(work, axis=0, keepdims=True)
                idx = jnp.min(jnp.where(work == m, blk_f, float(LANES)), axis=0, keepdims=True)
                work = jnp.where(blk_f == idx, TAKEN, work)
            sel_sc[...] = jnp.where(work == TAKEN, 1.0, 0.0)

    blocks_per_tile = SEL_TILE // SEL_LEN
    row_minus_lane = (lax.broadcasted_iota(jnp.int32, (SEL_TILE, qb), 0)
                      - lax.broadcasted_iota(jnp.int32, (SEL_TILE, qb), 1))
    n_tiles = (s0 + qb - 1) // SEL_TILE + 1

    def scores(kb, q_op):
        k0 = pl.multiple_of(kb * SEL_TILE, SEL_TILE)
        picked = sel_sc[pl.ds(pl.multiple_of(kb * blocks_per_tile, blocks_per_tile), blocks_per_tile), :]
        picked = jnp.where(picked > 0.5, 0.0, NEG)
        picked = jnp.concatenate(
            [jnp.broadcast_to(picked[j : j + 1, :], (SEL_LEN, qb)) for j in range(blocks_per_tile)], axis=0)
        bias = jnp.where(row_minus_lane <= s0 - k0, picked, NEG)
        s_ = jnp.dot(ks_ref[0, 0, pl.ds(k0, SEL_TILE), :], q_op, preferred_element_type=F32)
        return s_ + jnp.concatenate([bias] * GROUP, axis=1)

    def values(kb, p):
        k0 = pl.multiple_of(kb * SEL_TILE, SEL_TILE)
        return jnp.dot(vst_ref[0, 0, :, pl.ds(k0, SEL_TILE)], p, preferred_element_type=F32)

    def bounded_path():
        o_win = window(True)
        select_blocks()
        q_off = extended(offset_s)

        def tile(kb, acc):
            return acc + values(kb, jnp.exp2(scores(kb, q_off)).astype(BF16))

        def quad(j, acc):
            return tile(4 * j + 3, tile(4 * j + 2, tile(4 * j + 1, tile(4 * j, acc))))

        quads = n_tiles // 4
        acc = lax.fori_loop(0, quads, quad, jnp.zeros((V_ROWS, cols), F32))
        return o_win, lax.fori_loop(4 * quads, n_tiles, tile, acc)

    def exact_path():
        o_win = window(False)
        select_blocks()

        def absorb(kb, s_, m_run, acc):
            m_new = jnp.maximum(m_run, jnp.max(s_, axis=0, keepdims=True))
            p = jnp.exp2(s_ - m_new).astype(BF16)
            return m_new, jnp.exp2(m_run - m_new) * acc + values(kb, p)

        sa_sc[...] = scores(0, q_ext)

        def sel_step(j, carry):
            m_run, acc = carry
            sb_sc[...] = scores(2 * j + 1, q_ext)
            m_run, acc = absorb(2 * j, sa_sc[...], m_run, acc)
            sa_sc[...] = scores(2 * j + 2, q_ext)
            return absorb(2 * j + 1, sb_sc[...], m_run, acc)

        init = (jnp.full((1, cols), NEG, F32), jnp.zeros((V_ROWS, cols), F32))
        full_trips = (n_tiles - 1) // 2
        m_run, acc = lax.fori_loop(0, full_trips, sel_step, init)
        m_run, acc = absorb(2 * full_trips, sa_sc[...], m_run, acc)

        def last_tile(carry):
            return absorb(n_tiles - 1, scores(n_tiles - 1, q_ext), *carry)

        return o_win, lax.cond(n_tiles - 2 * full_trips == 2, last_tile, lambda carry: carry, (m_run, acc))[1]

    o_w, acc_s = lax.cond(bound_ok, bounded_path, exact_path)

    o_s = acc_s[:HEAD_DIM] / jnp.maximum(acc_s[HEAD_DIM : HEAD_DIM + 1], 1e-30)

    gates_t = _sigmoid(gate_ref[...].astype(F32)).T
    outs = []
    for r in range(GROUP):
        sl = slice(r * qb, (r + 1) * qb)
        outs.append(gates_t[3 * r : 3 * r + 1] * o_c[:, sl] + gates_t[3 * r + 1 : 3 * r + 2] * o_s[:, sl]
                    + gates_t[3 * r + 2 : 3 * r + 3] * o_w[:, sl])
    o_ref[...] = jnp.concatenate(outs, axis=0).T.astype(o_ref.dtype)


def nsa_attention(proj, cos_q, sin_q, kc, vct, ks, vst, kw, vwt, overlap_t, batch, seq):
    nq = seq // Q_BLOCK
    nc = kc.shape[2]
    gw = GROUP * HEAD_DIM
    per_bg = lambda shape: pl.BlockSpec((1, 1) + shape, lambda b, g, i: (b, g, 0, 0))
    return pl.pallas_call(
        _nsa_kernel,
        grid=(batch, N_KV, nq),
        in_specs=[
            pl.BlockSpec((Q_BLOCK, gw), lambda b, g, i: (b * nq + i, COL_Q // gw + g)),
            pl.BlockSpec((Q_BLOCK, LANES), lambda b, g, i: (b * nq + i, COL_GN // LANES + g)),
            pl.BlockSpec((1, HEAD_DIM // 2, Q_BLOCK), lambda b, g, i: (b, 0, i)),
            pl.BlockSpec((1, HEAD_DIM // 2, Q_BLOCK), lambda b, g, i: (b, 0, i)),
            per_bg((nc, HEAD_DIM)), per_bg((HEAD_DIM, nc)),
            per_bg((seq, LANES)), per_bg((V_ROWS, seq)),
            per_bg((seq, LANES)), per_bg((V_ROWS, seq)),
            pl.BlockSpec((LANES, nc), lambda b, g, i: (0, 0)),
        ],
        out_specs=pl.BlockSpec((Q_BLOCK, gw), lambda b, g, i: (b * nq + i, g)),
        out_shape=jax.ShapeDtypeStruct((batch * seq, N_HEADS * HEAD_DIM), BF16),
        scratch_shapes=[pltpu.VMEM((LANES, Q_BLOCK), F32), pltpu.VMEM((SEL_TILE, GROUP * Q_BLOCK), F32),
                        pltpu.VMEM((SEL_TILE, GROUP * Q_BLOCK), F32), pltpu.VMEM((HEAD_DIM, GROUP * Q_BLOCK), F32),
                        pltpu.VMEM((LANES, Q_BLOCK), F32), pltpu.VMEM((SUBLANES, LANES), F32)],
        compiler_params=_cparams(("arbitrary", "arbitrary", "arbitrary")),
        name="nsa_attention",
    )(proj, proj, cos_q, sin_q, kc, vct, ks, vst, kw, vwt, overlap_t)


def _mem_kernel(q_ref, kv_ref, o_ref):
    q = q_ref[...]
    kv = kv_ref[0]
    mem_w = MEM_HEADS * MEM_HEAD_DIM
    outs = []
    for h in range(MEM_HEADS):
        sl = slice(h * MEM_HEAD_DIM, (h + 1) * MEM_HEAD_DIM)
        qh = (q[:, sl] * (MEM_HEAD_DIM ** -0.5)).astype(BF16)
        s = lax.dot_general(qh, kv[:, sl], (((1,), (1,)), ((), ())), preferred_element_type=F32)
        p = jnp.exp(s - jnp.max(s, axis=-1, keepdims=True))
        p = p / jnp.sum(p, axis=-1, keepdims=True)
        outs.append(jnp.dot(p.astype(BF16), kv[:, mem_w + h * MEM_HEAD_DIM : mem_w + (h + 1) * MEM_HEAD_DIM],
                            preferred_element_type=F32))
    o_ref[...] = jnp.concatenate(outs, axis=1).astype(o_ref.dtype)


def memory_attention(proj, kv_mem, batch, seq, tm):
    nt = seq // tm
    mem_len = kv_mem.shape[1]
    return pl.pallas_call(
        _mem_kernel,
        grid=(batch, nt),
        in_specs=[
            pl.BlockSpec((tm, D_MODEL), lambda b, t: (b * nt + t, COL_QM // D_MODEL)),
            pl.BlockSpec((1, mem_len, 2 * D_MODEL), lambda b, t: (b, 0, 0)),
        ],
        out_specs=pl.BlockSpec((tm, D_MODEL), lambda b, t: (b * nt + t, 0)),
        out_shape=jax.ShapeDtypeStruct((batch * seq, D_MODEL), BF16),
        compiler_params=_cparams(("arbitrary", "arbitrary")),
        name="memory_attention",
    )(proj, kv_mem)


def _merge_kernel(oa_ref, ob_ref, oc_ref, ga_ref, gb_ref, gc_ref, wa_ref, wb_ref, wc_ref, wo_ref,
                  gpost_ref, x_ref, gnext_ref, xo_ref, ho_ref):
    gate = lambda ref: _sigmoid(ref[...].astype(F32))
    merged = gate(ga_ref) * jnp.dot(oa_ref[...], wa_ref[...], preferred_element_type=F32)
    merged += gate(gb_ref) * jnp.dot(ob_ref[...], wb_ref[...], preferred_element_type=F32)
    merged += gate(gc_ref) * jnp.dot(oc_ref[...], wc_ref[...], preferred_element_type=F32)
    y = jnp.dot(merged.astype(BF16), wo_ref[...], preferred_element_type=F32)
    x_new = x_ref[...] + _rms(y, gpost_ref[...])
    xo_ref[...] = x_new
    ho_ref[...] = _rms(x_new, gnext_ref[...]).astype(ho_ref.dtype)


def merge_out(oa, ob, oc, proj, wa, wb, wc, wo, layer, g_post, x, g_next, tm):
    n, d = x.shape
    row = lambda col: pl.BlockSpec((tm, d), lambda i: (i, col))
    full = lambda shape: pl.BlockSpec(shape, lambda i: (0, 0))
    w_spec = pl.BlockSpec((None, d, d), lambda i: (layer, 0, 0))
    gm = COL_GM // d
    return pl.pallas_call(
        _merge_kernel,
        grid=(n // tm,),
        in_specs=[row(0), row(0), row(0), row(gm), row(gm + 1), row(gm + 2),
                  w_spec, w_spec, w_spec, w_spec, full((1, d)), row(0), full((1, d))],
        out_specs=[row(0), row(0)],
        out_shape=[jax.ShapeDtypeStruct((n, d), F32), jax.ShapeDtypeStruct((n, d), BF16)],
        compiler_params=_cparams(("arbitrary",)),
        name="merge_out",
    )(oa, ob, oc, proj, proj, proj, wa, wb, wc, wo, g_post.reshape(1, d), x, g_next.reshape(1, d))


def _mlp_kernel(h_ref, w1_ref, w2_ref, gpost_ref, x_ref, gnext_ref, xo_ref, ho_ref, acc_ref):
    k = pl.program_id(1)

    @pl.when(k == 0)
    def _():
        acc_ref[...] = jnp.zeros_like(acc_ref)

    u = jnp.maximum(jnp.dot(h_ref[...], w1_ref[...], preferred_element_type=F32), 0.0)
    acc_ref[...] += jnp.dot((u * u).astype(BF16), w2_ref[...], preferred_element_type=F32)

    @pl.when(k == pl.num_programs(1) - 1)
    def _():
        x_new = x_ref[...] + _rms(acc_ref[...], gpost_ref[...])
        xo_ref[...] = x_new
        ho_ref[...] = _rms(x_new, gnext_ref[...]).astype(ho_ref.dtype)


def mlp(h, w1, w2, layer, g_post, x, g_next, tm, tf):
    n, d = x.shape
    ff = w1.shape[2]
    return pl.pallas_call(
        _mlp_kernel,
        grid=(n // tm, ff // tf),
        in_specs=[
            pl.BlockSpec((tm, d), lambda i, k: (i, 0)),
            pl.BlockSpec((None, d, tf), lambda i, k: (layer, 0, k)),
            pl.BlockSpec((None, tf, d), lambda i, k: (layer, k, 0)),
            pl.BlockSpec((1, d), lambda i, k: (0, 0)),
            pl.BlockSpec((tm, d), lambda i, k: (i, 0)),
            pl.BlockSpec((1, d), lambda i, k: (0, 0)),
        ],
        out_specs=[pl.BlockSpec((tm, d), lambda i, k: (i, 0))] * 2,
        out_shape=[jax.ShapeDtypeStruct((n, d), F32), jax.ShapeDtypeStruct((n, d), BF16)],
        scratch_shapes=[pltpu.VMEM((tm, d), F32)],
        compiler_params=_cparams(("arbitrary", "arbitrary")),
        name="mlp",
    )(h, w1, w2, g_post.reshape(1, d), x, g_next.reshape(1, d))


def _pack_w_in(w):
    o_kv, o_gn, o_qm, o_gm = 3072, 4608, 4656, 5680
    per_group = GROUP * 3
    lead = w.shape[:-1]
    gn = w[..., o_gn:o_qm].reshape(lead + (N_KV, per_group))
    gn = jnp.pad(gn, ((0, 0),) * (len(lead) + 1) + ((0, LANES - per_group),)).reshape(lead + (N_KV * LANES,))
    packed = jnp.concatenate([w[..., :o_kv], w[..., o_qm:o_gm], w[..., o_gm:], w[..., o_kv:o_gn], gn], axis=-1)
    assert packed.shape[-1] == D_INP
    return packed.astype(BF16)


def _overlap_matrix_t(nc):
    c0 = np.arange(nc)[:, None] * CMP_STRIDE
    s0 = np.arange(LANES)[None, :] * SEL_LEN
    ov = np.clip(np.minimum(c0 + CMP_LEN, s0 + SEL_LEN) - np.maximum(c0, s0), 0, None).astype(np.float32) / CMP_LEN
    return jnp.asarray(ov.T, dtype=BF16)


def kernel(x, mem, positions, ln_mix_pre, w_in, conv_w, conv_b, lru_wr, lru_br, lru_wi, lru_bi, lru_lambda, cmp_pe, cmp_w1, cmp_b1, cmp_w2, ln_mem, w_mem_kv, w_br_rnn, w_br_nsa, w_br_mem, w_out, ln_mix_post, ln_mlp_pre, mlp_w1, mlp_w2, ln_mlp_post):
    batch, seq, d = x.shape
    depth = w_in.shape[0]
    mem_len = mem.shape[1]
    n = batch * seq
    nc = seq // CMP_STRIDE
    assert d == D_MODEL and seq % (2 * SEL_TILE) == 0 and seq // SEL_LEN <= LANES and seq >= WIN_KEYS

    tm = min(512, seq)
    tm_big = min(1024, seq)
    xf = x.reshape(n, d)
    memf = mem.reshape(batch * mem_len, d)
    cos_t, sin_t = rope_tables(positions.reshape(n), tm)
    pos_c = jnp.pad(positions[:, CMP_LEN - 1 :: CMP_STRIDE], ((0, 0), (0, 1)))
    cos_c, sin_c = rope_tables(pos_c.reshape(batch * nc), nc)
    cos_c, sin_c = cos_c.reshape(batch, nc, LANES), sin_c.reshape(batch, nc, LANES)
    cos_q, sin_q = rope_tables_t(positions, tm)
    overlap_t = _overlap_matrix_t(nc)

    w_in_b, w_kv_b = _pack_w_in(w_in.astype(BF16)), w_mem_kv.astype(BF16)
    wr_b, wi_b = lru_wr.astype(BF16), lru_wi.astype(BF16)
    w1c_b, w2c_b = cmp_w1.astype(BF16), cmp_w2.astype(BF16)
    wa_b, wb_b, wc_b, wo_b = (w.astype(BF16) for w in (w_br_rnn, w_br_nsa, w_br_mem, w_out))
    w1_b, w2_b = mlp_w1.astype(BF16), mlp_w2.astype(BF16)

    h = rmsnorm_bf16(xf, ln_mix_pre[0], tm)
    for l in range(depth):
        proj = matmul(h, w_in_b, l, tm_big, 1024, BF16, "in_proj")

        o_a = rglru_branch(proj, batch, seq, conv_w[l], conv_b[l], wr_b[l], lru_br[l], wi_b[l], lru_bi[l],
                           lru_lambda[l], min(256, seq))

        kc_raw, vc_raw, ks, vst, kw, vwt = kv_prep(proj, cos_t, sin_t, batch, seq, tm)
        chunks = lambda a: a.reshape(batch * N_KV, nc, CMP_STRIDE * HEAD_DIM)
        pe = cmp_pe[l].reshape(2, 2, CMP_STRIDE * HEAD_DIM)
        b1c = cmp_b1[l].reshape(2, 1, -1)
        k_cmp = compress(chunks(kc_raw), 0, pe, w1c_b[l], b1c, w2c_b[l], cos_c, sin_c, batch, True)
        v_cmp = compress(chunks(vc_raw), 1, pe, w1c_b[l], b1c, w2c_b[l], cos_c, sin_c, batch, False)
        kc = k_cmp.reshape(batch, N_KV, nc, HEAD_DIM)
        vct = v_cmp.reshape(batch, N_KV, nc, HEAD_DIM).transpose(0, 1, 3, 2)
        o_b = nsa_attention(proj, cos_q, sin_q, kc, vct, ks, vst, kw, vwt, overlap_t, batch, seq)

        mem_h = rmsnorm_bf16(memf, ln_mem[l], mem_len)
        kv_mem = matmul(mem_h, w_kv_b, l, mem_len, 1024, BF16, "mem_kv")
        o_c = memory_attention(proj, kv_mem.reshape(batch, mem_len, 2 * D_MODEL), batch, seq, tm)

        xf, h = merge_out(o_a, o_b, o_c, proj, wa_b, wb_b, wc_b, wo_b, l, ln_mix_post[l], xf, ln_mlp_pre[l], tm)
        xf, h = mlp(h, w1_b, w2_b, l, ln_mlp_post[l], xf, ln_mix_pre[(l + 1) % depth], tm_big, 1024)
    return xf.reshape(batch, seq, d)
```

```python
import functools

import jax
import jax.numpy as jnp
import numpy as np
from jax import lax
from jax.experimental import pallas as pl
from jax.experimental.pallas import tpu as pltpu

F32 = jnp.float32
BF16 = jnp.bfloat16

D_MODEL = 1024
LRU_BLOCKS = 8
LRU_BW = D_MODEL // LRU_BLOCKS
CONV_W = 4
LRU_C = 8.0
N_HEADS = 16
HEAD_DIM = 64
N_KV = 4
GROUP = N_HEADS // N_KV
KV_W = N_KV * HEAD_DIM
CMP_STRIDE = 16
CMP_LEN = 32
SEL_LEN = 64
SEL_SHIFT = 6
N_SELECT = 16
WINDOW = 512
Q_BLOCK = 256
MEM_HEADS = 4
MEM_HEAD_DIM = D_MODEL // MEM_HEADS
D_FF = 4 * D_MODEL
ROPE_THETA = 10000.0
EPS = 1e-6
NEG = -1e30
FORCE = 1e4
TAKEN = -3e38
LOG2_E = 1.4426950408889634
MAX_OFFSET = 45.0
BOUND_SLACK = 1.01

LANES = 128
SUBLANES = 8
SEL_TILE = 512
WIN_KEYS = WINDOW + Q_BLOCK
V_ROWS = 80
VMEM_LIMIT = 48 * 1024 * 1024

COL_XR, COL_YR, COL_Q, COL_QM, COL_GM, COL_KV, COL_GN = 0, 1024, 2048, 3072, 4096, 7168, 8704
D_INP = 9216


def _cparams(sem):
    return pltpu.CompilerParams(dimension_semantics=sem, vmem_limit_bytes=VMEM_LIMIT)


def _sigmoid(x):
    return 0.5 * jnp.tanh(0.5 * x) + 0.5


def _gelu_tanh(x):
    return 0.5 * x * (1.0 + jnp.tanh(0.7978845608028654 * (x + 0.044715 * (x * x * x))))


def _rms(x, g):
    return x * lax.rsqrt(jnp.mean(x * x, axis=-1, keepdims=True) + EPS) * g


def _rms_kernel(x_ref, g_ref, o_ref):
    o_ref[...] = _rms(x_ref[...], g_ref[...]).astype(o_ref.dtype)


def rmsnorm_bf16(x, g, tm):
    m, d = x.shape
    return pl.pallas_call(
        _rms_kernel,
        grid=(m // tm,),
        in_specs=[pl.BlockSpec((tm, d), lambda i: (i, 0)), pl.BlockSpec((1, d), lambda i: (0, 0))],
        out_specs=pl.BlockSpec((tm, d), lambda i: (i, 0)),
        out_shape=jax.ShapeDtypeStruct((m, d), BF16),
        compiler_params=_cparams(("arbitrary",)),
        name="rmsnorm",
    )(x, g.reshape(1, d))


def _mm_kernel(a_ref, w_ref, o_ref):
    o_ref[...] = jnp.dot(a_ref[...], w_ref[...], preferred_element_type=F32).astype(o_ref.dtype)


def matmul(a, w, layer, tm, tn, out_dtype, name):
    m, k = a.shape
    n = w.shape[2]
    return pl.pallas_call(
        _mm_kernel,
        grid=(n // tn, m // tm),
        in_specs=[pl.BlockSpec((tm, k), lambda j, i: (i, 0)), pl.BlockSpec((None, k, tn), lambda j, i: (layer, 0, j))],
        out_specs=pl.BlockSpec((tm, tn), lambda j, i: (i, j)),
        out_shape=jax.ShapeDtypeStruct((m, n), out_dtype),
        compiler_params=_cparams(("arbitrary", "arbitrary")),
        name=name,
    )(a, w)


def _rope_table_kernel(pos_ref, inv_ref, cos_ref, sin_ref):
    ang = pos_ref[...].astype(F32) * inv_ref[...]
    lane = lax.broadcasted_iota(jnp.int32, ang.shape, 1)
    cos_ref[...] = jnp.cos(ang)
    sin_ref[...] = jnp.where((lane & (HEAD_DIM - 1)) < HEAD_DIM // 2, -1.0, 1.0) * jnp.sin(ang)


def rope_tables(pos_flat, tm):
    n = pos_flat.shape[0]
    half = HEAD_DIM // 2
    inv = ROPE_THETA ** (-jnp.arange(half, dtype=F32) * 2.0 / HEAD_DIM)
    inv_full = jnp.tile(inv, LANES // half).reshape(1, LANES)
    return pl.pallas_call(
        _rope_table_kernel,
        grid=(n // tm,),
        in_specs=[pl.BlockSpec((tm, 1), lambda i: (i, 0)), pl.BlockSpec((1, LANES), lambda i: (0, 0))],
        out_specs=[pl.BlockSpec((tm, LANES), lambda i: (i, 0))] * 2,
        out_shape=[jax.ShapeDtypeStruct((n, LANES), F32)] * 2,
        compiler_params=_cparams(("arbitrary",)),
        name="rope_tables",
    )(pos_flat.reshape(n, 1), inv_full)


def _rope_table_t_kernel(pos_ref, inv_ref, cos_ref, sin_ref):
    ang = inv_ref[...] * pos_ref[0].astype(F32)
    cos_ref[0] = jnp.cos(ang)
    sin_ref[0] = jnp.sin(ang)


def rope_tables_t(positions, tm):
    batch, seq = positions.shape
    half = HEAD_DIM // 2
    inv = (ROPE_THETA ** (-jnp.arange(half, dtype=F32) * 2.0 / HEAD_DIM)).reshape(half, 1)
    return pl.pallas_call(
        _rope_table_t_kernel,
        grid=(batch, seq // tm),
        in_specs=[pl.BlockSpec((1, 1, tm), lambda b, i: (b, 0, i)), pl.BlockSpec((half, 1), lambda b, i: (0, 0))],
        out_specs=[pl.BlockSpec((1, half, tm), lambda b, i: (b, 0, i))] * 2,
        out_shape=[jax.ShapeDtypeStruct((batch, half, seq), F32)] * 2,
        compiler_params=_cparams(("arbitrary", "arbitrary")),
        name="rope_tables_t",
    )(positions.reshape(batch, 1, seq), inv)


def _rope128(x, cos_t, sin_t):
    lane = lax.broadcasted_iota(jnp.int32, x.shape, 1)
    first = (lane & (HEAD_DIM - 1)) < HEAD_DIM // 2
    partner = jnp.where(first, pltpu.roll(x, LANES - HEAD_DIM // 2, 1), pltpu.roll(x, HEAD_DIM // 2, 1))
    return x * cos_t + partner * sin_t


def _rglru_kernel(xr_ref, yr_ref, cw_ref, cb_ref, wr_ref, br_ref, wi_ref, bi_ref, lam_ref, o_ref, h_sc, tail_sc):
    @pl.when(pl.program_id(1) == 0)
    def _():
        h_sc[...] = jnp.zeros_like(h_sc)
        tail_sc[...] = jnp.zeros_like(tail_sc)

    xr = xr_ref[...].astype(F32)
    t_len, d = xr.shape
    tail = tail_sc[...]
    row8 = lax.broadcasted_iota(jnp.int32, (SUBLANES, d), 0)
    cw = cw_ref[...]
    xc = cb_ref[...] + xr * cw[CONV_W - 1 : CONV_W, :]
    for k in range(1, CONV_W):
        rolled = pltpu.roll(xr, k, 0)
        head = jnp.where(row8 < k, pltpu.roll(tail, k, 0), rolled[0:SUBLANES])
        shifted = jnp.concatenate([head, rolled[SUBLANES:]], axis=0)
        xc = xc + shifted * cw[CONV_W - 1 - k : CONV_W - k, :]
    tail_sc[...] = xr[t_len - SUBLANES :]

    xcb = xc.astype(BF16)
    rl, il = [], []
    for n in range(LRU_BLOCKS):
        xb = xcb[:, n * LRU_BW : (n + 1) * LRU_BW]
        rl.append(jnp.dot(xb, wr_ref[n], preferred_element_type=F32))
        il.append(jnp.dot(xb, wi_ref[n], preferred_element_type=F32))
    r = _sigmoid(jnp.concatenate(rl, axis=1) + br_ref[...])
    ig = _sigmoid(jnp.concatenate(il, axis=1) + bi_ref[...])
    softplus_neg_lam = jnp.log1p(jnp.exp(-lam_ref[...]))
    log_a = (-LRU_C * softplus_neg_lam) * r
    a = jnp.exp(log_a)
    one_minus_a2 = 1.0 - a * a
    root = jnp.where(one_minus_a2 > 0.0, one_minus_a2 * lax.rsqrt(one_minus_a2), 0.0)
    b = root * (ig * xc)

    sub = lax.broadcasted_iota(jnp.int32, (t_len, d), 0) & (SUBLANES - 1)
    step = 1
    while step < SUBLANES:
        keep = sub >= step
        a_sh = jnp.where(keep, pltpu.roll(a, step, 0), 1.0)
        b_sh = jnp.where(keep, pltpu.roll(b, step, 0), 0.0)
        b = a * b_sh + b
        a = a * a_sh
        step *= 2
    h_prev = h_sc[...]
    groups = []
    for g in range(t_len // SUBLANES):
        sl = slice(g * SUBLANES, (g + 1) * SUBLANES)
        groups.append(b[sl] + a[sl] * h_prev)
        h_prev = jnp.broadcast_to(groups[-1][SUBLANES - 1 :], (SUBLANES, d))
    h_sc[...] = h_prev
    h = jnp.concatenate(groups, axis=0)
    o_ref[...] = (h * _gelu_tanh(yr_ref[...].astype(F32))).astype(o_ref.dtype)


def rglru_branch(proj, batch, seq, conv_w, conv_b, wr, br, wi, bi, lam, t_len):
    d = D_MODEL
    nt = seq // t_len
    vec = lambda v: v.reshape(1, d)
    full2 = lambda shape: pl.BlockSpec(shape, lambda b, t: (0,) * len(shape))
    return pl.pallas_call(
        _rglru_kernel,
        grid=(batch, nt),
        in_specs=[
            pl.BlockSpec((t_len, d), lambda b, t: (b * nt + t, COL_XR // d)),
            pl.BlockSpec((t_len, d), lambda b, t: (b * nt + t, COL_YR // d)),
            full2((CONV_W, d)), full2((1, d)),
            full2((LRU_BLOCKS, LRU_BW, LRU_BW)), full2((1, d)),
            full2((LRU_BLOCKS, LRU_BW, LRU_BW)), full2((1, d)),
            full2((1, d)),
        ],
        out_specs=pl.BlockSpec((t_len, d), lambda b, t: (b * nt + t, 0)),
        out_shape=jax.ShapeDtypeStruct((batch * seq, d), BF16),
        scratch_shapes=[pltpu.VMEM((SUBLANES, d), F32), pltpu.VMEM((SUBLANES, d), F32)],
        compiler_params=_cparams(("arbitrary", "arbitrary")),
        name="rglru",
    )(proj, proj, conv_w, vec(conv_b), wr.astype(BF16), vec(br), wi.astype(BF16), vec(bi), vec(lam))


def _kv_prep_kernel(kvc_ref, kvs_ref, kvw_ref, cos_ref, sin_ref, kco_ref, vco_ref, kso_ref, vso_ref, kwo_ref, vwo_ref):
    cos_t, sin_t = cos_ref[...], sin_ref[...]
    t_len = cos_t.shape[0]
    xc = kvc_ref[...]
    for g in range(N_KV):
        kco_ref[0, g] = xc[:, g * HEAD_DIM : (g + 1) * HEAD_DIM]
        vco_ref[0, g] = xc[:, KV_W + g * HEAD_DIM : KV_W + (g + 1) * HEAD_DIM]
    pad_row = lax.broadcasted_iota(jnp.int32, (V_ROWS - HEAD_DIM, t_len), 0)
    pad_rows = jnp.where(pad_row == 0, 1.0, 0.0).astype(BF16)
    pad_lane = lax.broadcasted_iota(jnp.int32, (t_len, LANES - HEAD_DIM), 1)
    pad_lanes = jnp.where(pad_lane == 0, 1.0, 0.0)
    for src, k_dst, v_dst in ((kvs_ref, kso_ref, vso_ref), (kvw_ref, kwo_ref, vwo_ref)):
        x = src[...].astype(F32)
        roped = [_rope128(x[:, c * LANES : (c + 1) * LANES], cos_t, sin_t) for c in range(KV_W // LANES)]
        v_t = x[:, KV_W:].T
        for g in range(N_KV):
            pair = roped[g * HEAD_DIM // LANES]
            lo = g * HEAD_DIM % LANES
            k_dst[0, g] = jnp.concatenate([pair[:, lo : lo + HEAD_DIM], pad_lanes], axis=1).astype(BF16)
            v_dst[0, g, :HEAD_DIM, :] = v_t[g * HEAD_DIM : (g + 1) * HEAD_DIM].astype(BF16)
            v_dst[0, g, HEAD_DIM:, :] = pad_rows


def kv_prep(proj, cos_t, sin_t, batch, seq, tm):
    nt = seq // tm
    blk = lambda col: pl.BlockSpec((tm, 2 * KV_W), lambda b, t: (b * nt + t, col // (2 * KV_W)))
    tab = pl.BlockSpec((tm, LANES), lambda b, t: (b * nt + t, 0))
    k_spec = pl.BlockSpec((1, N_KV, tm, HEAD_DIM), lambda b, t: (b, 0, t, 0))
    v_spec = pl.BlockSpec((1, N_KV, V_ROWS, tm), lambda b, t: (b, 0, 0, t))
    k_shape = jax.ShapeDtypeStruct((batch, N_KV, seq, HEAD_DIM), BF16)
    v_shape = jax.ShapeDtypeStruct((batch, N_KV, V_ROWS, seq), BF16)
    kx_spec = pl.BlockSpec((1, N_KV, tm, LANES), lambda b, t: (b, 0, t, 0))
    kx_shape = jax.ShapeDtypeStruct((batch, N_KV, seq, LANES), BF16)
    return pl.pallas_call(
        _kv_prep_kernel,
        grid=(batch, nt),
        in_specs=[blk(COL_KV), blk(COL_KV + 2 * KV_W), blk(COL_KV + 4 * KV_W), tab, tab],
        out_specs=[k_spec, k_spec, kx_spec, v_spec, kx_spec, v_spec],
        out_shape=[k_shape, k_shape, kx_shape, v_shape, kx_shape, v_shape],
        compiler_params=_cparams(("arbitrary", "arbitrary")),
        name="kv_prep",
    )(proj, proj, proj, cos_t, sin_t)


def _compress_kernel(x_ref, pe_ref, w1_ref, b1_ref, w2_ref, cos_ref, sin_ref, o_ref, *, rotary):
    x = x_ref[0]
    half = CMP_STRIDE * HEAD_DIM
    pe = pe_ref[0]
    w1 = w1_ref[0]
    u = jnp.dot((x + pe[0:1]).astype(BF16), w1[:half], preferred_element_type=F32)
    v = jnp.dot((x + pe[1:2]).astype(BF16), w1[half:], preferred_element_type=F32)
    hid = _gelu_tanh(u + pltpu.roll(v, v.shape[0] - 1, 0) + b1_ref[0])
    out = jnp.dot(hid.astype(BF16), w2_ref[0], preferred_element_type=F32)
    if rotary:
        hh = HEAD_DIM // 2
        partner = jnp.concatenate([out[:, hh:], out[:, :hh]], axis=1)
        out = out * cos_ref[0][:, :HEAD_DIM] + partner * sin_ref[0][:, :HEAD_DIM]
    o_ref[0] = out.astype(o_ref.dtype)


def compress(x_chunks, j, pe, w1, b1, w2, cos_c, sin_c, batch, rotary):
    bg, nc, width = x_chunks.shape
    return pl.pallas_call(
        functools.partial(_compress_kernel, rotary=rotary),
        grid=(bg,),
        in_specs=[
            pl.BlockSpec((1, nc, width), lambda i: (i, 0, 0)),
            pl.BlockSpec((1, 2, width), lambda i: (j, 0, 0)),
            pl.BlockSpec((1, 2 * width, w1.shape[2]), lambda i: (j, 0, 0)),
            pl.BlockSpec((1, 1, w1.shape[2]), lambda i: (j, 0, 0)),
            pl.BlockSpec((1, w2.shape[1], HEAD_DIM), lambda i: (j, 0, 0)),
            pl.BlockSpec((1, nc, LANES), lambda i: (i // N_KV, 0, 0)),
            pl.BlockSpec((1, nc, LANES), lambda i: (i // N_KV, 0, 0)),
        ],
        out_specs=pl.BlockSpec((1, nc, HEAD_DIM), lambda i: (i, 0, 0)),
        out_shape=jax.ShapeDtypeStruct((bg, nc, HEAD_DIM), BF16),
        compiler_params=_cparams(("arbitrary",)),
        name="compress_k" if rotary else "compress_v",
    )(x_chunks, pe, w1, b1, w2, cos_c, sin_c)


def _nsa_kernel(q_ref, gate_ref, cos_ref, sin_ref, kc_ref, vct_ref, ks_ref, vst_ref, kw_ref, vwt_ref, ovt_ref,
                o_ref, sel_sc, sa_sc, sb_sc, oc_sc, imp_sc, kmax_sc):
    qb = Q_BLOCK
    s0 = pl.program_id(2) * qb
    nc = kc_ref.shape[2]
    hh = HEAD_DIM // 2
    cols = GROUP * qb

    qt = q_ref[...].astype(F32).T
    cos_t, sin_t = cos_ref[0], sin_ref[0]
    heads = []
    for r in range(GROUP):
        x1 = qt[r * HEAD_DIM : r * HEAD_DIM + hh]
        x2 = qt[r * HEAD_DIM + hh : (r + 1) * HEAD_DIM]
        heads.append(jnp.concatenate([x1 * cos_t - x2 * sin_t, x2 * cos_t + x1 * sin_t], axis=0))
    q_t = (jnp.concatenate(heads, axis=1) * (HEAD_DIM ** -0.5 * LOG2_E)).astype(BF16)

    tq = s0 + lax.broadcasted_iota(jnp.int32, (1, qb), 1)

    @pl.when(pl.program_id(2) == 0)
    def _():
        for row, k_ref in enumerate((ks_ref, kw_ref)):
            def norm_step(t, best, k_ref=k_ref):
                k_rows = k_ref[0, 0, pl.ds(pl.multiple_of(t * SEL_TILE, SEL_TILE), SEL_TILE), :].astype(F32)
                sq = jnp.sum(k_rows * k_rows, axis=1, keepdims=True)
                return jnp.maximum(best, jnp.max(sq, axis=0, keepdims=True))

            best = lax.fori_loop(0, k_ref.shape[2] // SEL_TILE, norm_step, jnp.zeros((1, 1), F32))
            kmax_sc[row : row + 1, :] = jnp.broadcast_to(jnp.sqrt(best), (1, LANES))

    q_f32 = q_t.astype(F32)
    q_norm = jnp.sqrt(jnp.sum(q_f32 * q_f32, axis=0, keepdims=True))
    offset_s = q_norm * (kmax_sc[0:1, 0:1] * BOUND_SLACK)
    offset_w = q_norm * (kmax_sc[1:2, 0:1] * BOUND_SLACK)
    bound_ok = jnp.max(jnp.maximum(offset_s, offset_w)) <= MAX_OFFSET
    ext_row = lax.broadcasted_iota(jnp.int32, (SUBLANES, cols), 0)
    zero_rows = jnp.zeros((LANES - HEAD_DIM - SUBLANES, cols), F32)

    def extended(offset):
        return jnp.concatenate([q_f32, jnp.where(ext_row == 0, -offset, 0.0), zero_rows], axis=0).astype(BF16)

    q_ext = extended(jnp.zeros_like(offset_s))

    def compressed(n_c):
        cmp_end = lax.broadcasted_iota(jnp.int32, (n_c, qb), 0) * CMP_STRIDE + (CMP_LEN - 1)
        bias_c = jnp.where(cmp_end <= tq, 0.0, NEG)
        has_key = jnp.where(tq >= CMP_LEN - 1, 1.0, 0.0)
        sc = jnp.dot(kc_ref[0, 0, :n_c, :], q_t, preferred_element_type=F32) + jnp.concatenate([bias_c] * GROUP, axis=1)
        p_c = jnp.exp2(sc - jnp.max(sc, axis=0, keepdims=True))
        norm = jnp.concatenate([has_key] * GROUP, axis=1) / jnp.maximum(jnp.sum(p_c, axis=0, keepdims=True), 1e-30)
        p_c = p_c * norm
        oc_sc[...] = jnp.dot(vct_ref[0, 0, :, :n_c], p_c.astype(BF16), preferred_element_type=F32)
        p_sum = p_c[:, :qb]
        for r in range(1, GROUP):
            p_sum = p_sum + p_c[:, r * qb : (r + 1) * qb]
        p_hi = p_sum.astype(BF16)
        p_lo = (p_sum - p_hi.astype(F32)).astype(BF16)
        ovt = ovt_ref[:, :n_c]
        imp_sc[...] = (jnp.dot(ovt, p_hi, preferred_element_type=F32)
                       + jnp.dot(ovt, p_lo, preferred_element_type=F32))

    last_cmp = (s0 + qb - CMP_LEN) // CMP_STRIDE
    for v in range(nc // LANES):
        pl.when(last_cmp // LANES == v)(functools.partial(compressed, (v + 1) * LANES))
    o_c = oc_sc[...]
    imp = imp_sc[...]

    def window(bounded):
        w0 = pl.multiple_of(jnp.maximum(s0 - WINDOW, 0), LANES)
        wpos = w0 + lax.broadcasted_iota(jnp.int32, (WIN_KEYS, qb), 0)
        bias_w = jnp.where((wpos <= tq) & (wpos > tq - WINDOW), 0.0, NEG)
        q_op = extended(offset_w) if bounded else q_ext
        s_w = jnp.dot(kw_ref[0, 0, pl.ds(w0, WIN_KEYS), :], q_op, preferred_element_type=F32)
        s_w = s_w + jnp.concatenate([bias_w] * GROUP, axis=1)
        if not bounded:
            s_w = s_w - jnp.max(s_w, axis=0, keepdims=True)
        acc_w = jnp.dot(vwt_ref[0, 0, :, pl.ds(w0, WIN_KEYS)], jnp.exp2(s_w).astype(BF16),
                        preferred_element_type=F32)
        return acc_w[:HEAD_DIM] / jnp.maximum(acc_w[HEAD_DIM : HEAD_DIM + 1], 1e-30)

    def select_blocks():
        blk = lax.broadcasted_iota(jnp.int32, (LANES, qb), 0)
        cur = tq >> SEL_SHIFT
        forced = (blk == 0) | (blk == cur)
        causal = blk * SEL_LEN <= tq
        work = jnp.where(forced, TAKEN, jnp.where(causal, imp, -FORCE))
        for r in range(N_SELECT - 1):
            hit = work == jnp.max(work, axis=0, keepdims=True)
            if r == N_SELECT - 2:
                hit = hit & (cur == 0)
            work = jnp.where(hit, TAKEN, work)
        taken = work == TAKEN
        sel_sc[...] = jnp.where(taken, 1.0, 0.0)
        n_taken = jnp.sum(jnp.where(taken & causal, 1.0, 0.0), axis=0, keepdims=True)

        @pl.when(jnp.max(n_taken) > N_SELECT)
        def _():
            work = jnp.where(forced, FORCE, jnp.where(causal, imp, -FORCE))
            blk_f = blk.astype(F32)
            for _ in range(N_SELECT):
                m = jnp.max(work, axis=0, keepdims=True)
                idx = jnp.min(jnp.where(work == m, blk_f, float(LANES)), axis=0, keepdims=True)
                work = jnp.where(blk_f == idx, TAKEN, work)
            sel_sc[...] = jnp.where(work == TAKEN, 1.0, 0.0)

    blocks_per_tile = SEL_TILE // SEL_LEN
    row_minus_lane = (lax.broadcasted_iota(jnp.int32, (SEL_TILE, qb), 0)
                      - lax.broadcasted_iota(jnp.int32, (SEL_TILE, qb), 1))
    n_tiles = (s0 + qb - 1) // SEL_TILE + 1

    def scores(kb, q_op):
        k0 = pl.multiple_of(kb * SEL_TILE, SEL_TILE)
        picked = sel_sc[pl.ds(pl.multiple_of(kb * blocks_per_tile, blocks_per_tile), blocks_per_tile), :]
        picked = jnp.where(picked > 0.5, 0.0, NEG)
        picked = jnp.concatenate(
            [jnp.broadcast_to(picked[j : j + 1, :], (SEL_LEN, qb)) for j in range(blocks_per_tile)], axis=0)
        bias = jnp.where(row_minus_lane <= s0 - k0, picked, NEG)
        s_ = jnp.dot(ks_ref[0, 0, pl.ds(k0, SEL_TILE), :], q_op, preferred_element_type=F32)
        return s_ + jnp.concatenate([bias] * GROUP, axis=1)

    def values(kb, p):
        k0 = pl.multiple_of(kb * SEL_TILE, SEL_TILE)
        return jnp.dot(vst_ref[0, 0, :, pl.ds(k0, SEL_TILE)], p, preferred_element_type=F32)

    def bounded_path():
        o_win = window(True)
        select_blocks()
        q_off = extended(offset_s)

        def tile(kb, acc):
            return acc + values(kb, jnp.exp2(scores(kb, q_off)).astype(BF16))

        def run(first, count, acc):
            for t in range(count):
                acc = tile(first + t, acc)
            return acc

        octs = n_tiles // 8
        acc = lax.fori_loop(0, octs, lambda j, acc: run(8 * j, 8, acc), jnp.zeros((V_ROWS, cols), F32))
        done = 8 * octs
        for count in (4, 2, 1):
            take = ((n_tiles - done) // count) % 2 == 1
            acc = lax.cond(take, functools.partial(run, done, count), lambda acc: acc, acc)
            done = done + jnp.where(take, count, 0)
        return o_win, acc

    def exact_path():
        o_win = window(False)
        select_blocks()

        def absorb(kb, s_, m_run, acc):
            m_new = jnp.maximum(m_run, jnp.max(s_, axis=0, keepdims=True))
            p = jnp.exp2(s_ - m_new).astype(BF16)
            return m_new, jnp.exp2(m_run - m_new) * acc + values(kb, p)

        sa_sc[...] = scores(0, q_ext)

        def sel_step(j, carry):
            m_run, acc = carry
            sb_sc[...] = scores(2 * j + 1, q_ext)
            m_run, acc = absorb(2 * j, sa_sc[...], m_run, acc)
            sa_sc[...] = scores(2 * j + 2, q_ext)
            return absorb(2 * j + 1, sb_sc[...], m_run, acc)

        init = (jnp.full((1, cols), NEG, F32), jnp.zeros((V_ROWS, cols), F32))
        full_trips = (n_tiles - 1) // 2
        m_run, acc = lax.fori_loop(0, full_trips, sel_step, init)
        m_run, acc = absorb(2 * full_trips, sa_sc[...], m_run, acc)

        def last_tile(carry):
            return absorb(n_tiles - 1, scores(n_tiles - 1, q_ext), *carry)

        return o_win, lax.cond(n_tiles - 2 * full_trips == 2, last_tile, lambda carry: carry, (m_run, acc))[1]

    o_w, acc_s = lax.cond(bound_ok, bounded_path, exact_path)

    o_s = acc_s[:HEAD_DIM] / jnp.maximum(acc_s[HEAD_DIM : HEAD_DIM + 1], 1e-30)

    gates_t = _sigmoid(gate_ref[...].astype(F32)).T
    outs = []
    for r in range(GROUP):
        sl = slice(r * qb, (r + 1) * qb)
        outs.append(gates_t[3 * r : 3 * r + 1] * o_c[:, sl] + gates_t[3 * r + 1 : 3 * r + 2] * o_s[:, sl]
                    + gates_t[3 * r + 2 : 3 * r + 3] * o_w[:, sl])
    o_ref[...] = jnp.concatenate(outs, axis=0).T.astype(o_ref.dtype)


def nsa_attention(proj, cos_q, sin_q, kc, vct, ks, vst, kw, vwt, overlap_t, batch, seq):
    nq = seq // Q_BLOCK
    nc = kc.shape[2]
    gw = GROUP * HEAD_DIM
    per_bg = lambda shape: pl.BlockSpec((1, 1) + shape, lambda b, g, i: (b, g, 0, 0))
    return pl.pallas_call(
        _nsa_kernel,
        grid=(batch, N_KV, nq),
        in_specs=[
            pl.BlockSpec((Q_BLOCK, gw), lambda b, g, i: (b * nq + i, COL_Q // gw + g)),
            pl.BlockSpec((Q_BLOCK, LANES), lambda b, g, i: (b * nq + i, COL_GN // LANES + g)),
            pl.BlockSpec((1, HEAD_DIM // 2, Q_BLOCK), lambda b, g, i: (b, 0, i)),
            pl.BlockSpec((1, HEAD_DIM // 2, Q_BLOCK), lambda b, g, i: (b, 0, i)),
            per_bg((nc, HEAD_DIM)), per_bg((HEAD_DIM, nc)),
            per_bg((seq, LANES)), per_bg((V_ROWS, seq)),
            per_bg((seq, LANES)), per_bg((V_ROWS, seq)),
            pl.BlockSpec((LANES, nc), lambda b, g, i: (0, 0)),
        ],
        out_specs=pl.BlockSpec((Q_BLOCK, gw), lambda b, g, i: (b * nq + i, g)),
        out_shape=jax.ShapeDtypeStruct((batch * seq, N_HEADS * HEAD_DIM), BF16),
        scratch_shapes=[pltpu.VMEM((LANES, Q_BLOCK), F32), pltpu.VMEM((SEL_TILE, GROUP * Q_BLOCK), F32),
                        pltpu.VMEM((SEL_TILE, GROUP * Q_BLOCK), F32), pltpu.VMEM((HEAD_DIM, GROUP * Q_BLOCK), F32),
                        pltpu.VMEM((LANES, Q_BLOCK), F32), pltpu.VMEM((SUBLANES, LANES), F32)],
        compiler_params=_cparams(("arbitrary", "arbitrary", "arbitrary")),
        name="nsa_attention",
    )(proj, proj, cos_q, sin_q, kc, vct, ks, vst, kw, vwt, overlap_t)


def _mem_kernel(q_ref, kv_ref, o_ref):
    q = q_ref[...]
    kv = kv_ref[0]
    mem_w = MEM_HEADS * MEM_HEAD_DIM
    outs = []
    for h in range(MEM_HEADS):
        sl = slice(h * MEM_HEAD_DIM, (h + 1) * MEM_HEAD_DIM)
        qh = (q[:, sl] * (MEM_HEAD_DIM ** -0.5)).astype(BF16)
        s = lax.dot_general(qh, kv[:, sl], (((1,), (1,)), ((), ())), preferred_element_type=F32)
        p = jnp.exp(s - jnp.max(s, axis=-1, keepdims=True))
        p = p / jnp.sum(p, axis=-1, keepdims=True)
        outs.append(jnp.dot(p.astype(BF16), kv[:, mem_w + h * MEM_HEAD_DIM : mem_w + (h + 1) * MEM_HEAD_DIM],
                            preferred_element_type=F32))
    o_ref[...] = jnp.concatenate(outs, axis=1).astype(o_ref.dtype)


def memory_attention(proj, kv_mem, batch, seq, tm):
    nt = seq // tm
    mem_len = kv_mem.shape[1]
    return pl.pallas_call(
        _mem_kernel,
        grid=(batch, nt),
        in_specs=[
            pl.BlockSpec((tm, D_MODEL), lambda b, t: (b * nt + t, COL_QM // D_MODEL)),
            pl.BlockSpec((1, mem_len, 2 * D_MODEL), lambda b, t: (b, 0, 0)),
        ],
        out_specs=pl.BlockSpec((tm, D_MODEL), lambda b, t: (b * nt + t, 0)),
        out_shape=jax.ShapeDtypeStruct((batch * seq, D_MODEL), BF16),
        compiler_params=_cparams(("arbitrary", "arbitrary")),
        name="memory_attention",
    )(proj, kv_mem)


def _merge_kernel(oa_ref, ob_ref, oc_ref, ga_ref, gb_ref, gc_ref, wa_ref, wb_ref, wc_ref, wo_ref,
                  gpost_ref, x_ref, gnext_ref, xo_ref, ho_ref):
    gate = lambda ref: _sigmoid(ref[...].astype(F32))
    merged = gate(ga_ref) * jnp.dot(oa_ref[...], wa_ref[...], preferred_element_type=F32)
    merged += gate(gb_ref) * jnp.dot(ob_ref[...], wb_ref[...], preferred_element_type=F32)
    merged += gate(gc_ref) * jnp.dot(oc_ref[...], wc_ref[...], preferred_element_type=F32)
    y = jnp.dot(merged.astype(BF16), wo_ref[...], preferred_element_type=F32)
    x_new = x_ref[...] + _rms(y, gpost_ref[...])
    xo_ref[...] = x_new
    ho_ref[...] = _rms(x_new, gnext_ref[...]).astype(ho_ref.dtype)


def merge_out(oa, ob, oc, proj, wa, wb, wc, wo, layer, g_post, x, g_next, tm):
    n, d = x.shape
    row = lambda col: pl.BlockSpec((tm, d), lambda i: (i, col))
    full = lambda shape: pl.BlockSpec(shape, lambda i: (0, 0))
    w_spec = pl.BlockSpec((None, d, d), lambda i: (layer, 0, 0))
    gm = COL_GM // d
    return pl.pallas_call(
        _merge_kernel,
        grid=(n // tm,),
        in_specs=[row(0), row(0), row(0), row(gm), row(gm + 1), row(gm + 2),
                  w_spec, w_spec, w_spec, w_spec, full((1, d)), row(0), full((1, d))],
        out_specs=[row(0), row(0)],
        out_shape=[jax.ShapeDtypeStruct((n, d), F32), jax.ShapeDtypeStruct((n, d), BF16)],
        compiler_params=_cparams(("arbitrary",)),
        name="merge_out",
    )(oa, ob, oc, proj, proj, proj, wa, wb, wc, wo, g_post.reshape(1, d), x, g_next.reshape(1, d))


def _mlp_kernel(h_ref, w1_ref, w2_ref, gpost_ref, x_ref, gnext_ref, xo_ref, ho_ref, acc_ref):
    k = pl.program_id(1)

    @pl.when(k == 0)
    def _():
        acc_ref[...] = jnp.zeros_like(acc_ref)

    u = jnp.maximum(jnp.dot(h_ref[...], w1_ref[...], preferred_element_type=F32), 0.0)
    acc_ref[...] += jnp.dot((u * u).astype(BF16), w2_ref[...], preferred_element_type=F32)

    @pl.when(k == pl.num_programs(1) - 1)
    def _():
        x_new = x_ref[...] + _rms(acc_ref[...], gpost_ref[...])
        xo_ref[...] = x_new
        ho_ref[...] = _rms(x_new, gnext_ref[...]).astype(ho_ref.dtype)


def mlp(h, w1, w2, layer, g_post, x, g_next, tm, tf):
    n, d = x.shape
    ff = w1.shape[2]
    return pl.pallas_call(
        _mlp_kernel,
        grid=(n // tm, ff // tf),
        in_specs=[
            pl.BlockSpec((tm, d), lambda i, k: (i, 0)),
            pl.BlockSpec((None, d, tf), lambda i, k: (layer, 0, k)),
            pl.BlockSpec((None, tf, d), lambda i, k: (layer, k, 0)),
            pl.BlockSpec((1, d), lambda i, k: (0, 0)),
            pl.BlockSpec((tm, d), lambda i, k: (i, 0)),
            pl.BlockSpec((1, d), lambda i, k: (0, 0)),
        ],
        out_specs=[pl.BlockSpec((tm, d), lambda i, k: (i, 0))] * 2,
        out_shape=[jax.ShapeDtypeStruct((n, d), F32), jax.ShapeDtypeStruct((n, d), BF16)],
        scratch_shapes=[pltpu.VMEM((tm, d), F32)],
        compiler_params=_cparams(("arbitrary", "arbitrary")),
        name="mlp",
    )(h, w1, w2, g_post.reshape(1, d), x, g_next.reshape(1, d))


def _pack_w_in(w):
    o_kv, o_gn, o_qm, o_gm = 3072, 4608, 4656, 5680
    per_group = GROUP * 3
    lead = w.shape[:-1]
    gn = w[..., o_gn:o_qm].reshape(lead + (N_KV, per_group))
    gn = jnp.pad(gn, ((0, 0),) * (len(lead) + 1) + ((0, LANES - per_group),)).reshape(lead + (N_KV * LANES,))
    packed = jnp.concatenate([w[..., :o_kv], w[..., o_qm:o_gm], w[..., o_gm:], w[..., o_kv:o_gn], gn], axis=-1)
    assert packed.shape[-1] == D_INP
    return packed.astype(BF16)


def _overlap_matrix_t(nc):
    c0 = np.arange(nc)[:, None] * CMP_STRIDE
    s0 = np.arange(LANES)[None, :] * SEL_LEN
    ov = np.clip(np.minimum(c0 + CMP_LEN, s0 + SEL_LEN) - np.maximum(c0, s0), 0, None).astype(np.float32) / CMP_LEN
    return jnp.asarray(ov.T, dtype=BF16)


def kernel(x, mem, positions, ln_mix_pre, w_in, conv_w, conv_b, lru_wr, lru_br, lru_wi, lru_bi, lru_lambda, cmp_pe, cmp_w1, cmp_b1, cmp_w2, ln_mem, w_mem_kv, w_br_rnn, w_br_nsa, w_br_mem, w_out, ln_mix_post, ln_mlp_pre, mlp_w1, mlp_w2, ln_mlp_post):
    batch, seq, d = x.shape
    depth = w_in.shape[0]
    mem_len = mem.shape[1]
    n = batch * seq
    nc = seq // CMP_STRIDE
    assert d == D_MODEL and seq % (2 * SEL_TILE) == 0 and seq // SEL_LEN <= LANES and seq >= WIN_KEYS

    tm = min(512, seq)
    tm_big = min(1024, seq)
    xf = x.reshape(n, d)
    memf = mem.reshape(batch * mem_len, d)
    cos_t, sin_t = rope_tables(positions.reshape(n), tm)
    pos_c = jnp.pad(positions[:, CMP_LEN - 1 :: CMP_STRIDE], ((0, 0), (0, 1)))
    cos_c, sin_c = rope_tables(pos_c.reshape(batch * nc), nc)
    cos_c, sin_c = cos_c.reshape(batch, nc, LANES), sin_c.reshape(batch, nc, LANES)
    cos_q, sin_q = rope_tables_t(positions, tm)
    overlap_t = _overlap_matrix_t(nc)

    w_in_b, w_kv_b = _pack_w_in(w_in.astype(BF16)), w_mem_kv.astype(BF16)
    wr_b, wi_b = lru_wr.astype(BF16), lru_wi.astype(BF16)
    w1c_b, w2c_b = cmp_w1.astype(BF16), cmp_w2.astype(BF16)
    wa_b, wb_b, wc_b, wo_b = (w.astype(BF16) for w in (w_br_rnn, w_br_nsa, w_br_mem, w_out))
    w1_b, w2_b = mlp_w1.astype(BF16), mlp_w2.astype(BF16)

    h = rmsnorm_bf16(xf, ln_mix_pre[0], tm)
    for l in range(depth):
        proj = matmul(h, w_in_b, l, tm_big, 1024, BF16, "in_proj")

        o_a = rglru_branch(proj, batch, seq, conv_w[l], conv_b[l], wr_b[l], lru_br[l], wi_b[l], lru_bi[l],
                           lru_lambda[l], min(256, seq))

        kc_raw, vc_raw, ks, vst, kw, vwt = kv_prep(proj, cos_t, sin_t, batch, seq, tm)
        chunks = lambda a: a.reshape(batch * N_KV, nc, CMP_STRIDE * HEAD_DIM)
        pe = cmp_pe[l].reshape(2, 2, CMP_STRIDE * HEAD_DIM)
        b1c = cmp_b1[l].reshape(2, 1, -1)
        k_cmp = compress(chunks(kc_raw), 0, pe, w1c_b[l], b1c, w2c_b[l], cos_c, sin_c, batch, True)
        v_cmp = compress(chunks(vc_raw), 1, pe, w1c_b[l], b1c, w2c_b[l], cos_c, sin_c, batch, False)
        kc = k_cmp.reshape(batch, N_KV, nc, HEAD_DIM)
        vct = v_cmp.reshape(batch, N_KV, nc, HEAD_DIM).transpose(0, 1, 3, 2)
        o_b = nsa_attention(proj, cos_q, sin_q, kc, vct, ks, vst, kw, vwt, overlap_t, batch, seq)

        mem_h = rmsnorm_bf16(memf, ln_mem[l], mem_len)
        kv_mem = matmul(mem_h, w_kv_b, l, mem_len, 1024, BF16, "mem_kv")
        o_c = memory_attention(proj, kv_mem.reshape(batch, mem_len, 2 * D_MODEL), batch, seq, tm)

        xf, h = merge_out(o_a, o_b, o_c, proj, wa_b, wb_b, wc_b, wo_b, l, ln_mix_post[l], xf, ln_mlp_pre[l], tm)
        xf, h = mlp(h, w1_b, w2_b, l, ln_mlp_post[l], xf, ln_mix_pre[(l + 1) % depth], tm_big, 1024)
    return xf.reshape(batch, seq, d)
```

```python
import functools

import jax
import jax.numpy as jnp
import numpy as np
from jax import lax
from jax.experimental import pallas as pl
from jax.experimental.pallas import tpu as pltpu

F32 = jnp.float32
BF16 = jnp.bfloat16

D_MODEL = 1024
LRU_BLOCKS = 8
LRU_BW = D_MODEL // LRU_BLOCKS
CONV_W = 4
LRU_C = 8.0
N_HEADS = 16
HEAD_DIM = 64
N_KV = 4
GROUP = N_HEADS // N_KV
KV_W = N_KV * HEAD_DIM
CMP_STRIDE = 16
CMP_LEN = 32
SEL_LEN = 64
SEL_SHIFT = 6
N_SELECT = 16
WINDOW = 512
Q_BLOCK = 256
MEM_HEADS = 4
MEM_HEAD_DIM = D_MODEL // MEM_HEADS
D_FF = 4 * D_MODEL
ROPE_THETA = 10000.0
EPS = 1e-6
NEG = -1e30
FORCE = 1e4
TAKEN = -3e38
LOG2_E = 1.4426950408889634
MAX_OFFSET = 45.0
BOUND_SLACK = 1.01

LANES = 128
SUBLANES = 8
SEL_TILE = 512
WIN_KEYS = WINDOW + Q_BLOCK
V_ROWS = 80
VMEM_LIMIT = 48 * 1024 * 1024

COL_XR, COL_YR, COL_Q, COL_QM, COL_GM, COL_KV, COL_GN = 0, 1024, 2048, 3072, 4096, 7168, 8704
D_INP = 9216


def _cparams(sem):
    return pltpu.CompilerParams(dimension_semantics=sem, vmem_limit_bytes=VMEM_LIMIT)


def _sigmoid(x):
    return 0.5 * jnp.tanh(0.5 * x) + 0.5


def _gelu_tanh(x):
    return 0.5 * x * (1.0 + jnp.tanh(0.7978845608028654 * (x + 0.044715 * (x * x * x))))


def _rms(x, g):
    return x * lax.rsqrt(jnp.mean(x * x, axis=-1, keepdims=True) + EPS) * g


def _rms_kernel(x_ref, g_ref, o_ref):
    o_ref[...] = _rms(x_ref[...], g_ref[...]).astype(o_ref.dtype)


def rmsnorm_bf16(x, g, tm):
    m, d = x.shape
    return pl.pallas_call(
        _rms_kernel,
        grid=(m // tm,),
        in_specs=[pl.BlockSpec((tm, d), lambda i: (i, 0)), pl.BlockSpec((1, d), lambda i: (0, 0))],
        out_specs=pl.BlockSpec((tm, d), lambda i: (i, 0)),
        out_shape=jax.ShapeDtypeStruct((m, d), BF16),
        compiler_params=_cparams(("arbitrary",)),
        name="rmsnorm",
    )(x, g.reshape(1, d))


def _mm_kernel(a_ref, w_ref, o_ref):
    o_ref[...] = jnp.dot(a_ref[...], w_ref[...], preferred_element_type=F32).astype(o_ref.dtype)


def matmul(a, w, layer, tm, tn, out_dtype, name):
    m, k = a.shape
    n = w.shape[2]
    return pl.pallas_call(
        _mm_kernel,
        grid=(n // tn, m // tm),
        in_specs=[pl.BlockSpec((tm, k), lambda j, i: (i, 0)), pl.BlockSpec((None, k, tn), lambda j, i: (layer, 0, j))],
        out_specs=pl.BlockSpec((tm, tn), lambda j, i: (i, j)),
        out_shape=jax.ShapeDtypeStruct((m, n), out_dtype),
        compiler_params=_cparams(("arbitrary", "arbitrary")),
        name=name,
    )(a, w)


def _rope_table_kernel(pos_ref, inv_ref, cos_ref, sin_ref):
    ang = pos_ref[...].astype(F32) * inv_ref[...]
    lane = lax.broadcasted_iota(jnp.int32, ang.shape, 1)
    cos_ref[...] = jnp.cos(ang)
    sin_ref[...] = jnp.where((lane & (HEAD_DIM - 1)) < HEAD_DIM // 2, -1.0, 1.0) * jnp.sin(ang)


def rope_tables(pos_flat, tm):
    n = pos_flat.shape[0]
    half = HEAD_DIM // 2
    inv = ROPE_THETA ** (-jnp.arange(half, dtype=F32) * 2.0 / HEAD_DIM)
    inv_full = jnp.tile(inv, LANES // half).reshape(1, LANES)
    return pl.pallas_call(
        _rope_table_kernel,
        grid=(n // tm,),
        in_specs=[pl.BlockSpec((tm, 1), lambda i: (i, 0)), pl.BlockSpec((1, LANES), lambda i: (0, 0))],
        out_specs=[pl.BlockSpec((tm, LANES), lambda i: (i, 0))] * 2,
        out_shape=[jax.ShapeDtypeStruct((n, LANES), F32)] * 2,
        compiler_params=_cparams(("arbitrary",)),
        name="rope_tables",
    )(pos_flat.reshape(n, 1), inv_full)


def _rope_table_t_kernel(pos_ref, inv_ref, cos_ref, sin_ref):
    ang = inv_ref[...] * pos_ref[0].astype(F32)
    cos_ref[0] = jnp.cos(ang)
    sin_ref[0] = jnp.sin(ang)


def rope_tables_t(positions, tm):
    batch, seq = positions.shape
    half = HEAD_DIM // 2
    inv = (ROPE_THETA ** (-jnp.arange(half, dtype=F32) * 2.0 / HEAD_DIM)).reshape(half, 1)
    return pl.pallas_call(
        _rope_table_t_kernel,
        grid=(batch, seq // tm),
        in_specs=[pl.BlockSpec((1, 1, tm), lambda b, i: (b, 0, i)), pl.BlockSpec((half, 1), lambda b, i: (0, 0))],
        out_specs=[pl.BlockSpec((1, half, tm), lambda b, i: (b, 0, i))] * 2,
        out_shape=[jax.ShapeDtypeStruct((batch, half, seq), F32)] * 2,
        compiler_params=_cparams(("arbitrary", "arbitrary")),
        name="rope_tables_t",
    )(positions.reshape(batch, 1, seq), inv)


def _rope128(x, cos_t, sin_t):
    lane = lax.broadcasted_iota(jnp.int32, x.shape, 1)
    first = (lane & (HEAD_DIM - 1)) < HEAD_DIM // 2
    partner = jnp.where(first, pltpu.roll(x, LANES - HEAD_DIM // 2, 1), pltpu.roll(x, HEAD_DIM // 2, 1))
    return x * cos_t + partner * sin_t


def _rglru_kernel(xr_ref, yr_ref, cw_ref, cb_ref, wr_ref, br_ref, wi_ref, bi_ref, lam_ref, o_ref, h_sc, tail_sc):
    @pl.when(pl.program_id(1) == 0)
    def _():
        h_sc[...] = jnp.zeros_like(h_sc)
        tail_sc[...] = jnp.zeros_like(tail_sc)

    xr = xr_ref[...].astype(F32)
    t_len, d = xr.shape
    tail = tail_sc[...]
    row8 = lax.broadcasted_iota(jnp.int32, (SUBLANES, d), 0)
    cw = cw_ref[...]
    xc = cb_ref[...] + xr * cw[CONV_W - 1 : CONV_W, :]
    for k in range(1, CONV_W):
        rolled = pltpu.roll(xr, k, 0)
        head = jnp.where(row8 < k, pltpu.roll(tail, k, 0), rolled[0:SUBLANES])
        shifted = jnp.concatenate([head, rolled[SUBLANES:]], axis=0)
        xc = xc + shifted * cw[CONV_W - 1 - k : CONV_W - k, :]
    tail_sc[...] = xr[t_len - SUBLANES :]

    xcb = xc.astype(BF16)
    rl, il = [], []
    for n in range(LRU_BLOCKS):
        xb = xcb[:, n * LRU_BW : (n + 1) * LRU_BW]
        rl.append(jnp.dot(xb, wr_ref[n], preferred_element_type=F32))
        il.append(jnp.dot(xb, wi_ref[n], preferred_element_type=F32))
    r = _sigmoid(jnp.concatenate(rl, axis=1) + br_ref[...])
    ig = _sigmoid(jnp.concatenate(il, axis=1) + bi_ref[...])
    softplus_neg_lam = jnp.log1p(jnp.exp(-lam_ref[...]))
    log_a = (-LRU_C * softplus_neg_lam) * r
    a = jnp.exp(log_a)
    one_minus_a2 = 1.0 - a * a
    root = jnp.where(one_minus_a2 > 0.0, one_minus_a2 * lax.rsqrt(one_minus_a2), 0.0)
    b = root * (ig * xc)

    sub = lax.broadcasted_iota(jnp.int32, (t_len, d), 0) & (SUBLANES - 1)
    step = 1
    while step < SUBLANES:
        keep = sub >= step
        a_sh = jnp.where(keep, pltpu.roll(a, step, 0), 1.0)
        b_sh = jnp.where(keep, pltpu.roll(b, step, 0), 0.0)
        b = a * b_sh + b
        a = a * a_sh
        step *= 2
    h_prev = h_sc[...]
    groups = []
    for g in range(t_len // SUBLANES):
        sl = slice(g * SUBLANES, (g + 1) * SUBLANES)
        groups.append(b[sl] + a[sl] * h_prev)
        h_prev = jnp.broadcast_to(groups[-1][SUBLANES - 1 :], (SUBLANES, d))
    h_sc[...] = h_prev
    h = jnp.concatenate(groups, axis=0)
    o_ref[...] = (h * _gelu_tanh(yr_ref[...].astype(F32))).astype(o_ref.dtype)


def rglru_branch(proj, batch, seq, conv_w, conv_b, wr, br, wi, bi, lam, t_len):
    d = D_MODEL
    nt = seq // t_len
    vec = lambda v: v.reshape(1, d)
    full2 = lambda shape: pl.BlockSpec(shape, lambda b, t: (0,) * len(shape))
    return pl.pallas_call(
        _rglru_kernel,
        grid=(batch, nt),
        in_specs=[
            pl.BlockSpec((t_len, d), lambda b, t: (b * nt + t, COL_XR // d)),
            pl.BlockSpec((t_len, d), lambda b, t: (b * nt + t, COL_YR // d)),
            full2((CONV_W, d)), full2((1, d)),
            full2((LRU_BLOCKS, LRU_BW, LRU_BW)), full2((1, d)),
            full2((LRU_BLOCKS, LRU_BW, LRU_BW)), full2((1, d)),
            full2((1, d)),
        ],
        out_specs=pl.BlockSpec((t_len, d), lambda b, t: (b * nt + t, 0)),
        out_shape=jax.ShapeDtypeStruct((batch * seq, d), BF16),
        scratch_shapes=[pltpu.VMEM((SUBLANES, d), F32), pltpu.VMEM((SUBLANES, d), F32)],
        compiler_params=_cparams(("arbitrary", "arbitrary")),
        name="rglru",
    )(proj, proj, conv_w, vec(conv_b), wr.astype(BF16), vec(br), wi.astype(BF16), vec(bi), vec(lam))


def _kv_prep_kernel(kvc_ref, kvs_ref, kvw_ref, cos_ref, sin_ref, kco_ref, vco_ref, kso_ref, vso_ref, kwo_ref, vwo_ref):
    cos_t, sin_t = cos_ref[...], sin_ref[...]
    t_len = cos_t.shape[0]
    xc = kvc_ref[...]
    for g in range(N_KV):
        kco_ref[0, g] = xc[:, g * HEAD_DIM : (g + 1) * HEAD_DIM]
        vco_ref[0, g] = xc[:, KV_W + g * HEAD_DIM : KV_W + (g + 1) * HEAD_DIM]
    pad_row = lax.broadcasted_iota(jnp.int32, (V_ROWS - HEAD_DIM, t_len), 0)
    pad_rows = jnp.where(pad_row == 0, 1.0, 0.0).astype(BF16)
    pad_lane = lax.broadcasted_iota(jnp.int32, (t_len, LANES - HEAD_DIM), 1)
    pad_lanes = jnp.where(pad_lane == 0, 1.0, 0.0)
    for src, k_dst, v_dst in ((kvs_ref, kso_ref, vso_ref), (kvw_ref, kwo_ref, vwo_ref)):
        x = src[...].astype(F32)
        roped = [_rope128(x[:, c * LANES : (c + 1) * LANES], cos_t, sin_t) for c in range(KV_W // LANES)]
        v_t = x[:, KV_W:].T
        for g in range(N_KV):
            pair = roped[g * HEAD_DIM // LANES]
            lo = g * HEAD_DIM % LANES
            k_dst[0, g] = jnp.concatenate([pair[:, lo : lo + HEAD_DIM], pad_lanes], axis=1).astype(BF16)
            v_dst[0, g, :HEAD_DIM, :] = v_t[g * HEAD_DIM : (g + 1) * HEAD_DIM].astype(BF16)
            v_dst[0, g, HEAD_DIM:, :] = pad_rows


def kv_prep(proj, cos_t, sin_t, batch, seq, tm):
    nt = seq // tm
    blk = lambda col: pl.BlockSpec((tm, 2 * KV_W), lambda b, t: (b * nt + t, col // (2 * KV_W)))
    tab = pl.BlockSpec((tm, LANES), lambda b, t: (b * nt + t, 0))
    k_spec = pl.BlockSpec((1, N_KV, tm, HEAD_DIM), lambda b, t: (b, 0, t, 0))
    v_spec = pl.BlockSpec((1, N_KV, V_ROWS, tm), lambda b, t: (b, 0, 0, t))
    k_shape = jax.ShapeDtypeStruct((batch, N_KV, seq, HEAD_DIM), BF16)
    v_shape = jax.ShapeDtypeStruct((batch, N_KV, V_ROWS, seq), BF16)
    kx_spec = pl.BlockSpec((1, N_KV, tm, LANES), lambda b, t: (b, 0, t, 0))
    kx_shape = jax.ShapeDtypeStruct((batch, N_KV, seq, LANES), BF16)
    return pl.pallas_call(
        _kv_prep_kernel,
        grid=(batch, nt),
        in_specs=[blk(COL_KV), blk(COL_KV + 2 * KV_W), blk(COL_KV + 4 * KV_W), tab, tab],
        out_specs=[k_spec, k_spec, kx_spec, v_spec, kx_spec, v_spec],
        out_shape=[k_shape, k_shape, kx_shape, v_shape, kx_shape, v_shape],
        compiler_params=_cparams(("arbitrary", "arbitrary")),
        name="kv_prep",
    )(proj, proj, proj, cos_t, sin_t)


def _compress_kernel(x_ref, pe_ref, w1_ref, b1_ref, w2_ref, cos_ref, sin_ref, o_ref, *, rotary):
    x = x_ref[0]
    half = CMP_STRIDE * HEAD_DIM
    pe = pe_ref[0]
    w1 = w1_ref[0]
    u = jnp.dot((x + pe[0:1]).astype(BF16), w1[:half], preferred_element_type=F32)
    v = jnp.dot((x + pe[1:2]).astype(BF16), w1[half:], preferred_element_type=F32)
    hid = _gelu_tanh(u + pltpu.roll(v, v.shape[0] - 1, 0) + b1_ref[0])
    out = jnp.dot(hid.astype(BF16), w2_ref[0], preferred_element_type=F32)
    if rotary:
        hh = HEAD_DIM // 2
        partner = jnp.concatenate([out[:, hh:], out[:, :hh]], axis=1)
        out = out * cos_ref[0][:, :HEAD_DIM] + partner * sin_ref[0][:, :HEAD_DIM]
    o_ref[0] = out.astype(o_ref.dtype)


def compress(x_chunks, j, pe, w1, b1, w2, cos_c, sin_c, batch, rotary):
    bg, nc, width = x_chunks.shape
    return pl.pallas_call(
        functools.partial(_compress_kernel, rotary=rotary),
        grid=(bg,),
        in_specs=[
            pl.BlockSpec((1, nc, width), lambda i: (i, 0, 0)),
            pl.BlockSpec((1, 2, width), lambda i: (j, 0, 0)),
            pl.BlockSpec((1, 2 * width, w1.shape[2]), lambda i: (j, 0, 0)),
            pl.BlockSpec((1, 1, w1.shape[2]), lambda i: (j, 0, 0)),
            pl.BlockSpec((1, w2.shape[1], HEAD_DIM), lambda i: (j, 0, 0)),
            pl.BlockSpec((1, nc, LANES), lambda i: (i // N_KV, 0, 0)),
            pl.BlockSpec((1, nc, LANES), lambda i: (i // N_KV, 0, 0)),
        ],
        out_specs=pl.BlockSpec((1, nc, HEAD_DIM), lambda i: (i, 0, 0)),
        out_shape=jax.ShapeDtypeStruct((bg, nc, HEAD_DIM), BF16),
        compiler_params=_cparams(("arbitrary",)),
        name="compress_k" if rotary else "compress_v",
    )(x_chunks, pe, w1, b1, w2, cos_c, sin_c)


def _nsa_kernel(q_ref, gate_ref, cos_ref, sin_ref, kc_ref, vct_ref, ks_ref, vst_ref, kw_ref, vwt_ref, ovt_ref,
                o_ref, sel_sc, sa_sc, sb_sc, oc_sc, imp_sc, kmax_sc):
    qb = Q_BLOCK
    s0 = pl.program_id(2) * qb
    nc = kc_ref.shape[2]
    hh = HEAD_DIM // 2
    cols = GROUP * qb

    qt = q_ref[...].astype(F32).T
    cos_t, sin_t = cos_ref[0], sin_ref[0]
    heads = []
    for r in range(GROUP):
        x1 = qt[r * HEAD_DIM : r * HEAD_DIM + hh]
        x2 = qt[r * HEAD_DIM + hh : (r + 1) * HEAD_DIM]
        heads.append(jnp.concatenate([x1 * cos_t - x2 * sin_t, x2 * cos_t + x1 * sin_t], axis=0))
    q_t = (jnp.concatenate(heads, axis=1) * (HEAD_DIM ** -0.5 * LOG2_E)).astype(BF16)

    tq = s0 + lax.broadcasted_iota(jnp.int32, (1, qb), 1)

    @pl.when(pl.program_id(2) == 0)
    def _():
        for row, k_ref in enumerate((ks_ref, kw_ref)):
            def norm_step(t, best, k_ref=k_ref):
                k_rows = k_ref[0, 0, pl.ds(pl.multiple_of(t * SEL_TILE, SEL_TILE), SEL_TILE), :].astype(F32)
                sq = jnp.sum(k_rows * k_rows, axis=1, keepdims=True)
                return jnp.maximum(best, jnp.max(sq, axis=0, keepdims=True))

            best = lax.fori_loop(0, k_ref.shape[2] // SEL_TILE, norm_step, jnp.zeros((1, 1), F32))
            kmax_sc[row : row + 1, :] = jnp.broadcast_to(jnp.sqrt(best), (1, LANES))

    q_f32 = q_t.astype(F32)
    q_norm = jnp.sqrt(jnp.sum(q_f32 * q_f32, axis=0, keepdims=True))
    offset_s = q_norm * (kmax_sc[0:1, 0:1] * BOUND_SLACK)
    offset_w = q_norm * (kmax_sc[1:2, 0:1] * BOUND_SLACK)
    bound_ok = jnp.max(jnp.maximum(offset_s, offset_w)) <= MAX_OFFSET
    ext_row = lax.broadcasted_iota(jnp.int32, (SUBLANES, cols), 0)
    zero_rows = jnp.zeros((LANES - HEAD_DIM - SUBLANES, cols), F32)

    def extended(offset):
        return jnp.concatenate([q_f32, jnp.where(ext_row == 0, -offset, 0.0), zero_rows], axis=0).astype(BF16)

    q_ext = extended(jnp.zeros_like(offset_s))

    def compressed(n_c):
        cmp_end = lax.broadcasted_iota(jnp.int32, (n_c, qb), 0) * CMP_STRIDE + (CMP_LEN - 1)
        bias_c = jnp.where(cmp_end <= tq, 0.0, NEG)
        has_key = jnp.where(tq >= CMP_LEN - 1, 1.0, 0.0)
        sc = jnp.dot(kc_ref[0, 0, :n_c, :], q_t, preferred_element_type=F32) + jnp.concatenate([bias_c] * GROUP, axis=1)
        p_c = jnp.exp2(sc - jnp.max(sc, axis=0, keepdims=True))
        norm = jnp.concatenate([has_key] * GROUP, axis=1) / jnp.maximum(jnp.sum(p_c, axis=0, keepdims=True), 1e-30)
        p_c = p_c * norm
        oc_sc[...] = jnp.dot(vct_ref[0, 0, :, :n_c], p_c.astype(BF16), preferred_element_type=F32)
        p_sum = p_c[:, :qb]
        for r in range(1, GROUP):
            p_sum = p_sum + p_c[:, r * qb : (r + 1) * qb]
        p_hi = p_sum.astype(BF16)
        p_lo = (p_sum - p_hi.astype(F32)).astype(BF16)
        ovt = ovt_ref[:, :n_c]
        imp_sc[...] = (jnp.dot(ovt, p_hi, preferred_element_type=F32)
                       + jnp.dot(ovt, p_lo, preferred_element_type=F32))

    last_cmp = (s0 + qb - CMP_LEN) // CMP_STRIDE
    for v in range(nc // LANES):
        pl.when(last_cmp // LANES == v)(functools.partial(compressed, (v + 1) * LANES))
    o_c = oc_sc[...]
    imp = imp_sc[...]

    def window(bounded):
        w0 = pl.multiple_of(jnp.maximum(s0 - WINDOW, 0), LANES)
        wpos = w0 + lax.broadcasted_iota(jnp.int32, (WIN_KEYS, qb), 0)
        bias_w = jnp.where((wpos <= tq) & (wpos > tq - WINDOW), 0.0, NEG)
        q_op = extended(offset_w) if bounded else q_ext
        s_w = jnp.dot(kw_ref[0, 0, pl.ds(w0, WIN_KEYS), :], q_op, preferred_element_type=F32)
        s_w = s_w + jnp.concatenate([bias_w] * GROUP, axis=1)
        if not bounded:
            s_w = s_w - jnp.max(s_w, axis=0, keepdims=True)
        acc_w = jnp.dot(vwt_ref[0, 0, :, pl.ds(w0, WIN_KEYS)], jnp.exp2(s_w).astype(BF16),
                        preferred_element_type=F32)
        return acc_w[:HEAD_DIM] / jnp.maximum(acc_w[HEAD_DIM : HEAD_DIM + 1], 1e-30)

    def select_blocks():
        blk = lax.broadcasted_iota(jnp.int32, (LANES, qb), 0)
        cur = tq >> SEL_SHIFT
        forced = (blk == 0) | (blk == cur)
        causal = blk * SEL_LEN <= tq
        work = jnp.where(forced, TAKEN, jnp.where(causal, imp, -FORCE))
        for r in range(N_SELECT - 1):
            hit = work == jnp.max(work, axis=0, keepdims=True)
            if r == N_SELECT - 2:
                hit = hit & (cur == 0)
            work = jnp.where(hit, TAKEN, work)
        taken = work == TAKEN
        sel_sc[...] = jnp.where(taken, 1.0, 0.0)
        n_taken = jnp.sum(jnp.where(taken & causal, 1.0, 0.0), axis=0, keepdims=True)

        @pl.when(jnp.max(n_taken) > N_SELECT)
        def _():
            work = jnp.where(forced, FORCE, jnp.where(causal, imp, -FORCE))
            blk_f = blk.astype(F32)
            for _ in range(N_SELECT):
                m = jnp.max(work, axis=0, keepdims=True)
                idx = jnp.min(jnp.where(work == m, blk_f, float(LANES)), axis=0, keepdims=True)
                work = jnp.where(blk_f == idx, TAKEN, work)
            sel_sc[...] = jnp.where(work == TAKEN, 1.0, 0.0)

    blocks_per_tile = SEL_TILE // SEL_LEN
    row_minus_lane = (lax.broadcasted_iota(jnp.int32, (SEL_TILE, qb), 0)
                      - lax.broadcasted_iota(jnp.int32, (SEL_TILE, qb), 1))
    n_tiles = (s0 + qb - 1) // SEL_TILE + 1

    def scores(kb, q_op):
        k0 = pl.multiple_of(kb * SEL_TILE, SEL_TILE)
        picked = sel_sc[pl.ds(pl.multiple_of(kb * blocks_per_tile, blocks_per_tile), blocks_per_tile), :]
        picked = jnp.where(picked > 0.5, 0.0, NEG)
        picked = jnp.concatenate(
            [jnp.broadcast_to(picked[j : j + 1, :], (SEL_LEN, qb)) for j in range(blocks_per_tile)], axis=0)
        bias = jnp.where(row_minus_lane <= s0 - k0, picked, NEG)
        s_ = jnp.dot(ks_ref[0, 0, pl.ds(k0, SEL_TILE), :], q_op, preferred_element_type=F32)
        return s_ + jnp.concatenate([bias] * GROUP, axis=1)

    def values(kb, p):
        k0 = pl.multiple_of(kb * SEL_TILE, SEL_TILE)
        return jnp.dot(vst_ref[0, 0, :, pl.ds(k0, SEL_TILE)], p, preferred_element_type=F32)

    def bounded_path():
        o_win = window(True)
        select_blocks()
        q_off = extended(offset_s)

        def tile(kb, acc):
            return acc + values(kb, jnp.exp2(scores(kb, q_off)).astype(BF16))

        def run(first, count, acc):
            for t in range(count):
                acc = tile(first + t, acc)
            return acc

        octs = n_tiles // 8
        acc = lax.fori_loop(0, octs, lambda j, acc: run(8 * j, 8, acc), jnp.zeros((V_ROWS, cols), F32))
        done = 8 * octs
        for count in (4, 2, 1):
            take = ((n_tiles - done) // count) % 2 == 1
            acc = lax.cond(take, functools.partial(run, done, count), lambda acc: acc, acc)
            done = done + jnp.where(take, count, 0)
        return o_win, acc

    def exact_path():
        o_win = window(False)
        select_blocks()

        def absorb(kb, s_, m_run, acc):
            m_new = jnp.maximum(m_run, jnp.max(s_, axis=0, keepdims=True))
            p = jnp.exp2(s_ - m_new).astype(BF16)
            return m_new, jnp.exp2(m_run - m_new) * acc + values(kb, p)

        sa_sc[...] = scores(0, q_ext)

        def sel_step(j, carry):
            m_run, acc = carry
            sb_sc[...] = scores(2 * j + 1, q_ext)
            m_run, acc = absorb(2 * j, sa_sc[...], m_run, acc)
            sa_sc[...] = scores(2 * j + 2, q_ext)
            return absorb(2 * j + 1, sb_sc[...], m_run, acc)

        init = (jnp.full((1, cols), NEG, F32), jnp.zeros((V_ROWS, cols), F32))
        full_trips = (n_tiles - 1) // 2
        m_run, acc = lax.fori_loop(0, full_trips, sel_step, init)
        m_run, acc = absorb(2 * full_trips, sa_sc[...], m_run, acc)

        def last_tile(carry):
            return absorb(n_tiles - 1, scores(n_tiles - 1, q_ext), *carry)

        return o_win, lax.cond(n_tiles - 2 * full_trips == 2, last_tile, lambda carry: carry, (m_run, acc))[1]

    o_w, acc_s = lax.cond(bound_ok, bounded_path, exact_path)

    o_s = acc_s[:HEAD_DIM] / jnp.maximum(acc_s[HEAD_DIM : HEAD_DIM + 1], 1e-30)

    gates_t = _sigmoid(gate_ref[...].astype(F32)).T
    outs = []
    for r in range(GROUP):
        sl = slice(r * qb, (r + 1) * qb)
        outs.append(gates_t[3 * r : 3 * r + 1] * o_c[:, sl] + gates_t[3 * r + 1 : 3 * r + 2] * o_s[:, sl]
                    + gates_t[3 * r + 2 : 3 * r + 3] * o_w[:, sl])
    o_ref[...] = jnp.concatenate(outs, axis=0).T.astype(o_ref.dtype)


def nsa_attention(proj, cos_q, sin_q, kc, vct, ks, vst, kw, vwt, overlap_t, batch, seq):
    nq = seq // Q_BLOCK
    nc = kc.shape[2]
    gw = GROUP * HEAD_DIM
    per_bg = lambda shape: pl.BlockSpec((1, 1) + shape, lambda b, g, i: (b, g, 0, 0))
    return pl.pallas_call(
        _nsa_kernel,
        grid=(batch, N_KV, nq),
        in_specs=[
            pl.BlockSpec((Q_BLOCK, gw), lambda b, g, i: (b * nq + i, COL_Q // gw + g)),
            pl.BlockSpec((Q_BLOCK, LANES), lambda b, g, i: (b * nq + i, COL_GN // LANES + g)),
            pl.BlockSpec((1, HEAD_DIM // 2, Q_BLOCK), lambda b, g, i: (b, 0, i)),
            pl.BlockSpec((1, HEAD_DIM // 2, Q_BLOCK), lambda b, g, i: (b, 0, i)),
            per_bg((nc, HEAD_DIM)), per_bg((HEAD_DIM, nc)),
            per_bg((seq, LANES)), per_bg((V_ROWS, seq)),
            per_bg((seq, LANES)), per_bg((V_ROWS, seq)),
            pl.BlockSpec((LANES, nc), lambda b, g, i: (0, 0)),
        ],
        out_specs=pl.BlockSpec((Q_BLOCK, gw), lambda b, g, i: (b * nq + i, g)),
        out_shape=jax.ShapeDtypeStruct((batch * seq, N_HEADS * HEAD_DIM), BF16),
        scratch_shapes=[pltpu.VMEM((LANES, Q_BLOCK), F32), pltpu.VMEM((SEL_TILE, GROUP * Q_BLOCK), F32),
                        pltpu.VMEM((SEL_TILE, GROUP * Q_BLOCK), F32), pltpu.VMEM((HEAD_DIM, GROUP * Q_BLOCK), F32),
                        pltpu.VMEM((LANES, Q_BLOCK), F32), pltpu.VMEM((SUBLANES, LANES), F32)],
        compiler_params=_cparams(("arbitrary", "arbitrary", "arbitrary")),
        name="nsa_attention",
    )(proj, proj, cos_q, sin_q, kc, vct, ks, vst, kw, vwt, overlap_t)


def _mem_kernel(q_ref, kv_ref, o_ref):
    q = q_ref[...]
    kv = kv_ref[0]
    mem_w = MEM_HEADS * MEM_HEAD_DIM
    outs = []
    for h in range(MEM_HEADS):
        sl = slice(h * MEM_HEAD_DIM, (h + 1) * MEM_HEAD_DIM)
        qh = (q[:, sl] * (MEM_HEAD_DIM ** -0.5)).astype(BF16)
        s = lax.dot_general(qh, kv[:, sl], (((1,), (1,)), ((), ())), preferred_element_type=F32)
        p = jnp.exp(s - jnp.max(s, axis=-1, keepdims=True))
        p = p / jnp.sum(p, axis=-1, keepdims=True)
        outs.append(jnp.dot(p.astype(BF16), kv[:, mem_w + h * MEM_HEAD_DIM : mem_w + (h + 1) * MEM_HEAD_DIM],
                            preferred_element_type=F32))
    o_ref[...] = jnp.concatenate(outs, axis=1).astype(o_ref.dtype)


def memory_attention(proj, kv_mem, batch, seq, tm):
    nt = seq // tm
    mem_len = kv_mem.shape[1]
    return pl.pallas_call(
        _mem_kernel,
        grid=(batch, nt),
        in_specs=[
            pl.BlockSpec((tm, D_MODEL), lambda b, t: (b * nt + t, COL_QM // D_MODEL)),
            pl.BlockSpec((1, mem_len, 2 * D_MODEL), lambda b, t: (b, 0, 0)),
        ],
        out_specs=pl.BlockSpec((tm, D_MODEL), lambda b, t: (b * nt + t, 0)),
        out_shape=jax.ShapeDtypeStruct((batch * seq, D_MODEL), BF16),
        compiler_params=_cparams(("arbitrary", "arbitrary")),
        name="memory_attention",
    )(proj, kv_mem)


def _merge_kernel(oa_ref, ob_ref, oc_ref, ga_ref, gb_ref, gc_ref, wa_ref, wb_ref, wc_ref, wo_ref,
                  gpost_ref, x_ref, gnext_ref, xo_ref, ho_ref):
    gate = lambda ref: _sigmoid(ref[...].astype(F32))
    merged = gate(ga_ref) * jnp.dot(oa_ref[...], wa_ref[...], preferred_element_type=F32)
    merged += gate(gb_ref) * jnp.dot(ob_ref[...], wb_ref[...], preferred_element_type=F32)
    merged += gate(gc_ref) * jnp.dot(oc_ref[...], wc_ref[...], preferred_element_type=F32)
    y = jnp.dot(merged.astype(BF16), wo_ref[...], preferred_element_type=F32)
    x_new = x_ref[...] + _rms(y, gpost_ref[...])
    xo_ref[...] = x_new
    ho_ref[...] = _rms(x_new, gnext_ref[...]).astype(ho_ref.dtype)


def merge_out(oa, ob, oc, proj, wa, wb, wc, wo, layer, g_post, x, g_next, tm):
    n, d = x.shape
    row = lambda col: pl.BlockSpec((tm, d), lambda i: (i, col))
    full = lambda shape: pl.BlockSpec(shape, lambda i: (0, 0))
    w_spec = pl.BlockSpec((None, d, d), lambda i: (layer, 0, 0))
    gm = COL_GM // d
    return pl.pallas_call(
        _merge_kernel,
        grid=(n // tm,),
        in_specs=[row(0), row(0), row(0), row(gm), row(gm + 1), row(gm + 2),
                  w_spec, w_spec, w_spec, w_spec, full((1, d)), row(0), full((1, d))],
        out_specs=[row(0), row(0)],
        out_shape=[jax.ShapeDtypeStruct((n, d), F32), jax.ShapeDtypeStruct((n, d), BF16)],
        compiler_params=_cparams(("arbitrary",)),
        name="merge_out",
    )(oa, ob, oc, proj, proj, proj, wa, wb, wc, wo, g_post.reshape(1, d), x, g_next.reshape(1, d))


def _mlp_kernel(h_ref, w1_ref, w2_ref, gpost_ref, x_ref, gnext_ref, xo_ref, ho_ref, acc_ref):
    k = pl.program_id(1)

    @pl.when(k == 0)
    def _():
        acc_ref[...] = jnp.zeros_like(acc_ref)

    u = jnp.maximum(jnp.dot(h_ref[...], w1_ref[...], preferred_element_type=F32), 0.0)
    acc_ref[...] += jnp.dot((u * u).astype(BF16), w2_ref[...], preferred_element_type=F32)

    @pl.when(k == pl.num_programs(1) - 1)
    def _():
        x_new = x_ref[...] + _rms(acc_ref[...], gpost_ref[...])
        xo_ref[...] = x_new
        ho_ref[...] = _rms(x_new, gnext_ref[...]).astype(ho_ref.dtype)


def mlp(h, w1, w2, layer, g_post, x, g_next, tm, tf):
    n, d = x.shape
    ff = w1.shape[2]
    return pl.pallas_call(
        _mlp_kernel,
        grid=(n // tm, ff // tf),
        in_specs=[
            pl.BlockSpec((tm, d), lambda i, k: (i, 0)),
            pl.BlockSpec((None, d, tf), lambda i, k: (layer, 0, k)),
            pl.BlockSpec((None, tf, d), lambda i, k: (layer, k, 0)),
            pl.BlockSpec((1, d), lambda i, k: (0, 0)),
            pl.BlockSpec((tm, d), lambda i, k: (i, 0)),
            pl.BlockSpec((1, d), lambda i, k: (0, 0)),
        ],
        out_specs=[pl.BlockSpec((tm, d), lambda i, k: (i, 0))] * 2,
        out_shape=[jax.ShapeDtypeStruct((n, d), F32), jax.ShapeDtypeStruct((n, d), BF16)],
        scratch_shapes=[pltpu.VMEM((tm, d), F32)],
        compiler_params=_cparams(("arbitrary", "arbitrary")),
        name="mlp",
    )(h, w1, w2, g_post.reshape(1, d), x, g_next.reshape(1, d))


def _pack_w_in(w):
    o_kv, o_gn, o_qm, o_gm = 3072, 4608, 4656, 5680
    per_group = GROUP * 3
    lead = w.shape[:-1]
    gn = w[..., o_gn:o_qm].reshape(lead + (N_KV, per_group))
    gn = jnp.pad(gn, ((0, 0),) * (len(lead) + 1) + ((0, LANES - per_group),)).reshape(lead + (N_KV * LANES,))
    packed = jnp.concatenate([w[..., :o_kv], w[..., o_qm:o_gm], w[..., o_gm:], w[..., o_kv:o_gn], gn], axis=-1)
    assert packed.shape[-1] == D_INP
    return packed.astype(BF16)


def _overlap_matrix_t(nc):
    c0 = np.arange(nc)[:, None] * CMP_STRIDE
    s0 = np.arange(LANES)[None, :] * SEL_LEN
    ov = np.clip(np.minimum(c0 + CMP_LEN, s0 + SEL_LEN) - np.maximum(c0, s0), 0, None).astype(np.float32) / CMP_LEN
    return jnp.asarray(ov.T, dtype=BF16)


def kernel(x, mem, positions, ln_mix_pre, w_in, conv_w, conv_b, lru_wr, lru_br, lru_wi, lru_bi, lru_lambda, cmp_pe, cmp_w1, cmp_b1, cmp_w2, ln_mem, w_mem_kv, w_br_rnn, w_br_nsa, w_br_mem, w_out, ln_mix_post, ln_mlp_pre, mlp_w1, mlp_w2, ln_mlp_post):
    batch, seq, d = x.shape
    depth = w_in.shape[0]
    mem_len = mem.shape[1]
    n = batch * seq
    nc = seq // CMP_STRIDE
    assert d == D_MODEL and seq % (2 * SEL_TILE) == 0 and seq // SEL_LEN <= LANES and seq >= WIN_KEYS

    tm = min(512, seq)
    tm_big = min(1024, seq)
    xf = x.reshape(n, d)
    memf = mem.reshape(batch * mem_len, d)
    cos_t, sin_t = rope_tables(positions.reshape(n), tm)
    pos_c = jnp.pad(positions[:, CMP_LEN - 1 :: CMP_STRIDE], ((0, 0), (0, 1)))
    cos_c, sin_c = rope_tables(pos_c.reshape(batch * nc), nc)
    cos_c, sin_c = cos_c.reshape(batch, nc, LANES), sin_c.reshape(batch, nc, LANES)
    cos_q, sin_q = rope_tables_t(positions, tm)
    overlap_t = _overlap_matrix_t(nc)

    w_in_b, w_kv_b = _pack_w_in(w_in.astype(BF16)), w_mem_kv.astype(BF16)
    wr_b, wi_b = lru_wr.astype(BF16), lru_wi.astype(BF16)
    w1c_b, w2c_b = cmp_w1.astype(BF16), cmp_w2.astype(BF16)
    wa_b, wb_b, wc_b, wo_b = (w.astype(BF16) for w in (w_br_rnn, w_br_nsa, w_br_mem, w_out))
    w1_b, w2_b = mlp_w1.astype(BF16), mlp_w2.astype(BF16)

    h = rmsnorm_bf16(xf, ln_mix_pre[0], tm)
    for l in range(depth):
        proj = matmul(h, w_in_b, l, min(2048, seq), 1024, BF16, "in_proj")

        o_a = rglru_branch(proj, batch, seq, conv_w[l], conv_b[l], wr_b[l], lru_br[l], wi_b[l], lru_bi[l],
                           lru_lambda[l], min(256, seq))

        kc_raw, vc_raw, ks, vst, kw, vwt = kv_prep(proj, cos_t, sin_t, batch, seq, tm)
        chunks = lambda a: a.reshape(batch * N_KV, nc, CMP_STRIDE * HEAD_DIM)
        pe = cmp_pe[l].reshape(2, 2, CMP_STRIDE * HEAD_DIM)
        b1c = cmp_b1[l].reshape(2, 1, -1)
        k_cmp = compress(chunks(kc_raw), 0, pe, w1c_b[l], b1c, w2c_b[l], cos_c, sin_c, batch, True)
        v_cmp = compress(chunks(vc_raw), 1, pe, w1c_b[l], b1c, w2c_b[l], cos_c, sin_c, batch, False)
        kc = k_cmp.reshape(batch, N_KV, nc, HEAD_DIM)
        vct = v_cmp.reshape(batch, N_KV, nc, HEAD_DIM).transpose(0, 1, 3, 2)
        o_b = nsa_attention(proj, cos_q, sin_q, kc, vct, ks, vst, kw, vwt, overlap_t, batch, seq)

        mem_h = rmsnorm_bf16(memf, ln_mem[l], mem_len)
        kv_mem = matmul(mem_h, w_kv_b, l, mem_len, 1024, BF16, "mem_kv")
        o_c = memory_attention(proj, kv_mem.reshape(batch, mem_len, 2 * D_MODEL), batch, seq, tm)

        xf, h = merge_out(o_a, o_b, o_c, proj, wa_b, wb_b, wc_b, wo_b, l, ln_mix_post[l], xf, ln_mlp_pre[l], tm)
        xf, h = mlp(h, w1_b, w2_b, l, ln_mlp_post[l], xf, ln_mix_pre[(l + 1) % depth], tm_big, 1024)
    return xf.reshape(batch, seq, d)
```

```python
import functools

import jax
import jax.numpy as jnp
import numpy as np
from jax import lax
from jax.experimental import pallas as pl
from jax.experimental.pallas import tpu as pltpu

F32 = jnp.float32
BF16 = jnp.bfloat16

D_MODEL = 1024
LRU_BLOCKS = 8
LRU_BW = D_MODEL // LRU_BLOCKS
CONV_W = 4
LRU_C = 8.0
N_HEADS = 16
HEAD_DIM = 64
N_KV = 4
GROUP = N_HEADS // N_KV
KV_W = N_KV * HEAD_DIM
CMP_STRIDE = 16
CMP_LEN = 32
SEL_LEN = 64
SEL_SHIFT = 6
N_SELECT = 16
WINDOW = 512
Q_BLOCK = 256
MEM_HEADS = 4
MEM_HEAD_DIM = D_MODEL // MEM_HEADS
D_FF = 4 * D_MODEL
ROPE_THETA = 10000.0
EPS = 1e-6
NEG = -1e30
FORCE = 1e4
TAKEN = -3e38
LOG2_E = 1.4426950408889634
MAX_OFFSET = 45.0
BOUND_SLACK = 1.01

LANES = 128
SUBLANES = 8
SEL_TILE = 512
WIN_KEYS = WINDOW + Q_BLOCK
V_ROWS = 80
VMEM_LIMIT = 48 * 1024 * 1024

COL_XR, COL_YR, COL_Q, COL_QM, COL_GM, COL_KV, COL_GN = 0, 1024, 2048, 3072, 4096, 7168, 8704
D_INP = 9216


def _cparams(sem):
    return pltpu.CompilerParams(dimension_semantics=sem, vmem_limit_bytes=VMEM_LIMIT)


def _sigmoid(x):
    return 0.5 * jnp.tanh(0.5 * x) + 0.5


def _gelu_tanh(x):
    return 0.5 * x * (1.0 + jnp.tanh(0.7978845608028654 * (x + 0.044715 * (x * x * x))))


def _rms(x, g):
    return x * lax.rsqrt(jnp.mean(x * x, axis=-1, keepdims=True) + EPS) * g


def _rms_kernel(x_ref, g_ref, o_ref):
    o_ref[...] = _rms(x_ref[...], g_ref[...]).astype(o_ref.dtype)


def rmsnorm_bf16(x, g, tm):
    m, d = x.shape
    return pl.pallas_call(
        _rms_kernel,
        grid=(m // tm,),
        in_specs=[pl.BlockSpec((tm, d), lambda i: (i, 0)), pl.BlockSpec((1, d), lambda i: (0, 0))],
        out_specs=pl.BlockSpec((tm, d), lambda i: (i, 0)),
        out_shape=jax.ShapeDtypeStruct((m, d), BF16),
        compiler_params=_cparams(("arbitrary",)),
        name="rmsnorm",
    )(x, g.reshape(1, d))


def _mm_kernel(a_ref, w_ref, o_ref):
    o_ref[...] = jnp.dot(a_ref[...], w_ref[...], preferred_element_type=F32).astype(o_ref.dtype)


def matmul(a, w, layer, tm, tn, out_dtype, name):
    m, k = a.shape
    n = w.shape[2]
    return pl.pallas_call(
        _mm_kernel,
        grid=(n // tn, m // tm),
        in_specs=[pl.BlockSpec((tm, k), lambda j, i: (i, 0)), pl.BlockSpec((None, k, tn), lambda j, i: (layer, 0, j))],
        out_specs=pl.BlockSpec((tm, tn), lambda j, i: (i, j)),
        out_shape=jax.ShapeDtypeStruct((m, n), out_dtype),
        compiler_params=_cparams(("arbitrary", "arbitrary")),
        name=name,
    )(a, w)


def _rope_table_kernel(pos_ref, inv_ref, cos_ref, sin_ref):
    ang = pos_ref[...].astype(F32) * inv_ref[...]
    lane = lax.broadcasted_iota(jnp.int32, ang.shape, 1)
    cos_ref[...] = jnp.cos(ang)
    sin_ref[...] = jnp.where((lane & (HEAD_DIM - 1)) < HEAD_DIM // 2, -1.0, 1.0) * jnp.sin(ang)


def rope_tables(pos_flat, tm):
    n = pos_flat.shape[0]
    half = HEAD_DIM // 2
    inv = ROPE_THETA ** (-jnp.arange(half, dtype=F32) * 2.0 / HEAD_DIM)
    inv_full = jnp.tile(inv, LANES // half).reshape(1, LANES)
    return pl.pallas_call(
        _rope_table_kernel,
        grid=(n // tm,),
        in_specs=[pl.BlockSpec((tm, 1), lambda i: (i, 0)), pl.BlockSpec((1, LANES), lambda i: (0, 0))],
        out_specs=[pl.BlockSpec((tm, LANES), lambda i: (i, 0))] * 2,
        out_shape=[jax.ShapeDtypeStruct((n, LANES), F32)] * 2,
        compiler_params=_cparams(("arbitrary",)),
        name="rope_tables",
    )(pos_flat.reshape(n, 1), inv_full)


def _rope_table_t_kernel(pos_ref, inv_ref, cos_ref, sin_ref):
    ang = inv_ref[...] * pos_ref[0].astype(F32)
    cos_ref[0] = jnp.cos(ang)
    sin_ref[0] = jnp.sin(ang)


def rope_tables_t(positions, tm):
    batch, seq = positions.shape
    half = HEAD_DIM // 2
    inv = (ROPE_THETA ** (-jnp.arange(half, dtype=F32) * 2.0 / HEAD_DIM)).reshape(half, 1)
    return pl.pallas_call(
        _rope_table_t_kernel,
        grid=(batch, seq // tm),
        in_specs=[pl.BlockSpec((1, 1, tm), lambda b, i: (b, 0, i)), pl.BlockSpec((half, 1), lambda b, i: (0, 0))],
        out_specs=[pl.BlockSpec((1, half, tm), lambda b, i: (b, 0, i))] * 2,
        out_shape=[jax.ShapeDtypeStruct((batch, half, seq), F32)] * 2,
        compiler_params=_cparams(("arbitrary", "arbitrary")),
        name="rope_tables_t",
    )(positions.reshape(batch, 1, seq), inv)


def _rope128(x, cos_t, sin_t):
    lane = lax.broadcasted_iota(jnp.int32, x.shape, 1)
    first = (lane & (HEAD_DIM - 1)) < HEAD_DIM // 2
    partner = jnp.where(first, pltpu.roll(x, LANES - HEAD_DIM // 2, 1), pltpu.roll(x, HEAD_DIM // 2, 1))
    return x * cos_t + partner * sin_t


def _rglru_kernel(xr_ref, yr_ref, cw_ref, cb_ref, wr_ref, br_ref, wi_ref, bi_ref, lam_ref, o_ref, h_sc, tail_sc):
    @pl.when(pl.program_id(1) == 0)
    def _():
        h_sc[...] = jnp.zeros_like(h_sc)
        tail_sc[...] = jnp.zeros_like(tail_sc)

    xr = xr_ref[...].astype(F32)
    t_len, d = xr.shape
    tail = tail_sc[...]
    row8 = lax.broadcasted_iota(jnp.int32, (SUBLANES, d), 0)
    cw = cw_ref[...]
    xc = cb_ref[...] + xr * cw[CONV_W - 1 : CONV_W, :]
    for k in range(1, CONV_W):
        rolled = pltpu.roll(xr, k, 0)
        head = jnp.where(row8 < k, pltpu.roll(tail, k, 0), rolled[0:SUBLANES])
        shifted = jnp.concatenate([head, rolled[SUBLANES:]], axis=0)
        xc = xc + shifted * cw[CONV_W - 1 - k : CONV_W - k, :]
    tail_sc[...] = xr[t_len - SUBLANES :]

    xcb = xc.astype(BF16)
    rl, il = [], []
    for n in range(LRU_BLOCKS):
        xb = xcb[:, n * LRU_BW : (n + 1) * LRU_BW]
        rl.append(jnp.dot(xb, wr_ref[n], preferred_element_type=F32))
        il.append(jnp.dot(xb, wi_ref[n], preferred_element_type=F32))
    r = _sigmoid(jnp.concatenate(rl, axis=1) + br_ref[...])
    ig = _sigmoid(jnp.concatenate(il, axis=1) + bi_ref[...])
    softplus_neg_lam = jnp.log1p(jnp.exp(-lam_ref[...]))
    log_a = (-LRU_C * softplus_neg_lam) * r
    a = jnp.exp(log_a)
    one_minus_a2 = 1.0 - a * a
    root = jnp.where(one_minus_a2 > 0.0, one_minus_a2 * lax.rsqrt(one_minus_a2), 0.0)
    b = root * (ig * xc)

    sub = lax.broadcasted_iota(jnp.int32, (t_len, d), 0) & (SUBLANES - 1)
    step = 1
    while step < SUBLANES:
        keep = sub >= step
        a_sh = jnp.where(keep, pltpu.roll(a, step, 0), 1.0)
        b_sh = jnp.where(keep, pltpu.roll(b, step, 0), 0.0)
        b = a * b_sh + b
        a = a * a_sh
        step *= 2
    h_prev = h_sc[...]
    groups = []
    for g in range(t_len // SUBLANES):
        sl = slice(g * SUBLANES, (g + 1) * SUBLANES)
        groups.append(b[sl] + a[sl] * h_prev)
        h_prev = jnp.broadcast_to(groups[-1][SUBLANES - 1 :], (SUBLANES, d))
    h_sc[...] = h_prev
    h = jnp.concatenate(groups, axis=0)
    o_ref[...] = (h * _gelu_tanh(yr_ref[...].astype(F32))).astype(o_ref.dtype)


def rglru_branch(proj, batch, seq, conv_w, conv_b, wr, br, wi, bi, lam, t_len):
    d = D_MODEL
    nt = seq // t_len
    vec = lambda v: v.reshape(1, d)
    full2 = lambda shape: pl.BlockSpec(shape, lambda b, t: (0,) * len(shape))
    return pl.pallas_call(
        _rglru_kernel,
        grid=(batch, nt),
        in_specs=[
            pl.BlockSpec((t_len, d), lambda b, t: (b * nt + t, COL_XR // d)),
            pl.BlockSpec((t_len, d), lambda b, t: (b * nt + t, COL_YR // d)),
            full2((CONV_W, d)), full2((1, d)),
            full2((LRU_BLOCKS, LRU_BW, LRU_BW)), full2((1, d)),
            full2((LRU_BLOCKS, LRU_BW, LRU_BW)), full2((1, d)),
            full2((1, d)),
        ],
        out_specs=pl.BlockSpec((t_len, d), lambda b, t: (b * nt + t, 0)),
        out_shape=jax.ShapeDtypeStruct((batch * seq, d), BF16),
        scratch_shapes=[pltpu.VMEM((SUBLANES, d), F32), pltpu.VMEM((SUBLANES, d), F32)],
        compiler_params=_cparams(("arbitrary", "arbitrary")),
        name="rglru",
    )(proj, proj, conv_w, vec(conv_b), wr.astype(BF16), vec(br), wi.astype(BF16), vec(bi), vec(lam))


def _kv_prep_kernel(kvc_ref, kvs_ref, kvw_ref, cos_ref, sin_ref, kco_ref, vco_ref, kso_ref, vso_ref, kwo_ref, vwo_ref):
    cos_t, sin_t = cos_ref[...], sin_ref[...]
    t_len = cos_t.shape[0]
    xc = kvc_ref[...]
    for g in range(N_KV):
        kco_ref[0, g] = xc[:, g * HEAD_DIM : (g + 1) * HEAD_DIM]
        vco_ref[0, g] = xc[:, KV_W + g * HEAD_DIM : KV_W + (g + 1) * HEAD_DIM]
    pad_row = lax.broadcasted_iota(jnp.int32, (V_ROWS - HEAD_DIM, t_len), 0)
    pad_rows = jnp.where(pad_row == 0, 1.0, 0.0).astype(BF16)
    pad_lane = lax.broadcasted_iota(jnp.int32, (t_len, LANES - HEAD_DIM), 1)
    pad_lanes = jnp.where(pad_lane == 0, 1.0, 0.0)
    for src, k_dst, v_dst in ((kvs_ref, kso_ref, vso_ref), (kvw_ref, kwo_ref, vwo_ref)):
        x = src[...].astype(F32)
        roped = [_rope128(x[:, c * LANES : (c + 1) * LANES], cos_t, sin_t) for c in range(KV_W // LANES)]
        v_t = x[:, KV_W:].T
        for g in range(N_KV):
            pair = roped[g * HEAD_DIM // LANES]
            lo = g * HEAD_DIM % LANES
            k_dst[0, g] = jnp.concatenate([pair[:, lo : lo + HEAD_DIM], pad_lanes], axis=1).astype(BF16)
            v_dst[0, g, :HEAD_DIM, :] = v_t[g * HEAD_DIM : (g + 1) * HEAD_DIM].astype(BF16)
            v_dst[0, g, HEAD_DIM:, :] = pad_rows


def kv_prep(proj, cos_t, sin_t, batch, seq, tm):
    nt = seq // tm
    blk = lambda col: pl.BlockSpec((tm, 2 * KV_W), lambda b, t: (b * nt + t, col // (2 * KV_W)))
    tab = pl.BlockSpec((tm, LANES), lambda b, t: (b * nt + t, 0))
    k_spec = pl.BlockSpec((1, N_KV, tm, HEAD_DIM), lambda b, t: (b, 0, t, 0))
    v_spec = pl.BlockSpec((1, N_KV, V_ROWS, tm), lambda b, t: (b, 0, 0, t))
    k_shape = jax.ShapeDtypeStruct((batch, N_KV, seq, HEAD_DIM), BF16)
    v_shape = jax.ShapeDtypeStruct((batch, N_KV, V_ROWS, seq), BF16)
    kx_spec = pl.BlockSpec((1, N_KV, tm, LANES), lambda b, t: (b, 0, t, 0))
    kx_shape = jax.ShapeDtypeStruct((batch, N_KV, seq, LANES), BF16)
    return pl.pallas_call(
        _kv_prep_kernel,
        grid=(batch, nt),
        in_specs=[blk(COL_KV), blk(COL_KV + 2 * KV_W), blk(COL_KV + 4 * KV_W), tab, tab],
        out_specs=[k_spec, k_spec, kx_spec, v_spec, kx_spec, v_spec],
        out_shape=[k_shape, k_shape, kx_shape, v_shape, kx_shape, v_shape],
        compiler_params=_cparams(("arbitrary", "arbitrary")),
        name="kv_prep",
    )(proj, proj, proj, cos_t, sin_t)


def _compress_kernel(x_ref, pe_ref, w1_ref, b1_ref, w2_ref, cos_ref, sin_ref, o_ref, *, rotary):
    x = x_ref[0]
    half = CMP_STRIDE * HEAD_DIM
    pe = pe_ref[0]
    w1 = w1_ref[0]
    u = jnp.dot((x + pe[0:1]).astype(BF16), w1[:half], preferred_element_type=F32)
    v = jnp.dot((x + pe[1:2]).astype(BF16), w1[half:], preferred_element_type=F32)
    hid = _gelu_tanh(u + pltpu.roll(v, v.shape[0] - 1, 0) + b1_ref[0])
    out = jnp.dot(hid.astype(BF16), w2_ref[0], preferred_element_type=F32)
    if rotary:
        hh = HEAD_DIM // 2
        partner = jnp.concatenate([out[:, hh:], out[:, :hh]], axis=1)
        out = out * cos_ref[0][:, :HEAD_DIM] + partner * sin_ref[0][:, :HEAD_DIM]
    o_ref[0] = out.astype(o_ref.dtype)


def compress(x_chunks, j, pe, w1, b1, w2, cos_c, sin_c, batch, rotary):
    bg, nc, width = x_chunks.shape
    return pl.pallas_call(
        functools.partial(_compress_kernel, rotary=rotary),
        grid=(bg,),
        in_specs=[
            pl.BlockSpec((1, nc, width), lambda i: (i, 0, 0)),
            pl.BlockSpec((1, 2, width), lambda i: (j, 0, 0)),
            pl.BlockSpec((1, 2 * width, w1.shape[2]), lambda i: (j, 0, 0)),
            pl.BlockSpec((1, 1, w1.shape[2]), lambda i: (j, 0, 0)),
            pl.BlockSpec((1, w2.shape[1], HEAD_DIM), lambda i: (j, 0, 0)),
            pl.BlockSpec((1, nc, LANES), lambda i: (i // N_KV, 0, 0)),
            pl.BlockSpec((1, nc, LANES), lambda i: (i // N_KV, 0, 0)),
        ],
        out_specs=pl.BlockSpec((1, nc, HEAD_DIM), lambda i: (i, 0, 0)),
        out_shape=jax.ShapeDtypeStruct((bg, nc, HEAD_DIM), BF16),
        compiler_params=_cparams(("arbitrary",)),
        name="compress_k" if rotary else "compress_v",
    )(x_chunks, pe, w1, b1, w2, cos_c, sin_c)


def _nsa_kernel(q_ref, gate_ref, cos_ref, sin_ref, kc_ref, vct_ref, ks_ref, vst_ref, kw_ref, vwt_ref, ovt_ref,
                o_ref, sel_sc, sa_sc, sb_sc, oc_sc, imp_sc, kmax_sc):
    qb = Q_BLOCK
    s0 = pl.program_id(2) * qb
    nc = kc_ref.shape[2]
    hh = HEAD_DIM // 2
    cols = GROUP * qb

    qt = q_ref[...].astype(F32).T
    cos_t, sin_t = cos_ref[0], sin_ref[0]
    heads = []
    for r in range(GROUP):
        x1 = qt[r * HEAD_DIM : r * HEAD_DIM + hh]
        x2 = qt[r * HEAD_DIM + hh : (r + 1) * HEAD_DIM]
        heads.append(jnp.concatenate([x1 * cos_t - x2 * sin_t, x2 * cos_t + x1 * sin_t], axis=0))
    q_t = (jnp.concatenate(heads, axis=1) * (HEAD_DIM ** -0.5 * LOG2_E)).astype(BF16)

    tq = s0 + lax.broadcasted_iota(jnp.int32, (1, qb), 1)

    @pl.when(pl.program_id(2) == 0)
    def _():
        for row, k_ref in enumerate((ks_ref, kw_ref)):
            def norm_step(t, best, k_ref=k_ref):
                k_rows = k_ref[0, 0, pl.ds(pl.multiple_of(t * SEL_TILE, SEL_TILE), SEL_TILE), :].astype(F32)
                sq = jnp.sum(k_rows * k_rows, axis=1, keepdims=True)
                return jnp.maximum(best, jnp.max(sq, axis=0, keepdims=True))

            best = lax.fori_loop(0, k_ref.shape[2] // SEL_TILE, norm_step, jnp.zeros((1, 1), F32))
            kmax_sc[row : row + 1, :] = jnp.broadcast_to(jnp.sqrt(best), (1, LANES))

    q_f32 = q_t.astype(F32)
    q_norm = jnp.sqrt(jnp.sum(q_f32 * q_f32, axis=0, keepdims=True))
    offset_s = q_norm * (kmax_sc[0:1, 0:1] * BOUND_SLACK)
    offset_w = q_norm * (kmax_sc[1:2, 0:1] * BOUND_SLACK)
    bound_ok = jnp.max(jnp.maximum(offset_s, offset_w)) <= MAX_OFFSET
    ext_row = lax.broadcasted_iota(jnp.int32, (SUBLANES, cols), 0)
    zero_rows = jnp.zeros((LANES - HEAD_DIM - SUBLANES, cols), F32)

    def extended(offset):
        return jnp.concatenate([q_f32, jnp.where(ext_row == 0, -offset, 0.0), zero_rows], axis=0).astype(BF16)

    q_ext = extended(jnp.zeros_like(offset_s))

    def compressed(n_c):
        cmp_end = lax.broadcasted_iota(jnp.int32, (n_c, qb), 0) * CMP_STRIDE + (CMP_LEN - 1)
        bias_c = jnp.where(cmp_end <= tq, 0.0, NEG)
        has_key = jnp.where(tq >= CMP_LEN - 1, 1.0, 0.0)
        sc = jnp.dot(kc_ref[0, 0, :n_c, :], q_t, preferred_element_type=F32) + jnp.concatenate([bias_c] * GROUP, axis=1)
        p_c = jnp.exp2(sc - jnp.max(sc, axis=0, keepdims=True))
        norm = jnp.concatenate([has_key] * GROUP, axis=1) / jnp.maximum(jnp.sum(p_c, axis=0, keepdims=True), 1e-30)
        p_c = p_c * norm
        oc_sc[...] = jnp.dot(vct_ref[0, 0, :, :n_c], p_c.astype(BF16), preferred_element_type=F32)
        p_sum = p_c[:, :qb]
        for r in range(1, GROUP):
            p_sum = p_sum + p_c[:, r * qb : (r + 1) * qb]
        p_hi = p_sum.astype(BF16)
        p_lo = (p_sum - p_hi.astype(F32)).astype(BF16)
        ovt = ovt_ref[:, :n_c]
        imp_sc[...] = (jnp.dot(ovt, p_hi, preferred_element_type=F32)
                       + jnp.dot(ovt, p_lo, preferred_element_type=F32))

    last_cmp = (s0 + qb - CMP_LEN) // CMP_STRIDE
    for v in range(nc // LANES):
        pl.when(last_cmp // LANES == v)(functools.partial(compressed, (v + 1) * LANES))
    o_c = oc_sc[...]
    imp = imp_sc[...]

    def window(bounded):
        w0 = pl.multiple_of(jnp.maximum(s0 - WINDOW, 0), LANES)
        wpos = w0 + lax.broadcasted_iota(jnp.int32, (WIN_KEYS, qb), 0)
        bias_w = jnp.where((wpos <= tq) & (wpos > tq - WINDOW), 0.0, NEG)
        q_op = extended(offset_w) if bounded else q_ext
        s_w = jnp.dot(kw_ref[0, 0, pl.ds(w0, WIN_KEYS), :], q_op, preferred_element_type=F32)
        s_w = s_w + jnp.concatenate([bias_w] * GROUP, axis=1)
        if not bounded:
            s_w = s_w - jnp.max(s_w, axis=0, keepdims=True)
        acc_w = jnp.dot(vwt_ref[0, 0, :, pl.ds(w0, WIN_KEYS)], jnp.exp2(s_w).astype(BF16),
                        preferred_element_type=F32)
        return acc_w[:HEAD_DIM] / jnp.maximum(acc_w[HEAD_DIM : HEAD_DIM + 1], 1e-30)

    def select_blocks():
        blk = lax.broadcasted_iota(jnp.int32, (LANES, qb), 0)
        cur = tq >> SEL_SHIFT
        forced = (blk == 0) | (blk == cur)
        causal = blk * SEL_LEN <= tq
        work = jnp.where(forced, TAKEN, jnp.where(causal, imp, -FORCE))
        for r in range(N_SELECT - 1):
            hit = work == jnp.max(work, axis=0, keepdims=True)
            if r == N_SELECT - 2:
                hit = hit & (cur == 0)
            work = jnp.where(hit, TAKEN, work)
        taken = work == TAKEN
        sel_sc[...] = jnp.where(taken, 1.0, 0.0)
        n_taken = jnp.sum(jnp.where(taken & causal, 1.0, 0.0), axis=0, keepdims=True)

        @pl.when(jnp.max(n_taken) > N_SELECT)
        def _():
            work = jnp.where(forced, FORCE, jnp.where(causal, imp, -FORCE))
            blk_f = blk.astype(F32)
            for _ in range(N_SELECT):
                m = jnp.max(work, axis=0, keepdims=True)
                idx = jnp.min(jnp.where(work == m, blk_f, float(LANES)), axis=0, keepdims=True)
                work = jnp.where(blk_f == idx, TAKEN, work)
            sel_sc[...] = jnp.where(work == TAKEN, 1.0, 0.0)

    blocks_per_tile = SEL_TILE // SEL_LEN
    row_minus_lane = (lax.broadcasted_iota(jnp.int32, (SEL_TILE, qb), 0)
                      - lax.broadcasted_iota(jnp.int32, (SEL_TILE, qb), 1))
    n_tiles = (s0 + qb - 1) // SEL_TILE + 1

    def scores(kb, q_op):
        k0 = pl.multiple_of(kb * SEL_TILE, SEL_TILE)
        picked = sel_sc[pl.ds(pl.multiple_of(kb * blocks_per_tile, blocks_per_tile), blocks_per_tile), :]
        picked = jnp.where(picked > 0.5, 0.0, NEG)
        picked = jnp.concatenate(
            [jnp.broadcast_to(picked[j : j + 1, :], (SEL_LEN, qb)) for j in range(blocks_per_tile)], axis=0)
        bias = jnp.where(row_minus_lane <= s0 - k0, picked, NEG)
        s_ = jnp.dot(ks_ref[0, 0, pl.ds(k0, SEL_TILE), :], q_op, preferred_element_type=F32)
        return s_ + jnp.concatenate([bias] * GROUP, axis=1)

    def values(kb, p):
        k0 = pl.multiple_of(kb * SEL_TILE, SEL_TILE)
        return jnp.dot(vst_ref[0, 0, :, pl.ds(k0, SEL_TILE)], p, preferred_element_type=F32)

    def bounded_path():
        o_win = window(True)
        select_blocks()
        q_off = extended(offset_s)

        def tile(kb, acc):
            return acc + values(kb, jnp.exp2(scores(kb, q_off)).astype(BF16))

        def run(first, count, acc):
            for t in range(count):
                acc = tile(first + t, acc)
            return acc

        octs = n_tiles // 8
        acc = lax.fori_loop(0, octs, lambda j, acc: run(8 * j, 8, acc), jnp.zeros((V_ROWS, cols), F32))
        done = 8 * octs
        for count in (4, 2, 1):
            take = ((n_tiles - done) // count) % 2 == 1
            acc = lax.cond(take, functools.partial(run, done, count), lambda acc: acc, acc)
            done = done + jnp.where(take, count, 0)
        return o_win, acc

    def exact_path():
        o_win = window(False)
        select_blocks()

        def absorb(kb, s_, m_run, acc):
            m_new = jnp.maximum(m_run, jnp.max(s_, axis=0, keepdims=True))
            p = jnp.exp2(s_ - m_new).astype(BF16)
            return m_new, jnp.exp2(m_run - m_new) * acc + values(kb, p)

        sa_sc[...] = scores(0, q_ext)

        def sel_step(j, carry):
            m_run, acc = carry
            sb_sc[...] = scores(2 * j + 1, q_ext)
            m_run, acc = absorb(2 * j, sa_sc[...], m_run, acc)
            sa_sc[...] = scores(2 * j + 2, q_ext)
            return absorb(2 * j + 1, sb_sc[...], m_run, acc)

        init = (jnp.full((1, cols), NEG, F32), jnp.zeros((V_ROWS, cols), F32))
        full_trips = (n_tiles - 1) // 2
        m_run, acc = lax.fori_loop(0, full_trips, sel_step, init)
        m_run, acc = absorb(2 * full_trips, sa_sc[...], m_run, acc)

        def last_tile(carry):
            return absorb(n_tiles - 1, scores(n_tiles - 1, q_ext), *carry)

        return o_win, lax.cond(n_tiles - 2 * full_trips == 2, last_tile, lambda carry: carry, (m_run, acc))[1]

    o_w, acc_s = lax.cond(bound_ok, bounded_path, exact_path)

    o_s = acc_s[:HEAD_DIM] / jnp.maximum(acc_s[HEAD_DIM : HEAD_DIM + 1], 1e-30)

    gates_t = _sigmoid(gate_ref[...].astype(F32)).T
    outs = []
    for r in range(GROUP):
        sl = slice(r * qb, (r + 1) * qb)
        outs.append(gates_t[3 * r : 3 * r + 1] * o_c[:, sl] + gates_t[3 * r + 1 : 3 * r + 2] * o_s[:, sl]
                    + gates_t[3 * r + 2 : 3 * r + 3] * o_w[:, sl])
    o_ref[...] = jnp.concatenate(outs, axis=0).T.astype(o_ref.dtype)


def nsa_attention(proj, cos_q, sin_q, kc, vct, ks, vst, kw, vwt, overlap_t, batch, seq):
    nq = seq // Q_BLOCK
    nc = kc.shape[2]
    gw = GROUP * HEAD_DIM
    per_bg = lambda shape: pl.BlockSpec((1, 1) + shape, lambda b, g, i: (b, g, 0, 0))
    return pl.pallas_call(
        _nsa_kernel,
        grid=(batch, N_KV, nq),
        in_specs=[
            pl.BlockSpec((Q_BLOCK, gw), lambda b, g, i: (b * nq + i, COL_Q // gw + g)),
            pl.BlockSpec((Q_BLOCK, LANES), lambda b, g, i: (b * nq + i, COL_GN // LANES + g)),
            pl.BlockSpec((1, HEAD_DIM // 2, Q_BLOCK), lambda b, g, i: (b, 0, i)),
            pl.BlockSpec((1, HEAD_DIM // 2, Q_BLOCK), lambda b, g, i: (b, 0, i)),
            per_bg((nc, HEAD_DIM)), per_bg((HEAD_DIM, nc)),
            per_bg((seq, LANES)), per_bg((V_ROWS, seq)),
            per_bg((seq, LANES)), per_bg((V_ROWS, seq)),
            pl.BlockSpec((LANES, nc), lambda b, g, i: (0, 0)),
        ],
        out_specs=pl.BlockSpec((Q_BLOCK, gw), lambda b, g, i: (b * nq + i, g)),
        out_shape=jax.ShapeDtypeStruct((batch * seq, N_HEADS * HEAD_DIM), BF16),
        scratch_shapes=[pltpu.VMEM((LANES, Q_BLOCK), F32), pltpu.VMEM((SEL_TILE, GROUP * Q_BLOCK), F32),
                        pltpu.VMEM((SEL_TILE, GROUP * Q_BLOCK), F32), pltpu.VMEM((HEAD_DIM, GROUP * Q_BLOCK), F32),
                        pltpu.VMEM((LANES, Q_BLOCK), F32), pltpu.VMEM((SUBLANES, LANES), F32)],
        compiler_params=_cparams(("arbitrary", "arbitrary", "arbitrary")),
        name="nsa_attention",
    )(proj, proj, cos_q, sin_q, kc, vct, ks, vst, kw, vwt, overlap_t)


def _mem_kernel(q_ref, kv_ref, o_ref):
    q = q_ref[...]
    kv = kv_ref[0]
    mem_w = MEM_HEADS * MEM_HEAD_DIM
    outs = []
    for h in range(MEM_HEADS):
        sl = slice(h * MEM_HEAD_DIM, (h + 1) * MEM_HEAD_DIM)
        qh = (q[:, sl] * (MEM_HEAD_DIM ** -0.5)).astype(BF16)
        s = lax.dot_general(qh, kv[:, sl], (((1,), (1,)), ((), ())), preferred_element_type=F32)
        p = jnp.exp(s - jnp.max(s, axis=-1, keepdims=True))
        p = p / jnp.sum(p, axis=-1, keepdims=True)
        outs.append(jnp.dot(p.astype(BF16), kv[:, mem_w + h * MEM_HEAD_DIM : mem_w + (h + 1) * MEM_HEAD_DIM],
                            preferred_element_type=F32))
    o_ref[...] = jnp.concatenate(outs, axis=1).astype(o_ref.dtype)


def memory_attention(proj, kv_mem, batch, seq, tm):
    nt = seq // tm
    mem_len = kv_mem.shape[1]
    return pl.pallas_call(
        _mem_kernel,
        grid=(batch, nt),
        in_specs=[
            pl.BlockSpec((tm, D_MODEL), lambda b, t: (b * nt + t, COL_QM // D_MODEL)),
            pl.BlockSpec((1, mem_len, 2 * D_MODEL), lambda b, t: (b, 0, 0)),
        ],
        out_specs=pl.BlockSpec((tm, D_MODEL), lambda b, t: (b * nt + t, 0)),
        out_shape=jax.ShapeDtypeStruct((batch * seq, D_MODEL), BF16),
        compiler_params=_cparams(("arbitrary", "arbitrary")),
        name="memory_attention",
    )(proj, kv_mem)


def _merge_kernel(oa_ref, ob_ref, oc_ref, ga_ref, gb_ref, gc_ref, wa_ref, wb_ref, wc_ref, wo_ref,
                  gpost_ref, x_ref, gnext_ref, xo_ref, ho_ref):
    gate = lambda ref: _sigmoid(ref[...].astype(F32))
    merged = gate(ga_ref) * jnp.dot(oa_ref[...], wa_ref[...], preferred_element_type=F32)
    merged += gate(gb_ref) * jnp.dot(ob_ref[...], wb_ref[...], preferred_element_type=F32)
    merged += gate(gc_ref) * jnp.dot(oc_ref[...], wc_ref[...], preferred_element_type=F32)
    y = jnp.dot(merged.astype(BF16), wo_ref[...], preferred_element_type=F32)
    x_new = x_ref[...] + _rms(y, gpost_ref[...])
    xo_ref[...] = x_new
    ho_ref[...] = _rms(x_new, gnext_ref[...]).astype(ho_ref.dtype)


def merge_out(oa, ob, oc, proj, wa, wb, wc, wo, layer, g_post, x, g_next, tm):
    n, d = x.shape
    row = lambda col: pl.BlockSpec((tm, d), lambda i: (i, col))
    full = lambda shape: pl.BlockSpec(shape, lambda i: (0, 0))
    w_spec = pl.BlockSpec((None, d, d), lambda i: (layer, 0, 0))
    gm = COL_GM // d
    return pl.pallas_call(
        _merge_kernel,
        grid=(n // tm,),
        in_specs=[row(0), row(0), row(0), row(gm), row(gm + 1), row(gm + 2),
                  w_spec, w_spec, w_spec, w_spec, full((1, d)), row(0), full((1, d))],
        out_specs=[row(0), row(0)],
        out_shape=[jax.ShapeDtypeStruct((n, d), F32), jax.ShapeDtypeStruct((n, d), BF16)],
        compiler_params=_cparams(("arbitrary",)),
        name="merge_out",
    )(oa, ob, oc, proj, proj, proj, wa, wb, wc, wo, g_post.reshape(1, d), x, g_next.reshape(1, d))


def _mlp_kernel(h_ref, w1_ref, w2_ref, gpost_ref, x_ref, gnext_ref, xo_ref, ho_ref, *, tf):
    h = h_ref[...]
    acc = None
    for k in range(w1_ref.shape[1] // tf):
        u = jnp.maximum(jnp.dot(h, w1_ref[:, k * tf : (k + 1) * tf], preferred_element_type=F32), 0.0)
        part = jnp.dot((u * u).astype(BF16), w2_ref[k * tf : (k + 1) * tf, :], preferred_element_type=F32)
        acc = part if acc is None else acc + part
    x_new = x_ref[...] + _rms(acc, gpost_ref[...])
    xo_ref[...] = x_new
    ho_ref[...] = _rms(x_new, gnext_ref[...]).astype(ho_ref.dtype)


def mlp(h, w1, w2, layer, g_post, x, g_next, tm, tf):
    n, d = x.shape
    ff = w1.shape[2]
    once = pl.Buffered(1)
    return pl.pallas_call(
        functools.partial(_mlp_kernel, tf=tf),
        grid=(n // tm,),
        in_specs=[
            pl.BlockSpec((tm, d), lambda i: (i, 0)),
            pl.BlockSpec((None, d, ff), lambda i: (layer, 0, 0), pipeline_mode=once),
            pl.BlockSpec((None, ff, d), lambda i: (layer, 0, 0), pipeline_mode=once),
            pl.BlockSpec((1, d), lambda i: (0, 0)),
            pl.BlockSpec((tm, d), lambda i: (i, 0)),
            pl.BlockSpec((1, d), lambda i: (0, 0)),
        ],
        out_specs=[pl.BlockSpec((tm, d), lambda i: (i, 0))] * 2,
        out_shape=[jax.ShapeDtypeStruct((n, d), F32), jax.ShapeDtypeStruct((n, d), BF16)],
        compiler_params=_cparams(("arbitrary",)),
        name="mlp",
    )(h, w1, w2, g_post.reshape(1, d), x, g_next.reshape(1, d))


def _pack_w_in(w):
    o_kv, o_gn, o_qm, o_gm = 3072, 4608, 4656, 5680
    per_group = GROUP * 3
    lead = w.shape[:-1]
    gn = w[..., o_gn:o_qm].reshape(lead + (N_KV, per_group))
    gn = jnp.pad(gn, ((0, 0),) * (len(lead) + 1) + ((0, LANES - per_group),)).reshape(lead + (N_KV * LANES,))
    packed = jnp.concatenate([w[..., :o_kv], w[..., o_qm:o_gm], w[..., o_gm:], w[..., o_kv:o_gn], gn], axis=-1)
    assert packed.shape[-1] == D_INP
    return packed.astype(BF16)


def _overlap_matrix_t(nc):
    c0 = np.arange(nc)[:, None] * CMP_STRIDE
    s0 = np.arange(LANES)[None, :] * SEL_LEN
    ov = np.clip(np.minimum(c0 + CMP_LEN, s0 + SEL_LEN) - np.maximum(c0, s0), 0, None).astype(np.float32) / CMP_LEN
    return jnp.asarray(ov.T, dtype=BF16)


def kernel(x, mem, positions, ln_mix_pre, w_in, conv_w, conv_b, lru_wr, lru_br, lru_wi, lru_bi, lru_lambda, cmp_pe, cmp_w1, cmp_b1, cmp_w2, ln_mem, w_mem_kv, w_br_rnn, w_br_nsa, w_br_mem, w_out, ln_mix_post, ln_mlp_pre, mlp_w1, mlp_w2, ln_mlp_post):
    batch, seq, d = x.shape
    depth = w_in.shape[0]
    mem_len = mem.shape[1]
    n = batch * seq
    nc = seq // CMP_STRIDE
    assert d == D_MODEL and seq % (2 * SEL_TILE) == 0 and seq // SEL_LEN <= LANES and seq >= WIN_KEYS

    tm = min(512, seq)
    tm_big = min(1024, seq)
    xf = x.reshape(n, d)
    memf = mem.reshape(batch * mem_len, d)
    cos_t, sin_t = rope_tables(positions.reshape(n), tm)
    pos_c = jnp.pad(positions[:, CMP_LEN - 1 :: CMP_STRIDE], ((0, 0), (0, 1)))
    cos_c, sin_c = rope_tables(pos_c.reshape(batch * nc), nc)
    cos_c, sin_c = cos_c.reshape(batch, nc, LANES), sin_c.reshape(batch, nc, LANES)
    cos_q, sin_q = rope_tables_t(positions, tm)
    overlap_t = _overlap_matrix_t(nc)

    w_in_b, w_kv_b = _pack_w_in(w_in.astype(BF16)), w_mem_kv.astype(BF16)
    wr_b, wi_b = lru_wr.astype(BF16), lru_wi.astype(BF16)
    w1c_b, w2c_b = cmp_w1.astype(BF16), cmp_w2.astype(BF16)
    wa_b, wb_b, wc_b, wo_b = (w.astype(BF16) for w in (w_br_rnn, w_br_nsa, w_br_mem, w_out))
    w1_b, w2_b = mlp_w1.astype(BF16), mlp_w2.astype(BF16)

    h = rmsnorm_bf16(xf, ln_mix_pre[0], tm)
    for l in range(depth):
        proj = matmul(h, w_in_b, l, min(2048, seq), 1024, BF16, "in_proj")

        o_a = rglru_branch(proj, batch, seq, conv_w[l], conv_b[l], wr_b[l], lru_br[l], wi_b[l], lru_bi[l],
                           lru_lambda[l], min(256, seq))

        kc_raw, vc_raw, ks, vst, kw, vwt = kv_prep(proj, cos_t, sin_t, batch, seq, tm)
        chunks = lambda a: a.reshape(batch * N_KV, nc, CMP_STRIDE * HEAD_DIM)
        pe = cmp_pe[l].reshape(2, 2, CMP_STRIDE * HEAD_DIM)
        b1c = cmp_b1[l].reshape(2, 1, -1)
        k_cmp = compress(chunks(kc_raw), 0, pe, w1c_b[l], b1c, w2c_b[l], cos_c, sin_c, batch, True)
        v_cmp = compress(chunks(vc_raw), 1, pe, w1c_b[l], b1c, w2c_b[l], cos_c, sin_c, batch, False)
        kc = k_cmp.reshape(batch, N_KV, nc, HEAD_DIM)
        vct = v_cmp.reshape(batch, N_KV, nc, HEAD_DIM).transpose(0, 1, 3, 2)
        o_b = nsa_attention(proj, cos_q, sin_q, kc, vct, ks, vst, kw, vwt, overlap_t, batch, seq)

        mem_h = rmsnorm_bf16(memf, ln_mem[l], mem_len)
        kv_mem = matmul(mem_h, w_kv_b, l, mem_len, 1024, BF16, "mem_kv")
        o_c = memory_attention(proj, kv_mem.reshape(batch, mem_len, 2 * D_MODEL), batch, seq, tm)

        xf, h = merge_out(o_a, o_b, o_c, proj, wa_b, wb_b, wc_b, wo_b, l, ln_mix_post[l], xf, ln_mlp_pre[l], tm)
        xf, h = mlp(h, w1_b, w2_b, l, ln_mlp_post[l], xf, ln_mix_pre[(l + 1) % depth], tm_big, 1024)
    return xf.reshape(batch, seq, d)
```
